```python
import jax, jax.numpy as jnp
from jax import lax
import numpy as np

D_MODEL = 2048
BATCH = 4
SEQ = 2048
DEPTH = 2

GRID_W = 64
CTX_LEN = 256
HEAD_DIM = 128
HEADS_A = 8
KV_HEADS_A = 2
GROUP_A = HEADS_A // KV_HEADS_A
HEADS_B = 8
KV_HEADS_B = 2
GROUP_B = HEADS_B // KV_HEADS_B
WIDTH_A = HEADS_A * HEAD_DIM
KVW_A = KV_HEADS_A * HEAD_DIM
WIDTH_B = HEADS_B * HEAD_DIM
KVW_B = KV_HEADS_B * HEAD_DIM
IN_WIDTHS = (WIDTH_A, KVW_A, KVW_A, WIDTH_B, KVW_B, KVW_B, D_MODEL, D_MODEL)
IN_COLS = WIDTH_A + 2 * KVW_A + WIDTH_B + 2 * KVW_B + 2 * D_MODEL
Q_BLOCK = 128
WINDOW = 128
ROPE_BASE = 10000.0
ATTN_SCALE = HEAD_DIM ** -0.5
N_GROUPS = 8
EXPERTS_PER_GROUP = 8
N_EXPERTS = N_GROUPS * EXPERTS_PER_GROUP
TOP_K = 2
D_EXPERT = D_MODEL // 4
MOE_BLOCK = 128
EPS = 1e-6
NEG_INF = -1e30

kernel_name = "hybrid_dit_gqa_window_sink_hmoe"


def _rmsnorm(x, w):
    xf = x.astype(jnp.float32)
    y = xf * lax.rsqrt(jnp.mean(xf * xf, axis=-1, keepdims=True) + EPS)
    return (y * w.astype(jnp.float32)).astype(x.dtype)


def _modulate(x, shift, scale):
    return x * (1.0 + scale) + shift


def _split_in(p):
    idx, acc = [], 0
    for w in IN_WIDTHS[:-1]:
        acc += w
        idx.append(acc)
    return jnp.split(p, idx, axis=-1)


def _rope2d(x, row, col):
    half = HEAD_DIM // 2
    n_freq = half // 2
    freqs = ROPE_BASE ** (-jnp.arange(n_freq, dtype=jnp.float32) / n_freq)
    expand = (slice(None),) + (None,) * (x.ndim - 3)

    def rot(xh, pos):
        ang = pos.astype(jnp.float32)[:, None] * freqs[None, :]
        cos, sin = jnp.cos(ang)[expand], jnp.sin(ang)[expand]
        x1, x2 = xh[..., :n_freq], xh[..., n_freq:]
        return jnp.concatenate([x1 * cos - x2 * sin, x1 * sin + x2 * cos], axis=-1)

    xf = x.astype(jnp.float32)
    out = jnp.concatenate([rot(xf[..., :half], row), rot(xf[..., half:], col)], axis=-1)
    return out.astype(x.dtype)


def _dense_attention(q, k, v):
    s = jnp.einsum('bqkgd,bskd->bkgqs', q, k).astype(jnp.float32) * ATTN_SCALE
    p = jax.nn.softmax(s, axis=-1).astype(v.dtype)
    return jnp.einsum('bkgqs,bskd->bqkgd', p, v)


def _block_dense_attention(q, k, v):
    B, S = q.shape[:2]
    nb = S // Q_BLOCK
    qb = q.reshape((B, nb, Q_BLOCK) + q.shape[2:]).swapaxes(0, 1)
    out = lax.map(lambda qi: _dense_attention(qi, k, v), qb)
    return out.swapaxes(0, 1).reshape(q.shape)


def _window_sink_attention(q, k, v, kc, vc, sink):
    B, S, KV, G, D = q.shape
    nb = S // Q_BLOCK
    qb = q.reshape(B, nb, Q_BLOCK, KV, G, D)
    pad = ((0, 0), (Q_BLOCK, Q_BLOCK), (0, 0), (0, 0))
    kp = jnp.pad(k, pad).reshape(B, nb + 2, Q_BLOCK, KV, D)
    vp = jnp.pad(v, pad).reshape(B, nb + 2, Q_BLOCK, KV, D)

    def band(t):
        return jnp.concatenate([t[:, :-2], t[:, 1:-1], t[:, 2:]], axis=2)

    kb, vb = band(kp), band(vp)
    blk = jnp.arange(nb, dtype=jnp.int32)[:, None] * Q_BLOCK
    qidx = blk + jnp.arange(Q_BLOCK, dtype=jnp.int32)[None, :]
    kidx = blk - Q_BLOCK + jnp.arange(3 * Q_BLOCK, dtype=jnp.int32)[None, :]
    valid = ((jnp.abs(qidx[:, :, None] - kidx[:, None, :]) <= WINDOW)
             & (kidx[:, None, :] >= 0) & (kidx[:, None, :] < S))
    s_loc = jnp.einsum('bnqkgd,bnskd->bnkgqs', qb, kb).astype(jnp.float32) * ATTN_SCALE
    s_loc = jnp.where(valid[None, :, None, None], s_loc, NEG_INF)
    s_ctx = jnp.einsum('bnqkgd,bckd->bnkgqc', qb, kc).astype(jnp.float32) * ATTN_SCALE
    s_sink = jnp.broadcast_to(sink.astype(jnp.float32).reshape(1, 1, KV, G, 1, 1),
                              s_loc.shape[:-1] + (1,))
    p = jax.nn.softmax(jnp.concatenate([s_loc, s_ctx, s_sink], axis=-1), axis=-1).astype(v.dtype)
    n_loc = 3 * Q_BLOCK
    n_ctx = kc.shape[1]
    out = (jnp.einsum('bnkgqs,bnskd->bnqkgd', p[..., :n_loc], vb)
           + jnp.einsum('bnkgqc,bckd->bnqkgd', p[..., n_loc:n_loc + n_ctx], vc))
    return out.reshape(B, S, KV, G, D)


def _ctx_sink_attention(q, k, v, sink):
    B, C, KV, G, D = q.shape
    s = jnp.einsum('bqkgd,bskd->bkgqs', q, k).astype(jnp.float32) * ATTN_SCALE
    s_sink = jnp.broadcast_to(sink.astype(jnp.float32).reshape(1, KV, G, 1, 1), s.shape[:-1] + (1,))
    p = jax.nn.softmax(jnp.concatenate([s, s_sink], axis=-1), axis=-1)[..., :-1].astype(v.dtype)
    return jnp.einsum('bkgqs,bskd->bqkgd', p, v)


def _merge(ya, yb, ga, gb, w_br_a, w_br_b, w_out):
    m = jax.nn.sigmoid(ga) * (ya @ w_br_a) + jax.nn.sigmoid(gb) * (yb @ w_br_b)
    return m @ w_out


def _moe(h, w_rg, b_rg, w_re, b_re, w1, w3, w2):
    N, D = h.shape
    gl = (h @ w_rg).astype(jnp.float32) + b_rg.astype(jnp.float32)
    g = jnp.argmax(gl, axis=-1).astype(jnp.int32)
    pg = jnp.take_along_axis(jax.nn.softmax(gl, axis=-1), g[:, None], axis=-1)
    el = ((h @ w_re).astype(jnp.float32) + b_re.astype(jnp.float32)).reshape(N, N_GROUPS, EXPERTS_PER_GROUP)
    el_g = jnp.take_along_axis(el, g[:, None, None], axis=1)[:, 0]
    tv, ti = lax.top_k(el_g, TOP_K)
    wts = jax.nn.softmax(tv, axis=-1) * pg
    eid = g[:, None] * EXPERTS_PER_GROUP + ti.astype(jnp.int32)

    A = N * TOP_K
    e_flat = eid.reshape(A)
    w_flat = wts.reshape(A)
    tok = jnp.repeat(jnp.arange(N, dtype=jnp.int32), TOP_K)
    order = jnp.argsort(e_flat)
    e_s, tok_s, w_s = e_flat[order], tok[order], w_flat[order]
    counts = jnp.bincount(e_flat, length=N_EXPERTS).astype(jnp.int32)
    start = jnp.cumsum(counts) - counts
    padded = (counts + MOE_BLOCK - 1) // MOE_BLOCK * MOE_BLOCK
    pad_end = jnp.cumsum(padded)
    pad_start = pad_end - padded
    dest = pad_start[e_s] + jnp.arange(A, dtype=jnp.int32) - start[e_s]
    n_blk = -(-A // MOE_BLOCK) + N_EXPERTS
    P = n_blk * MOE_BLOCK
    xbuf = jnp.zeros((P, D), h.dtype).at[dest].set(h[tok_s])
    blk_e = jnp.clip(jnp.searchsorted(pad_end, jnp.arange(n_blk, dtype=jnp.int32) * MOE_BLOCK, side='right'),
                     0, N_EXPERTS - 1)

    def expert_block(args):
        xb, e = args
        return (jax.nn.silu(xb @ w1[e]) * (xb @ w3[e])) @ w2[e]

    ybuf = lax.map(expert_block, (xbuf.reshape(n_blk, MOE_BLOCK, D), blk_e)).reshape(P, D)
    contrib = (ybuf[dest] * w_s[:, None]).astype(h.dtype)
    return jnp.zeros((N, D), h.dtype).at[tok_s].add(contrib)


def _layer(x, xc, c, c_ctx, w_mod, b_mod, norm_mix, norm_ffn, w_in, qn_a, kn_a, sink_b,
           w_br_a, w_br_b, w_out, w_rg, b_rg, w_re, b_re, w1, w3, w2, row, col, ctx_out):
    B, S, D = x.shape
    C = xc.shape[1]
    mod = jax.nn.silu(c) @ w_mod + b_mod
    sh1, sc1, g1, sh2, sc2, g2 = [m[:, None, :] for m in jnp.split(mod, 6, axis=-1)]
    modc = jax.nn.silu(c_ctx) @ w_mod + b_mod
    csh1, csc1, cg1, csh2, csc2, cg2 = jnp.split(modc, 6, axis=-1)

    h = _modulate(_rmsnorm(x, norm_mix), sh1, sc1)
    hc = _modulate(_rmsnorm(xc, norm_mix), csh1, csc1)
    p = jnp.concatenate([hc, h], axis=1) @ w_in
    qa, ka, va, qb, kb, vb, ga, gb = _split_in(p)
    T = C + S
    qa = _rmsnorm(qa.reshape(B, T, KV_HEADS_A, GROUP_A, HEAD_DIM), qn_a)
    ka = _rmsnorm(ka.reshape(B, T, KV_HEADS_A, HEAD_DIM), kn_a)
    va = va.reshape(B, T, KV_HEADS_A, HEAD_DIM)
    qb = qb.reshape(B, T, KV_HEADS_B, GROUP_B, HEAD_DIM)
    kb = kb.reshape(B, T, KV_HEADS_B, HEAD_DIM)
    vb = vb.reshape(B, T, KV_HEADS_B, HEAD_DIM)

    ka_all = jnp.concatenate([ka[:, :C], _rope2d(ka[:, C:], row, col)], axis=1)
    ya = _block_dense_attention(_rope2d(qa[:, C:], row, col), ka_all, va).reshape(B, S, WIDTH_A)
    yb = _window_sink_attention(_rope2d(qb[:, C:], row, col), _rope2d(kb[:, C:], row, col),
                                vb[:, C:], kb[:, :C], vb[:, :C], sink_b).reshape(B, S, WIDTH_B)
    x = x + g1 * _merge(ya, yb, ga[:, C:], gb[:, C:], w_br_a, w_br_b, w_out)

    if ctx_out:
        ya_c = _dense_attention(qa[:, :C], ka[:, :C], va[:, :C]).reshape(B, C, WIDTH_A)
        yb_c = _ctx_sink_attention(qb[:, :C], kb[:, :C], vb[:, :C], sink_b).reshape(B, C, WIDTH_B)
        xc = xc + cg1 * _merge(ya_c, yb_c, ga[:, :C], gb[:, :C], w_br_a, w_br_b, w_out)
        h2 = _modulate(_rmsnorm(x, norm_ffn), sh2, sc2)
        h2c = _modulate(_rmsnorm(xc, norm_ffn), csh2, csc2)
        tokens = jnp.concatenate([h2c.reshape(B * C, D), h2.reshape(B * S, D)], axis=0)
        y = _moe(tokens, w_rg, b_rg, w_re, b_re, w1, w3, w2)
        xc = xc + cg2 * y[:B * C].reshape(B, C, D)
        y_lat = y[B * C:].reshape(B, S, D)
    else:
        h2 = _modulate(_rmsnorm(x, norm_ffn), sh2, sc2)
        y_lat = _moe(h2.reshape(B * S, D), w_rg, b_rg, w_re, b_re, w1, w3, w2).reshape(B, S, D)
    x = x + g2 * y_lat
    return x, xc


def setup_inputs(seed: int = 0) -> dict:
    key = jax.random.key(seed)
    ks = jax.random.split(key, 24)
    f32 = jnp.float32

    def nrm(k, shape, scale):
        return jax.random.normal(k, shape, f32) * scale

    D, L = D_MODEL, DEPTH
    return {
        "x": nrm(ks[0], (BATCH, SEQ, D), 1.0),
        "c": nrm(ks[1], (BATCH, D), 1.0),
        "ctx": nrm(ks[2], (BATCH, CTX_LEN, D), 1.0),
        "c_ctx": nrm(ks[3], (D,), 1.0),
        "w_mod": nrm(ks[4], (L, D, 6 * D), 0.5 * D ** -0.5),
        "b_mod": nrm(ks[5], (L, 6 * D), 0.02),
        "norm_mix": 1.0 + nrm(ks[6], (L, D), 0.02),
        "norm_ffn": 1.0 + nrm(ks[7], (L, D), 0.02),
        "w_in": nrm(ks[8], (L, D, IN_COLS), D ** -0.5),
        "qn_a": 1.0 + nrm(ks[9], (L, HEAD_DIM), 0.02),
        "kn_a": 1.0 + nrm(ks[10], (L, HEAD_DIM), 0.02),
        "sink_b": nrm(ks[11], (L, HEADS_B), 0.5),
        "w_br_a": nrm(ks[12], (L, WIDTH_A, D), WIDTH_A ** -0.5),
        "w_br_b": nrm(ks[13], (L, WIDTH_B, D), WIDTH_B ** -0.5),
        "w_out": nrm(ks[14], (L, D, D), D ** -0.5),
        "w_rg": nrm(ks[15], (L, D, N_GROUPS), D ** -0.5),
        "b_rg": nrm(ks[16], (L, N_GROUPS), 0.01),
        "w_re": nrm(ks[17], (L, D, N_EXPERTS), D ** -0.5),
        "b_re": nrm(ks[18], (L, N_EXPERTS), 0.01),
        "w1": nrm(ks[19], (L, N_EXPERTS, D, D_EXPERT), D ** -0.5),
        "w3": nrm(ks[20], (L, N_EXPERTS, D, D_EXPERT), D ** -0.5),
        "w2": nrm(ks[21], (L, N_EXPERTS, D_EXPERT, D), D_EXPERT ** -0.5),
        "norm_final": 1.0 + nrm(ks[22], (D,), 0.02),
    }


def reference(x, c, ctx, c_ctx, w_mod, b_mod, norm_mix, norm_ffn, w_in, qn_a, kn_a, sink_b,
              w_br_a, w_br_b, w_out, w_rg, b_rg, w_re, b_re, w1, w3, w2, norm_final):
    S = x.shape[1]
    rows_n = S // GRID_W
    row = jnp.repeat(jnp.arange(rows_n, dtype=jnp.int32), GRID_W)
    col = jnp.tile(jnp.arange(GRID_W, dtype=jnp.int32), rows_n)
    xc = ctx
    for l in range(DEPTH):
        x, xc = _layer(x, xc, c, c_ctx, w_mod[l], b_mod[l], norm_mix[l], norm_ffn[l], w_in[l],
                       qn_a[l], kn_a[l], sink_b[l], w_br_a[l], w_br_b[l], w_out[l],
                       w_rg[l], b_rg[l], w_re[l], b_re[l], w1[l], w3[l], w2[l],
                       row, col, l < DEPTH - 1)
    return _rmsnorm(x, norm_final)
```

```python
import functools

import jax
import jax.numpy as jnp
from jax import lax
from jax.experimental import pallas as pl
from jax.experimental.pallas import tpu as pltpu

F32 = jnp.float32
BF16 = jnp.bfloat16

GRID_W = 64
HEAD_DIM = 128
HEADS = 8
KV_HEADS = 2
GROUP = HEADS // KV_HEADS
WIDTH = HEADS * HEAD_DIM
KVW = KV_HEADS * HEAD_DIM
WINDOW = 128
ROPE_BASE = 10000.0
ATTN_SCALE = HEAD_DIM ** -0.5
N_GROUPS = 8
EXPERTS_PER_GROUP = 8
N_EXPERTS = N_GROUPS * EXPERTS_PER_GROUP
TOP_K = 2
EPS = 1e-6
NEG_INF = -1e30

ROW_TILE = 256
PROJ_ROWS = 768
PROJ_COLS = 512
MOE_ROWS = 128
ROUTER_LANES = 128
MOD_ROWS = 8
VMEM_LIMIT = 56 * 1024 * 1024


def _params(*sem):
    return pltpu.CompilerParams(dimension_semantics=sem, vmem_limit_bytes=VMEM_LIMIT)


def _rms(y):
    return y * lax.rsqrt(jnp.mean(y * y, axis=-1, keepdims=True) + EPS)


def _mod_kernel(c_ref, w_ref, b_ref, o_ref):
    c = c_ref[...]
    a = (c * jax.nn.sigmoid(c)).astype(BF16)
    o_ref[...] = jnp.dot(a, w_ref[...].astype(BF16), preferred_element_type=F32) + b_ref[...]


def _mod_vectors(cc, w_mod, b_mod):
    L, D, N = w_mod.shape
    tn = 1024
    return pl.pallas_call(
        _mod_kernel,
        grid=(L, N // tn),
        in_specs=[
            pl.BlockSpec((MOD_ROWS, D), lambda l, j: (0, 0)),
            pl.BlockSpec((None, D, tn), lambda l, j: (l, 0, j)),
            pl.BlockSpec((None, 1, tn), lambda l, j: (l, 0, j)),
        ],
        out_specs=pl.BlockSpec((None, MOD_ROWS, tn), lambda l, j: (l, 0, j)),
        out_shape=jax.ShapeDtypeStruct((L, MOD_ROWS, N), F32),
        compiler_params=_params("parallel", "parallel"),
        name="mod_vectors",
    )(cc, w_mod, b_mod.reshape(L, 1, N))


class _Rows:
    def __init__(self, batch, tiles_per_batch, latent_only):
        self.batch = batch
        self.tpb = tiles_per_batch
        self.latent_only = latent_only
        self.n = batch * (tiles_per_batch - 1 if latent_only else tiles_per_batch)

    def tile(self, i):
        if self.latent_only:
            per = self.tpb - 1
            return (i // per) * self.tpb + 1 + i % per
        return i

    def mod_row(self, i):
        if self.latent_only:
            return i // (self.tpb - 1)
        return jnp.where(i % self.tpb == 0, self.batch, i // self.tpb)

    def row_spec(self, width):
        return pl.BlockSpec((ROW_TILE, width), lambda i: (self.tile(i), 0))

    def mod_spec(self, d, k):
        return pl.BlockSpec((None, 1, d), lambda i: (self.mod_row(i) * 6 + k, 0, 0))


def _const_spec(shape):
    zeros = (0,) * len(shape)
    return pl.BlockSpec(shape, lambda *_: zeros, pipeline_mode=pl.Buffered(1))


def _norm_mod_kernel(x_ref, nw_ref, sh_ref, sc_ref, h_ref):
    y = _rms(x_ref[...]) * nw_ref[...]
    h_ref[...] = (y * (1.0 + sc_ref[...]) + sh_ref[...]).astype(BF16)


def _norm_mod(x_all, nw, mod_l, rows):
    M, D = x_all.shape
    return pl.pallas_call(
        _norm_mod_kernel,
        grid=(rows.n,),
        in_specs=[rows.row_spec(D), _const_spec((1, D)), rows.mod_spec(D, 0), rows.mod_spec(D, 1)],
        out_specs=rows.row_spec(D),
        out_shape=jax.ShapeDtypeStruct((M, D), BF16),
        compiler_params=_params("parallel"),
        name="norm_mod",
    )(x_all, nw, mod_l, mod_l)


def _rope(y, cos, sin):
    lane = lax.broadcasted_iota(jnp.int32, y.shape, 1)
    quarter = HEAD_DIM // 4
    partner = jnp.where((lane & quarter) == 0,
                        pltpu.roll(y, HEAD_DIM - quarter, 1), pltpu.roll(y, quarter, 1))
    return y * cos + partner * sin


def _proj_kernel(h_ref, w_ref, cos_ref, sin_ref, gain_ref, *out_refs, kinds):
    acc = jnp.dot(h_ref[...], w_ref[...], preferred_element_type=F32)
    for c, (kind, out_idx, out_chunk) in enumerate(kinds):
        y = acc[:, c * HEAD_DIM:(c + 1) * HEAD_DIM]
        if kind in ("q_norm", "k_norm"):
            y = _rms(y) * gain_ref[...]
        if kind in ("q_norm", "k_norm", "q_rope", "k_rope"):
            y = _rope(y, cos_ref[...], sin_ref[...])
        if kind in ("q_norm", "q_rope"):
            y = y * ATTN_SCALE
        if kind == "gate":
            y = jax.nn.sigmoid(y)
        out_refs[out_idx][:, out_chunk * HEAD_DIM:(out_chunk + 1) * HEAD_DIM] = y.astype(BF16)


def _proj(h, w_in_l, cos, sin, gain, col0, kinds, out_widths, tiles_per_batch, name):
    M, D = h.shape
    n_col_tiles = out_widths[0][1]
    cb0 = col0 // PROJ_COLS
    proj_tiles_per_batch = tiles_per_batch * ROW_TILE // PROJ_ROWS
    return pl.pallas_call(
        functools.partial(_proj_kernel, kinds=kinds),
        grid=(n_col_tiles, M // PROJ_ROWS),
        in_specs=[
            pl.BlockSpec((PROJ_ROWS, D), lambda j, i: (i, 0)),
            pl.BlockSpec((D, PROJ_COLS), lambda j, i: (0, cb0 + j)),
            pl.BlockSpec((PROJ_ROWS, HEAD_DIM), lambda j, i: (i % proj_tiles_per_batch, 0)),
            pl.BlockSpec((PROJ_ROWS, HEAD_DIM), lambda j, i: (i % proj_tiles_per_batch, 0)),
            pl.BlockSpec((1, HEAD_DIM), lambda j, i: (0, 0)),
        ],
        out_specs=[pl.BlockSpec((PROJ_ROWS, w), lambda j, i: (i, j)) for w, _ in out_widths],
        out_shape=[jax.ShapeDtypeStruct((M, w * n), BF16) for w, n in out_widths],
        compiler_params=_params("parallel", "parallel"),
        name=name,
    )(h, w_in_l, cos, sin, gain)


def _stack_heads(q_ref):
    return jnp.concatenate([q_ref[:, g * HEAD_DIM:(g + 1) * HEAD_DIM] for g in range(GROUP)], axis=0)


def _unstack_heads(o_ref, o, l):
    rows = o_ref.shape[0]
    o = o * (1.0 / l)
    for g in range(GROUP):
        o_ref[:, g * HEAD_DIM:(g + 1) * HEAD_DIM] = o[g * rows:(g + 1) * rows].astype(o_ref.dtype)


def _scores(q, k):
    return lax.dot_general(q, k, (((1,), (1,)), ((), ())), preferred_element_type=F32)


def _online_step(q, k, v, m, l, acc, mask=None):
    s = _scores(q, k)
    if mask is not None:
        s = jnp.where(mask, s, NEG_INF)
    m_new = jnp.maximum(m, jnp.max(s, axis=-1, keepdims=True))
    alpha = jnp.exp(m - m_new)
    p = jnp.exp(s - m_new)
    l = alpha * l + jnp.sum(p, axis=-1, keepdims=True)
    acc = alpha * acc + jnp.dot(p.astype(BF16), v, preferred_element_type=F32)
    return m_new, l, acc


def _sink_column(sink_ref, kvh, rows):
    return jnp.concatenate(
        [jnp.full((rows, 1), sink_ref[kvh * GROUP + g], F32) for g in range(GROUP)], axis=0)


def _dense_attn_kernel(q_ref, k_ref, v_ref, o_ref, *, ctx_len, total_len, key_chunk, ctx_tile):
    def run(n_keys, chunk):
        q = _stack_heads(q_ref)
        r = q.shape[0]
        m = jnp.full((r, 1), -jnp.inf, F32)
        l = jnp.zeros((r, 1), F32)
        acc = jnp.zeros((r, HEAD_DIM), F32)
        for c in range(n_keys // chunk):
            m, l, acc = _online_step(q, k_ref[c * chunk:(c + 1) * chunk, :],
                                     v_ref[c * chunk:(c + 1) * chunk, :], m, l, acc)
        _unstack_heads(o_ref, acc, l)

    if ctx_tile:
        is_ctx = pl.program_id(2) == 0
        pl.when(is_ctx)(lambda: run(ctx_len, ctx_len))
        pl.when(jnp.logical_not(is_ctx))(lambda: run(total_len, key_chunk))
    else:
        run(total_len, key_chunk)


def _window_attn_kernel(sink_ref, q_ref, k_ref, v_ref, o_ref, *, ctx_len, total_len, ctx_tile, q_off):
    kvh = pl.program_id(1)
    tq = q_ref.shape[0]
    span = tq + 2 * WINDOW

    def init():
        q = _stack_heads(q_ref)
        r = q.shape[0]
        return q, _sink_column(sink_ref, kvh, tq), jnp.ones((r, 1), F32), jnp.zeros((r, HEAD_DIM), F32)

    def run_ctx():
        q, m, l, acc = init()
        m, l, acc = _online_step(q, k_ref[0:ctx_len, :], v_ref[0:ctx_len, :], m, l, acc)
        _unstack_heads(o_ref, acc, l)

    def run_latent():
        q, m, l, acc = init()
        m, l, acc = _online_step(q, k_ref[0:ctx_len, :], v_ref[0:ctx_len, :], m, l, acc)
        q0 = (pl.program_id(2) + q_off) * tq - ctx_len
        n_latent = total_len - ctx_len
        start = pl.multiple_of(jnp.clip(q0 - WINDOW, 0, n_latent - span), WINDOW)
        k = k_ref[pl.ds(ctx_len + start, span), :]
        v = v_ref[pl.ds(ctx_len + start, span), :]
        r = q.shape[0]
        qpos = q0 + lax.broadcasted_iota(jnp.int32, (r, span), 0) % tq
        kpos = start + lax.broadcasted_iota(jnp.int32, (r, span), 1)
        mask = jnp.abs(qpos - kpos) <= WINDOW
        m, l, acc = _online_step(q, k, v, m, l, acc, mask)
        _unstack_heads(o_ref, acc, l)

    if ctx_tile:
        is_ctx = pl.program_id(2) == 0
        pl.when(is_ctx)(run_ctx)
        pl.when(jnp.logical_not(is_ctx))(run_latent)
    else:
        run_latent()


def _attention(q, k, v, sink, batch, total_len, ctx_len, latent_only, window):
    M = q.shape[0]
    q3, k3, v3 = (a.reshape(batch, total_len, a.shape[1]) for a in (q, k, v))
    tiles = total_len // ROW_TILE
    q_off = 1 if latent_only else 0
    gw = GROUP * HEAD_DIM
    q_spec = pl.BlockSpec((None, ROW_TILE, gw), lambda b, h, i, *_: (b, i + q_off, h))
    kv_spec = pl.BlockSpec((None, total_len, HEAD_DIM), lambda b, h, i, *_: (b, 0, h))
    grid = (batch, KV_HEADS, tiles - q_off)
    out_shape = jax.ShapeDtypeStruct(q3.shape, BF16)
    if window:
        kern = functools.partial(_window_attn_kernel, ctx_len=ctx_len, total_len=total_len,
                                 ctx_tile=not latent_only, q_off=q_off)
        out = pl.pallas_call(
            kern,
            grid_spec=pltpu.PrefetchScalarGridSpec(
                num_scalar_prefetch=1, grid=grid,
                in_specs=[q_spec, kv_spec, kv_spec], out_specs=q_spec),
            out_shape=out_shape,
            compiler_params=_params("parallel", "parallel", "parallel"),
            name="window_attention",
        )(sink, q3, k3, v3)
    else:
        kern = functools.partial(_dense_attn_kernel, ctx_len=ctx_len, total_len=total_len,
                                 key_chunk=total_len // 3, ctx_tile=not latent_only)
        out = pl.pallas_call(
            kern, grid=grid, in_specs=[q_spec, kv_spec, kv_spec], out_specs=q_spec,
            out_shape=out_shape,
            compiler_params=_params("parallel", "parallel", "parallel"),
            name="dense_attention",
        )(q3, k3, v3)
    return out.reshape(M, q.shape[1])


def _merge_kernel(ya_ref, yb_ref, ga_ref, gb_ref, wa_ref, wb_ref, wo_ref, x_ref, g1_ref,
                  nw_ref, sh_ref, sc_ref, wr_ref, br_ref, xo_ref, h_ref, lg_ref):
    a = jnp.dot(ya_ref[...], wa_ref[...], preferred_element_type=F32)
    b = jnp.dot(yb_ref[...], wb_ref[...], preferred_element_type=F32)
    m = ga_ref[...].astype(F32) * a + gb_ref[...].astype(F32) * b
    o = jnp.dot(m.astype(BF16), wo_ref[...], preferred_element_type=F32)
    xn = x_ref[...] + g1_ref[...] * o
    xo_ref[...] = xn
    h = (_rms(xn) * nw_ref[...]) * (1.0 + sc_ref[...]) + sh_ref[...]
    h_ref[...] = h.astype(BF16)
    lg_ref[...] = jnp.dot(h, wr_ref[...], preferred_element_type=F32,
                          precision=lax.Precision.HIGHEST) + br_ref[...]


def _merge(ya, yb, gates, wa, wb, wo, x_all, mod_l, nw, wr, br, rows):
    M, D = x_all.shape
    W = ya.shape[1]
    return pl.pallas_call(
        _merge_kernel,
        grid=(rows.n,),
        in_specs=[
            rows.row_spec(W), rows.row_spec(W),
            pl.BlockSpec((ROW_TILE, D), lambda i: (rows.tile(i), 0)),
            pl.BlockSpec((ROW_TILE, D), lambda i: (rows.tile(i), 1)),
            _const_spec((W, D)), _const_spec((W, D)), _const_spec((D, D)),
            rows.row_spec(D), rows.mod_spec(D, 2),
            _const_spec((1, D)), rows.mod_spec(D, 3), rows.mod_spec(D, 4),
            _const_spec((D, ROUTER_LANES)), _const_spec((1, ROUTER_LANES)),
        ],
        out_specs=[rows.row_spec(D), rows.row_spec(D), rows.row_spec(ROUTER_LANES)],
        out_shape=[jax.ShapeDtypeStruct((M, D), F32), jax.ShapeDtypeStruct((M, D), BF16),
                   jax.ShapeDtypeStruct((M, ROUTER_LANES), F32)],
        compiler_params=_params("parallel"),
        name="merge",
    )(ya, yb, gates, gates, wa, wb, wo, x_all, mod_l, nw, mod_l, mod_l, wr, br)


def _route(logits):
    n = logits.shape[0]
    gl = logits[:, :N_GROUPS]
    el = logits[:, N_GROUPS:N_GROUPS + N_EXPERTS].reshape(n, N_GROUPS, EXPERTS_PER_GROUP)
    g = jnp.argmax(gl, axis=-1).astype(jnp.int32)
    pg = jnp.take_along_axis(jax.nn.softmax(gl, axis=-1), g[:, None], axis=-1)
    el_g = jnp.take_along_axis(el, g[:, None, None], axis=1)[:, 0]
    tv, ti = lax.top_k(el_g, TOP_K)
    wts = jax.nn.softmax(tv, axis=-1) * pg
    eid = g[:, None] * EXPERTS_PER_GROUP + ti.astype(jnp.int32)
    return eid, wts


def _dispatch(eid, tok_rows):
    n = eid.shape[0]
    a = n * TOP_K
    e_flat = eid.reshape(a)
    onehot = (e_flat[:, None] == jnp.arange(N_EXPERTS, dtype=jnp.int32)[None, :]).astype(jnp.int32)
    csum = jnp.cumsum(onehot, axis=0)
    rank = jnp.take_along_axis(csum, e_flat[:, None], axis=1)[:, 0] - 1
    counts = csum[-1]
    padded = (counts + MOE_ROWS - 1) // MOE_ROWS * MOE_ROWS
    pad_end = jnp.cumsum(padded)
    pad_start = pad_end - padded
    dest = pad_start[e_flat] + rank
    n_blk = a // MOE_ROWS + N_EXPERTS
    slot_row = jnp.zeros((n_blk * MOE_ROWS,), jnp.int32).at[dest].set(jnp.repeat(tok_rows, TOP_K))
    blk = jnp.arange(n_blk, dtype=jnp.int32)
    n_valid = (pad_end[-1] // MOE_ROWS).astype(jnp.int32)
    blk_c = jnp.minimum(blk, n_valid - 1)
    blk_e = jnp.clip(jnp.searchsorted(pad_end, blk_c * MOE_ROWS, side="right"), 0, N_EXPERTS - 1).astype(jnp.int32)
    return dest, slot_row, blk_e, blk_c, n_valid.reshape(1)


def _experts_kernel(blk_e_ref, blk_c_ref, n_valid_ref, x_ref, w1_ref, w3_ref, w2_ref, o_ref,
                    w1b, w3b, w2b):
    i = pl.program_id(0)
    prev = blk_e_ref[jnp.maximum(i - 1, 0)]
    fresh = jnp.logical_or(i == 0, blk_e_ref[i] != prev)

    @pl.when(fresh)
    def _():
        w1b[...] = w1_ref[...].astype(BF16)
        w3b[...] = w3_ref[...].astype(BF16)
        w2b[...] = w2_ref[...].astype(BF16)

    @pl.when(i < n_valid_ref[0])
    def _():
        x = x_ref[...]
        a = jnp.dot(x, w1b[...], preferred_element_type=F32)
        b = jnp.dot(x, w3b[...], preferred_element_type=F32)
        hid = (a * jax.nn.sigmoid(a)) * b
        o_ref[...] = jnp.dot(hid.astype(BF16), w2b[...], preferred_element_type=F32).astype(o_ref.dtype)


def _experts(xbuf, w1, w3, w2, layer, blk_e, blk_c, n_valid):
    P, D = xbuf.shape
    de = w1.shape[-1]
    n_blk = P // MOE_ROWS
    return pl.pallas_call(
        _experts_kernel,
        grid_spec=pltpu.PrefetchScalarGridSpec(
            num_scalar_prefetch=3, grid=(n_blk,),
            in_specs=[
                pl.BlockSpec((MOE_ROWS, D), lambda i, e, c, n: (c[i], 0)),
                pl.BlockSpec((None, None, D, de), lambda i, e, c, n: (layer, e[i], 0, 0)),
                pl.BlockSpec((None, None, D, de), lambda i, e, c, n: (layer, e[i], 0, 0)),
                pl.BlockSpec((None, None, de, D), lambda i, e, c, n: (layer, e[i], 0, 0)),
            ],
            out_specs=pl.BlockSpec((MOE_ROWS, D), lambda i, e, c, n: (c[i], 0)),
            scratch_shapes=[pltpu.VMEM((D, de), BF16), pltpu.VMEM((D, de), BF16), pltpu.VMEM((de, D), BF16)],
        ),
        out_shape=jax.ShapeDtypeStruct((P, D), BF16),
        compiler_params=_params("arbitrary"),
        name="experts",
    )(blk_e, blk_c, n_valid, xbuf, w1, w3, w2)


def _moe_mix(x_ref, y_ref, w_ref, g2_ref):
    d = x_ref.shape[1]
    w = w_ref[...]
    y = w[:, 0:1] * y_ref[:, :d].astype(F32) + w[:, 1:2] * y_ref[:, d:].astype(F32)
    return x_ref[...] + g2_ref[...] * y


def _combine_next_kernel(x_ref, y_ref, w_ref, g2_ref, nw_ref, sh_ref, sc_ref, xo_ref, h_ref):
    xn = _moe_mix(x_ref, y_ref, w_ref, g2_ref)
    xo_ref[...] = xn
    h_ref[...] = ((_rms(xn) * nw_ref[...]) * (1.0 + sc_ref[...]) + sh_ref[...]).astype(BF16)


def _combine_final_kernel(x_ref, y_ref, w_ref, g2_ref, nf_ref, o_ref):
    o_ref[...] = _rms(_moe_mix(x_ref, y_ref, w_ref, g2_ref)) * nf_ref[...]


def _combine_next(x_new, ysel, wts, mod_l, nw_next, mod_next, rows):
    M, D = x_new.shape
    return pl.pallas_call(
        _combine_next_kernel,
        grid=(rows.n,),
        in_specs=[rows.row_spec(D), rows.row_spec(TOP_K * D), rows.row_spec(TOP_K), rows.mod_spec(D, 5),
                  _const_spec((1, D)), rows.mod_spec(D, 0), rows.mod_spec(D, 1)],
        out_specs=[rows.row_spec(D), rows.row_spec(D)],
        out_shape=[jax.ShapeDtypeStruct((M, D), F32), jax.ShapeDtypeStruct((M, D), BF16)],
        compiler_params=_params("parallel"),
        name="combine_next",
    )(x_new, ysel, wts, mod_l, nw_next, mod_next, mod_next)


def _combine_final(x_new, ysel, wts, mod_l, nf, rows):
    M, D = x_new.shape
    n = ysel.shape[0]
    dense = lambda width: pl.BlockSpec((ROW_TILE, width), lambda i: (i, 0))
    return pl.pallas_call(
        _combine_final_kernel,
        grid=(rows.n,),
        in_specs=[rows.row_spec(D), dense(TOP_K * D), dense(TOP_K), rows.mod_spec(D, 5), _const_spec((1, D))],
        out_specs=dense(D),
        out_shape=jax.ShapeDtypeStruct((n, D), F32),
        compiler_params=_params("parallel"),
        name="combine_final",
    )(x_new, ysel, wts, mod_l, nf)


def _rope_tables(ctx_len, seq):
    quarter = HEAD_DIM // 4
    freqs = ROPE_BASE ** (-jnp.arange(quarter, dtype=F32) / quarter)
    pos = jnp.arange(seq, dtype=jnp.int32)
    row = (pos // GRID_W).astype(F32)[:, None] * freqs[None, :]
    col = (pos % GRID_W).astype(F32)[:, None] * freqs[None, :]
    cos = jnp.concatenate([jnp.cos(row), jnp.cos(row), jnp.cos(col), jnp.cos(col)], axis=-1)
    sin = jnp.concatenate([-jnp.sin(row), jnp.sin(row), -jnp.sin(col), jnp.sin(col)], axis=-1)
    cos = jnp.concatenate([jnp.ones((ctx_len, HEAD_DIM), F32), cos], axis=0)
    sin = jnp.concatenate([jnp.zeros((ctx_len, HEAD_DIM), F32), sin], axis=0)
    return cos, sin


def kernel(x, c, ctx, c_ctx, w_mod, b_mod, norm_mix, norm_ffn, w_in, qn_a, kn_a, sink_b, w_br_a, w_br_b, w_out, w_rg, b_rg, w_re, b_re, w1, w3, w2, norm_final):
    B, S, D = x.shape
    C = ctx.shape[1]
    L = w_mod.shape[0]
    T = C + S
    M = B * T
    assert C == ROW_TILE and S % ROW_TILE == 0 and T % PROJ_ROWS == 0 and B < MOD_ROWS
    tiles_per_batch = T // ROW_TILE

    cc = jnp.concatenate([c, c_ctx[None, :], jnp.zeros((MOD_ROWS - B - 1, D), F32)], axis=0)
    mod = _mod_vectors(cc, w_mod, b_mod).reshape(L, MOD_ROWS * 6, 1, D)
    cos, sin = _rope_tables(C, S)
    x_all = jnp.concatenate([ctx, x], axis=1).reshape(M, D)
    latent_rows = (jnp.arange(B, dtype=jnp.int32)[:, None] * T + C
                   + jnp.arange(S, dtype=jnp.int32)[None, :]).reshape(B * S)
    all_rows = jnp.arange(M, dtype=jnp.int32)

    q_kinds = lambda kind: tuple((kind, 0, k) for k in range(PROJ_COLS // HEAD_DIM))
    kv_kinds = lambda kind: ((kind, 0, 0), (kind, 0, 1), ("v", 1, 0), ("v", 1, 1))
    col = {"qa": 0, "kva": WIDTH, "qb": WIDTH + 2 * KVW, "kvb": 2 * WIDTH + 2 * KVW, "gates": 2 * WIDTH + 4 * KVW}

    every = _Rows(B, tiles_per_batch, latent_only=False)
    latent = _Rows(B, tiles_per_batch, latent_only=True)
    h = _norm_mod(x_all, norm_mix[0][None, :], mod[0], every)
    out = None
    for l in range(L):
        last = l == L - 1
        rows = latent if last else every
        w_in_l = w_in[l].astype(BF16)
        proj = functools.partial(_proj, h, w_in_l, cos, sin, tiles_per_batch=tiles_per_batch)
        (qa,) = proj(qn_a[l][None, :], col["qa"], q_kinds("q_norm"), [(PROJ_COLS, WIDTH // PROJ_COLS)], name="proj_qa")
        ka, va = proj(kn_a[l][None, :], col["kva"], kv_kinds("k_norm"), [(KVW, 1), (KVW, 1)], name="proj_kva")
        (qb,) = proj(qn_a[l][None, :], col["qb"], q_kinds("q_rope"), [(PROJ_COLS, WIDTH // PROJ_COLS)], name="proj_qb")
        kb, vb = proj(kn_a[l][None, :], col["kvb"], kv_kinds("k_rope"), [(KVW, 1), (KVW, 1)], name="proj_kvb")
        (gates,) = proj(qn_a[l][None, :], col["gates"], q_kinds("gate"), [(PROJ_COLS, 2 * D // PROJ_COLS)], name="proj_gates")

        ya = _attention(qa, ka, va, None, B, T, C, last, window=False)
        yb = _attention(qb, kb, vb, sink_b[l], B, T, C, last, window=True)

        w_router = jnp.concatenate(
            [w_rg[l], w_re[l], jnp.zeros((D, ROUTER_LANES - N_GROUPS - N_EXPERTS), F32)], axis=1)
        b_router = jnp.concatenate(
            [b_rg[l], b_re[l], jnp.zeros((ROUTER_LANES - N_GROUPS - N_EXPERTS,), F32)])[None, :]
        x_new, h2, logits = _merge(ya, yb, gates, w_br_a[l].astype(BF16), w_br_b[l].astype(BF16),
                                   w_out[l].astype(BF16), x_all, mod[l], norm_ffn[l][None, :],
                                   w_router, b_router, rows)

        tok_rows = latent_rows if last else all_rows
        eid, wts = _route(logits[tok_rows] if last else logits)
        dest, slot_row, blk_e, blk_c, n_valid = _dispatch(eid, tok_rows)
        ybuf = _experts(h2[slot_row], w1, w3, w2, l, blk_e, blk_c, n_valid)
        ysel = ybuf[dest].reshape(tok_rows.shape[0], TOP_K * D)
        if last:
            out = _combine_final(x_new, ysel, wts, mod[l], norm_final[None, :], rows).reshape(B, S, D)
        else:
            x_all, h = _combine_next(x_new, ysel, wts, mod[l], norm_mix[l + 1][None, :], mod[l + 1], rows)
    return out
```

```python
import functools

import jax
import jax.numpy as jnp
from jax import lax
from jax.experimental import pallas as pl
from jax.experimental.pallas import tpu as pltpu

F32 = jnp.float32
BF16 = jnp.bfloat16

GRID_W = 64
HEAD_DIM = 128
HEADS = 8
KV_HEADS = 2
GROUP = HEADS // KV_HEADS
WIDTH = HEADS * HEAD_DIM
KVW = KV_HEADS * HEAD_DIM
WINDOW = 128
ROPE_BASE = 10000.0
ATTN_SCALE = HEAD_DIM ** -0.5
LOG2E = 1.4426950408889634
N_GROUPS = 8
EXPERTS_PER_GROUP = 8
N_EXPERTS = N_GROUPS * EXPERTS_PER_GROUP
TOP_K = 2
EPS = 1e-6
NEG_INF = -1e30

ROW_TILE = 256
PROJ_ROWS = 768
PROJ_COLS = 512
MOE_ROWS = 128
ROUTER_LANES = 128
WINDOW_CHAIN_HEADS = 2
MOD_ROWS = 8
VMEM_LIMIT = 56 * 1024 * 1024


def _params(*sem):
    return pltpu.CompilerParams(dimension_semantics=sem, vmem_limit_bytes=VMEM_LIMIT)


def _rms(y):
    return y * lax.rsqrt(jnp.mean(y * y, axis=-1, keepdims=True) + EPS)


def _mod_kernel(c_ref, w_ref, b_ref, o_ref):
    c = c_ref[...]
    a = (c * jax.nn.sigmoid(c)).astype(BF16)
    o_ref[...] = jnp.dot(a, w_ref[...].astype(BF16), preferred_element_type=F32) + b_ref[...]


def _mod_vectors(cc, w_mod, b_mod):
    L, D, N = w_mod.shape
    tn = 1024
    return pl.pallas_call(
        _mod_kernel,
        grid=(L, N // tn),
        in_specs=[
            pl.BlockSpec((MOD_ROWS, D), lambda l, j: (0, 0)),
            pl.BlockSpec((None, D, tn), lambda l, j: (l, 0, j)),
            pl.BlockSpec((None, 1, tn), lambda l, j: (l, 0, j)),
        ],
        out_specs=pl.BlockSpec((None, MOD_ROWS, tn), lambda l, j: (l, 0, j)),
        out_shape=jax.ShapeDtypeStruct((L, MOD_ROWS, N), F32),
        compiler_params=_params("parallel", "parallel"),
        name="mod_vectors",
    )(cc, w_mod, b_mod.reshape(L, 1, N))


class _Rows:
    def __init__(self, batch, tiles_per_batch, latent_only):
        self.batch = batch
        self.tpb = tiles_per_batch
        self.latent_only = latent_only
        self.n = batch * (tiles_per_batch - 1 if latent_only else tiles_per_batch)

    def tile(self, i):
        if self.latent_only:
            per = self.tpb - 1
            return (i // per) * self.tpb + 1 + i % per
        return i

    def mod_row(self, i):
        if self.latent_only:
            return i // (self.tpb - 1)
        return jnp.where(i % self.tpb == 0, self.batch, i // self.tpb)

    def row_spec(self, width):
        return pl.BlockSpec((ROW_TILE, width), lambda i: (self.tile(i), 0))

    def mod_spec(self, d, k):
        return pl.BlockSpec((None, 1, d), lambda i: (self.mod_row(i) * 6 + k, 0, 0))


def _const_spec(shape):
    zeros = (0,) * len(shape)
    return pl.BlockSpec(shape, lambda *_: zeros, pipeline_mode=pl.Buffered(1))


def _norm_mod_kernel(x_ref, nw_ref, sh_ref, sc_ref, h_ref):
    y = _rms(x_ref[...]) * nw_ref[...]
    h_ref[...] = (y * (1.0 + sc_ref[...]) + sh_ref[...]).astype(BF16)


def _norm_mod(x_all, nw, mod_l, rows):
    M, D = x_all.shape
    return pl.pallas_call(
        _norm_mod_kernel,
        grid=(rows.n,),
        in_specs=[rows.row_spec(D), _const_spec((1, D)), rows.mod_spec(D, 0), rows.mod_spec(D, 1)],
        out_specs=rows.row_spec(D),
        out_shape=jax.ShapeDtypeStruct((M, D), BF16),
        compiler_params=_params("parallel"),
        name="norm_mod",
    )(x_all, nw, mod_l, mod_l)


def _rope(y, cos, sin):
    lane = lax.broadcasted_iota(jnp.int32, y.shape, 1)
    quarter = HEAD_DIM // 4
    partner = jnp.where((lane & quarter) == 0,
                        pltpu.roll(y, HEAD_DIM - quarter, 1), pltpu.roll(y, quarter, 1))
    return y * cos + partner * sin


def _proj_kernel(h_ref, w_ref, cos_ref, sin_ref, gain_ref, *out_refs, kinds):
    acc = jnp.dot(h_ref[...], w_ref[...], preferred_element_type=F32)
    for c, (kind, out_idx, out_chunk) in enumerate(kinds):
        y = acc[:, c * HEAD_DIM:(c + 1) * HEAD_DIM]
        if kind in ("q_norm", "k_norm"):
            y = _rms(y) * gain_ref[...]
        if kind in ("q_norm", "k_norm", "q_rope", "k_rope"):
            y = _rope(y, cos_ref[...], sin_ref[...])
        if kind in ("q_norm", "q_rope"):
            y = y * (ATTN_SCALE * LOG2E)
        if kind == "gate":
            y = jax.nn.sigmoid(y)
        out_refs[out_idx][:, out_chunk * HEAD_DIM:(out_chunk + 1) * HEAD_DIM] = y.astype(BF16)


def _proj(h, w_in_l, cos, sin, gain, col0, kinds, out_widths, tiles_per_batch, name):
    M, D = h.shape
    n_col_tiles = out_widths[0][1]
    cb0 = col0 // PROJ_COLS
    proj_tiles_per_batch = tiles_per_batch * ROW_TILE // PROJ_ROWS
    return pl.pallas_call(
        functools.partial(_proj_kernel, kinds=kinds),
        grid=(n_col_tiles, M // PROJ_ROWS),
        in_specs=[
            pl.BlockSpec((PROJ_ROWS, D), lambda j, i: (i, 0)),
            pl.BlockSpec((D, PROJ_COLS), lambda j, i: (0, cb0 + j)),
            pl.BlockSpec((PROJ_ROWS, HEAD_DIM), lambda j, i: (i % proj_tiles_per_batch, 0)),
            pl.BlockSpec((PROJ_ROWS, HEAD_DIM), lambda j, i: (i % proj_tiles_per_batch, 0)),
            pl.BlockSpec((1, HEAD_DIM), lambda j, i: (0, 0)),
        ],
        out_specs=[pl.BlockSpec((PROJ_ROWS, w), lambda j, i: (i, j)) for w, _ in out_widths],
        out_shape=[jax.ShapeDtypeStruct((M, w * n), BF16) for w, n in out_widths],
        compiler_params=_params("parallel", "parallel"),
        name=name,
    )(h, w_in_l, cos, sin, gain)


def _stack_heads(q_ref, g0, n):
    return jnp.concatenate([q_ref[:, g * HEAD_DIM:(g + 1) * HEAD_DIM] for g in range(g0, g0 + n)], axis=0)


def _write_heads(o_ref, g0, n, o):
    rows = o_ref.shape[0]
    for j in range(n):
        o_ref[:, (g0 + j) * HEAD_DIM:(g0 + j + 1) * HEAD_DIM] = o[j * rows:(j + 1) * rows].astype(o_ref.dtype)


def _scores(q, k):
    return lax.dot_general(q, k, (((1,), (1,)), ((), ())), preferred_element_type=F32)


def _online_step(q, k, v, m, l, acc):
    s = _scores(q, k)
    m_new = jnp.maximum(m, jnp.max(s, axis=-1, keepdims=True))
    alpha = jnp.exp2(m - m_new)
    p = jnp.exp2(s - m_new)
    l = alpha * l + jnp.sum(p, axis=-1, keepdims=True)
    acc = alpha * acc + jnp.dot(p.astype(BF16), v, preferred_element_type=F32)
    return m_new, l, acc


def _dense_attn_kernel(q_ref, k_ref, v_ref, o_ref, *, ctx_len, total_len, key_chunk, ctx_tile):
    def run(n_keys, chunk):
        q = _stack_heads(q_ref, 0, GROUP)
        r = q.shape[0]
        m = jnp.full((r, 1), -jnp.inf, F32)
        l = jnp.zeros((r, 1), F32)
        acc = jnp.zeros((r, HEAD_DIM), F32)
        for c in range(n_keys // chunk):
            m, l, acc = _online_step(q, k_ref[c * chunk:(c + 1) * chunk, :],
                                     v_ref[c * chunk:(c + 1) * chunk, :], m, l, acc)
        _write_heads(o_ref, 0, GROUP, acc * (1.0 / l))

    if ctx_tile:
        is_ctx = pl.program_id(2) == 0
        pl.when(is_ctx)(lambda: run(ctx_len, ctx_len))
        pl.when(jnp.logical_not(is_ctx))(lambda: run(total_len, key_chunk))
    else:
        run(total_len, key_chunk)


def _window_attn_kernel(sink_ref, q_ref, k_ref, v_ref, o_ref, *, ctx_len, total_len, ctx_tile, q_off):
    kvh = pl.program_id(1)
    tq = q_ref.shape[0]
    span = tq + 2 * WINDOW
    n = WINDOW_CHAIN_HEADS

    def attend(parts):
        for g0 in range(0, GROUP, n):
            q = _stack_heads(q_ref, g0, n)
            sink = jnp.concatenate(
                [jnp.full((tq, 1), sink_ref[kvh * GROUP + g] * LOG2E, F32) for g in range(g0, g0 + n)], axis=0)
            scores = []
            m = sink
            for k, _, bias in parts:
                s = _scores(q, k)
                if bias is not None:
                    s = s + bias
                scores.append(s)
                m = jnp.maximum(m, jnp.max(s, axis=-1, keepdims=True))
            l = jnp.exp2(sink - m)
            o = jnp.zeros((n * tq, HEAD_DIM), F32)
            for s, (_, v, _) in zip(scores, parts):
                p = jnp.exp2(s - m)
                l = l + jnp.sum(p, axis=-1, keepdims=True)
                o = o + jnp.dot(p.astype(BF16), v, preferred_element_type=F32)
            _write_heads(o_ref, g0, n, o * (1.0 / l))

    def ctx_part():
        return k_ref[0:ctx_len, :], v_ref[0:ctx_len, :], None

    def run_latent():
        q0 = (pl.program_id(2) + q_off) * tq - ctx_len
        n_latent = total_len - ctx_len
        start = pl.multiple_of(jnp.clip(q0 - WINDOW, 0, n_latent - span), WINDOW)
        row = lax.broadcasted_iota(jnp.int32, (n * tq, span), 0) & (tq - 1)
        col = lax.broadcasted_iota(jnp.int32, (n * tq, span), 1)
        bias = jnp.where(jnp.abs(col - row + (start - q0)) <= WINDOW, 0.0, NEG_INF).astype(F32)
        attend([ctx_part(), (k_ref[pl.ds(ctx_len + start, span), :], v_ref[pl.ds(ctx_len + start, span), :], bias)])

    if ctx_tile:
        is_ctx = pl.program_id(2) == 0
        pl.when(is_ctx)(lambda: attend([ctx_part()]))
        pl.when(jnp.logical_not(is_ctx))(run_latent)
    else:
        run_latent()


def _attention(q, k, v, sink, batch, total_len, ctx_len, latent_only, window):
    M = q.shape[0]
    q3, k3, v3 = (a.reshape(batch, total_len, a.shape[1]) for a in (q, k, v))
    tiles = total_len // ROW_TILE
    q_off = 1 if latent_only else 0
    gw = GROUP * HEAD_DIM
    q_spec = pl.BlockSpec((None, ROW_TILE, gw), lambda b, h, i, *_: (b, i + q_off, h))
    kv_spec = pl.BlockSpec((None, total_len, HEAD_DIM), lambda b, h, i, *_: (b, 0, h))
    grid = (batch, KV_HEADS, tiles - q_off)
    out_shape = jax.ShapeDtypeStruct(q3.shape, BF16)
    if window:
        kern = functools.partial(_window_attn_kernel, ctx_len=ctx_len, total_len=total_len,
                                 ctx_tile=not latent_only, q_off=q_off)
        out = pl.pallas_call(
            kern,
            grid_spec=pltpu.PrefetchScalarGridSpec(
                num_scalar_prefetch=1, grid=grid,
                in_specs=[q_spec, kv_spec, kv_spec], out_specs=q_spec),
            out_shape=out_shape,
            compiler_params=_params("parallel", "parallel", "parallel"),
            name="window_attention",
        )(sink, q3, k3, v3)
    else:
        kern = functools.partial(_dense_attn_kernel, ctx_len=ctx_len, total_len=total_len,
                                 key_chunk=total_len // 3, ctx_tile=not latent_only)
        out = pl.pallas_call(
            kern, grid=grid, in_specs=[q_spec, kv_spec, kv_spec], out_specs=q_spec,
            out_shape=out_shape,
            compiler_params=_params("parallel", "parallel", "parallel"),
            name="dense_attention",
        )(q3, k3, v3)
    return out.reshape(M, q.shape[1])


def _merge_kernel(ya_ref, yb_ref, ga_ref, gb_ref, wa_ref, wb_ref, wo_ref, x_ref, g1_ref,
                  nw_ref, sh_ref, sc_ref, wr_ref, br_ref, xo_ref, h_ref, lg_ref):
    a = jnp.dot(ya_ref[...], wa_ref[...], preferred_element_type=F32)
    b = jnp.dot(yb_ref[...], wb_ref[...], preferred_element_type=F32)
    m = ga_ref[...].astype(F32) * a + gb_ref[...].astype(F32) * b
    o = jnp.dot(m.astype(BF16), wo_ref[...], preferred_element_type=F32)
    xn = x_ref[...] + g1_ref[...] * o
    xo_ref[...] = xn
    h = (_rms(xn) * nw_ref[...]) * (1.0 + sc_ref[...]) + sh_ref[...]
    h_ref[...] = h
    h_hi = h.astype(BF16)
    h_lo = (h - h_hi.astype(F32)).astype(BF16)
    lg = jnp.dot(h_hi, wr_ref[...], preferred_element_type=F32)
    lg_lo = jnp.dot(h_lo, wr_ref[:, :ROUTER_LANES], preferred_element_type=F32)
    lg_ref[...] = lg[:, :ROUTER_LANES] + (lg[:, ROUTER_LANES:] + lg_lo) + br_ref[...]


def _merge(ya, yb, gates, wa, wb, wo, x_all, mod_l, nw, wr, br, rows):
    M, D = x_all.shape
    W = ya.shape[1]
    return pl.pallas_call(
        _merge_kernel,
        grid=(rows.n,),
        in_specs=[
            rows.row_spec(W), rows.row_spec(W),
            pl.BlockSpec((ROW_TILE, D), lambda i: (rows.tile(i), 0)),
            pl.BlockSpec((ROW_TILE, D), lambda i: (rows.tile(i), 1)),
            _const_spec((W, D)), _const_spec((W, D)), _const_spec((D, D)),
            rows.row_spec(D), rows.mod_spec(D, 2),
            _const_spec((1, D)), rows.mod_spec(D, 3), rows.mod_spec(D, 4),
            _const_spec((D, 2 * ROUTER_LANES)), _const_spec((1, ROUTER_LANES)),
        ],
        out_specs=[rows.row_spec(D), rows.row_spec(D), rows.row_spec(ROUTER_LANES)],
        out_shape=[jax.ShapeDtypeStruct((M, D), F32), jax.ShapeDtypeStruct((M, D), F32),
                   jax.ShapeDtypeStruct((M, ROUTER_LANES), F32)],
        compiler_params=_params("parallel"),
        name="merge",
    )(ya, yb, gates, gates, wa, wb, wo, x_all, mod_l, nw, mod_l, mod_l, wr, br)


def _route(logits):
    n = logits.shape[0]
    gl = logits[:, :N_GROUPS]
    el = logits[:, N_GROUPS:N_GROUPS + N_EXPERTS].reshape(n, N_GROUPS, EXPERTS_PER_GROUP)
    g = jnp.argmax(gl, axis=-1).astype(jnp.int32)
    pg = 1.0 / jnp.sum(jnp.exp(gl - jnp.max(gl, axis=-1, keepdims=True)), axis=-1, keepdims=True)
    in_group = jnp.arange(N_GROUPS, dtype=jnp.int32)[None, :, None] == g[:, None, None]
    el_g = jnp.sum(jnp.where(in_group, el, 0.0), axis=1)
    tv, ti = lax.top_k(el_g, TOP_K)
    wts = jax.nn.softmax(tv, axis=-1) * pg
    eid = g[:, None] * EXPERTS_PER_GROUP + ti.astype(jnp.int32)
    return eid, wts


def _dispatch(eid, tok_rows):
    n = eid.shape[0]
    a = n * TOP_K
    e_flat = eid.reshape(a)
    onehot = (e_flat[:, None] == jnp.arange(N_EXPERTS, dtype=jnp.int32)[None, :]).astype(jnp.int32)
    csum = jnp.cumsum(onehot, axis=0)
    counts = csum[-1]
    padded = (counts + MOE_ROWS - 1) // MOE_ROWS * MOE_ROWS
    pad_end = jnp.cumsum(padded)
    pad_start = pad_end - padded
    dest = jnp.sum(onehot * (csum - 1 + pad_start[None, :]), axis=1)
    n_blk = a // MOE_ROWS + N_EXPERTS
    slot_row = jnp.zeros((n_blk * MOE_ROWS,), jnp.int32).at[dest].set(jnp.repeat(tok_rows, TOP_K))
    blk = jnp.arange(n_blk, dtype=jnp.int32)
    n_valid = (pad_end[-1] // MOE_ROWS).astype(jnp.int32)
    blk_c = jnp.minimum(blk, n_valid - 1)
    blk_e = jnp.clip(jnp.searchsorted(pad_end, blk_c * MOE_ROWS, side="right"), 0, N_EXPERTS - 1).astype(jnp.int32)
    blk_rows = jnp.where(blk < n_valid, jnp.clip(counts[blk_e] - (blk * MOE_ROWS - pad_start[blk_e]), 0, MOE_ROWS), 0)
    experts = jnp.arange(N_EXPERTS, dtype=jnp.int32)
    used = counts > 0
    first_used_from = lax.cummin(jnp.where(used, experts, N_EXPERTS), reverse=True)
    next_used = jnp.concatenate([first_used_from[1:], jnp.full((1,), N_EXPERTS, jnp.int32)])
    next_used = jnp.where(next_used == N_EXPERTS, -1, next_used)
    parity = (jnp.cumsum(used.astype(jnp.int32)) - 1) % 2
    return (dest, blk_e, blk_rows.astype(jnp.int32), next_used[blk_e].astype(jnp.int32),
            parity[blk_e].astype(jnp.int32), slot_row, n_valid.reshape(1))


def _experts_kernel(blk_e, blk_rows, blk_next, blk_par, slot_row, n_valid,
                    h_hbm, w1_hbm, w3_hbm, w2_hbm, o_ref,
                    xg, xsem, ws1, ws3, ws2, wsem, w1b, w3b, w2b, *, layer):
    i = pl.program_id(0)
    nv = n_valid[0]
    e = blk_e[i]
    par = blk_par[i]
    fresh = jnp.logical_or(i == 0, e != blk_e[jnp.maximum(i - 1, 0)])

    def weight_copies(expert, p):
        return (pltpu.make_async_copy(w1_hbm.at[layer, expert], ws1.at[p], wsem.at[p, 0]),
                pltpu.make_async_copy(w3_hbm.at[layer, expert], ws3.at[p], wsem.at[p, 1]),
                pltpu.make_async_copy(w2_hbm.at[layer, expert], ws2.at[p], wsem.at[p, 2]))

    def row_copy(blk, j):
        r = slot_row[blk * MOE_ROWS + j]
        return pltpu.make_async_copy(h_hbm.at[pl.ds(r, 1), :], xg.at[blk % 2, pl.ds(j, 1), :], xsem.at[blk % 2])

    def for_rows(blk, fn):
        def body(j, carry):
            fn(row_copy(blk, j))
            return carry
        lax.fori_loop(0, blk_rows[blk], body, 0)

    @pl.when(i == 0)
    def _():
        for cp in weight_copies(e, par):
            cp.start()
        xg[...] = jnp.zeros(xg.shape, xg.dtype)
        for_rows(0, lambda cp: cp.start())

    @pl.when(i + 1 < nv)
    def _():
        for_rows(i + 1, lambda cp: cp.start())

    @pl.when(fresh)
    def _():
        for cp in weight_copies(e, par):
            cp.wait()
        nxt = blk_next[i]

        @pl.when(nxt >= 0)
        def _():
            for cp in weight_copies(nxt, 1 - par):
                cp.start()

        w1b[...] = ws1[par].astype(BF16)
        w3b[...] = ws3[par].astype(BF16)
        w2b[...] = ws2[par].astype(BF16)

    @pl.when(i < nv)
    def _():
        for_rows(i, lambda cp: cp.wait())
        x = xg[i % 2].astype(BF16)
        a = jnp.dot(x, w1b[...], preferred_element_type=F32)
        b = jnp.dot(x, w3b[...], preferred_element_type=F32)
        hid = (a * jax.nn.sigmoid(a)) * b
        o_ref[...] = jnp.dot(hid.astype(BF16), w2b[...], preferred_element_type=F32).astype(o_ref.dtype)


def _experts(h2, w1, w3, w2, layer, blk_e, blk_rows, blk_next, blk_par, slot_row, n_valid):
    D = h2.shape[1]
    de = w1.shape[-1]
    n_blk = blk_e.shape[0]
    any_spec = pl.BlockSpec(memory_space=pl.ANY)
    return pl.pallas_call(
        functools.partial(_experts_kernel, layer=layer),
        grid_spec=pltpu.PrefetchScalarGridSpec(
            num_scalar_prefetch=6, grid=(n_blk,),
            in_specs=[any_spec, any_spec, any_spec, any_spec],
            out_specs=pl.BlockSpec((MOE_ROWS, D), lambda i, *s: (jnp.minimum(i, s[5][0] - 1), 0)),
            scratch_shapes=[
                pltpu.VMEM((2, MOE_ROWS, D), F32), pltpu.SemaphoreType.DMA((2,)),
                pltpu.VMEM((2, D, de), F32), pltpu.VMEM((2, D, de), F32), pltpu.VMEM((2, de, D), F32),
                pltpu.SemaphoreType.DMA((2, 3)),
                pltpu.VMEM((D, de), BF16), pltpu.VMEM((D, de), BF16), pltpu.VMEM((de, D), BF16),
            ],
        ),
        out_shape=jax.ShapeDtypeStruct((n_blk * MOE_ROWS, D), BF16),
        compiler_params=_params("arbitrary"),
        name="experts",
    )(blk_e, blk_rows, blk_next, blk_par, slot_row, n_valid, h2, w1, w3, w2)


def _moe_mix(x_ref, y0_ref, y1_ref, w_ref, g2_ref):
    w = w_ref[...]
    y = w[:, 0:1] * y0_ref[...].astype(F32) + w[:, 1:2] * y1_ref[...].astype(F32)
    return x_ref[...] + g2_ref[...] * y


def _combine_next_kernel(x_ref, y0_ref, y1_ref, w_ref, g2_ref, nw_ref, sh_ref, sc_ref, xo_ref, h_ref):
    xn = _moe_mix(x_ref, y0_ref, y1_ref, w_ref, g2_ref)
    xo_ref[...] = xn
    h_ref[...] = ((_rms(xn) * nw_ref[...]) * (1.0 + sc_ref[...]) + sh_ref[...]).astype(BF16)


def _combine_final_kernel(x_ref, y0_ref, y1_ref, w_ref, g2_ref, nf_ref, o_ref):
    o_ref[...] = _rms(_moe_mix(x_ref, y0_ref, y1_ref, w_ref, g2_ref)) * nf_ref[...]


def _token_specs(n_tokens, d):
    tiles = n_tokens // ROW_TILE
    return [pl.BlockSpec((ROW_TILE, d), lambda i: (i, 0)), pl.BlockSpec((ROW_TILE, d), lambda i: (i + tiles, 0)),
            pl.BlockSpec((ROW_TILE, TOP_K), lambda i: (i, 0))]


def _combine_next(x_new, ysel, wts, mod_l, nw_next, mod_next, rows):
    M, D = x_new.shape
    return pl.pallas_call(
        _combine_next_kernel,
        grid=(rows.n,),
        in_specs=[rows.row_spec(D), *_token_specs(M, D), rows.mod_spec(D, 5),
                  _const_spec((1, D)), rows.mod_spec(D, 0), rows.mod_spec(D, 1)],
        out_specs=[rows.row_spec(D), rows.row_spec(D)],
        out_shape=[jax.ShapeDtypeStruct((M, D), F32), jax.ShapeDtypeStruct((M, D), BF16)],
        compiler_params=_params("parallel"),
        name="combine_next",
    )(x_new, ysel, ysel, wts, mod_l, nw_next, mod_next, mod_next)


def _combine_final(x_new, ysel, wts, mod_l, nf, rows):
    M, D = x_new.shape
    n = wts.shape[0]
    return pl.pallas_call(
        _combine_final_kernel,
        grid=(rows.n,),
        in_specs=[rows.row_spec(D), *_token_specs(n, D), rows.mod_spec(D, 5), _const_spec((1, D))],
        out_specs=pl.BlockSpec((ROW_TILE, D), lambda i: (i, 0)),
        out_shape=jax.ShapeDtypeStruct((n, D), F32),
        compiler_params=_params("parallel"),
        name="combine_final",
    )(x_new, ysel, ysel, wts, mod_l, nf)


def _rope_tables(ctx_len, seq):
    quarter = HEAD_DIM // 4
    freqs = ROPE_BASE ** (-jnp.arange(quarter, dtype=F32) / quarter)
    pos = jnp.arange(seq, dtype=jnp.int32)
    row = (pos // GRID_W).astype(F32)[:, None] * freqs[None, :]
    col = (pos % GRID_W).astype(F32)[:, None] * freqs[None, :]
    cos = jnp.concatenate([jnp.cos(row), jnp.cos(row), jnp.cos(col), jnp.cos(col)], axis=-1)
    sin = jnp.concatenate([-jnp.sin(row), jnp.sin(row), -jnp.sin(col), jnp.sin(col)], axis=-1)
    cos = jnp.concatenate([jnp.ones((ctx_len, HEAD_DIM), F32), cos], axis=0)
    sin = jnp.concatenate([jnp.zeros((ctx_len, HEAD_DIM), F32), sin], axis=0)
    return cos, sin


def kernel(x, c, ctx, c_ctx, w_mod, b_mod, norm_mix, norm_ffn, w_in, qn_a, kn_a, sink_b, w_br_a, w_br_b, w_out, w_rg, b_rg, w_re, b_re, w1, w3, w2, norm_final):
    B, S, D = x.shape
    C = ctx.shape[1]
    L = w_mod.shape[0]
    T = C + S
    M = B * T
    assert C == ROW_TILE and S % ROW_TILE == 0 and T % PROJ_ROWS == 0 and B < MOD_ROWS
    tiles_per_batch = T // ROW_TILE

    cc = jnp.concatenate([c, c_ctx[None, :], jnp.zeros((MOD_ROWS - B - 1, D), F32)], axis=0)
    mod = _mod_vectors(cc, w_mod, b_mod).reshape(L, MOD_ROWS * 6, 1, D)
    cos, sin = _rope_tables(C, S)
    x_all = jnp.concatenate([ctx, x], axis=1).reshape(M, D)
    latent_rows = (jnp.arange(B, dtype=jnp.int32)[:, None] * T + C
                   + jnp.arange(S, dtype=jnp.int32)[None, :]).reshape(B * S)
    all_rows = jnp.arange(M, dtype=jnp.int32)

    q_kinds = lambda kind: tuple((kind, 0, k) for k in range(PROJ_COLS // HEAD_DIM))
    kv_kinds = lambda kind: ((kind, 0, 0), (kind, 0, 1), ("v", 1, 0), ("v", 1, 1))
    col = {"qa": 0, "kva": WIDTH, "qb": WIDTH + 2 * KVW, "kvb": 2 * WIDTH + 2 * KVW, "gates": 2 * WIDTH + 4 * KVW}

    every = _Rows(B, tiles_per_batch, latent_only=False)
    latent = _Rows(B, tiles_per_batch, latent_only=True)
    h = _norm_mod(x_all, norm_mix[0][None, :], mod[0], every)
    out = None
    for l in range(L):
        last = l == L - 1
        rows = latent if last else every
        w_in_l = w_in[l].astype(BF16)
        proj = functools.partial(_proj, h, w_in_l, cos, sin, tiles_per_batch=tiles_per_batch)
        (qa,) = proj(qn_a[l][None, :], col["qa"], q_kinds("q_norm"), [(PROJ_COLS, WIDTH // PROJ_COLS)], name="proj_qa")
        ka, va = proj(kn_a[l][None, :], col["kva"], kv_kinds("k_norm"), [(KVW, 1), (KVW, 1)], name="proj_kva")
        (qb,) = proj(qn_a[l][None, :], col["qb"], q_kinds("q_rope"), [(PROJ_COLS, WIDTH // PROJ_COLS)], name="proj_qb")
        kb, vb = proj(kn_a[l][None, :], col["kvb"], kv_kinds("k_rope"), [(KVW, 1), (KVW, 1)], name="proj_kvb")
        (gates,) = proj(qn_a[l][None, :], col["gates"], q_kinds("gate"), [(PROJ_COLS, 2 * D // PROJ_COLS)], name="proj_gates")

        ya = _attention(qa, ka, va, None, B, T, C, last, window=False)
        yb = _attention(qb, kb, vb, sink_b[l], B, T, C, last, window=True)

        w_router = jnp.concatenate(
            [w_rg[l], w_re[l], jnp.zeros((D, ROUTER_LANES - N_GROUPS - N_EXPERTS), F32)], axis=1)
        w_router_hi = w_router.astype(BF16)
        w_router = jnp.concatenate([w_router_hi, (w_router - w_router_hi.astype(F32)).astype(BF16)], axis=1)
        b_router = jnp.concatenate(
            [b_rg[l], b_re[l], jnp.zeros((ROUTER_LANES - N_GROUPS - N_EXPERTS,), F32)])[None, :]
        x_new, h2, logits = _merge(ya, yb, gates, w_br_a[l].astype(BF16), w_br_b[l].astype(BF16),
                                   w_out[l].astype(BF16), x_all, mod[l], norm_ffn[l][None, :],
                                   w_router, b_router, rows)

        tok_rows = latent_rows if last else all_rows
        eid, wts = _route(logits[tok_rows] if last else logits)
        dest, *plan = _dispatch(eid, tok_rows)
        ybuf = _experts(h2, w1, w3, w2, l, *plan)
        ysel = ybuf[dest.reshape(-1, TOP_K).T.reshape(-1)]
        if last:
            out = _combine_final(x_new, ysel, wts, mod[l], norm_final[None, :], rows).reshape(B, S, D)
        else:
            x_all, h = _combine_next(x_new, ysel, wts, mod[l], norm_mix[l + 1][None, :], mod[l + 1], rows)
    return out
```

```python
import functools

import jax
import jax.numpy as jnp
from jax import lax
from jax.experimental import pallas as pl
from jax.experimental.pallas import tpu as pltpu

F32 = jnp.float32
BF16 = jnp.bfloat16

GRID_W = 64
HEAD_DIM = 128
HEADS = 8
KV_HEADS = 2
GROUP = HEADS // KV_HEADS
WIDTH = HEADS * HEAD_DIM
KVW = KV_HEADS * HEAD_DIM
WINDOW = 128
ROPE_BASE = 10000.0
ATTN_SCALE = HEAD_DIM ** -0.5
LOG2E = 1.4426950408889634
N_GROUPS = 8
EXPERTS_PER_GROUP = 8
N_EXPERTS = N_GROUPS * EXPERTS_PER_GROUP
TOP_K = 2
EPS = 1e-6
NEG_INF = -1e30

ROW_TILE = 256
PROJ_ROWS = 768
PROJ_COLS = 512
MOE_ROWS = 128
ROUTER_LANES = 128
WINDOW_CHAIN_HEADS = 2
MOD_ROWS = 8
VMEM_LIMIT = 56 * 1024 * 1024


def _params(*sem):
    return pltpu.CompilerParams(dimension_semantics=sem, vmem_limit_bytes=VMEM_LIMIT)


def _rms(y):
    return y * lax.rsqrt(jnp.mean(y * y, axis=-1, keepdims=True) + EPS)


def _mod_kernel(c_ref, w_ref, b_ref, o_ref):
    c = c_ref[...]
    a = (c * jax.nn.sigmoid(c)).astype(BF16)
    o_ref[...] = jnp.dot(a, w_ref[...].astype(BF16), preferred_element_type=F32) + b_ref[...]


def _mod_vectors(cc, w_mod, b_mod):
    L, D, N = w_mod.shape
    tn = 1024
    return pl.pallas_call(
        _mod_kernel,
        grid=(L, N // tn),
        in_specs=[
            pl.BlockSpec((MOD_ROWS, D), lambda l, j: (0, 0)),
            pl.BlockSpec((None, D, tn), lambda l, j: (l, 0, j)),
            pl.BlockSpec((None, 1, tn), lambda l, j: (l, 0, j)),
        ],
        out_specs=pl.BlockSpec((None, MOD_ROWS, tn), lambda l, j: (l, 0, j)),
        out_shape=jax.ShapeDtypeStruct((L, MOD_ROWS, N), F32),
        compiler_params=_params("parallel", "parallel"),
        name="mod_vectors",
    )(cc, w_mod, b_mod.reshape(L, 1, N))


class _Rows:
    def __init__(self, batch, tiles_per_batch, latent_only):
        self.batch = batch
        self.tpb = tiles_per_batch
        self.latent_only = latent_only
        self.n = batch * (tiles_per_batch - 1 if latent_only else tiles_per_batch)

    def tile(self, i):
        if self.latent_only:
            per = self.tpb - 1
            return (i // per) * self.tpb + 1 + i % per
        return i

    def mod_row(self, i):
        if self.latent_only:
            return i // (self.tpb - 1)
        return jnp.where(i % self.tpb == 0, self.batch, i // self.tpb)

    def row_spec(self, width):
        return pl.BlockSpec((ROW_TILE, width), lambda i: (self.tile(i), 0))

    def mod_spec(self, d, k):
        return pl.BlockSpec((None, 1, d), lambda i: (self.mod_row(i) * 6 + k, 0, 0))


def _const_spec(shape):
    zeros = (0,) * len(shape)
    return pl.BlockSpec(shape, lambda *_: zeros, pipeline_mode=pl.Buffered(1))


def _norm_mod_kernel(x_ref, nw_ref, sh_ref, sc_ref, h_ref):
    y = _rms(x_ref[...]) * nw_ref[...]
    h_ref[...] = (y * (1.0 + sc_ref[...]) + sh_ref[...]).astype(BF16)


def _norm_mod(x_all, nw, mod_l, rows):
    M, D = x_all.shape
    return pl.pallas_call(
        _norm_mod_kernel,
        grid=(rows.n,),
        in_specs=[rows.row_spec(D), _const_spec((1, D)), rows.mod_spec(D, 0), rows.mod_spec(D, 1)],
        out_specs=rows.row_spec(D),
        out_shape=jax.ShapeDtypeStruct((M, D), BF16),
        compiler_params=_params("parallel"),
        name="norm_mod",
    )(x_all, nw, mod_l, mod_l)


def _rope(y, cos, sin):
    lane = lax.broadcasted_iota(jnp.int32, y.shape, 1)
    quarter = HEAD_DIM // 4
    partner = jnp.where((lane & quarter) == 0,
                        pltpu.roll(y, HEAD_DIM - quarter, 1), pltpu.roll(y, quarter, 1))
    return y * cos + partner * sin


def _proj_kernel(h_ref, w_ref, cos_ref, sin_ref, gain_ref, *refs, kinds):
    *out_refs, wb_ref = refs

    @pl.when(pl.program_id(1) == 0)
    def _():
        wb_ref[...] = w_ref[...].astype(BF16)

    acc = jnp.dot(h_ref[...], wb_ref[...], preferred_element_type=F32)
    for c, (kind, out_idx, out_chunk) in enumerate(kinds):
        y = acc[:, c * HEAD_DIM:(c + 1) * HEAD_DIM]
        if kind in ("q_norm", "k_norm"):
            y = _rms(y) * gain_ref[...]
        if kind in ("q_norm", "k_norm", "q_rope", "k_rope"):
            y = _rope(y, cos_ref[...], sin_ref[...])
        if kind in ("q_norm", "q_rope"):
            y = y * (ATTN_SCALE * LOG2E)
        if kind == "gate":
            y = jax.nn.sigmoid(y)
        out_refs[out_idx][:, out_chunk * HEAD_DIM:(out_chunk + 1) * HEAD_DIM] = y.astype(BF16)


def _proj(h, w_in, layer, cos, sin, gain, col0, kinds, out_widths, tiles_per_batch, name):
    M, D = h.shape
    n_col_tiles = out_widths[0][1]
    cb0 = col0 // PROJ_COLS
    proj_tiles_per_batch = tiles_per_batch * ROW_TILE // PROJ_ROWS
    return pl.pallas_call(
        functools.partial(_proj_kernel, kinds=kinds),
        grid=(n_col_tiles, M // PROJ_ROWS),
        in_specs=[
            pl.BlockSpec((PROJ_ROWS, D), lambda j, i: (i, 0)),
            pl.BlockSpec((None, D, PROJ_COLS), lambda j, i: (layer, 0, cb0 + j)),
            pl.BlockSpec((PROJ_ROWS, HEAD_DIM), lambda j, i: (i % proj_tiles_per_batch, 0)),
            pl.BlockSpec((PROJ_ROWS, HEAD_DIM), lambda j, i: (i % proj_tiles_per_batch, 0)),
            pl.BlockSpec((1, HEAD_DIM), lambda j, i: (0, 0)),
        ],
        out_specs=[pl.BlockSpec((PROJ_ROWS, w), lambda j, i: (i, j)) for w, _ in out_widths],
        out_shape=[jax.ShapeDtypeStruct((M, w * n), BF16) for w, n in out_widths],
        scratch_shapes=[pltpu.VMEM((D, PROJ_COLS), BF16)],
        compiler_params=_params("parallel", "arbitrary"),
        name=name,
    )(h, w_in, cos, sin, gain)


def _stack_heads(q_ref, g0, n):
    return jnp.concatenate([q_ref[:, g * HEAD_DIM:(g + 1) * HEAD_DIM] for g in range(g0, g0 + n)], axis=0)


def _write_heads(o_ref, g0, n, o):
    rows = o_ref.shape[0]
    for j in range(n):
        o_ref[:, (g0 + j) * HEAD_DIM:(g0 + j + 1) * HEAD_DIM] = o[j * rows:(j + 1) * rows].astype(o_ref.dtype)


def _scores(q, k):
    return lax.dot_general(q, k, (((1,), (1,)), ((), ())), preferred_element_type=F32)


def _online_step(q, k, v, m, l, acc):
    s = _scores(q, k)
    m_new = jnp.maximum(m, jnp.max(s, axis=-1, keepdims=True))
    alpha = jnp.exp2(m - m_new)
    p = jnp.exp2(s - m_new)
    l = alpha * l + jnp.sum(p, axis=-1, keepdims=True)
    acc = alpha * acc + jnp.dot(p.astype(BF16), v, preferred_element_type=F32)
    return m_new, l, acc


def _dense_attn_kernel(q_ref, k_ref, v_ref, o_ref, *, ctx_len, total_len, key_chunk, ctx_tile):
    def run(n_keys, chunk):
        q = _stack_heads(q_ref, 0, GROUP)
        r = q.shape[0]
        m = jnp.full((r, 1), -jnp.inf, F32)
        l = jnp.zeros((r, 1), F32)
        acc = jnp.zeros((r, HEAD_DIM), F32)
        for c in range(n_keys // chunk):
            m, l, acc = _online_step(q, k_ref[c * chunk:(c + 1) * chunk, :],
                                     v_ref[c * chunk:(c + 1) * chunk, :], m, l, acc)
        _write_heads(o_ref, 0, GROUP, acc * (1.0 / l))

    if ctx_tile:
        is_ctx = pl.program_id(2) == 0
        pl.when(is_ctx)(lambda: run(ctx_len, ctx_len))
        pl.when(jnp.logical_not(is_ctx))(lambda: run(total_len, key_chunk))
    else:
        run(total_len, key_chunk)


def _window_attn_kernel(sink_ref, q_ref, k_ref, v_ref, o_ref, *, ctx_len, total_len, ctx_tile, q_off):
    kvh = pl.program_id(1)
    tq = q_ref.shape[0]
    span = tq + 2 * WINDOW
    n = WINDOW_CHAIN_HEADS

    def attend(parts):
        for g0 in range(0, GROUP, n):
            q = _stack_heads(q_ref, g0, n)
            sink = jnp.concatenate(
                [jnp.full((tq, 1), sink_ref[kvh * GROUP + g] * LOG2E, F32) for g in range(g0, g0 + n)], axis=0)
            scores = []
            m = sink
            for k, _, bias in parts:
                s = _scores(q, k)
                if bias is not None:
                    s = s + bias
                scores.append(s)
                m = jnp.maximum(m, jnp.max(s, axis=-1, keepdims=True))
            l = jnp.exp2(sink - m)
            o = jnp.zeros((n * tq, HEAD_DIM), F32)
            for s, (_, v, _) in zip(scores, parts):
                p = jnp.exp2(s - m)
                l = l + jnp.sum(p, axis=-1, keepdims=True)
                o = o + jnp.dot(p.astype(BF16), v, preferred_element_type=F32)
            _write_heads(o_ref, g0, n, o * (1.0 / l))

    def ctx_part():
        return k_ref[0:ctx_len, :], v_ref[0:ctx_len, :], None

    def run_latent():
        q0 = (pl.program_id(2) + q_off) * tq - ctx_len
        n_latent = total_len - ctx_len
        start = pl.multiple_of(jnp.clip(q0 - WINDOW, 0, n_latent - span), WINDOW)
        row = lax.broadcasted_iota(jnp.int32, (n * tq, span), 0) & (tq - 1)
        col = lax.broadcasted_iota(jnp.int32, (n * tq, span), 1)
        bias = jnp.where(jnp.abs(col - row + (start - q0)) <= WINDOW, 0.0, NEG_INF).astype(F32)
        attend([ctx_part(), (k_ref[pl.ds(ctx_len + start, span), :], v_ref[pl.ds(ctx_len + start, span), :], bias)])

    if ctx_tile:
        is_ctx = pl.program_id(2) == 0
        pl.when(is_ctx)(lambda: attend([ctx_part()]))
        pl.when(jnp.logical_not(is_ctx))(run_latent)
    else:
        run_latent()


def _attention(q, k, v, sink, batch, total_len, ctx_len, latent_only, window):
    M = q.shape[0]
    q3, k3, v3 = (a.reshape(batch, total_len, a.shape[1]) for a in (q, k, v))
    tiles = total_len // ROW_TILE
    q_off = 1 if latent_only else 0
    gw = GROUP * HEAD_DIM
    q_spec = pl.BlockSpec((None, ROW_TILE, gw), lambda b, h, i, *_: (b, i + q_off, h))
    kv_spec = pl.BlockSpec((None, total_len, HEAD_DIM), lambda b, h, i, *_: (b, 0, h))
    grid = (batch, KV_HEADS, tiles - q_off)
    out_shape = jax.ShapeDtypeStruct(q3.shape, BF16)
    if window:
        kern = functools.partial(_window_attn_kernel, ctx_len=ctx_len, total_len=total_len,
                                 ctx_tile=not latent_only, q_off=q_off)
        out = pl.pallas_call(
            kern,
            grid_spec=pltpu.PrefetchScalarGridSpec(
                num_scalar_prefetch=1, grid=grid,
                in_specs=[q_spec, kv_spec, kv_spec], out_specs=q_spec),
            out_shape=out_shape,
            compiler_params=_params("parallel", "parallel", "parallel"),
            name="window_attention",
        )(sink, q3, k3, v3)
    else:
        kern = functools.partial(_dense_attn_kernel, ctx_len=ctx_len, total_len=total_len,
                                 key_chunk=total_len // 3, ctx_tile=not latent_only)
        out = pl.pallas_call(
            kern, grid=grid, in_specs=[q_spec, kv_spec, kv_spec], out_specs=q_spec,
            out_shape=out_shape,
            compiler_params=_params("parallel", "parallel", "parallel"),
            name="dense_attention",
        )(q3, k3, v3)
    return out.reshape(M, q.shape[1])


def _merge_kernel(ya_ref, yb_ref, ga_ref, gb_ref, wa_ref, wb_ref, wo_ref, x_ref, g1_ref,
                  nw_ref, sh_ref, sc_ref, wr_ref, br_ref, xo_ref, h_ref, lg_ref):
    a = jnp.dot(ya_ref[...], wa_ref[...], preferred_element_type=F32)
    b = jnp.dot(yb_ref[...], wb_ref[...], preferred_element_type=F32)
    m = ga_ref[...].astype(F32) * a + gb_ref[...].astype(F32) * b
    o = jnp.dot(m.astype(BF16), wo_ref[...], preferred_element_type=F32)
    xn = x_ref[...] + g1_ref[...] * o
    xo_ref[...] = xn
    h = (_rms(xn) * nw_ref[...]) * (1.0 + sc_ref[...]) + sh_ref[...]
    h_ref[...] = h
    h_hi = h.astype(BF16)
    h_lo = (h - h_hi.astype(F32)).astype(BF16)
    lg = jnp.dot(h_hi, wr_ref[...], preferred_element_type=F32)
    lg_lo = jnp.dot(h_lo, wr_ref[:, :ROUTER_LANES], preferred_element_type=F32)
    lg_ref[...] = lg[:, :ROUTER_LANES] + (lg[:, ROUTER_LANES:] + lg_lo) + br_ref[...]


def _merge(ya, yb, gates, wa, wb, wo, x_all, mod_l, nw, wr, br, rows):
    M, D = x_all.shape
    W = ya.shape[1]
    return pl.pallas_call(
        _merge_kernel,
        grid=(rows.n,),
        in_specs=[
            rows.row_spec(W), rows.row_spec(W),
            pl.BlockSpec((ROW_TILE, D), lambda i: (rows.tile(i), 0)),
            pl.BlockSpec((ROW_TILE, D), lambda i: (rows.tile(i), 1)),
            _const_spec((W, D)), _const_spec((W, D)), _const_spec((D, D)),
            rows.row_spec(D), rows.mod_spec(D, 2),
            _const_spec((1, D)), rows.mod_spec(D, 3), rows.mod_spec(D, 4),
            _const_spec((D, 2 * ROUTER_LANES)), _const_spec((1, ROUTER_LANES)),
        ],
        out_specs=[rows.row_spec(D), rows.row_spec(D), rows.row_spec(ROUTER_LANES)],
        out_shape=[jax.ShapeDtypeStruct((M, D), F32), jax.ShapeDtypeStruct((M, D), F32),
                   jax.ShapeDtypeStruct((M, ROUTER_LANES), F32)],
        compiler_params=_params("parallel"),
        name="merge",
    )(ya, yb, gates, gates, wa, wb, wo, x_all, mod_l, nw, mod_l, mod_l, wr, br)


def _route(logits):
    n = logits.shape[0]
    gl = logits[:, :N_GROUPS]
    el = logits[:, N_GROUPS:N_GROUPS + N_EXPERTS].reshape(n, N_GROUPS, EXPERTS_PER_GROUP)
    g = jnp.argmax(gl, axis=-1).astype(jnp.int32)
    pg = 1.0 / jnp.sum(jnp.exp(gl - jnp.max(gl, axis=-1, keepdims=True)), axis=-1, keepdims=True)
    in_group = jnp.arange(N_GROUPS, dtype=jnp.int32)[None, :, None] == g[:, None, None]
    el_g = jnp.sum(jnp.where(in_group, el, 0.0), axis=1)
    tv, ti = lax.top_k(el_g, TOP_K)
    wts = jax.nn.softmax(tv, axis=-1) * pg
    eid = g[:, None] * EXPERTS_PER_GROUP + ti.astype(jnp.int32)
    return eid, wts


def _dispatch(eid, tok_rows):
    n = eid.shape[0]
    a = n * TOP_K
    e_flat = eid.reshape(a)
    onehot = (e_flat[:, None] == jnp.arange(N_EXPERTS, dtype=jnp.int32)[None, :]).astype(jnp.int32)
    csum = jnp.cumsum(onehot, axis=0)
    counts = csum[-1]
    padded = (counts + MOE_ROWS - 1) // MOE_ROWS * MOE_ROWS
    pad_end = jnp.cumsum(padded)
    pad_start = pad_end - padded
    dest = jnp.sum(onehot * (csum - 1 + pad_start[None, :]), axis=1)
    n_blk = a // MOE_ROWS + N_EXPERTS
    slot_row = jnp.zeros((n_blk * MOE_ROWS,), jnp.int32).at[dest].set(jnp.repeat(tok_rows, TOP_K))
    blk = jnp.arange(n_blk, dtype=jnp.int32)
    n_valid = (pad_end[-1] // MOE_ROWS).astype(jnp.int32)
    blk_c = jnp.minimum(blk, n_valid - 1)
    blk_e = jnp.clip(jnp.searchsorted(pad_end, blk_c * MOE_ROWS, side="right"), 0, N_EXPERTS - 1).astype(jnp.int32)
    experts = jnp.arange(N_EXPERTS, dtype=jnp.int32)
    used = counts > 0
    first_used_from = lax.cummin(jnp.where(used, experts, N_EXPERTS), reverse=True)
    next_used = jnp.concatenate([first_used_from[1:], jnp.full((1,), N_EXPERTS, jnp.int32)])
    next_used = jnp.where(next_used == N_EXPERTS, -1, next_used)
    parity = (jnp.cumsum(used.astype(jnp.int32)) - 1) % 2
    return (dest, blk_e, next_used[blk_e].astype(jnp.int32), parity[blk_e].astype(jnp.int32), slot_row,
            n_valid.reshape(1))


def _experts_kernel(blk_e, blk_next, blk_par, slot_row, n_valid,
                    h_hbm, w1_hbm, w3_hbm, w2_hbm, o_ref,
                    xg, xsem, ws1, ws3, ws2, wsem, w1b, w3b, w2b, *, layer):
    i = pl.program_id(0)
    nv = n_valid[0]
    e = blk_e[i]
    par = blk_par[i]
    fresh = jnp.logical_or(i == 0, e != blk_e[jnp.maximum(i - 1, 0)])

    def weight_copies(expert, p):
        return (pltpu.make_async_copy(w1_hbm.at[layer, expert], ws1.at[p], wsem.at[p, 0]),
                pltpu.make_async_copy(w3_hbm.at[layer, expert], ws3.at[p], wsem.at[p, 1]),
                pltpu.make_async_copy(w2_hbm.at[layer, expert], ws2.at[p], wsem.at[p, 2]))

    def for_rows(blk, fn):
        buf = blk & 1
        base = blk * MOE_ROWS
        for j in range(MOE_ROWS):
            r = slot_row[base + j]
            fn(pltpu.make_async_copy(h_hbm.at[pl.ds(r, 1), :], xg.at[buf, pl.ds(j, 1), :], xsem.at[buf]))

    @pl.when(i == 0)
    def _():
        for cp in weight_copies(e, par):
            cp.start()
        for_rows(0, lambda cp: cp.start())

    @pl.when(i == nv)
    def _():
        for_rows(i, lambda cp: cp.wait())

    @pl.when(fresh)
    def _():
        for cp in weight_copies(e, par):
            cp.wait()
        nxt = blk_next[i]

        @pl.when(nxt >= 0)
        def _():
            for cp in weight_copies(nxt, 1 - par):
                cp.start()

        w1b[...] = ws1[par].astype(BF16)
        w3b[...] = ws3[par].astype(BF16)
        w2b[...] = ws2[par].astype(BF16)

    @pl.when(i < nv)
    def _():
        for_rows(i, lambda cp: cp.wait())
        x = xg[i & 1].astype(BF16)
        for_rows(i + 1, lambda cp: cp.start())
        a = jnp.dot(x, w1b[...], preferred_element_type=F32)
        b = jnp.dot(x, w3b[...], preferred_element_type=F32)
        hid = (a * jax.nn.sigmoid(a)) * b
        o_ref[...] = jnp.dot(hid.astype(BF16), w2b[...], preferred_element_type=F32).astype(o_ref.dtype)


def _experts(h2, w1, w3, w2, layer, blk_e, blk_next, blk_par, slot_row, n_valid):
    D = h2.shape[1]
    de = w1.shape[-1]
    n_blk = blk_e.shape[0]
    any_spec = pl.BlockSpec(memory_space=pl.ANY)
    return pl.pallas_call(
        functools.partial(_experts_kernel, layer=layer),
        grid_spec=pltpu.PrefetchScalarGridSpec(
            num_scalar_prefetch=5, grid=(n_blk,),
            in_specs=[any_spec, any_spec, any_spec, any_spec],
            out_specs=pl.BlockSpec((MOE_ROWS, D), lambda i, *s: (jnp.minimum(i, s[4][0] - 1), 0)),
            scratch_shapes=[
                pltpu.VMEM((2, MOE_ROWS, D), F32), pltpu.SemaphoreType.DMA((2,)),
                pltpu.VMEM((2, D, de), F32), pltpu.VMEM((2, D, de), F32), pltpu.VMEM((2, de, D), F32),
                pltpu.SemaphoreType.DMA((2, 3)),
                pltpu.VMEM((D, de), BF16), pltpu.VMEM((D, de), BF16), pltpu.VMEM((de, D), BF16),
            ],
        ),
        out_shape=jax.ShapeDtypeStruct((n_blk * MOE_ROWS, D), BF16),
        compiler_params=_params("arbitrary"),
        name="experts",
    )(blk_e, blk_next, blk_par, slot_row, n_valid, h2, w1, w3, w2)


def _moe_mix(x_ref, y0_ref, y1_ref, w_ref, g2_ref):
    w = w_ref[...]
    y = w[:, 0:1] * y0_ref[...].astype(F32) + w[:, 1:2] * y1_ref[...].astype(F32)
    return x_ref[...] + g2_ref[...] * y


def _combine_next_kernel(x_ref, y0_ref, y1_ref, w_ref, g2_ref, nw_ref, sh_ref, sc_ref, xo_ref, h_ref):
    xn = _moe_mix(x_ref, y0_ref, y1_ref, w_ref, g2_ref)
    xo_ref[...] = xn
    h_ref[...] = ((_rms(xn) * nw_ref[...]) * (1.0 + sc_ref[...]) + sh_ref[...]).astype(BF16)


def _combine_final_kernel(x_ref, y0_ref, y1_ref, w_ref, g2_ref, nf_ref, o_ref):
    o_ref[...] = _rms(_moe_mix(x_ref, y0_ref, y1_ref, w_ref, g2_ref)) * nf_ref[...]


def _token_specs(n_tokens, d):
    tiles = n_tokens // ROW_TILE
    return [pl.BlockSpec((ROW_TILE, d), lambda i: (i, 0)), pl.BlockSpec((ROW_TILE, d), lambda i: (i + tiles, 0)),
            pl.BlockSpec((ROW_TILE, TOP_K), lambda i: (i, 0))]


def _combine_next(x_new, ysel, wts, mod_l, nw_next, mod_next, rows):
    M, D = x_new.shape
    return pl.pallas_call(
        _combine_next_kernel,
        grid=(rows.n,),
        in_specs=[rows.row_spec(D), *_token_specs(M, D), rows.mod_spec(D, 5),
                  _const_spec((1, D)), rows.mod_spec(D, 0), rows.mod_spec(D, 1)],
        out_specs=[rows.row_spec(D), rows.row_spec(D)],
        out_shape=[jax.ShapeDtypeStruct((M, D), F32), jax.ShapeDtypeStruct((M, D), BF16)],
        compiler_params=_params("parallel"),
        name="combine_next",
    )(x_new, ysel, ysel, wts, mod_l, nw_next, mod_next, mod_next)


def _combine_final(x_new, ysel, wts, mod_l, nf, rows):
    M, D = x_new.shape
    n = wts.shape[0]
    return pl.pallas_call(
        _combine_final_kernel,
        grid=(rows.n,),
        in_specs=[rows.row_spec(D), *_token_specs(n, D), rows.mod_spec(D, 5), _const_spec((1, D))],
        out_specs=pl.BlockSpec((ROW_TILE, D), lambda i: (i, 0)),
        out_shape=jax.ShapeDtypeStruct((n, D), F32),
        compiler_params=_params("parallel"),
        name="combine_final",
    )(x_new, ysel, ysel, wts, mod_l, nf)


def _rope_tables(ctx_len, seq):
    quarter = HEAD_DIM // 4
    freqs = ROPE_BASE ** (-jnp.arange(quarter, dtype=F32) / quarter)
    pos = jnp.arange(seq, dtype=jnp.int32)
    row = (pos // GRID_W).astype(F32)[:, None] * freqs[None, :]
    col = (pos % GRID_W).astype(F32)[:, None] * freqs[None, :]
    cos = jnp.concatenate([jnp.cos(row), jnp.cos(row), jnp.cos(col), jnp.cos(col)], axis=-1)
    sin = jnp.concatenate([-jnp.sin(row), jnp.sin(row), -jnp.sin(col), jnp.sin(col)], axis=-1)
    cos = jnp.concatenate([jnp.ones((ctx_len, HEAD_DIM), F32), cos], axis=0)
    sin = jnp.concatenate([jnp.zeros((ctx_len, HEAD_DIM), F32), sin], axis=0)
    return cos, sin


def kernel(x, c, ctx, c_ctx, w_mod, b_mod, norm_mix, norm_ffn, w_in, qn_a, kn_a, sink_b, w_br_a, w_br_b, w_out, w_rg, b_rg, w_re, b_re, w1, w3, w2, norm_final):
    B, S, D = x.shape
    C = ctx.shape[1]
    L = w_mod.shape[0]
    T = C + S
    M = B * T
    assert C == ROW_TILE and S % ROW_TILE == 0 and T % PROJ_ROWS == 0 and B < MOD_ROWS
    tiles_per_batch = T // ROW_TILE

    cc = jnp.concatenate([c, c_ctx[None, :], jnp.zeros((MOD_ROWS - B - 1, D), F32)], axis=0)
    mod = _mod_vectors(cc, w_mod, b_mod).reshape(L, MOD_ROWS * 6, 1, D)
    cos, sin = _rope_tables(C, S)
    x_all = jnp.concatenate([ctx, x], axis=1).reshape(M, D)
    latent_rows = (jnp.arange(B, dtype=jnp.int32)[:, None] * T + C
                   + jnp.arange(S, dtype=jnp.int32)[None, :]).reshape(B * S)
    all_rows = jnp.arange(M, dtype=jnp.int32)

    q_kinds = lambda kind: tuple((kind, 0, k) for k in range(PROJ_COLS // HEAD_DIM))
    kv_kinds = lambda kind: ((kind, 0, 0), (kind, 0, 1), ("v", 1, 0), ("v", 1, 1))
    col = {"qa": 0, "kva": WIDTH, "qb": WIDTH + 2 * KVW, "kvb": 2 * WIDTH + 2 * KVW, "gates": 2 * WIDTH + 4 * KVW}

    every = _Rows(B, tiles_per_batch, latent_only=False)
    latent = _Rows(B, tiles_per_batch, latent_only=True)
    h = _norm_mod(x_all, norm_mix[0][None, :], mod[0], every)
    out = None
    for l in range(L):
        last = l == L - 1
        rows = latent if last else every
        proj = functools.partial(_proj, h, w_in, l, cos, sin, tiles_per_batch=tiles_per_batch)
        (qa,) = proj(qn_a[l][None, :], col["qa"], q_kinds("q_norm"), [(PROJ_COLS, WIDTH // PROJ_COLS)], name="proj_qa")
        ka, va = proj(kn_a[l][None, :], col["kva"], kv_kinds("k_norm"), [(KVW, 1), (KVW, 1)], name="proj_kva")
        (qb,) = proj(qn_a[l][None, :], col["qb"], q_kinds("q_rope"), [(PROJ_COLS, WIDTH // PROJ_COLS)], name="proj_qb")
        kb, vb = proj(kn_a[l][None, :], col["kvb"], kv_kinds("k_rope"), [(KVW, 1), (KVW, 1)], name="proj_kvb")
        (gates,) = proj(qn_a[l][None, :], col["gates"], q_kinds("gate"), [(PROJ_COLS, 2 * D // PROJ_COLS)], name="proj_gates")

        ya = _attention(qa, ka, va, None, B, T, C, last, window=False)
        yb = _attention(qb, kb, vb, sink_b[l], B, T, C, last, window=True)

        w_router = jnp.concatenate(
            [w_rg[l], w_re[l], jnp.zeros((D, ROUTER_LANES - N_GROUPS - N_EXPERTS), F32)], axis=1)
        w_router_hi = w_router.astype(BF16)
        w_router = jnp.concatenate([w_router_hi, (w_router - w_router_hi.astype(F32)).astype(BF16)], axis=1)
        b_router = jnp.concatenate(
            [b_rg[l], b_re[l], jnp.zeros((ROUTER_LANES - N_GROUPS - N_EXPERTS,), F32)])[None, :]
        x_new, h2, logits = _merge(ya, yb, gates, w_br_a[l].astype(BF16), w_br_b[l].astype(BF16),
                                   w_out[l].astype(BF16), x_all, mod[l], norm_ffn[l][None, :],
                                   w_router, b_router, rows)

        tok_rows = latent_rows if last else all_rows
        eid, wts = _route(logits[tok_rows] if last else logits)
        dest, *plan = _dispatch(eid, tok_rows)
        ybuf = _experts(h2, w1, w3, w2, l, *plan)
        ysel = ybuf[dest.reshape(-1, TOP_K).T.reshape(-1)]
        if last:
            out = _combine_final(x_new, ysel, wts, mod[l], norm_final[None, :], rows).reshape(B, S, D)
        else:
            x_all, h = _combine_next(x_new, ysel, wts, mod[l], norm_mix[l + 1][None, :], mod[l + 1], rows)
    return out
```

```python
import functools

import jax
import jax.numpy as jnp
from jax import lax
from jax.experimental import pallas as pl
from jax.experimental.pallas import tpu as pltpu

F32 = jnp.float32
BF16 = jnp.bfloat16

GRID_W = 64
HEAD_DIM = 128
LANES = 128
HEADS = 8
KV_HEADS = 2
GROUP = HEADS // KV_HEADS
WIDTH = HEADS * HEAD_DIM
KVW = KV_HEADS * HEAD_DIM
WINDOW = 128
ROPE_BASE = 10000.0
ATTN_SCALE = HEAD_DIM ** -0.5
LOG2E = 1.4426950408889634
N_GROUPS = 8
EXPERTS_PER_GROUP = 8
N_EXPERTS = N_GROUPS * EXPERTS_PER_GROUP
TOP_K = 2
EPS = 1e-6
NEG_INF = -1e30

ROW_TILE = 256
PROJ_ROWS = 768
PROJ_COLS = 512
MOE_ROWS = 128
ROUTER_LANES = 128
WINDOW_CHAIN_HEADS = 2
MOD_ROWS = 8
VMEM_LIMIT = 56 * 1024 * 1024


def _params(*sem):
    return pltpu.CompilerParams(dimension_semantics=sem, vmem_limit_bytes=VMEM_LIMIT)


def _rms(y):
    return y * lax.rsqrt(jnp.mean(y * y, axis=-1, keepdims=True) + EPS)


def _mod_kernel(c_ref, w_ref, b_ref, o_ref):
    c = c_ref[...]
    a = (c * jax.nn.sigmoid(c)).astype(BF16)
    o_ref[...] = jnp.dot(a, w_ref[...].astype(BF16), preferred_element_type=F32) + b_ref[...]


def _mod_vectors(cc, w_mod, b_mod):
    L, D, N = w_mod.shape
    tn = 1024
    return pl.pallas_call(
        _mod_kernel,
        grid=(L, N // tn),
        in_specs=[
            pl.BlockSpec((MOD_ROWS, D), lambda l, j: (0, 0)),
            pl.BlockSpec((None, D, tn), lambda l, j: (l, 0, j)),
            pl.BlockSpec((None, 1, tn), lambda l, j: (l, 0, j)),
        ],
        out_specs=pl.BlockSpec((None, MOD_ROWS, tn), lambda l, j: (l, 0, j)),
        out_shape=jax.ShapeDtypeStruct((L, MOD_ROWS, N), F32),
        compiler_params=_params("parallel", "parallel"),
        name="mod_vectors",
    )(cc, w_mod, b_mod.reshape(L, 1, N))


class _Rows:
    def __init__(self, batch, tiles_per_batch, latent_only):
        self.batch = batch
        self.tpb = tiles_per_batch
        self.latent_only = latent_only
        self.n = batch * (tiles_per_batch - 1 if latent_only else tiles_per_batch)

    def tile(self, i):
        if self.latent_only:
            per = self.tpb - 1
            return (i // per) * self.tpb + 1 + i % per
        return i

    def mod_row(self, i):
        if self.latent_only:
            return i // (self.tpb - 1)
        return jnp.where(i % self.tpb == 0, self.batch, i // self.tpb)

    def row_spec(self, width):
        return pl.BlockSpec((ROW_TILE, width), lambda i: (self.tile(i), 0))

    def mod_spec(self, d, k):
        return pl.BlockSpec((None, 1, d), lambda i: (self.mod_row(i) * 6 + k, 0, 0))


def _const_spec(shape):
    zeros = (0,) * len(shape)
    return pl.BlockSpec(shape, lambda *_: zeros, pipeline_mode=pl.Buffered(1))


def _norm_mod_kernel(x_ref, nw_ref, sh_ref, sc_ref, h_ref):
    y = _rms(x_ref[...]) * nw_ref[...]
    h_ref[...] = (y * (1.0 + sc_ref[...]) + sh_ref[...]).astype(BF16)


def _norm_mod(x_all, nw, mod_l, rows):
    M, D = x_all.shape
    return pl.pallas_call(
        _norm_mod_kernel,
        grid=(rows.n,),
        in_specs=[rows.row_spec(D), _const_spec((1, D)), rows.mod_spec(D, 0), rows.mod_spec(D, 1)],
        out_specs=rows.row_spec(D),
        out_shape=jax.ShapeDtypeStruct((M, D), BF16),
        compiler_params=_params("parallel"),
        name="norm_mod",
    )(x_all, nw, mod_l, mod_l)


def _rope(y, cos, sin):
    lane = lax.broadcasted_iota(jnp.int32, y.shape, 1)
    quarter = HEAD_DIM // 4
    partner = jnp.where((lane & quarter) == 0,
                        pltpu.roll(y, HEAD_DIM - quarter, 1), pltpu.roll(y, quarter, 1))
    return y * cos + partner * sin


def _proj_kernel(h_ref, w_ref, cos_ref, sin_ref, gain_ref, *refs, kinds):
    *out_refs, wb_ref = refs

    @pl.when(pl.program_id(1) == 0)
    def _():
        wb_ref[...] = w_ref[...].astype(BF16)

    acc = jnp.dot(h_ref[...], wb_ref[...], preferred_element_type=F32)
    for c, (kind, out_idx, out_chunk) in enumerate(kinds):
        y = acc[:, c * HEAD_DIM:(c + 1) * HEAD_DIM]
        if kind in ("q_norm", "k_norm"):
            y = _rms(y) * gain_ref[...]
        if kind in ("q_norm", "k_norm", "q_rope", "k_rope"):
            y = _rope(y, cos_ref[...], sin_ref[...])
        if kind in ("q_norm", "q_rope"):
            y = y * (ATTN_SCALE * LOG2E)
        if kind == "gate":
            y = jax.nn.sigmoid(y)
        out_refs[out_idx][:, out_chunk * HEAD_DIM:(out_chunk + 1) * HEAD_DIM] = y.astype(BF16)


def _proj(h, w_in, layer, cos, sin, gain, col0, kinds, out_widths, tiles_per_batch, name):
    M, D = h.shape
    n_col_tiles = out_widths[0][1]
    cb0 = col0 // PROJ_COLS
    proj_tiles_per_batch = tiles_per_batch * ROW_TILE // PROJ_ROWS
    return pl.pallas_call(
        functools.partial(_proj_kernel, kinds=kinds),
        grid=(n_col_tiles, M // PROJ_ROWS),
        in_specs=[
            pl.BlockSpec((PROJ_ROWS, D), lambda j, i: (i, 0)),
            pl.BlockSpec((None, D, PROJ_COLS), lambda j, i: (layer, 0, cb0 + j)),
            pl.BlockSpec((PROJ_ROWS, HEAD_DIM), lambda j, i: (i % proj_tiles_per_batch, 0)),
            pl.BlockSpec((PROJ_ROWS, HEAD_DIM), lambda j, i: (i % proj_tiles_per_batch, 0)),
            pl.BlockSpec((1, HEAD_DIM), lambda j, i: (0, 0)),
        ],
        out_specs=[pl.BlockSpec((PROJ_ROWS, w), lambda j, i: (i, j)) for w, _ in out_widths],
        out_shape=[jax.ShapeDtypeStruct((M, w * n), BF16) for w, n in out_widths],
        scratch_shapes=[pltpu.VMEM((D, PROJ_COLS), BF16)],
        compiler_params=_params("parallel", "arbitrary"),
        name=name,
    )(h, w_in, cos, sin, gain)


def _stack_heads(q_ref, g0, n):
    return jnp.concatenate([q_ref[:, g * HEAD_DIM:(g + 1) * HEAD_DIM] for g in range(g0, g0 + n)], axis=0)


def _write_heads(o_ref, g0, n, o):
    rows = o_ref.shape[0]
    for j in range(n):
        o_ref[:, (g0 + j) * HEAD_DIM:(g0 + j + 1) * HEAD_DIM] = o[j * rows:(j + 1) * rows].astype(o_ref.dtype)


def _scores(q, k):
    return lax.dot_general(q, k, (((1,), (1,)), ((), ())), preferred_element_type=F32)


def _online_step(q, k, v, m, l, acc):
    s = _scores(q, k)
    m_new = jnp.maximum(m, jnp.max(s, axis=-1, keepdims=True))
    alpha = jnp.exp2(m - m_new)
    p = jnp.exp2(s - m_new)
    l = alpha * l + jnp.sum(p, axis=-1, keepdims=True)
    acc = alpha * acc + jnp.dot(p.astype(BF16), v, preferred_element_type=F32)
    return m_new, l, acc


def _dense_attn_kernel(q_ref, k_ref, v_ref, o_ref, *, ctx_len, total_len, key_chunk, ctx_tile):
    def run(n_keys, chunk):
        q = _stack_heads(q_ref, 0, GROUP)
        r = q.shape[0]
        m = jnp.full((r, 1), -jnp.inf, F32)
        l = jnp.zeros((r, 1), F32)
        acc = jnp.zeros((r, HEAD_DIM), F32)
        for c in range(n_keys // chunk):
            m, l, acc = _online_step(q, k_ref[c * chunk:(c + 1) * chunk, :],
                                     v_ref[c * chunk:(c + 1) * chunk, :], m, l, acc)
        _write_heads(o_ref, 0, GROUP, acc * (1.0 / l))

    if ctx_tile:
        is_ctx = pl.program_id(2) == 0
        pl.when(is_ctx)(lambda: run(ctx_len, ctx_len))
        pl.when(jnp.logical_not(is_ctx))(lambda: run(total_len, key_chunk))
    else:
        run(total_len, key_chunk)


def _window_attn_kernel(sink_ref, q_ref, k_ref, v_ref, o_ref, *, ctx_len, total_len, ctx_tile, q_off):
    kvh = pl.program_id(1)
    tq = q_ref.shape[0]
    span = tq + 2 * WINDOW
    n = WINDOW_CHAIN_HEADS

    def attend(parts):
        for g0 in range(0, GROUP, n):
            q = _stack_heads(q_ref, g0, n)
            sink = jnp.concatenate(
                [jnp.full((tq, 1), sink_ref[kvh * GROUP + g] * LOG2E, F32) for g in range(g0, g0 + n)], axis=0)
            scores = []
            m = sink
            for k, _, bias in parts:
                s = _scores(q, k)
                if bias is not None:
                    s = s + bias
                scores.append(s)
                m = jnp.maximum(m, jnp.max(s, axis=-1, keepdims=True))
            l = jnp.exp2(sink - m)
            o = jnp.zeros((n * tq, HEAD_DIM), F32)
            for s, (_, v, _) in zip(scores, parts):
                p = jnp.exp2(s - m)
                l = l + jnp.sum(p, axis=-1, keepdims=True)
                o = o + jnp.dot(p.astype(BF16), v, preferred_element_type=F32)
            _write_heads(o_ref, g0, n, o * (1.0 / l))

    def ctx_part():
        return k_ref[0:ctx_len, :], v_ref[0:ctx_len, :], None

    def run_latent():
        q0 = (pl.program_id(2) + q_off) * tq - ctx_len
        n_latent = total_len - ctx_len
        start = pl.multiple_of(jnp.clip(q0 - WINDOW, 0, n_latent - span), WINDOW)
        row = lax.broadcasted_iota(jnp.int32, (n * tq, span), 0) & (tq - 1)
        col = lax.broadcasted_iota(jnp.int32, (n * tq, span), 1)
        bias = jnp.where(jnp.abs(col - row + (start - q0)) <= WINDOW, 0.0, NEG_INF).astype(F32)
        attend([ctx_part(), (k_ref[pl.ds(ctx_len + start, span), :], v_ref[pl.ds(ctx_len + start, span), :], bias)])

    if ctx_tile:
        is_ctx = pl.program_id(2) == 0
        pl.when(is_ctx)(lambda: attend([ctx_part()]))
        pl.when(jnp.logical_not(is_ctx))(run_latent)
    else:
        run_latent()


def _attention(q, k, v, sink, batch, total_len, ctx_len, latent_only, window):
    M = q.shape[0]
    q3, k3, v3 = (a.reshape(batch, total_len, a.shape[1]) for a in (q, k, v))
    tiles = total_len // ROW_TILE
    q_off = 1 if latent_only else 0
    gw = GROUP * HEAD_DIM
    q_spec = pl.BlockSpec((None, ROW_TILE, gw), lambda b, h, i, *_: (b, i + q_off, h))
    kv_spec = pl.BlockSpec((None, total_len, HEAD_DIM), lambda b, h, i, *_: (b, 0, h))
    grid = (batch, KV_HEADS, tiles - q_off)
    out_shape = jax.ShapeDtypeStruct(q3.shape, BF16)
    if window:
        kern = functools.partial(_window_attn_kernel, ctx_len=ctx_len, total_len=total_len,
                                 ctx_tile=not latent_only, q_off=q_off)
        out = pl.pallas_call(
            kern,
            grid_spec=pltpu.PrefetchScalarGridSpec(
                num_scalar_prefetch=1, grid=grid,
                in_specs=[q_spec, kv_spec, kv_spec], out_specs=q_spec),
            out_shape=out_shape,
            compiler_params=_params("parallel", "parallel", "parallel"),
            name="window_attention",
        )(sink, q3, k3, v3)
    else:
        kern = functools.partial(_dense_attn_kernel, ctx_len=ctx_len, total_len=total_len,
                                 key_chunk=total_len // 3, ctx_tile=not latent_only)
        out = pl.pallas_call(
            kern, grid=grid, in_specs=[q_spec, kv_spec, kv_spec], out_specs=q_spec,
            out_shape=out_shape,
            compiler_params=_params("parallel", "parallel", "parallel"),
            name="dense_attention",
        )(q3, k3, v3)
    return out.reshape(M, q.shape[1])


def _merge_kernel(ya_ref, yb_ref, ga_ref, gb_ref, wa_ref, wb_ref, wo_ref, x_ref, g1_ref,
                  nw_ref, sh_ref, sc_ref, wr_ref, br_ref, xo_ref, h_ref, lg_ref):
    a = jnp.dot(ya_ref[...], wa_ref[...], preferred_element_type=F32)
    b = jnp.dot(yb_ref[...], wb_ref[...], preferred_element_type=F32)
    m = ga_ref[...].astype(F32) * a + gb_ref[...].astype(F32) * b
    o = jnp.dot(m.astype(BF16), wo_ref[...], preferred_element_type=F32)
    xn = x_ref[...] + g1_ref[...] * o
    xo_ref[...] = xn
    h = (_rms(xn) * nw_ref[...]) * (1.0 + sc_ref[...]) + sh_ref[...]
    chunks = h.shape[1] // LANES
    for c in range(chunks):
        h_ref[pl.ds(c, h.shape[0], stride=chunks), :] = h[:, c * LANES:(c + 1) * LANES]
    h_hi = h.astype(BF16)
    h_lo = (h - h_hi.astype(F32)).astype(BF16)
    lg = jnp.dot(h_hi, wr_ref[...], preferred_element_type=F32)
    lg_lo = jnp.dot(h_lo, wr_ref[:, :ROUTER_LANES], preferred_element_type=F32)
    lg_ref[...] = lg[:, :ROUTER_LANES] + (lg[:, ROUTER_LANES:] + lg_lo) + br_ref[...]


def _merge(ya, yb, gates, wa, wb, wo, x_all, mod_l, nw, wr, br, rows):
    M, D = x_all.shape
    W = ya.shape[1]
    return pl.pallas_call(
        _merge_kernel,
        grid=(rows.n,),
        in_specs=[
            rows.row_spec(W), rows.row_spec(W),
            pl.BlockSpec((ROW_TILE, D), lambda i: (rows.tile(i), 0)),
            pl.BlockSpec((ROW_TILE, D), lambda i: (rows.tile(i), 1)),
            _const_spec((W, D)), _const_spec((W, D)), _const_spec((D, D)),
            rows.row_spec(D), rows.mod_spec(D, 2),
            _const_spec((1, D)), rows.mod_spec(D, 3), rows.mod_spec(D, 4),
            _const_spec((D, 2 * ROUTER_LANES)), _const_spec((1, ROUTER_LANES)),
        ],
        out_specs=[rows.row_spec(D),
                   pl.BlockSpec((ROW_TILE * (D // LANES), LANES), lambda i: (rows.tile(i), 0)),
                   rows.row_spec(ROUTER_LANES)],
        out_shape=[jax.ShapeDtypeStruct((M, D), F32), jax.ShapeDtypeStruct((M * (D // LANES), LANES), F32),
                   jax.ShapeDtypeStruct((M, ROUTER_LANES), F32)],
        compiler_params=_params("parallel"),
        name="merge",
    )(ya, yb, gates, gates, wa, wb, wo, x_all, mod_l, nw, mod_l, mod_l, wr, br)


def _route(logits):
    n = logits.shape[0]
    gl = logits[:, :N_GROUPS]
    el = logits[:, N_GROUPS:N_GROUPS + N_EXPERTS].reshape(n, N_GROUPS, EXPERTS_PER_GROUP)
    g = jnp.argmax(gl, axis=-1).astype(jnp.int32)
    pg = 1.0 / jnp.sum(jnp.exp(gl - jnp.max(gl, axis=-1, keepdims=True)), axis=-1, keepdims=True)
    in_group = jnp.arange(N_GROUPS, dtype=jnp.int32)[None, :, None] == g[:, None, None]
    el_g = jnp.sum(jnp.where(in_group, el, 0.0), axis=1)
    tv, ti = lax.top_k(el_g, TOP_K)
    wts = jax.nn.softmax(tv, axis=-1) * pg
    eid = g[:, None] * EXPERTS_PER_GROUP + ti.astype(jnp.int32)
    return eid, wts


def _dispatch(eid, tok_rows, row_scale):
    n = eid.shape[0]
    a = n * TOP_K
    e_flat = eid.reshape(a)
    onehot = (e_flat[:, None] == jnp.arange(N_EXPERTS, dtype=jnp.int32)[None, :]).astype(jnp.int32)
    csum = jnp.cumsum(onehot, axis=0)
    counts = csum[-1]
    padded = (counts + MOE_ROWS - 1) // MOE_ROWS * MOE_ROWS
    pad_end = jnp.cumsum(padded)
    pad_start = pad_end - padded
    dest = jnp.sum(onehot * (csum - 1 + pad_start[None, :]), axis=1)
    n_blk = a // MOE_ROWS + N_EXPERTS
    slot_row = jnp.zeros((n_blk * MOE_ROWS,), jnp.int32).at[dest].set(jnp.repeat(tok_rows * row_scale, TOP_K))
    blk = jnp.arange(n_blk, dtype=jnp.int32)
    n_valid = (pad_end[-1] // MOE_ROWS).astype(jnp.int32)
    blk_c = jnp.minimum(blk, n_valid - 1)
    blk_e = jnp.clip(jnp.searchsorted(pad_end, blk_c * MOE_ROWS, side="right"), 0, N_EXPERTS - 1).astype(jnp.int32)
    experts = jnp.arange(N_EXPERTS, dtype=jnp.int32)
    used = counts > 0
    first_used_from = lax.cummin(jnp.where(used, experts, N_EXPERTS), reverse=True)
    next_used = jnp.concatenate([first_used_from[1:], jnp.full((1,), N_EXPERTS, jnp.int32)])
    next_used = jnp.where(next_used == N_EXPERTS, -1, next_used)
    parity = (jnp.cumsum(used.astype(jnp.int32)) - 1) % 2
    return (dest, blk_e, next_used[blk_e].astype(jnp.int32), parity[blk_e].astype(jnp.int32), slot_row,
            n_valid.reshape(1))


def _experts_kernel(blk_e, blk_next, blk_par, slot_row, n_valid,
                    h_hbm, w1_hbm, w3_hbm, w2_hbm, o_ref,
                    xg, xsem, ws1, ws3, ws2, wsem, w1b, w3b, w2b, *, layer):
    i = pl.program_id(0)
    nv = n_valid[0]
    chunks = xg.shape[1] // MOE_ROWS
    e = blk_e[i]
    par = blk_par[i]
    fresh = jnp.logical_or(i == 0, e != blk_e[jnp.maximum(i - 1, 0)])

    def weight_copies(expert, p):
        return (pltpu.make_async_copy(w1_hbm.at[layer, expert], ws1.at[p], wsem.at[p, 0]),
                pltpu.make_async_copy(w3_hbm.at[layer, expert], ws3.at[p], wsem.at[p, 1]),
                pltpu.make_async_copy(w2_hbm.at[layer, expert], ws2.at[p], wsem.at[p, 2]))

    def for_rows(blk, fn):
        buf = blk & 1
        base = blk * MOE_ROWS
        for j in range(MOE_ROWS):
            r = pl.multiple_of(slot_row[base + j], chunks)
            fn(pltpu.make_async_copy(h_hbm.at[pl.ds(r, chunks), :], xg.at[buf, pl.ds(j * chunks, chunks), :],
                                     xsem.at[buf]))

    @pl.when(i == 0)
    def _():
        for cp in weight_copies(e, par):
            cp.start()
        for_rows(0, lambda cp: cp.start())

    @pl.when(i == nv)
    def _():
        for_rows(i, lambda cp: cp.wait())

    @pl.when(fresh)
    def _():
        for cp in weight_copies(e, par):
            cp.wait()
        nxt = blk_next[i]

        @pl.when(nxt >= 0)
        def _():
            for cp in weight_copies(nxt, 1 - par):
                cp.start()

        w1b[...] = ws1[par].astype(BF16)
        w3b[...] = ws3[par].astype(BF16)
        w2b[...] = ws2[par].astype(BF16)

    @pl.when(i < nv)
    def _():
        for_rows(i, lambda cp: cp.wait())
        x = jnp.concatenate(
            [xg[i & 1, pl.ds(c, MOE_ROWS, stride=chunks), :] for c in range(chunks)], axis=1).astype(BF16)
        for_rows(i + 1, lambda cp: cp.start())
        a = jnp.dot(x, w1b[...], preferred_element_type=F32)
        b = jnp.dot(x, w3b[...], preferred_element_type=F32)
        hid = (a * jax.nn.sigmoid(a)) * b
        o_ref[...] = jnp.dot(hid.astype(BF16), w2b[...], preferred_element_type=F32).astype(o_ref.dtype)


def _experts(h2, w1, w3, w2, layer, blk_e, blk_next, blk_par, slot_row, n_valid):
    D, de = w1.shape[-2:]
    n_blk = blk_e.shape[0]
    any_spec = pl.BlockSpec(memory_space=pl.ANY)
    return pl.pallas_call(
        functools.partial(_experts_kernel, layer=layer),
        grid_spec=pltpu.PrefetchScalarGridSpec(
            num_scalar_prefetch=5, grid=(n_blk,),
            in_specs=[any_spec, any_spec, any_spec, any_spec],
            out_specs=pl.BlockSpec((MOE_ROWS, D), lambda i, *s: (jnp.minimum(i, s[4][0] - 1), 0)),
            scratch_shapes=[
                pltpu.VMEM((2, MOE_ROWS * (D // LANES), LANES), F32), pltpu.SemaphoreType.DMA((2,)),
                pltpu.VMEM((2, D, de), F32), pltpu.VMEM((2, D, de), F32), pltpu.VMEM((2, de, D), F32),
                pltpu.SemaphoreType.DMA((2, 3)),
                pltpu.VMEM((D, de), BF16), pltpu.VMEM((D, de), BF16), pltpu.VMEM((de, D), BF16),
            ],
        ),
        out_shape=jax.ShapeDtypeStruct((n_blk * MOE_ROWS, D), BF16),
        compiler_params=_params("arbitrary"),
        name="experts",
    )(blk_e, blk_next, blk_par, slot_row, n_valid, h2, w1, w3, w2)


def _moe_mix(x_ref, y0_ref, y1_ref, w_ref, g2_ref):
    w = w_ref[...]
    y = w[:, 0:1] * y0_ref[...].astype(F32) + w[:, 1:2] * y1_ref[...].astype(F32)
    return x_ref[...] + g2_ref[...] * y


def _combine_next_kernel(x_ref, y0_ref, y1_ref, w_ref, g2_ref, nw_ref, sh_ref, sc_ref, xo_ref, h_ref):
    xn = _moe_mix(x_ref, y0_ref, y1_ref, w_ref, g2_ref)
    xo_ref[...] = xn
    h_ref[...] = ((_rms(xn) * nw_ref[...]) * (1.0 + sc_ref[...]) + sh_ref[...]).astype(BF16)


def _combine_final_kernel(x_ref, y0_ref, y1_ref, w_ref, g2_ref, nf_ref, o_ref):
    o_ref[...] = _rms(_moe_mix(x_ref, y0_ref, y1_ref, w_ref, g2_ref)) * nf_ref[...]


def _token_specs(n_tokens, d):
    tiles = n_tokens // ROW_TILE
    return [pl.BlockSpec((ROW_TILE, d), lambda i: (i, 0)), pl.BlockSpec((ROW_TILE, d), lambda i: (i + tiles, 0)),
            pl.BlockSpec((ROW_TILE, TOP_K), lambda i: (i, 0))]


def _combine_next(x_new, ysel, wts, mod_l, nw_next, mod_next, rows):
    M, D = x_new.shape
    return pl.pallas_call(
        _combine_next_kernel,
        grid=(rows.n,),
        in_specs=[rows.row_spec(D), *_token_specs(M, D), rows.mod_spec(D, 5),
                  _const_spec((1, D)), rows.mod_spec(D, 0), rows.mod_spec(D, 1)],
        out_specs=[rows.row_spec(D), rows.row_spec(D)],
        out_shape=[jax.ShapeDtypeStruct((M, D), F32), jax.ShapeDtypeStruct((M, D), BF16)],
        compiler_params=_params("parallel"),
        name="combine_next",
    )(x_new, ysel, ysel, wts, mod_l, nw_next, mod_next, mod_next)


def _combine_final(x_new, ysel, wts, mod_l, nf, rows):
    M, D = x_new.shape
    n = wts.shape[0]
    return pl.pallas_call(
        _combine_final_kernel,
        grid=(rows.n,),
        in_specs=[rows.row_spec(D), *_token_specs(n, D), rows.mod_spec(D, 5), _const_spec((1, D))],
        out_specs=pl.BlockSpec((ROW_TILE, D), lambda i: (i, 0)),
        out_shape=jax.ShapeDtypeStruct((n, D), F32),
        compiler_params=_params("parallel"),
        name="combine_final",
    )(x_new, ysel, ysel, wts, mod_l, nf)


def _rope_tables(ctx_len, seq):
    quarter = HEAD_DIM // 4
    freqs = ROPE_BASE ** (-jnp.arange(quarter, dtype=F32) / quarter)
    pos = jnp.arange(seq, dtype=jnp.int32)
    row = (pos // GRID_W).astype(F32)[:, None] * freqs[None, :]
    col = (pos % GRID_W).astype(F32)[:, None] * freqs[None, :]
    cos = jnp.concatenate([jnp.cos(row), jnp.cos(row), jnp.cos(col), jnp.cos(col)], axis=-1)
    sin = jnp.concatenate([-jnp.sin(row), jnp.sin(row), -jnp.sin(col), jnp.sin(col)], axis=-1)
    cos = jnp.concatenate([jnp.ones((ctx_len, HEAD_DIM), F32), cos], axis=0)
    sin = jnp.concatenate([jnp.zeros((ctx_len, HEAD_DIM), F32), sin], axis=0)
    return cos, sin


def kernel(x, c, ctx, c_ctx, w_mod, b_mod, norm_mix, norm_ffn, w_in, qn_a, kn_a, sink_b, w_br_a, w_br_b, w_out, w_rg, b_rg, w_re, b_re, w1, w3, w2, norm_final):
    B, S, D = x.shape
    C = ctx.shape[1]
    L = w_mod.shape[0]
    T = C + S
    M = B * T
    assert C == ROW_TILE and S % ROW_TILE == 0 and T % PROJ_ROWS == 0 and B < MOD_ROWS
    tiles_per_batch = T // ROW_TILE

    cc = jnp.concatenate([c, c_ctx[None, :], jnp.zeros((MOD_ROWS - B - 1, D), F32)], axis=0)
    mod = _mod_vectors(cc, w_mod, b_mod).reshape(L, MOD_ROWS * 6, 1, D)
    cos, sin = _rope_tables(C, S)
    x_all = jnp.concatenate([ctx, x], axis=1).reshape(M, D)
    latent_rows = (jnp.arange(B, dtype=jnp.int32)[:, None] * T + C
                   + jnp.arange(S, dtype=jnp.int32)[None, :]).reshape(B * S)
    all_rows = jnp.arange(M, dtype=jnp.int32)

    q_kinds = lambda kind: tuple((kind, 0, k) for k in range(PROJ_COLS // HEAD_DIM))
    kv_kinds = lambda kind: ((kind, 0, 0), (kind, 0, 1), ("v", 1, 0), ("v", 1, 1))
    col = {"qa": 0, "kva": WIDTH, "qb": WIDTH + 2 * KVW, "kvb": 2 * WIDTH + 2 * KVW, "gates": 2 * WIDTH + 4 * KVW}

    every = _Rows(B, tiles_per_batch, latent_only=False)
    latent = _Rows(B, tiles_per_batch, latent_only=True)
    h = _norm_mod(x_all, norm_mix[0][None, :], mod[0], every)
    out = None
    for l in range(L):
        last = l == L - 1
        rows = latent if last else every
        proj = functools.partial(_proj, h, w_in, l, cos, sin, tiles_per_batch=tiles_per_batch)
        (qa,) = proj(qn_a[l][None, :], col["qa"], q_kinds("q_norm"), [(PROJ_COLS, WIDTH // PROJ_COLS)], name="proj_qa")
        ka, va = proj(kn_a[l][None, :], col["kva"], kv_kinds("k_norm"), [(KVW, 1), (KVW, 1)], name="proj_kva")
        (qb,) = proj(qn_a[l][None, :], col["qb"], q_kinds("q_rope"), [(PROJ_COLS, WIDTH // PROJ_COLS)], name="proj_qb")
        kb, vb = proj(kn_a[l][None, :], col["kvb"], kv_kinds("k_rope"), [(KVW, 1), (KVW, 1)], name="proj_kvb")
        (gates,) = proj(qn_a[l][None, :], col["gates"], q_kinds("gate"), [(PROJ_COLS, 2 * D // PROJ_COLS)], name="proj_gates")

        ya = _attention(qa, ka, va, None, B, T, C, last, window=False)
        yb = _attention(qb, kb, vb, sink_b[l], B, T, C, last, window=True)

        w_router = jnp.concatenate(
            [w_rg[l], w_re[l], jnp.zeros((D, ROUTER_LANES - N_GROUPS - N_EXPERTS), F32)], axis=1)
        w_router_hi = w_router.astype(BF16)
        w_router = jnp.concatenate([w_router_hi, (w_router - w_router_hi.astype(F32)).astype(BF16)], axis=1)
        b_router = jnp.concatenate(
            [b_rg[l], b_re[l], jnp.zeros((ROUTER_LANES - N_GROUPS - N_EXPERTS,), F32)])[None, :]
        x_new, h2, logits = _merge(ya, yb, gates, w_br_a[l].astype(BF16), w_br_b[l].astype(BF16),
                                   w_out[l].astype(BF16), x_all, mod[l], norm_ffn[l][None, :],
                                   w_router, b_router, rows)

        tok_rows = latent_rows if last else all_rows
        eid, wts = _route(logits[tok_rows] if last else logits)
        dest, *plan = _dispatch(eid, tok_rows, D // LANES)
        ybuf = _experts(h2, w1, w3, w2, l, *plan)
        ysel = ybuf[dest.reshape(-1, TOP_K).T.reshape(-1)]
        if last:
            out = _combine_final(x_new, ysel, wts, mod[l], norm_final[None, :], rows).reshape(B, S, D)
        else:
            x_all, h = _combine_next(x_new, ysel, wts, mod[l], norm_mix[l + 1][None, :], mod[l + 1], rows)
    return out
```

```python
import functools

import jax
import jax.numpy as jnp
from jax import lax
from jax.experimental import pallas as pl
from jax.experimental.pallas import tpu as pltpu

F32 = jnp.float32
BF16 = jnp.bfloat16

GRID_W = 64
HEAD_DIM = 128
LANES = 128
HEADS = 8
KV_HEADS = 2
GROUP = HEADS // KV_HEADS
WIDTH = HEADS * HEAD_DIM
KVW = KV_HEADS * HEAD_DIM
WINDOW = 128
ROPE_BASE = 10000.0
ATTN_SCALE = HEAD_DIM ** -0.5
LOG2E = 1.4426950408889634
N_GROUPS = 8
EXPERTS_PER_GROUP = 8
N_EXPERTS = N_GROUPS * EXPERTS_PER_GROUP
TOP_K = 2
EPS = 1e-6
NEG_INF = -1e30

ROW_TILE = 256
PROJ_ROWS = 768
PROJ_COLS = 512
MOE_ROWS = 128
WEIGHT_DMA_PRIORITY = 1
ROUTER_LANES = 128
WINDOW_CHAIN_HEADS = 2
MOD_ROWS = 8
VMEM_LIMIT = 56 * 1024 * 1024


def _params(*sem):
    return pltpu.CompilerParams(dimension_semantics=sem, vmem_limit_bytes=VMEM_LIMIT)


def _rms(y):
    return y * lax.rsqrt(jnp.mean(y * y, axis=-1, keepdims=True) + EPS)


def _mod_kernel(c_ref, w_ref, b_ref, o_ref):
    c = c_ref[...]
    a = (c * jax.nn.sigmoid(c)).astype(BF16)
    o_ref[...] = jnp.dot(a, w_ref[...].astype(BF16), preferred_element_type=F32) + b_ref[...]


def _mod_vectors(cc, w_mod, b_mod):
    L, D, N = w_mod.shape
    tn = 1024
    return pl.pallas_call(
        _mod_kernel,
        grid=(L, N // tn),
        in_specs=[
            pl.BlockSpec((MOD_ROWS, D), lambda l, j: (0, 0)),
            pl.BlockSpec((None, D, tn), lambda l, j: (l, 0, j)),
            pl.BlockSpec((None, 1, tn), lambda l, j: (l, 0, j)),
        ],
        out_specs=pl.BlockSpec((None, MOD_ROWS, tn), lambda l, j: (l, 0, j)),
        out_shape=jax.ShapeDtypeStruct((L, MOD_ROWS, N), F32),
        compiler_params=_params("parallel", "parallel"),
        name="mod_vectors",
    )(cc, w_mod, b_mod.reshape(L, 1, N))


class _Rows:
    def __init__(self, batch, tiles_per_batch, latent_only):
        self.batch = batch
        self.tpb = tiles_per_batch
        self.latent_only = latent_only
        self.n = batch * (tiles_per_batch - 1 if latent_only else tiles_per_batch)

    def tile(self, i):
        if self.latent_only:
            per = self.tpb - 1
            return (i // per) * self.tpb + 1 + i % per
        return i

    def mod_row(self, i):
        if self.latent_only:
            return i // (self.tpb - 1)
        return jnp.where(i % self.tpb == 0, self.batch, i // self.tpb)

    def row_spec(self, width):
        return pl.BlockSpec((ROW_TILE, width), lambda i: (self.tile(i), 0))

    def mod_spec(self, d, k):
        return pl.BlockSpec((None, 1, d), lambda i: (self.mod_row(i) * 6 + k, 0, 0))


def _const_spec(shape):
    zeros = (0,) * len(shape)
    return pl.BlockSpec(shape, lambda *_: zeros, pipeline_mode=pl.Buffered(1))


def _norm_mod_kernel(x_ref, nw_ref, sh_ref, sc_ref, h_ref):
    y = _rms(x_ref[...]) * nw_ref[...]
    h_ref[...] = (y * (1.0 + sc_ref[...]) + sh_ref[...]).astype(BF16)


def _norm_mod(x_all, nw, mod_l, rows):
    M, D = x_all.shape
    return pl.pallas_call(
        _norm_mod_kernel,
        grid=(rows.n,),
        in_specs=[rows.row_spec(D), _const_spec((1, D)), rows.mod_spec(D, 0), rows.mod_spec(D, 1)],
        out_specs=rows.row_spec(D),
        out_shape=jax.ShapeDtypeStruct((M, D), BF16),
        compiler_params=_params("parallel"),
        name="norm_mod",
    )(x_all, nw, mod_l, mod_l)


def _rope(y, cos, sin):
    lane = lax.broadcasted_iota(jnp.int32, y.shape, 1)
    quarter = HEAD_DIM // 4
    partner = jnp.where((lane & quarter) == 0,
                        pltpu.roll(y, HEAD_DIM - quarter, 1), pltpu.roll(y, quarter, 1))
    return y * cos + partner * sin


def _proj_kernel(h_ref, w_ref, cos_ref, sin_ref, gain_ref, *refs, kinds):
    *out_refs, wb_ref = refs

    @pl.when(pl.program_id(1) == 0)
    def _():
        wb_ref[...] = w_ref[...].astype(BF16)

    acc = jnp.dot(h_ref[...], wb_ref[...], preferred_element_type=F32)
    for c, (kind, out_idx, out_chunk) in enumerate(kinds):
        y = acc[:, c * HEAD_DIM:(c + 1) * HEAD_DIM]
        if kind in ("q_norm", "k_norm"):
            y = _rms(y) * gain_ref[...]
        if kind in ("q_norm", "k_norm", "q_rope", "k_rope"):
            y = _rope(y, cos_ref[...], sin_ref[...])
        if kind in ("q_norm", "q_rope"):
            y = y * (ATTN_SCALE * LOG2E)
        if kind == "gate":
            y = jax.nn.sigmoid(y)
        out_refs[out_idx][:, out_chunk * HEAD_DIM:(out_chunk + 1) * HEAD_DIM] = y.astype(BF16)


def _proj(h, w_in, layer, cos, sin, gain, col0, kinds, out_widths, tiles_per_batch, name):
    M, D = h.shape
    n_col_tiles = out_widths[0][1]
    cb0 = col0 // PROJ_COLS
    proj_tiles_per_batch = tiles_per_batch * ROW_TILE // PROJ_ROWS
    return pl.pallas_call(
        functools.partial(_proj_kernel, kinds=kinds),
        grid=(n_col_tiles, M // PROJ_ROWS),
        in_specs=[
            pl.BlockSpec((PROJ_ROWS, D), lambda j, i: (i, 0)),
            pl.BlockSpec((None, D, PROJ_COLS), lambda j, i: (layer, 0, cb0 + j)),
            pl.BlockSpec((PROJ_ROWS, HEAD_DIM), lambda j, i: (i % proj_tiles_per_batch, 0)),
            pl.BlockSpec((PROJ_ROWS, HEAD_DIM), lambda j, i: (i % proj_tiles_per_batch, 0)),
            pl.BlockSpec((1, HEAD_DIM), lambda j, i: (0, 0)),
        ],
        out_specs=[pl.BlockSpec((PROJ_ROWS, w), lambda j, i: (i, j)) for w, _ in out_widths],
        out_shape=[jax.ShapeDtypeStruct((M, w * n), BF16) for w, n in out_widths],
        scratch_shapes=[pltpu.VMEM((D, PROJ_COLS), BF16)],
        compiler_params=_params("parallel", "arbitrary"),
        name=name,
    )(h, w_in, cos, sin, gain)


def _stack_heads(q_ref, g0, n):
    return jnp.concatenate([q_ref[:, g * HEAD_DIM:(g + 1) * HEAD_DIM] for g in range(g0, g0 + n)], axis=0)


def _write_heads(o_ref, g0, n, o):
    rows = o_ref.shape[0]
    for j in range(n):
        o_ref[:, (g0 + j) * HEAD_DIM:(g0 + j + 1) * HEAD_DIM] = o[j * rows:(j + 1) * rows].astype(o_ref.dtype)


def _scores(q, k):
    return lax.dot_general(q, k, (((1,), (1,)), ((), ())), preferred_element_type=F32)


def _online_step(q, k, v, m, l, acc):
    s = _scores(q, k)
    m_new = jnp.maximum(m, jnp.max(s, axis=-1, keepdims=True))
    alpha = jnp.exp2(m - m_new)
    p = jnp.exp2(s - m_new)
    l = alpha * l + jnp.sum(p, axis=-1, keepdims=True)
    acc = alpha * acc + jnp.dot(p.astype(BF16), v, preferred_element_type=F32)
    return m_new, l, acc


def _dense_attn_kernel(q_ref, k_ref, v_ref, o_ref, *, ctx_len, total_len, key_chunk, ctx_tile):
    def run(n_keys, chunk):
        q = _stack_heads(q_ref, 0, GROUP)
        r = q.shape[0]
        m = jnp.full((r, 1), -jnp.inf, F32)
        l = jnp.zeros((r, 1), F32)
        acc = jnp.zeros((r, HEAD_DIM), F32)
        for c in range(n_keys // chunk):
            m, l, acc = _online_step(q, k_ref[c * chunk:(c + 1) * chunk, :],
                                     v_ref[c * chunk:(c + 1) * chunk, :], m, l, acc)
        _write_heads(o_ref, 0, GROUP, acc * (1.0 / l))

    if ctx_tile:
        is_ctx = pl.program_id(2) == 0
        pl.when(is_ctx)(lambda: run(ctx_len, ctx_len))
        pl.when(jnp.logical_not(is_ctx))(lambda: run(total_len, key_chunk))
    else:
        run(total_len, key_chunk)


def _window_attn_kernel(sink_ref, q_ref, k_ref, v_ref, o_ref, *, ctx_len, total_len, ctx_tile, q_off):
    kvh = pl.program_id(1)
    tq = q_ref.shape[0]
    span = tq + 2 * WINDOW
    n = WINDOW_CHAIN_HEADS

    def attend(parts):
        for g0 in range(0, GROUP, n):
            q = _stack_heads(q_ref, g0, n)
            sink = jnp.concatenate(
                [jnp.full((tq, 1), sink_ref[kvh * GROUP + g] * LOG2E, F32) for g in range(g0, g0 + n)], axis=0)
            scores = []
            m = sink
            for k, _, bias in parts:
                s = _scores(q, k)
                if bias is not None:
                    s = s + bias
                scores.append(s)
                m = jnp.maximum(m, jnp.max(s, axis=-1, keepdims=True))
            l = jnp.exp2(sink - m)
            o = jnp.zeros((n * tq, HEAD_DIM), F32)
            for s, (_, v, _) in zip(scores, parts):
                p = jnp.exp2(s - m)
                l = l + jnp.sum(p, axis=-1, keepdims=True)
                o = o + jnp.dot(p.astype(BF16), v, preferred_element_type=F32)
            _write_heads(o_ref, g0, n, o * (1.0 / l))

    def ctx_part():
        return k_ref[0:ctx_len, :], v_ref[0:ctx_len, :], None

    def run_latent():
        q0 = (pl.program_id(2) + q_off) * tq - ctx_len
        n_latent = total_len - ctx_len
        start = pl.multiple_of(jnp.clip(q0 - WINDOW, 0, n_latent - span), WINDOW)
        row = lax.broadcasted_iota(jnp.int32, (n * tq, span), 0) & (tq - 1)
        col = lax.broadcasted_iota(jnp.int32, (n * tq, span), 1)
        bias = jnp.where(jnp.abs(col - row + (start - q0)) <= WINDOW, 0.0, NEG_INF).astype(F32)
        attend([ctx_part(), (k_ref[pl.ds(ctx_len + start, span), :], v_ref[pl.ds(ctx_len + start, span), :], bias)])

    if ctx_tile:
        is_ctx = pl.program_id(2) == 0
        pl.when(is_ctx)(lambda: attend([ctx_part()]))
        pl.when(jnp.logical_not(is_ctx))(run_latent)
    else:
        run_latent()


def _attention(q, k, v, sink, batch, total_len, ctx_len, latent_only, window):
    M = q.shape[0]
    q3, k3, v3 = (a.reshape(batch, total_len, a.shape[1]) for a in (q, k, v))
    tiles = total_len // ROW_TILE
    q_off = 1 if latent_only else 0
    gw = GROUP * HEAD_DIM
    q_spec = pl.BlockSpec((None, ROW_TILE, gw), lambda b, h, i, *_: (b, i + q_off, h))
    kv_spec = pl.BlockSpec((None, total_len, HEAD_DIM), lambda b, h, i, *_: (b, 0, h))
    grid = (batch, KV_HEADS, tiles - q_off)
    out_shape = jax.ShapeDtypeStruct(q3.shape, BF16)
    if window:
        kern = functools.partial(_window_attn_kernel, ctx_len=ctx_len, total_len=total_len,
                                 ctx_tile=not latent_only, q_off=q_off)
        out = pl.pallas_call(
            kern,
            grid_spec=pltpu.PrefetchScalarGridSpec(
                num_scalar_prefetch=1, grid=grid,
                in_specs=[q_spec, kv_spec, kv_spec], out_specs=q_spec),
            out_shape=out_shape,
            compiler_params=_params("parallel", "parallel", "parallel"),
            name="window_attention",
        )(sink, q3, k3, v3)
    else:
        kern = functools.partial(_dense_attn_kernel, ctx_len=ctx_len, total_len=total_len,
                                 key_chunk=total_len // 3, ctx_tile=not latent_only)
        out = pl.pallas_call(
            kern, grid=grid, in_specs=[q_spec, kv_spec, kv_spec], out_specs=q_spec,
            out_shape=out_shape,
            compiler_params=_params("parallel", "parallel", "parallel"),
            name="dense_attention",
        )(q3, k3, v3)
    return out.reshape(M, q.shape[1])


def _merge_kernel(ya_ref, yb_ref, ga_ref, gb_ref, wa_ref, wb_ref, wo_ref, x_ref, g1_ref,
                  nw_ref, sh_ref, sc_ref, wr_ref, br_ref, xo_ref, h_ref, lg_ref):
    a = jnp.dot(ya_ref[...], wa_ref[...], preferred_element_type=F32)
    b = jnp.dot(yb_ref[...], wb_ref[...], preferred_element_type=F32)
    m = ga_ref[...].astype(F32) * a + gb_ref[...].astype(F32) * b
    o = jnp.dot(m.astype(BF16), wo_ref[...], preferred_element_type=F32)
    xn = x_ref[...] + g1_ref[...] * o
    xo_ref[...] = xn
    h = (_rms(xn) * nw_ref[...]) * (1.0 + sc_ref[...]) + sh_ref[...]
    chunks = h.shape[1] // LANES
    for c in range(chunks):
        h_ref[pl.ds(c, h.shape[0], stride=chunks), :] = h[:, c * LANES:(c + 1) * LANES]
    h_hi = h.astype(BF16)
    h_lo = (h - h_hi.astype(F32)).astype(BF16)
    lg = jnp.dot(h_hi, wr_ref[...], preferred_element_type=F32)
    lg_lo = jnp.dot(h_lo, wr_ref[:, :ROUTER_LANES], preferred_element_type=F32)
    lg_ref[...] = lg[:, :ROUTER_LANES] + (lg[:, ROUTER_LANES:] + lg_lo) + br_ref[...]


def _merge(ya, yb, gates, wa, wb, wo, x_all, mod_l, nw, wr, br, rows):
    M, D = x_all.shape
    W = ya.shape[1]
    return pl.pallas_call(
        _merge_kernel,
        grid=(rows.n,),
        in_specs=[
            rows.row_spec(W), rows.row_spec(W),
            pl.BlockSpec((ROW_TILE, D), lambda i: (rows.tile(i), 0)),
            pl.BlockSpec((ROW_TILE, D), lambda i: (rows.tile(i), 1)),
            _const_spec((W, D)), _const_spec((W, D)), _const_spec((D, D)),
            rows.row_spec(D), rows.mod_spec(D, 2),
            _const_spec((1, D)), rows.mod_spec(D, 3), rows.mod_spec(D, 4),
            _const_spec((D, 2 * ROUTER_LANES)), _const_spec((1, ROUTER_LANES)),
        ],
        out_specs=[rows.row_spec(D),
                   pl.BlockSpec((ROW_TILE * (D // LANES), LANES), lambda i: (rows.tile(i), 0)),
                   rows.row_spec(ROUTER_LANES)],
        out_shape=[jax.ShapeDtypeStruct((M, D), F32), jax.ShapeDtypeStruct((M * (D // LANES), LANES), F32),
                   jax.ShapeDtypeStruct((M, ROUTER_LANES), F32)],
        compiler_params=_params("parallel"),
        name="merge",
    )(ya, yb, gates, gates, wa, wb, wo, x_all, mod_l, nw, mod_l, mod_l, wr, br)


def _route(logits):
    n = logits.shape[0]
    gl = logits[:, :N_GROUPS]
    el = logits[:, N_GROUPS:N_GROUPS + N_EXPERTS].reshape(n, N_GROUPS, EXPERTS_PER_GROUP)
    g = jnp.argmax(gl, axis=-1).astype(jnp.int32)
    pg = 1.0 / jnp.sum(jnp.exp(gl - jnp.max(gl, axis=-1, keepdims=True)), axis=-1, keepdims=True)
    in_group = jnp.arange(N_GROUPS, dtype=jnp.int32)[None, :, None] == g[:, None, None]
    el_g = jnp.sum(jnp.where(in_group, el, 0.0), axis=1)
    tv, ti = lax.top_k(el_g, TOP_K)
    wts = jax.nn.softmax(tv, axis=-1) * pg
    eid = g[:, None] * EXPERTS_PER_GROUP + ti.astype(jnp.int32)
    return eid, wts


def _dispatch(eid, tok_rows, row_scale):
    n = eid.shape[0]
    a = n * TOP_K
    e_flat = eid.reshape(a)
    onehot = (e_flat[:, None] == jnp.arange(N_EXPERTS, dtype=jnp.int32)[None, :]).astype(jnp.int32)
    csum = jnp.cumsum(onehot, axis=0)
    counts = csum[-1]
    padded = (counts + MOE_ROWS - 1) // MOE_ROWS * MOE_ROWS
    pad_end = jnp.cumsum(padded)
    pad_start = pad_end - padded
    dest = jnp.sum(onehot * (csum - 1 + pad_start[None, :]), axis=1)
    n_blk = a // MOE_ROWS + N_EXPERTS
    slot_row = jnp.zeros((n_blk * MOE_ROWS,), jnp.int32).at[dest].set(jnp.repeat(tok_rows * row_scale, TOP_K))
    blk = jnp.arange(n_blk, dtype=jnp.int32)
    n_valid = (pad_end[-1] // MOE_ROWS).astype(jnp.int32)
    blk_c = jnp.minimum(blk, n_valid - 1)
    blk_e = jnp.clip(jnp.searchsorted(pad_end, blk_c * MOE_ROWS, side="right"), 0, N_EXPERTS - 1).astype(jnp.int32)
    experts = jnp.arange(N_EXPERTS, dtype=jnp.int32)
    used = counts > 0
    first_used_from = lax.cummin(jnp.where(used, experts, N_EXPERTS), reverse=True)
    next_used = jnp.concatenate([first_used_from[1:], jnp.full((1,), N_EXPERTS, jnp.int32)])
    next_used = jnp.where(next_used == N_EXPERTS, -1, next_used)
    parity = (jnp.cumsum(used.astype(jnp.int32)) - 1) % 2
    return (dest, blk_e, next_used[blk_e].astype(jnp.int32), parity[blk_e].astype(jnp.int32), slot_row,
            n_valid.reshape(1))


def _experts_kernel(blk_e, blk_next, blk_par, slot_row, n_valid,
                    h_hbm, w1_hbm, w3_hbm, w2_hbm, o_ref,
                    xg, xsem, ws1, ws3, ws2, wsem, w1b, w3b, w2b, *, layer):
    i = pl.program_id(0)
    nv = n_valid[0]
    chunks = xg.shape[1] // MOE_ROWS
    e = blk_e[i]
    par = blk_par[i]
    fresh = jnp.logical_or(i == 0, e != blk_e[jnp.maximum(i - 1, 0)])

    def weight_copies(expert, p):
        return (pltpu.make_async_copy(w1_hbm.at[layer, expert], ws1.at[p], wsem.at[p, 0]),
                pltpu.make_async_copy(w3_hbm.at[layer, expert], ws3.at[p], wsem.at[p, 1]),
                pltpu.make_async_copy(w2_hbm.at[layer, expert], ws2.at[p], wsem.at[p, 2]))

    def for_rows(blk, fn):
        buf = blk & 1
        base = blk * MOE_ROWS
        for j in range(MOE_ROWS):
            r = pl.multiple_of(slot_row[base + j], chunks)
            fn(pltpu.make_async_copy(h_hbm.at[pl.ds(r, chunks), :], xg.at[buf, pl.ds(j * chunks, chunks), :],
                                     xsem.at[buf]))

    @pl.when(i == 0)
    def _():
        for cp in weight_copies(e, par):
            cp.start(priority=WEIGHT_DMA_PRIORITY)
        for_rows(0, lambda cp: cp.start())

    @pl.when(i == nv)
    def _():
        for_rows(i, lambda cp: cp.wait())

    @pl.when(fresh)
    def _():
        for cp in weight_copies(e, par):
            cp.wait()
        nxt = blk_next[i]

        @pl.when(nxt >= 0)
        def _():
            for cp in weight_copies(nxt, 1 - par):
                cp.start(priority=WEIGHT_DMA_PRIORITY)

        w1b[...] = ws1[par].astype(BF16)
        w3b[...] = ws3[par].astype(BF16)
        w2b[...] = ws2[par].astype(BF16)

    @pl.when(i < nv)
    def _():
        for_rows(i, lambda cp: cp.wait())
        x = jnp.concatenate(
            [xg[i & 1, pl.ds(c, MOE_ROWS, stride=chunks), :] for c in range(chunks)], axis=1).astype(BF16)
        for_rows(i + 1, lambda cp: cp.start())
        a = jnp.dot(x, w1b[...], preferred_element_type=F32)
        b = jnp.dot(x, w3b[...], preferred_element_type=F32)
        hid = (a * jax.nn.sigmoid(a)) * b
        o_ref[...] = jnp.dot(hid.astype(BF16), w2b[...], preferred_element_type=F32).astype(o_ref.dtype)


def _experts(h2, w1, w3, w2, layer, blk_e, blk_next, blk_par, slot_row, n_valid):
    D, de = w1.shape[-2:]
    n_blk = blk_e.shape[0]
    any_spec = pl.BlockSpec(memory_space=pl.ANY)
    return pl.pallas_call(
        functools.partial(_experts_kernel, layer=layer),
        grid_spec=pltpu.PrefetchScalarGridSpec(
            num_scalar_prefetch=5, grid=(n_blk,),
            in_specs=[any_spec, any_spec, any_spec, any_spec],
            out_specs=pl.BlockSpec((MOE_ROWS, D), lambda i, *s: (jnp.minimum(i, s[4][0] - 1), 0)),
            scratch_shapes=[
                pltpu.VMEM((2, MOE_ROWS * (D // LANES), LANES), F32), pltpu.SemaphoreType.DMA((2,)),
                pltpu.VMEM((2, D, de), F32), pltpu.VMEM((2, D, de), F32), pltpu.VMEM((2, de, D), F32),
                pltpu.SemaphoreType.DMA((2, 3)),
                pltpu.VMEM((D, de), BF16), pltpu.VMEM((D, de), BF16), pltpu.VMEM((de, D), BF16),
            ],
        ),
        out_shape=jax.ShapeDtypeStruct((n_blk * MOE_ROWS, D), BF16),
        compiler_params=_params("arbitrary"),
        name="experts",
    )(blk_e, blk_next, blk_par, slot_row, n_valid, h2, w1, w3, w2)


def _moe_mix(x_ref, y0_ref, y1_ref, w_ref, g2_ref):
    w = w_ref[...]
    y = w[:, 0:1] * y0_ref[...].astype(F32) + w[:, 1:2] * y1_ref[...].astype(F32)
    return x_ref[...] + g2_ref[...] * y


def _combine_next_kernel(x_ref, y0_ref, y1_ref, w_ref, g2_ref, nw_ref, sh_ref, sc_ref, xo_ref, h_ref):
    xn = _moe_mix(x_ref, y0_ref, y1_ref, w_ref, g2_ref)
    xo_ref[...] = xn
    h_ref[...] = ((_rms(xn) * nw_ref[...]) * (1.0 + sc_ref[...]) + sh_ref[...]).astype(BF16)


def _combine_final_kernel(x_ref, y0_ref, y1_ref, w_ref, g2_ref, nf_ref, o_ref):
    o_ref[...] = _rms(_moe_mix(x_ref, y0_ref, y1_ref, w_ref, g2_ref)) * nf_ref[...]


def _token_specs(n_tokens, d):
    tiles = n_tokens // ROW_TILE
    return [pl.BlockSpec((ROW_TILE, d), lambda i: (i, 0)), pl.BlockSpec((ROW_TILE, d), lambda i: (i + tiles, 0)),
            pl.BlockSpec((ROW_TILE, TOP_K), lambda i: (i, 0))]


def _combine_next(x_new, ysel, wts, mod_l, nw_next, mod_next, rows):
    M, D = x_new.shape
    return pl.pallas_call(
        _combine_next_kernel,
        grid=(rows.n,),
        in_specs=[rows.row_spec(D), *_token_specs(M, D), rows.mod_spec(D, 5),
                  _const_spec((1, D)), rows.mod_spec(D, 0), rows.mod_spec(D, 1)],
        out_specs=[rows.row_spec(D), rows.row_spec(D)],
        out_shape=[jax.ShapeDtypeStruct((M, D), F32), jax.ShapeDtypeStruct((M, D), BF16)],
        compiler_params=_params("parallel"),
        name="combine_next",
    )(x_new, ysel, ysel, wts, mod_l, nw_next, mod_next, mod_next)


def _combine_final(x_new, ysel, wts, mod_l, nf, rows):
    M, D = x_new.shape
    n = wts.shape[0]
    return pl.pallas_call(
        _combine_final_kernel,
        grid=(rows.n,),
        in_specs=[rows.row_spec(D), *_token_specs(n, D), rows.mod_spec(D, 5), _const_spec((1, D))],
        out_specs=pl.BlockSpec((ROW_TILE, D), lambda i: (i, 0)),
        out_shape=jax.ShapeDtypeStruct((n, D), F32),
        compiler_params=_params("parallel"),
        name="combine_final",
    )(x_new, ysel, ysel, wts, mod_l, nf)


def _rope_tables(ctx_len, seq):
    quarter = HEAD_DIM // 4
    freqs = ROPE_BASE ** (-jnp.arange(quarter, dtype=F32) / quarter)
    pos = jnp.arange(seq, dtype=jnp.int32)
    row = (pos // GRID_W).astype(F32)[:, None] * freqs[None, :]
    col = (pos % GRID_W).astype(F32)[:, None] * freqs[None, :]
    cos = jnp.concatenate([jnp.cos(row), jnp.cos(row), jnp.cos(col), jnp.cos(col)], axis=-1)
    sin = jnp.concatenate([-jnp.sin(row), jnp.sin(row), -jnp.sin(col), jnp.sin(col)], axis=-1)
    cos = jnp.concatenate([jnp.ones((ctx_len, HEAD_DIM), F32), cos], axis=0)
    sin = jnp.concatenate([jnp.zeros((ctx_len, HEAD_DIM), F32), sin], axis=0)
    return cos, sin


def kernel(x, c, ctx, c_ctx, w_mod, b_mod, norm_mix, norm_ffn, w_in, qn_a, kn_a, sink_b, w_br_a, w_br_b, w_out, w_rg, b_rg, w_re, b_re, w1, w3, w2, norm_final):
    B, S, D = x.shape
    C = ctx.shape[1]
    L = w_mod.shape[0]
    T = C + S
    M = B * T
    assert C == ROW_TILE and S % ROW_TILE == 0 and T % PROJ_ROWS == 0 and B < MOD_ROWS
    tiles_per_batch = T // ROW_TILE

    cc = jnp.concatenate([c, c_ctx[None, :], jnp.zeros((MOD_ROWS - B - 1, D), F32)], axis=0)
    mod = _mod_vectors(cc, w_mod, b_mod).reshape(L, MOD_ROWS * 6, 1, D)
    cos, sin = _rope_tables(C, S)
    x_all = jnp.concatenate([ctx, x], axis=1).reshape(M, D)
    latent_rows = (jnp.arange(B, dtype=jnp.int32)[:, None] * T + C
                   + jnp.arange(S, dtype=jnp.int32)[None, :]).reshape(B * S)
    all_rows = jnp.arange(M, dtype=jnp.int32)

    q_kinds = lambda kind: tuple((kind, 0, k) for k in range(PROJ_COLS // HEAD_DIM))
    kv_kinds = lambda kind: ((kind, 0, 0), (kind, 0, 1), ("v", 1, 0), ("v", 1, 1))
    col = {"qa": 0, "kva": WIDTH, "qb": WIDTH + 2 * KVW, "kvb": 2 * WIDTH + 2 * KVW, "gates": 2 * WIDTH + 4 * KVW}

    every = _Rows(B, tiles_per_batch, latent_only=False)
    latent = _Rows(B, tiles_per_batch, latent_only=True)
    h = _norm_mod(x_all, norm_mix[0][None, :], mod[0], every)
    out = None
    for l in range(L):
        last = l == L - 1
        rows = latent if last else every
        proj = functools.partial(_proj, h, w_in, l, cos, sin, tiles_per_batch=tiles_per_batch)
        (qa,) = proj(qn_a[l][None, :], col["qa"], q_kinds("q_norm"), [(PROJ_COLS, WIDTH // PROJ_COLS)], name="proj_qa")
        ka, va = proj(kn_a[l][None, :], col["kva"], kv_kinds("k_norm"), [(KVW, 1), (KVW, 1)], name="proj_kva")
        (qb,) = proj(qn_a[l][None, :], col["qb"], q_kinds("q_rope"), [(PROJ_COLS, WIDTH // PROJ_COLS)], name="proj_qb")
        kb, vb = proj(kn_a[l][None, :], col["kvb"], kv_kinds("k_rope"), [(KVW, 1), (KVW, 1)], name="proj_kvb")
        (gates,) = proj(qn_a[l][None, :], col["gates"], q_kinds("gate"), [(PROJ_COLS, 2 * D // PROJ_COLS)], name="proj_gates")

        ya = _attention(qa, ka, va, None, B, T, C, last, window=False)
        yb = _attention(qb, kb, vb, sink_b[l], B, T, C, last, window=True)

        w_router = jnp.concatenate(
            [w_rg[l], w_re[l], jnp.zeros((D, ROUTER_LANES - N_GROUPS - N_EXPERTS), F32)], axis=1)
        w_router_hi = w_router.astype(BF16)
        w_router = jnp.concatenate([w_router_hi, (w_router - w_router_hi.astype(F32)).astype(BF16)], axis=1)
        b_router = jnp.concatenate(
            [b_rg[l], b_re[l], jnp.zeros((ROUTER_LANES - N_GROUPS - N_EXPERTS,), F32)])[None, :]
        x_new, h2, logits = _merge(ya, yb, gates, w_br_a[l].astype(BF16), w_br_b[l].astype(BF16),
                                   w_out[l].astype(BF16), x_all, mod[l], norm_ffn[l][None, :],
                                   w_router, b_router, rows)

        tok_rows = latent_rows if last else all_rows
        eid, wts = _route(logits[tok_rows] if last else logits)
        dest, *plan = _dispatch(eid, tok_rows, D // LANES)
        ybuf = _experts(h2, w1, w3, w2, l, *plan)
        ysel = ybuf[dest.reshape(-1, TOP_K).T.reshape(-1)]
        if last:
            out = _combine_final(x_new, ysel, wts, mod[l], norm_final[None, :], rows).reshape(B, S, D)
        else:
            x_all, h = _combine_next(x_new, ysel, wts, mod[l], norm_mix[l + 1][None, :], mod[l + 1], rows)
    return out
```

```python
import functools

import jax
import jax.numpy as jnp
from jax import lax
from jax.experimental import pallas as pl
from jax.experimental.pallas import tpu as pltpu

F32 = jnp.float32
BF16 = jnp.bfloat16

GRID_W = 64
HEAD_DIM = 128
LANES = 128
HEADS = 8
KV_HEADS = 2
GROUP = HEADS // KV_HEADS
WIDTH = HEADS * HEAD_DIM
KVW = KV_HEADS * HEAD_DIM
WINDOW = 128
ROPE_BASE = 10000.0
ATTN_SCALE = HEAD_DIM ** -0.5
LOG2E = 1.4426950408889634
N_GROUPS = 8
EXPERTS_PER_GROUP = 8
N_EXPERTS = N_GROUPS * EXPERTS_PER_GROUP
TOP_K = 2
EPS = 1e-6
NEG_INF = -1e30

ROW_TILE = 256
PROJ_ROWS = 768
PROJ_COLS = 512
MOE_ROWS = 128
GATHER_AHEAD = 2
WEIGHT_DMA_PRIORITY = 1
ROUTER_LANES = 128
WINDOW_CHAIN_HEADS = 2
MOD_ROWS = 8
VMEM_LIMIT = 56 * 1024 * 1024


def _params(*sem):
    return pltpu.CompilerParams(dimension_semantics=sem, vmem_limit_bytes=VMEM_LIMIT)


def _rms(y):
    return y * lax.rsqrt(jnp.mean(y * y, axis=-1, keepdims=True) + EPS)


def _mod_kernel(c_ref, w_ref, b_ref, o_ref):
    c = c_ref[...]
    a = (c * jax.nn.sigmoid(c)).astype(BF16)
    o_ref[...] = jnp.dot(a, w_ref[...].astype(BF16), preferred_element_type=F32) + b_ref[...]


def _mod_vectors(cc, w_mod, b_mod):
    L, D, N = w_mod.shape
    tn = 1024
    return pl.pallas_call(
        _mod_kernel,
        grid=(L, N // tn),
        in_specs=[
            pl.BlockSpec((MOD_ROWS, D), lambda l, j: (0, 0)),
            pl.BlockSpec((None, D, tn), lambda l, j: (l, 0, j)),
            pl.BlockSpec((None, 1, tn), lambda l, j: (l, 0, j)),
        ],
        out_specs=pl.BlockSpec((None, MOD_ROWS, tn), lambda l, j: (l, 0, j)),
        out_shape=jax.ShapeDtypeStruct((L, MOD_ROWS, N), F32),
        compiler_params=_params("parallel", "parallel"),
        name="mod_vectors",
    )(cc, w_mod, b_mod.reshape(L, 1, N))


class _Rows:
    def __init__(self, batch, tiles_per_batch, latent_only):
        self.batch = batch
        self.tpb = tiles_per_batch
        self.latent_only = latent_only
        self.n = batch * (tiles_per_batch - 1 if latent_only else tiles_per_batch)

    def tile(self, i):
        if self.latent_only:
            per = self.tpb - 1
            return (i // per) * self.tpb + 1 + i % per
        return i

    def mod_row(self, i):
        if self.latent_only:
            return i // (self.tpb - 1)
        return jnp.where(i % self.tpb == 0, self.batch, i // self.tpb)

    def row_spec(self, width):
        return pl.BlockSpec((ROW_TILE, width), lambda i: (self.tile(i), 0))

    def mod_spec(self, d, k):
        return pl.BlockSpec((None, 1, d), lambda i: (self.mod_row(i) * 6 + k, 0, 0))


def _const_spec(shape):
    zeros = (0,) * len(shape)
    return pl.BlockSpec(shape, lambda *_: zeros, pipeline_mode=pl.Buffered(1))


def _norm_mod_kernel(x_ref, nw_ref, sh_ref, sc_ref, h_ref):
    y = _rms(x_ref[...]) * nw_ref[...]
    h_ref[...] = (y * (1.0 + sc_ref[...]) + sh_ref[...]).astype(BF16)


def _norm_mod(x_all, nw, mod_l, rows):
    M, D = x_all.shape
    return pl.pallas_call(
        _norm_mod_kernel,
        grid=(rows.n,),
        in_specs=[rows.row_spec(D), _const_spec((1, D)), rows.mod_spec(D, 0), rows.mod_spec(D, 1)],
        out_specs=rows.row_spec(D),
        out_shape=jax.ShapeDtypeStruct((M, D), BF16),
        compiler_params=_params("parallel"),
        name="norm_mod",
    )(x_all, nw, mod_l, mod_l)


def _rope(y, cos, sin):
    lane = lax.broadcasted_iota(jnp.int32, y.shape, 1)
    quarter = HEAD_DIM // 4
    partner = jnp.where((lane & quarter) == 0,
                        pltpu.roll(y, HEAD_DIM - quarter, 1), pltpu.roll(y, quarter, 1))
    return y * cos + partner * sin


def _proj_kernel(h_ref, w_ref, cos_ref, sin_ref, gain_ref, *refs, kinds):
    *out_refs, wb_ref = refs

    @pl.when(pl.program_id(1) == 0)
    def _():
        wb_ref[...] = w_ref[...].astype(BF16)

    acc = jnp.dot(h_ref[...], wb_ref[...], preferred_element_type=F32)
    for c, (kind, out_idx, out_chunk) in enumerate(kinds):
        y = acc[:, c * HEAD_DIM:(c + 1) * HEAD_DIM]
        if kind in ("q_norm", "k_norm"):
            y = _rms(y) * gain_ref[...]
        if kind in ("q_norm", "k_norm", "q_rope", "k_rope"):
            y = _rope(y, cos_ref[...], sin_ref[...])
        if kind in ("q_norm", "q_rope"):
            y = y * (ATTN_SCALE * LOG2E)
        if kind == "gate":
            y = jax.nn.sigmoid(y)
        out_refs[out_idx][:, out_chunk * HEAD_DIM:(out_chunk + 1) * HEAD_DIM] = y.astype(BF16)


def _proj(h, w_in, layer, cos, sin, gain, col0, kinds, out_widths, tiles_per_batch, name):
    M, D = h.shape
    n_col_tiles = out_widths[0][1]
    cb0 = col0 // PROJ_COLS
    proj_tiles_per_batch = tiles_per_batch * ROW_TILE // PROJ_ROWS
    return pl.pallas_call(
        functools.partial(_proj_kernel, kinds=kinds),
        grid=(n_col_tiles, M // PROJ_ROWS),
        in_specs=[
            pl.BlockSpec((PROJ_ROWS, D), lambda j, i: (i, 0)),
            pl.BlockSpec((None, D, PROJ_COLS), lambda j, i: (layer, 0, cb0 + j)),
            pl.BlockSpec((PROJ_ROWS, HEAD_DIM), lambda j, i: (i % proj_tiles_per_batch, 0)),
            pl.BlockSpec((PROJ_ROWS, HEAD_DIM), lambda j, i: (i % proj_tiles_per_batch, 0)),
            pl.BlockSpec((1, HEAD_DIM), lambda j, i: (0, 0)),
        ],
        out_specs=[pl.BlockSpec((PROJ_ROWS, w), lambda j, i: (i, j)) for w, _ in out_widths],
        out_shape=[jax.ShapeDtypeStruct((M, w * n), BF16) for w, n in out_widths],
        scratch_shapes=[pltpu.VMEM((D, PROJ_COLS), BF16)],
        compiler_params=_params("parallel", "arbitrary"),
        name=name,
    )(h, w_in, cos, sin, gain)


def _stack_heads(q_ref, g0, n):
    return jnp.concatenate([q_ref[:, g * HEAD_DIM:(g + 1) * HEAD_DIM] for g in range(g0, g0 + n)], axis=0)


def _write_heads(o_ref, g0, n, o):
    rows = o_ref.shape[0]
    for j in range(n):
        o_ref[:, (g0 + j) * HEAD_DIM:(g0 + j + 1) * HEAD_DIM] = o[j * rows:(j + 1) * rows].astype(o_ref.dtype)


def _scores(q, k):
    return lax.dot_general(q, k, (((1,), (1,)), ((), ())), preferred_element_type=F32)


def _online_step(q, k, v, m, l, acc):
    s = _scores(q, k)
    m_new = jnp.maximum(m, jnp.max(s, axis=-1, keepdims=True))
    alpha = jnp.exp2(m - m_new)
    p = jnp.exp2(s - m_new)
    l = alpha * l + jnp.sum(p, axis=-1, keepdims=True)
    acc = alpha * acc + jnp.dot(p.astype(BF16), v, preferred_element_type=F32)
    return m_new, l, acc


def _dense_attn_kernel(q_ref, k_ref, v_ref, o_ref, *, ctx_len, total_len, key_chunk, ctx_tile):
    def run(n_keys, chunk):
        q = _stack_heads(q_ref, 0, GROUP)
        r = q.shape[0]
        m = jnp.full((r, 1), -jnp.inf, F32)
        l = jnp.zeros((r, 1), F32)
        acc = jnp.zeros((r, HEAD_DIM), F32)
        for c in range(n_keys // chunk):
            m, l, acc = _online_step(q, k_ref[c * chunk:(c + 1) * chunk, :],
                                     v_ref[c * chunk:(c + 1) * chunk, :], m, l, acc)
        _write_heads(o_ref, 0, GROUP, acc * (1.0 / l))

    if ctx_tile:
        is_ctx = pl.program_id(2) == 0
        pl.when(is_ctx)(lambda: run(ctx_len, ctx_len))
        pl.when(jnp.logical_not(is_ctx))(lambda: run(total_len, key_chunk))
    else:
        run(total_len, key_chunk)


def _window_attn_kernel(sink_ref, q_ref, k_ref, v_ref, o_ref, *, ctx_len, total_len, ctx_tile, q_off):
    kvh = pl.program_id(1)
    tq = q_ref.shape[0]
    span = tq + 2 * WINDOW
    n = WINDOW_CHAIN_HEADS

    def attend(parts):
        for g0 in range(0, GROUP, n):
            q = _stack_heads(q_ref, g0, n)
            sink = jnp.concatenate(
                [jnp.full((tq, 1), sink_ref[kvh * GROUP + g] * LOG2E, F32) for g in range(g0, g0 + n)], axis=0)
            scores = []
            m = sink
            for k, _, bias in parts:
                s = _scores(q, k)
                if bias is not None:
                    s = s + bias
                scores.append(s)
                m = jnp.maximum(m, jnp.max(s, axis=-1, keepdims=True))
            l = jnp.exp2(sink - m)
            o = jnp.zeros((n * tq, HEAD_DIM), F32)
            for s, (_, v, _) in zip(scores, parts):
                p = jnp.exp2(s - m)
                l = l + jnp.sum(p, axis=-1, keepdims=True)
                o = o + jnp.dot(p.astype(BF16), v, preferred_element_type=F32)
            _write_heads(o_ref, g0, n, o * (1.0 / l))

    def ctx_part():
        return k_ref[0:ctx_len, :], v_ref[0:ctx_len, :], None

    def run_latent():
        q0 = (pl.program_id(2) + q_off) * tq - ctx_len
        n_latent = total_len - ctx_len
        start = pl.multiple_of(jnp.clip(q0 - WINDOW, 0, n_latent - span), WINDOW)
        row = lax.broadcasted_iota(jnp.int32, (n * tq, span), 0) & (tq - 1)
        col = lax.broadcasted_iota(jnp.int32, (n * tq, span), 1)
        bias = jnp.where(jnp.abs(col - row + (start - q0)) <= WINDOW, 0.0, NEG_INF).astype(F32)
        attend([ctx_part(), (k_ref[pl.ds(ctx_len + start, span), :], v_ref[pl.ds(ctx_len + start, span), :], bias)])

    if ctx_tile:
        is_ctx = pl.program_id(2) == 0
        pl.when(is_ctx)(lambda: attend([ctx_part()]))
        pl.when(jnp.logical_not(is_ctx))(run_latent)
    else:
        run_latent()


def _attention(q, k, v, sink, batch, total_len, ctx_len, latent_only, window):
    M = q.shape[0]
    q3, k3, v3 = (a.reshape(batch, total_len, a.shape[1]) for a in (q, k, v))
    tiles = total_len // ROW_TILE
    q_off = 1 if latent_only else 0
    gw = GROUP * HEAD_DIM
    q_spec = pl.BlockSpec((None, ROW_TILE, gw), lambda b, h, i, *_: (b, i + q_off, h))
    kv_spec = pl.BlockSpec((None, total_len, HEAD_DIM), lambda b, h, i, *_: (b, 0, h))
    grid = (batch, KV_HEADS, tiles - q_off)
    out_shape = jax.ShapeDtypeStruct(q3.shape, BF16)
    if window:
        kern = functools.partial(_window_attn_kernel, ctx_len=ctx_len, total_len=total_len,
                                 ctx_tile=not latent_only, q_off=q_off)
        out = pl.pallas_call(
            kern,
            grid_spec=pltpu.PrefetchScalarGridSpec(
                num_scalar_prefetch=1, grid=grid,
                in_specs=[q_spec, kv_spec, kv_spec], out_specs=q_spec),
            out_shape=out_shape,
            compiler_params=_params("parallel", "parallel", "parallel"),
            name="window_attention",
        )(sink, q3, k3, v3)
    else:
        kern = functools.partial(_dense_attn_kernel, ctx_len=ctx_len, total_len=total_len,
                                 key_chunk=total_len // 3, ctx_tile=not latent_only)
        out = pl.pallas_call(
            kern, grid=grid, in_specs=[q_spec, kv_spec, kv_spec], out_specs=q_spec,
            out_shape=out_shape,
            compiler_params=_params("parallel", "parallel", "parallel"),
            name="dense_attention",
        )(q3, k3, v3)
    return out.reshape(M, q.shape[1])


def _merge_kernel(ya_ref, yb_ref, ga_ref, gb_ref, wa_ref, wb_ref, wo_ref, x_ref, g1_ref,
                  nw_ref, sh_ref, sc_ref, wr_ref, br_ref, xo_ref, h_ref, lg_ref):
    a = jnp.dot(ya_ref[...], wa_ref[...], preferred_element_type=F32)
    b = jnp.dot(yb_ref[...], wb_ref[...], preferred_element_type=F32)
    m = ga_ref[...].astype(F32) * a + gb_ref[...].astype(F32) * b
    o = jnp.dot(m.astype(BF16), wo_ref[...], preferred_element_type=F32)
    xn = x_ref[...] + g1_ref[...] * o
    xo_ref[...] = xn
    h = (_rms(xn) * nw_ref[...]) * (1.0 + sc_ref[...]) + sh_ref[...]
    chunks = h.shape[1] // LANES
    for c in range(chunks):
        h_ref[pl.ds(c, h.shape[0], stride=chunks), :] = h[:, c * LANES:(c + 1) * LANES]
    h_hi = h.astype(BF16)
    h_lo = (h - h_hi.astype(F32)).astype(BF16)
    lg = jnp.dot(h_hi, wr_ref[...], preferred_element_type=F32)
    lg_lo = jnp.dot(h_lo, wr_ref[:, :ROUTER_LANES], preferred_element_type=F32)
    lg_ref[...] = lg[:, :ROUTER_LANES] + (lg[:, ROUTER_LANES:] + lg_lo) + br_ref[...]


def _merge(ya, yb, gates, wa, wb, wo, x_all, mod_l, nw, wr, br, rows):
    M, D = x_all.shape
    W = ya.shape[1]
    return pl.pallas_call(
        _merge_kernel,
        grid=(rows.n,),
        in_specs=[
            rows.row_spec(W), rows.row_spec(W),
            pl.BlockSpec((ROW_TILE, D), lambda i: (rows.tile(i), 0)),
            pl.BlockSpec((ROW_TILE, D), lambda i: (rows.tile(i), 1)),
            _const_spec((W, D)), _const_spec((W, D)), _const_spec((D, D)),
            rows.row_spec(D), rows.mod_spec(D, 2),
            _const_spec((1, D)), rows.mod_spec(D, 3), rows.mod_spec(D, 4),
            _const_spec((D, 2 * ROUTER_LANES)), _const_spec((1, ROUTER_LANES)),
        ],
        out_specs=[rows.row_spec(D),
                   pl.BlockSpec((ROW_TILE * (D // LANES), LANES), lambda i: (rows.tile(i), 0)),
                   rows.row_spec(ROUTER_LANES)],
        out_shape=[jax.ShapeDtypeStruct((M, D), F32), jax.ShapeDtypeStruct((M * (D // LANES), LANES), F32),
                   jax.ShapeDtypeStruct((M, ROUTER_LANES), F32)],
        compiler_params=_params("parallel"),
        name="merge",
    )(ya, yb, gates, gates, wa, wb, wo, x_all, mod_l, nw, mod_l, mod_l, wr, br)


def _route(logits):
    n = logits.shape[0]
    gl = logits[:, :N_GROUPS]
    el = logits[:, N_GROUPS:N_GROUPS + N_EXPERTS].reshape(n, N_GROUPS, EXPERTS_PER_GROUP)
    g = jnp.argmax(gl, axis=-1).astype(jnp.int32)
    pg = 1.0 / jnp.sum(jnp.exp(gl - jnp.max(gl, axis=-1, keepdims=True)), axis=-1, keepdims=True)
    in_group = jnp.arange(N_GROUPS, dtype=jnp.int32)[None, :, None] == g[:, None, None]
    el_g = jnp.sum(jnp.where(in_group, el, 0.0), axis=1)
    tv, ti = lax.top_k(el_g, TOP_K)
    wts = jax.nn.softmax(tv, axis=-1) * pg
    eid = g[:, None] * EXPERTS_PER_GROUP + ti.astype(jnp.int32)
    return eid, wts


def _dispatch(eid, tok_rows, row_scale):
    n = eid.shape[0]
    a = n * TOP_K
    e_flat = eid.reshape(a)
    onehot = (e_flat[:, None] == jnp.arange(N_EXPERTS, dtype=jnp.int32)[None, :]).astype(jnp.int32)
    csum = jnp.cumsum(onehot, axis=0)
    counts = csum[-1]
    padded = (counts + MOE_ROWS - 1) // MOE_ROWS * MOE_ROWS
    pad_end = jnp.cumsum(padded)
    pad_start = pad_end - padded
    dest = jnp.sum(onehot * (csum - 1 + pad_start[None, :]), axis=1)
    n_grid = a // MOE_ROWS + N_EXPERTS - 1 + GATHER_AHEAD
    slot_row = jnp.zeros((n_grid * MOE_ROWS,), jnp.int32).at[dest].set(jnp.repeat(tok_rows * row_scale, TOP_K))
    blk = jnp.arange(n_grid, dtype=jnp.int32)
    n_valid = (pad_end[-1] // MOE_ROWS).astype(jnp.int32)
    blk_c = jnp.minimum(blk, n_valid - 1)
    blk_e = jnp.clip(jnp.searchsorted(pad_end, blk_c * MOE_ROWS, side="right"), 0, N_EXPERTS - 1).astype(jnp.int32)
    experts = jnp.arange(N_EXPERTS, dtype=jnp.int32)
    used = counts > 0
    first_used_from = lax.cummin(jnp.where(used, experts, N_EXPERTS), reverse=True)
    next_used = jnp.concatenate([first_used_from[1:], jnp.full((1,), N_EXPERTS, jnp.int32)])
    next_used = jnp.where(next_used == N_EXPERTS, -1, next_used)
    parity = (jnp.cumsum(used.astype(jnp.int32)) - 1) % 2
    return (dest, blk_e, next_used[blk_e].astype(jnp.int32), parity[blk_e].astype(jnp.int32), slot_row,
            n_valid.reshape(1))


def _experts_kernel(blk_e, blk_next, blk_par, slot_row, n_valid,
                    h_hbm, w1_hbm, w3_hbm, w2_hbm, o_ref,
                    xg, xsem, ws1, ws3, ws2, wsem, w1b, w3b, w2b, *, layer):
    i = pl.program_id(0)
    nv = n_valid[0]
    chunks = xg.shape[1] // MOE_ROWS
    e = blk_e[i]
    par = blk_par[i]
    fresh = jnp.logical_or(i == 0, e != blk_e[jnp.maximum(i - 1, 0)])

    def weight_copies(expert, p):
        return (pltpu.make_async_copy(w1_hbm.at[layer, expert], ws1.at[p], wsem.at[p, 0]),
                pltpu.make_async_copy(w3_hbm.at[layer, expert], ws3.at[p], wsem.at[p, 1]),
                pltpu.make_async_copy(w2_hbm.at[layer, expert], ws2.at[p], wsem.at[p, 2]))

    def for_rows(blk, fn):
        buf = lax.rem(blk, GATHER_AHEAD + 1)
        base = blk * MOE_ROWS
        for j in range(MOE_ROWS):
            r = pl.multiple_of(slot_row[base + j], chunks)
            fn(pltpu.make_async_copy(h_hbm.at[pl.ds(r, chunks), :], xg.at[buf, pl.ds(j * chunks, chunks), :],
                                     xsem.at[buf]))

    @pl.when(i == 0)
    def _():
        for cp in weight_copies(e, par):
            cp.start(priority=WEIGHT_DMA_PRIORITY)
        for blk in range(GATHER_AHEAD):
            for_rows(blk, lambda cp: cp.start())

    @pl.when(jnp.logical_and(i >= nv, i < nv + GATHER_AHEAD))
    def _():
        for_rows(i, lambda cp: cp.wait())

    @pl.when(fresh)
    def _():
        for cp in weight_copies(e, par):
            cp.wait()
        nxt = blk_next[i]

        @pl.when(nxt >= 0)
        def _():
            for cp in weight_copies(nxt, 1 - par):
                cp.start(priority=WEIGHT_DMA_PRIORITY)

        w1b[...] = ws1[par].astype(BF16)
        w3b[...] = ws3[par].astype(BF16)
        w2b[...] = ws2[par].astype(BF16)

    @pl.when(i < nv)
    def _():
        for_rows(i, lambda cp: cp.wait())
        x = jnp.concatenate(
            [xg[lax.rem(i, GATHER_AHEAD + 1), pl.ds(c, MOE_ROWS, stride=chunks), :] for c in range(chunks)],
            axis=1).astype(BF16)
        for_rows(i + GATHER_AHEAD, lambda cp: cp.start())
        a = jnp.dot(x, w1b[...], preferred_element_type=F32)
        b = jnp.dot(x, w3b[...], preferred_element_type=F32)
        hid = (a * jax.nn.sigmoid(a)) * b
        o_ref[...] = jnp.dot(hid.astype(BF16), w2b[...], preferred_element_type=F32).astype(o_ref.dtype)


def _experts(h2, w1, w3, w2, layer, blk_e, blk_next, blk_par, slot_row, n_valid):
    D, de = w1.shape[-2:]
    n_blk = blk_e.shape[0]
    any_spec = pl.BlockSpec(memory_space=pl.ANY)
    return pl.pallas_call(
        functools.partial(_experts_kernel, layer=layer),
        grid_spec=pltpu.PrefetchScalarGridSpec(
            num_scalar_prefetch=5, grid=(n_blk,),
            in_specs=[any_spec, any_spec, any_spec, any_spec],
            out_specs=pl.BlockSpec((MOE_ROWS, D), lambda i, *s: (jnp.minimum(i, s[4][0] - 1), 0)),
            scratch_shapes=[
                pltpu.VMEM((GATHER_AHEAD + 1, MOE_ROWS * (D // LANES), LANES), F32),
                pltpu.SemaphoreType.DMA((GATHER_AHEAD + 1,)),
                pltpu.VMEM((2, D, de), F32), pltpu.VMEM((2, D, de), F32), pltpu.VMEM((2, de, D), F32),
                pltpu.SemaphoreType.DMA((2, 3)),
                pltpu.VMEM((D, de), BF16), pltpu.VMEM((D, de), BF16), pltpu.VMEM((de, D), BF16),
            ],
        ),
        out_shape=jax.ShapeDtypeStruct((n_blk * MOE_ROWS, D), BF16),
        compiler_params=_params("arbitrary"),
        name="experts",
    )(blk_e, blk_next, blk_par, slot_row, n_valid, h2, w1, w3, w2)


def _moe_mix(x_ref, y0_ref, y1_ref, w_ref, g2_ref):
    w = w_ref[...]
    y = w[:, 0:1] * y0_ref[...].astype(F32) + w[:, 1:2] * y1_ref[...].astype(F32)
    return x_ref[...] + g2_ref[...] * y


def _combine_next_kernel(x_ref, y0_ref, y1_ref, w_ref, g2_ref, nw_ref, sh_ref, sc_ref, xo_ref, h_ref):
    xn = _moe_mix(x_ref, y0_ref, y1_ref, w_ref, g2_ref)
    xo_ref[...] = xn
    h_ref[...] = ((_rms(xn) * nw_ref[...]) * (1.0 + sc_ref[...]) + sh_ref[...]).astype(BF16)


def _combine_final_kernel(x_ref, y0_ref, y1_ref, w_ref, g2_ref, nf_ref, o_ref):
    o_ref[...] = _rms(_moe_mix(x_ref, y0_ref, y1_ref, w_ref, g2_ref)) * nf_ref[...]


def _token_specs(n_tokens, d):
    tiles = n_tokens // ROW_TILE
    return [pl.BlockSpec((ROW_TILE, d), lambda i: (i, 0)), pl.BlockSpec((ROW_TILE, d), lambda i: (i + tiles, 0)),
            pl.BlockSpec((ROW_TILE, TOP_K), lambda i: (i, 0))]


def _combine_next(x_new, ysel, wts, mod_l, nw_next, mod_next, rows):
    M, D = x_new.shape
    return pl.pallas_call(
        _combine_next_kernel,
        grid=(rows.n,),
        in_specs=[rows.row_spec(D), *_token_specs(M, D), rows.mod_spec(D, 5),
                  _const_spec((1, D)), rows.mod_spec(D, 0), rows.mod_spec(D, 1)],
        out_specs=[rows.row_spec(D), rows.row_spec(D)],
        out_shape=[jax.ShapeDtypeStruct((M, D), F32), jax.ShapeDtypeStruct((M, D), BF16)],
        compiler_params=_params("parallel"),
        name="combine_next",
    )(x_new, ysel, ysel, wts, mod_l, nw_next, mod_next, mod_next)


def _combine_final(x_new, ysel, wts, mod_l, nf, rows):
    M, D = x_new.shape
    n = wts.shape[0]
    return pl.pallas_call(
        _combine_final_kernel,
        grid=(rows.n,),
        in_specs=[rows.row_spec(D), *_token_specs(n, D), rows.mod_spec(D, 5), _const_spec((1, D))],
        out_specs=pl.BlockSpec((ROW_TILE, D), lambda i: (i, 0)),
        out_shape=jax.ShapeDtypeStruct((n, D), F32),
        compiler_params=_params("parallel"),
        name="combine_final",
    )(x_new, ysel, ysel, wts, mod_l, nf)


def _rope_tables(ctx_len, seq):
    quarter = HEAD_DIM // 4
    freqs = ROPE_BASE ** (-jnp.arange(quarter, dtype=F32) / quarter)
    pos = jnp.arange(seq, dtype=jnp.int32)
    row = (pos // GRID_W).astype(F32)[:, None] * freqs[None, :]
    col = (pos % GRID_W).astype(F32)[:, None] * freqs[None, :]
    cos = jnp.concatenate([jnp.cos(row), jnp.cos(row), jnp.cos(col), jnp.cos(col)], axis=-1)
    sin = jnp.concatenate([-jnp.sin(row), jnp.sin(row), -jnp.sin(col), jnp.sin(col)], axis=-1)
    cos = jnp.concatenate([jnp.ones((ctx_len, HEAD_DIM), F32), cos], axis=0)
    sin = jnp.concatenate([jnp.zeros((ctx_len, HEAD_DIM), F32), sin], axis=0)
    return cos, sin


def kernel(x, c, ctx, c_ctx, w_mod, b_mod, norm_mix, norm_ffn, w_in, qn_a, kn_a, sink_b, w_br_a, w_br_b, w_out, w_rg, b_rg, w_re, b_re, w1, w3, w2, norm_final):
    B, S, D = x.shape
    C = ctx.shape[1]
    L = w_mod.shape[0]
    T = C + S
    M = B * T
    assert C == ROW_TILE and S % ROW_TILE == 0 and T % PROJ_ROWS == 0 and B < MOD_ROWS
    tiles_per_batch = T // ROW_TILE

    cc = jnp.concatenate([c, c_ctx[None, :], jnp.zeros((MOD_ROWS - B - 1, D), F32)], axis=0)
    mod = _mod_vectors(cc, w_mod, b_mod).reshape(L, MOD_ROWS * 6, 1, D)
    cos, sin = _rope_tables(C, S)
    x_all = jnp.concatenate([ctx, x], axis=1).reshape(M, D)
    latent_rows = (jnp.arange(B, dtype=jnp.int32)[:, None] * T + C
                   + jnp.arange(S, dtype=jnp.int32)[None, :]).reshape(B * S)
    all_rows = jnp.arange(M, dtype=jnp.int32)

    q_kinds = lambda kind: tuple((kind, 0, k) for k in range(PROJ_COLS // HEAD_DIM))
    kv_kinds = lambda kind: ((kind, 0, 0), (kind, 0, 1), ("v", 1, 0), ("v", 1, 1))
    col = {"qa": 0, "kva": WIDTH, "qb": WIDTH + 2 * KVW, "kvb": 2 * WIDTH + 2 * KVW, "gates": 2 * WIDTH + 4 * KVW}

    every = _Rows(B, tiles_per_batch, latent_only=False)
    latent = _Rows(B, tiles_per_batch, latent_only=True)
    h = _norm_mod(x_all, norm_mix[0][None, :], mod[0], every)
    out = None
    for l in range(L):
        last = l == L - 1
        rows = latent if last else every
        proj = functools.partial(_proj, h, w_in, l, cos, sin, tiles_per_batch=tiles_per_batch)
        (qa,) = proj(qn_a[l][None, :], col["qa"], q_kinds("q_norm"), [(PROJ_COLS, WIDTH // PROJ_COLS)], name="proj_qa")
        ka, va = proj(kn_a[l][None, :], col["kva"], kv_kinds("k_norm"), [(KVW, 1), (KVW, 1)], name="proj_kva")
        (qb,) = proj(qn_a[l][None, :], col["qb"], q_kinds("q_rope"), [(PROJ_COLS, WIDTH // PROJ_COLS)], name="proj_qb")
        kb, vb = proj(kn_a[l][None, :], col["kvb"], kv_kinds("k_rope"), [(KVW, 1), (KVW, 1)], name="proj_kvb")
        (gates,) = proj(qn_a[l][None, :], col["gates"], q_kinds("gate"), [(PROJ_COLS, 2 * D // PROJ_COLS)], name="proj_gates")

        ya = _attention(qa, ka, va, None, B, T, C, last, window=False)
        yb = _attention(qb, kb, vb, sink_b[l], B, T, C, last, window=True)

        w_router = jnp.concatenate(
            [w_rg[l], w_re[l], jnp.zeros((D, ROUTER_LANES - N_GROUPS - N_EXPERTS), F32)], axis=1)
        w_router_hi = w_router.astype(BF16)
        w_router = jnp.concatenate([w_router_hi, (w_router - w_router_hi.astype(F32)).astype(BF16)], axis=1)
        b_router = jnp.concatenate(
            [b_rg[l], b_re[l], jnp.zeros((ROUTER_LANES - N_GROUPS - N_EXPERTS,), F32)])[None, :]
        x_new, h2, logits = _merge(ya, yb, gates, w_br_a[l].astype(BF16), w_br_b[l].astype(BF16),
                                   w_out[l].astype(BF16), x_all, mod[l], norm_ffn[l][None, :],
                                   w_router, b_router, rows)

        tok_rows = latent_rows if last else all_rows
        eid, wts = _route(logits[tok_rows] if last else logits)
        dest, *plan = _dispatch(eid, tok_rows, D // LANES)
        ybuf = _experts(h2, w1, w3, w2, l, *plan)
        ysel = ybuf[dest.reshape(-1, TOP_K).T.reshape(-1)]
        if last:
            out = _combine_final(x_new, ysel, wts, mod[l], norm_final[None, :], rows).reshape(B, S, D)
        else:
            x_all, h = _combine_next(x_new, ysel, wts, mod[l], norm_mix[l + 1][None, :], mod[l + 1], rows)
    return out
```

```python
import functools

import jax
import jax.numpy as jnp
from jax import lax
from jax.experimental import pallas as pl
from jax.experimental.pallas import tpu as pltpu

F32 = jnp.float32
BF16 = jnp.bfloat16

GRID_W = 64
HEAD_DIM = 128
LANES = 128
HEADS = 8
KV_HEADS = 2
GROUP = HEADS // KV_HEADS
WIDTH = HEADS * HEAD_DIM
KVW = KV_HEADS * HEAD_DIM
WINDOW = 128
ROPE_BASE = 10000.0
ATTN_SCALE = HEAD_DIM ** -0.5
LOG2E = 1.4426950408889634
N_GROUPS = 8
EXPERTS_PER_GROUP = 8
N_EXPERTS = N_GROUPS * EXPERTS_PER_GROUP
TOP_K = 2
EPS = 1e-6
NEG_INF = -1e30

ROW_TILE = 256
PROJ_ROWS = 768
PROJ_COLS = 512
MOE_ROWS = 128
GATHER_AHEAD = 2
WEIGHT_DMA_PRIORITY = 1
ROUTER_LANES = 128
DENSE_CHAIN_HEADS = 4
WINDOW_CHAIN_HEADS = 1
MOD_ROWS = 8
VMEM_LIMIT = 56 * 1024 * 1024


def _params(*sem):
    return pltpu.CompilerParams(dimension_semantics=sem, vmem_limit_bytes=VMEM_LIMIT)


def _rms(y):
    return y * lax.rsqrt(jnp.mean(y * y, axis=-1, keepdims=True) + EPS)


def _mod_kernel(c_ref, w_ref, b_ref, o_ref):
    c = c_ref[...]
    a = (c * jax.nn.sigmoid(c)).astype(BF16)
    o_ref[...] = jnp.dot(a, w_ref[...].astype(BF16), preferred_element_type=F32) + b_ref[...]


def _mod_vectors(cc, w_mod, b_mod):
    L, D, N = w_mod.shape
    tn = 1024
    return pl.pallas_call(
        _mod_kernel,
        grid=(L, N // tn),
        in_specs=[
            pl.BlockSpec((MOD_ROWS, D), lambda l, j: (0, 0)),
            pl.BlockSpec((None, D, tn), lambda l, j: (l, 0, j)),
            pl.BlockSpec((None, 1, tn), lambda l, j: (l, 0, j)),
        ],
        out_specs=pl.BlockSpec((None, MOD_ROWS, tn), lambda l, j: (l, 0, j)),
        out_shape=jax.ShapeDtypeStruct((L, MOD_ROWS, N), F32),
        compiler_params=_params("parallel", "parallel"),
        name="mod_vectors",
    )(cc, w_mod, b_mod.reshape(L, 1, N))


class _Rows:
    def __init__(self, batch, tiles_per_batch, latent_only):
        self.batch = batch
        self.tpb = tiles_per_batch
        self.latent_only = latent_only
        self.n = batch * (tiles_per_batch - 1 if latent_only else tiles_per_batch)

    def tile(self, i):
        if self.latent_only:
            per = self.tpb - 1
            return (i // per) * self.tpb + 1 + i % per
        return i

    def mod_row(self, i):
        if self.latent_only:
            return i // (self.tpb - 1)
        return jnp.where(i % self.tpb == 0, self.batch, i // self.tpb)

    def row_spec(self, width):
        return pl.BlockSpec((ROW_TILE, width), lambda i: (self.tile(i), 0))

    def mod_spec(self, d, k):
        return pl.BlockSpec((None, 1, d), lambda i: (self.mod_row(i) * 6 + k, 0, 0))


def _const_spec(shape):
    zeros = (0,) * len(shape)
    return pl.BlockSpec(shape, lambda *_: zeros, pipeline_mode=pl.Buffered(1))


def _norm_mod_kernel(x_ref, ctx_ref, nw_ref, sh_ref, sc_ref, xo_ref, h_ref):
    def emit(src_ref):
        x = src_ref[...]
        xo_ref[...] = x
        h_ref[...] = ((_rms(x) * nw_ref[...]) * (1.0 + sc_ref[...]) + sh_ref[...]).astype(BF16)

    is_ctx = pl.program_id(1) == 0
    pl.when(is_ctx)(lambda: emit(ctx_ref))
    pl.when(jnp.logical_not(is_ctx))(lambda: emit(x_ref))


def _norm_mod(x, ctx, nw, mod_l):
    B, S, D = x.shape
    tiles = 1 + S // ROW_TILE
    mod_spec = lambda k: pl.BlockSpec((None, 1, D), lambda b, t: (jnp.where(t == 0, B, b) * 6 + k, 0, 0))
    out_spec = pl.BlockSpec((ROW_TILE, D), lambda b, t: (b * tiles + t, 0))
    return pl.pallas_call(
        _norm_mod_kernel,
        grid=(B, tiles),
        in_specs=[pl.BlockSpec((None, ROW_TILE, D), lambda b, t: (b, jnp.maximum(t - 1, 0), 0)),
                  pl.BlockSpec((None, ROW_TILE, D), lambda b, t: (b, 0, 0)),
                  _const_spec((1, D)), mod_spec(0), mod_spec(1)],
        out_specs=[out_spec, out_spec],
        out_shape=[jax.ShapeDtypeStruct((B * tiles * ROW_TILE, D), F32),
                   jax.ShapeDtypeStruct((B * tiles * ROW_TILE, D), BF16)],
        compiler_params=_params("parallel", "parallel"),
        name="norm_mod",
    )(x, ctx, nw, mod_l, mod_l)


def _rope(y, cos, sin):
    lane = lax.broadcasted_iota(jnp.int32, y.shape, 1)
    quarter = HEAD_DIM // 4
    partner = jnp.where((lane & quarter) == 0,
                        pltpu.roll(y, HEAD_DIM - quarter, 1), pltpu.roll(y, quarter, 1))
    return y * cos + partner * sin


def _proj_kernel(h_ref, w_ref, cos_ref, sin_ref, gain_ref, *refs, kinds):
    *out_refs, wb_ref = refs

    @pl.when(pl.program_id(1) == 0)
    def _():
        wb_ref[...] = w_ref[...].astype(BF16)

    acc = jnp.dot(h_ref[...], wb_ref[...], preferred_element_type=F32)
    for c, (kind, out_idx, out_chunk) in enumerate(kinds):
        y = acc[:, c * HEAD_DIM:(c + 1) * HEAD_DIM]
        if kind in ("q_norm", "k_norm"):
            y = _rms(y) * gain_ref[...]
        if kind in ("q_norm", "k_norm", "q_rope", "k_rope"):
            y = _rope(y, cos_ref[...], sin_ref[...])
        if kind in ("q_norm", "q_rope"):
            y = y * (ATTN_SCALE * LOG2E)
        if kind == "gate":
            y = jax.nn.sigmoid(y)
        out_refs[out_idx][:, out_chunk * HEAD_DIM:(out_chunk + 1) * HEAD_DIM] = y.astype(BF16)


def _proj(h, w_in, layer, cos, sin, gain, col0, kinds, out_widths, tiles_per_batch, name):
    M, D = h.shape
    n_col_tiles = out_widths[0][1]
    cb0 = col0 // PROJ_COLS
    proj_tiles_per_batch = tiles_per_batch * ROW_TILE // PROJ_ROWS
    return pl.pallas_call(
        functools.partial(_proj_kernel, kinds=kinds),
        grid=(n_col_tiles, M // PROJ_ROWS),
        in_specs=[
            pl.BlockSpec((PROJ_ROWS, D), lambda j, i: (i, 0)),
            pl.BlockSpec((None, D, PROJ_COLS), lambda j, i: (layer, 0, cb0 + j)),
            pl.BlockSpec((PROJ_ROWS, HEAD_DIM), lambda j, i: (i % proj_tiles_per_batch, 0)),
            pl.BlockSpec((PROJ_ROWS, HEAD_DIM), lambda j, i: (i % proj_tiles_per_batch, 0)),
            pl.BlockSpec((1, HEAD_DIM), lambda j, i: (0, 0)),
        ],
        out_specs=[pl.BlockSpec((PROJ_ROWS, w), lambda j, i: (i, j)) for w, _ in out_widths],
        out_shape=[jax.ShapeDtypeStruct((M, w * n), BF16) for w, n in out_widths],
        scratch_shapes=[pltpu.VMEM((D, PROJ_COLS), BF16)],
        compiler_params=_params("parallel", "arbitrary"),
        name=name,
    )(h, w_in, cos, sin, gain)


def _stack_heads(q_ref, g0, n):
    return jnp.concatenate([q_ref[:, g * HEAD_DIM:(g + 1) * HEAD_DIM] for g in range(g0, g0 + n)], axis=0)


def _write_heads(o_ref, g0, n, o):
    rows = o_ref.shape[0]
    for j in range(n):
        o_ref[:, (g0 + j) * HEAD_DIM:(g0 + j + 1) * HEAD_DIM] = o[j * rows:(j + 1) * rows].astype(o_ref.dtype)


def _scores(q, k):
    return lax.dot_general(q, k, (((1,), (1,)), ((), ())), preferred_element_type=F32)


def _online_step(q, k, v, m, l, acc):
    s = _scores(q, k)
    m_new = jnp.maximum(m, jnp.max(s, axis=-1, keepdims=True))
    alpha = jnp.exp2(m - m_new)
    p = jnp.exp2(s - m_new)
    l = alpha * l + jnp.sum(p, axis=-1, keepdims=True)
    acc = alpha * acc + jnp.dot(p.astype(BF16), v, preferred_element_type=F32)
    return m_new, l, acc


def _dense_attn_kernel(q_ref, k_ref, v_ref, o_ref, *, ctx_len, total_len, key_chunk, ctx_tile):
    def run(n_keys, chunk):
        for g0 in range(0, GROUP, DENSE_CHAIN_HEADS):
            q = _stack_heads(q_ref, g0, DENSE_CHAIN_HEADS)
            r = q.shape[0]
            m = jnp.full((r, 1), -jnp.inf, F32)
            l = jnp.zeros((r, 1), F32)
            acc = jnp.zeros((r, HEAD_DIM), F32)
            for c in range(n_keys // chunk):
                m, l, acc = _online_step(q, k_ref[c * chunk:(c + 1) * chunk, :],
                                         v_ref[c * chunk:(c + 1) * chunk, :], m, l, acc)
            _write_heads(o_ref, g0, DENSE_CHAIN_HEADS, acc * (1.0 / l))

    if ctx_tile:
        is_ctx = pl.program_id(2) == 0
        pl.when(is_ctx)(lambda: run(ctx_len, ctx_len))
        pl.when(jnp.logical_not(is_ctx))(lambda: run(total_len, key_chunk))
    else:
        run(total_len, key_chunk)


def _window_attn_kernel(sink_ref, q_ref, k_ref, v_ref, o_ref, *, ctx_len, total_len, ctx_tile, q_off):
    kvh = pl.program_id(1)
    tq = q_ref.shape[0]
    span = tq + 2 * WINDOW
    n = WINDOW_CHAIN_HEADS

    def attend(parts):
        for g0 in range(0, GROUP, n):
            q = _stack_heads(q_ref, g0, n)
            sink = jnp.concatenate(
                [jnp.full((tq, 1), sink_ref[kvh * GROUP + g] * LOG2E, F32) for g in range(g0, g0 + n)], axis=0)
            scores = []
            m = sink
            for k, _, bias in parts:
                s = _scores(q, k)
                if bias is not None:
                    s = s + bias
                scores.append(s)
                m = jnp.maximum(m, jnp.max(s, axis=-1, keepdims=True))
            l = jnp.exp2(sink - m)
            o = jnp.zeros((n * tq, HEAD_DIM), F32)
            for s, (_, v, _) in zip(scores, parts):
                p = jnp.exp2(s - m)
                l = l + jnp.sum(p, axis=-1, keepdims=True)
                o = o + jnp.dot(p.astype(BF16), v, preferred_element_type=F32)
            _write_heads(o_ref, g0, n, o * (1.0 / l))

    def ctx_part():
        return k_ref[0:ctx_len, :], v_ref[0:ctx_len, :], None

    def run_latent():
        q0 = (pl.program_id(2) + q_off) * tq - ctx_len
        n_latent = total_len - ctx_len
        start = pl.multiple_of(jnp.clip(q0 - WINDOW, 0, n_latent - span), WINDOW)
        row = lax.broadcasted_iota(jnp.int32, (n * tq, span), 0) & (tq - 1)
        col = lax.broadcasted_iota(jnp.int32, (n * tq, span), 1)
        bias = jnp.where(jnp.abs(col - row + (start - q0)) <= WINDOW, 0.0, NEG_INF).astype(F32)
        attend([ctx_part(), (k_ref[pl.ds(ctx_len + start, span), :], v_ref[pl.ds(ctx_len + start, span), :], bias)])

    if ctx_tile:
        is_ctx = pl.program_id(2) == 0
        pl.when(is_ctx)(lambda: attend([ctx_part()]))
        pl.when(jnp.logical_not(is_ctx))(run_latent)
    else:
        run_latent()


def _attention(q, k, v, sink, batch, total_len, ctx_len, latent_only, window):
    M = q.shape[0]
    q3, k3, v3 = (a.reshape(batch, total_len, a.shape[1]) for a in (q, k, v))
    tiles = total_len // ROW_TILE
    q_off = 1 if latent_only else 0
    gw = GROUP * HEAD_DIM
    q_spec = pl.BlockSpec((None, ROW_TILE, gw), lambda b, h, i, *_: (b, i + q_off, h))
    kv_spec = pl.BlockSpec((None, total_len, HEAD_DIM), lambda b, h, i, *_: (b, 0, h))
    grid = (batch, KV_HEADS, tiles - q_off)
    out_shape = jax.ShapeDtypeStruct(q3.shape, BF16)
    if window:
        kern = functools.partial(_window_attn_kernel, ctx_len=ctx_len, total_len=total_len,
                                 ctx_tile=not latent_only, q_off=q_off)
        out = pl.pallas_call(
            kern,
            grid_spec=pltpu.PrefetchScalarGridSpec(
                num_scalar_prefetch=1, grid=grid,
                in_specs=[q_spec, kv_spec, kv_spec], out_specs=q_spec),
            out_shape=out_shape,
            compiler_params=_params("parallel", "parallel", "parallel"),
            name="window_attention",
        )(sink, q3, k3, v3)
    else:
        kern = functools.partial(_dense_attn_kernel, ctx_len=ctx_len, total_len=total_len,
                                 key_chunk=total_len // 3, ctx_tile=not latent_only)
        out = pl.pallas_call(
            kern, grid=grid, in_specs=[q_spec, kv_spec, kv_spec], out_specs=q_spec,
            out_shape=out_shape,
            compiler_params=_params("parallel", "parallel", "parallel"),
            name="dense_attention",
        )(q3, k3, v3)
    return out.reshape(M, q.shape[1])


def _merge_kernel(ya_ref, yb_ref, ga_ref, gb_ref, wa_ref, wb_ref, wo_ref, x_ref, g1_ref,
                  nw_ref, sh_ref, sc_ref, wr_ref, br_ref, xo_ref, h_ref, lg_ref):
    a = jnp.dot(ya_ref[...], wa_ref[...], preferred_element_type=F32)
    b = jnp.dot(yb_ref[...], wb_ref[...], preferred_element_type=F32)
    m = ga_ref[...].astype(F32) * a + gb_ref[...].astype(F32) * b
    o = jnp.dot(m.astype(BF16), wo_ref[...], preferred_element_type=F32)
    xn = x_ref[...] + g1_ref[...] * o
    xo_ref[...] = xn
    h = (_rms(xn) * nw_ref[...]) * (1.0 + sc_ref[...]) + sh_ref[...]
    chunks = h.shape[1] // LANES
    for c in range(chunks):
        h_ref[pl.ds(c, h.shape[0], stride=chunks), :] = h[:, c * LANES:(c + 1) * LANES]
    h_hi = h.astype(BF16)
    h_lo = (h - h_hi.astype(F32)).astype(BF16)
    lg = jnp.dot(h_hi, wr_ref[...], preferred_element_type=F32)
    lg_lo = jnp.dot(h_lo, wr_ref[:, :ROUTER_LANES], preferred_element_type=F32)
    lg_ref[...] = lg[:, :ROUTER_LANES] + (lg[:, ROUTER_LANES:] + lg_lo) + br_ref[...]


def _merge(ya, yb, gates, wa, wb, wo, x_all, mod_l, nw, wr, br, rows):
    M, D = x_all.shape
    W = ya.shape[1]
    return pl.pallas_call(
        _merge_kernel,
        grid=(rows.n,),
        in_specs=[
            rows.row_spec(W), rows.row_spec(W),
            pl.BlockSpec((ROW_TILE, D), lambda i: (rows.tile(i), 0)),
            pl.BlockSpec((ROW_TILE, D), lambda i: (rows.tile(i), 1)),
            _const_spec((W, D)), _const_spec((W, D)), _const_spec((D, D)),
            rows.row_spec(D), rows.mod_spec(D, 2),
            _const_spec((1, D)), rows.mod_spec(D, 3), rows.mod_spec(D, 4),
            _const_spec((D, 2 * ROUTER_LANES)), _const_spec((1, ROUTER_LANES)),
        ],
        out_specs=[rows.row_spec(D),
                   pl.BlockSpec((ROW_TILE * (D // LANES), LANES), lambda i: (rows.tile(i), 0)),
                   rows.row_spec(ROUTER_LANES)],
        out_shape=[jax.ShapeDtypeStruct((M, D), F32), jax.ShapeDtypeStruct((M * (D // LANES), LANES), F32),
                   jax.ShapeDtypeStruct((M, ROUTER_LANES), F32)],
        compiler_params=_params("parallel"),
        name="merge",
    )(ya, yb, gates, gates, wa, wb, wo, x_all, mod_l, nw, mod_l, mod_l, wr, br)


def _route(logits):
    n = logits.shape[0]
    gl = logits[:, :N_GROUPS]
    el = logits[:, N_GROUPS:N_GROUPS + N_EXPERTS].reshape(n, N_GROUPS, EXPERTS_PER_GROUP)
    g = jnp.argmax(gl, axis=-1).astype(jnp.int32)
    pg = 1.0 / jnp.sum(jnp.exp(gl - jnp.max(gl, axis=-1, keepdims=True)), axis=-1, keepdims=True)
    in_group = jnp.arange(N_GROUPS, dtype=jnp.int32)[None, :, None] == g[:, None, None]
    el_g = jnp.sum(jnp.where(in_group, el, 0.0), axis=1)
    tv, ti = lax.top_k(el_g, TOP_K)
    wts = jax.nn.softmax(tv, axis=-1) * pg
    eid = g[:, None] * EXPERTS_PER_GROUP + ti.astype(jnp.int32)
    return eid, wts


def _dispatch(eid, tok_rows, row_scale):
    n = eid.shape[0]
    a = n * TOP_K
    e_flat = eid.reshape(a)
    onehot = (e_flat[:, None] == jnp.arange(N_EXPERTS, dtype=jnp.int32)[None, :]).astype(jnp.int32)
    csum = jnp.cumsum(onehot, axis=0)
    counts = csum[-1]
    padded = (counts + MOE_ROWS - 1) // MOE_ROWS * MOE_ROWS
    pad_end = jnp.cumsum(padded)
    pad_start = pad_end - padded
    dest = jnp.sum(onehot * (csum - 1 + pad_start[None, :]), axis=1)
    n_grid = a // MOE_ROWS + N_EXPERTS - 1 + GATHER_AHEAD
    slot_row = jnp.zeros((n_grid * MOE_ROWS,), jnp.int32).at[dest].set(jnp.repeat(tok_rows * row_scale, TOP_K))
    blk = jnp.arange(n_grid, dtype=jnp.int32)
    n_valid = (pad_end[-1] // MOE_ROWS).astype(jnp.int32)
    blk_c = jnp.minimum(blk, n_valid - 1)
    blk_e = jnp.clip(jnp.searchsorted(pad_end, blk_c * MOE_ROWS, side="right"), 0, N_EXPERTS - 1).astype(jnp.int32)
    experts = jnp.arange(N_EXPERTS, dtype=jnp.int32)
    used = counts > 0
    first_used_from = lax.cummin(jnp.where(used, experts, N_EXPERTS), reverse=True)
    next_used = jnp.concatenate([first_used_from[1:], jnp.full((1,), N_EXPERTS, jnp.int32)])
    next_used = jnp.where(next_used == N_EXPERTS, -1, next_used)
    parity = (jnp.cumsum(used.astype(jnp.int32)) - 1) % 2
    return (dest, blk_e, next_used[blk_e].astype(jnp.int32), parity[blk_e].astype(jnp.int32), slot_row,
            n_valid.reshape(1))


def _experts_kernel(blk_e, blk_next, blk_par, slot_row, n_valid,
                    h_hbm, w1_hbm, w3_hbm, w2_hbm, o_ref,
                    xg, xsem, ws1, ws3, ws2, wsem, w1b, w3b, w2b, *, layer):
    i = pl.program_id(0)
    nv = n_valid[0]
    chunks = xg.shape[1] // MOE_ROWS
    e = blk_e[i]
    par = blk_par[i]
    fresh = jnp.logical_or(i == 0, e != blk_e[jnp.maximum(i - 1, 0)])

    def weight_copies(expert, p):
        return (pltpu.make_async_copy(w1_hbm.at[layer, expert], ws1.at[p], wsem.at[p, 0]),
                pltpu.make_async_copy(w3_hbm.at[layer, expert], ws3.at[p], wsem.at[p, 1]),
                pltpu.make_async_copy(w2_hbm.at[layer, expert], ws2.at[p], wsem.at[p, 2]))

    def for_rows(blk, fn):
        buf = lax.rem(blk, GATHER_AHEAD + 1)
        base = blk * MOE_ROWS
        for j in range(MOE_ROWS):
            r = pl.multiple_of(slot_row[base + j], chunks)
            fn(pltpu.make_async_copy(h_hbm.at[pl.ds(r, chunks), :], xg.at[buf, pl.ds(j * chunks, chunks), :],
                                     xsem.at[buf]))

    @pl.when(i == 0)
    def _():
        for cp in weight_copies(e, par):
            cp.start(priority=WEIGHT_DMA_PRIORITY)
        for blk in range(GATHER_AHEAD):
            for_rows(blk, lambda cp: cp.start())

    @pl.when(jnp.logical_and(i >= nv, i < nv + GATHER_AHEAD))
    def _():
        for_rows(i, lambda cp: cp.wait())

    @pl.when(fresh)
    def _():
        for cp in weight_copies(e, par):
            cp.wait()
        nxt = blk_next[i]

        @pl.when(nxt >= 0)
        def _():
            for cp in weight_copies(nxt, 1 - par):
                cp.start(priority=WEIGHT_DMA_PRIORITY)

        w1b[...] = ws1[par].astype(BF16)
        w3b[...] = ws3[par].astype(BF16)
        w2b[...] = ws2[par].astype(BF16)

    @pl.when(i < nv)
    def _():
        for_rows(i, lambda cp: cp.wait())
        x = jnp.concatenate(
            [xg[lax.rem(i, GATHER_AHEAD + 1), pl.ds(c, MOE_ROWS, stride=chunks), :] for c in range(chunks)],
            axis=1).astype(BF16)
        for_rows(i + GATHER_AHEAD, lambda cp: cp.start())
        a = jnp.dot(x, w1b[...], preferred_element_type=F32)
        b = jnp.dot(x, w3b[...], preferred_element_type=F32)
        hid = (a * jax.nn.sigmoid(a)) * b
        o_ref[...] = jnp.dot(hid.astype(BF16), w2b[...], preferred_element_type=F32).astype(o_ref.dtype)


def _experts(h2, w1, w3, w2, layer, blk_e, blk_next, blk_par, slot_row, n_valid):
    D, de = w1.shape[-2:]
    n_blk = blk_e.shape[0]
    any_spec = pl.BlockSpec(memory_space=pl.ANY)
    return pl.pallas_call(
        functools.partial(_experts_kernel, layer=layer),
        grid_spec=pltpu.PrefetchScalarGridSpec(
            num_scalar_prefetch=5, grid=(n_blk,),
            in_specs=[any_spec, any_spec, any_spec, any_spec],
            out_specs=pl.BlockSpec((MOE_ROWS, D), lambda i, *s: (jnp.minimum(i, s[4][0] - 1), 0)),
            scratch_shapes=[
                pltpu.VMEM((GATHER_AHEAD + 1, MOE_ROWS * (D // LANES), LANES), F32),
                pltpu.SemaphoreType.DMA((GATHER_AHEAD + 1,)),
                pltpu.VMEM((2, D, de), F32), pltpu.VMEM((2, D, de), F32), pltpu.VMEM((2, de, D), F32),
                pltpu.SemaphoreType.DMA((2, 3)),
                pltpu.VMEM((D, de), BF16), pltpu.VMEM((D, de), BF16), pltpu.VMEM((de, D), BF16),
            ],
        ),
        out_shape=jax.ShapeDtypeStruct((n_blk * MOE_ROWS, D), BF16),
        compiler_params=_params("arbitrary"),
        name="experts",
    )(blk_e, blk_next, blk_par, slot_row, n_valid, h2, w1, w3, w2)


def _moe_mix(x_ref, y0_ref, y1_ref, w_ref, g2_ref):
    w = w_ref[...]
    y = w[:, 0:1] * y0_ref[...].astype(F32) + w[:, 1:2] * y1_ref[...].astype(F32)
    return x_ref[...] + g2_ref[...] * y


def _combine_next_kernel(x_ref, y0_ref, y1_ref, w_ref, g2_ref, nw_ref, sh_ref, sc_ref, xo_ref, h_ref):
    xn = _moe_mix(x_ref, y0_ref, y1_ref, w_ref, g2_ref)
    xo_ref[...] = xn
    h_ref[...] = ((_rms(xn) * nw_ref[...]) * (1.0 + sc_ref[...]) + sh_ref[...]).astype(BF16)


def _combine_final_kernel(x_ref, y0_ref, y1_ref, w_ref, g2_ref, nf_ref, o_ref):
    o_ref[...] = _rms(_moe_mix(x_ref, y0_ref, y1_ref, w_ref, g2_ref)) * nf_ref[...]


def _token_specs(n_tokens, d):
    tiles = n_tokens // ROW_TILE
    return [pl.BlockSpec((ROW_TILE, d), lambda i: (i, 0)), pl.BlockSpec((ROW_TILE, d), lambda i: (i + tiles, 0)),
            pl.BlockSpec((ROW_TILE, TOP_K), lambda i: (i, 0))]


def _combine_next(x_new, ysel, wts, mod_l, nw_next, mod_next, rows):
    M, D = x_new.shape
    return pl.pallas_call(
        _combine_next_kernel,
        grid=(rows.n,),
        in_specs=[rows.row_spec(D), *_token_specs(M, D), rows.mod_spec(D, 5),
                  _const_spec((1, D)), rows.mod_spec(D, 0), rows.mod_spec(D, 1)],
        out_specs=[rows.row_spec(D), rows.row_spec(D)],
        out_shape=[jax.ShapeDtypeStruct((M, D), F32), jax.ShapeDtypeStruct((M, D), BF16)],
        compiler_params=_params("parallel"),
        name="combine_next",
    )(x_new, ysel, ysel, wts, mod_l, nw_next, mod_next, mod_next)


def _combine_final(x_new, ysel, wts, mod_l, nf, rows):
    M, D = x_new.shape
    n = wts.shape[0]
    return pl.pallas_call(
        _combine_final_kernel,
        grid=(rows.n,),
        in_specs=[rows.row_spec(D), *_token_specs(n, D), rows.mod_spec(D, 5), _const_spec((1, D))],
        out_specs=pl.BlockSpec((ROW_TILE, D), lambda i: (i, 0)),
        out_shape=jax.ShapeDtypeStruct((n, D), F32),
        compiler_params=_params("parallel"),
        name="combine_final",
    )(x_new, ysel, ysel, wts, mod_l, nf)


def _rope_tables(ctx_len, seq):
    quarter = HEAD_DIM // 4
    freqs = ROPE_BASE ** (-jnp.arange(quarter, dtype=F32) / quarter)
    pos = jnp.arange(seq, dtype=jnp.int32)
    row = (pos // GRID_W).astype(F32)[:, None] * freqs[None, :]
    col = (pos % GRID_W).astype(F32)[:, None] * freqs[None, :]
    cos = jnp.concatenate([jnp.cos(row), jnp.cos(row), jnp.cos(col), jnp.cos(col)], axis=-1)
    sin = jnp.concatenate([-jnp.sin(row), jnp.sin(row), -jnp.sin(col), jnp.sin(col)], axis=-1)
    cos = jnp.concatenate([jnp.ones((ctx_len, HEAD_DIM), F32), cos], axis=0)
    sin = jnp.concatenate([jnp.zeros((ctx_len, HEAD_DIM), F32), sin], axis=0)
    return cos, sin


def kernel(x, c, ctx, c_ctx, w_mod, b_mod, norm_mix, norm_ffn, w_in, qn_a, kn_a, sink_b, w_br_a, w_br_b, w_out, w_rg, b_rg, w_re, b_re, w1, w3, w2, norm_final):
    B, S, D = x.shape
    C = ctx.shape[1]
    L = w_mod.shape[0]
    T = C + S
    M = B * T
    assert C == ROW_TILE and S % ROW_TILE == 0 and T % PROJ_ROWS == 0 and B < MOD_ROWS
    tiles_per_batch = T // ROW_TILE

    cc = jnp.concatenate([c, c_ctx[None, :], jnp.zeros((MOD_ROWS - B - 1, D), F32)], axis=0)
    mod = _mod_vectors(cc, w_mod, b_mod).reshape(L, MOD_ROWS * 6, 1, D)
    cos, sin = _rope_tables(C, S)
    latent_rows = (jnp.arange(B, dtype=jnp.int32)[:, None] * T + C
                   + jnp.arange(S, dtype=jnp.int32)[None, :]).reshape(B * S)
    all_rows = jnp.arange(M, dtype=jnp.int32)

    q_kinds = lambda kind: tuple((kind, 0, k) for k in range(PROJ_COLS // HEAD_DIM))
    kv_kinds = lambda kind: ((kind, 0, 0), (kind, 0, 1), ("v", 1, 0), ("v", 1, 1))
    col = {"qa": 0, "kva": WIDTH, "qb": WIDTH + 2 * KVW, "kvb": 2 * WIDTH + 2 * KVW, "gates": 2 * WIDTH + 4 * KVW}

    every = _Rows(B, tiles_per_batch, latent_only=False)
    latent = _Rows(B, tiles_per_batch, latent_only=True)
    x_all, h = _norm_mod(x, ctx, norm_mix[0][None, :], mod[0])
    out = None
    for l in range(L):
        last = l == L - 1
        rows = latent if last else every
        proj = functools.partial(_proj, h, w_in, l, cos, sin, tiles_per_batch=tiles_per_batch)
        (qa,) = proj(qn_a[l][None, :], col["qa"], q_kinds("q_norm"), [(PROJ_COLS, WIDTH // PROJ_COLS)], name="proj_qa")
        ka, va = proj(kn_a[l][None, :], col["kva"], kv_kinds("k_norm"), [(KVW, 1), (KVW, 1)], name="proj_kva")
        (qb,) = proj(qn_a[l][None, :], col["qb"], q_kinds("q_rope"), [(PROJ_COLS, WIDTH // PROJ_COLS)], name="proj_qb")
        kb, vb = proj(kn_a[l][None, :], col["kvb"], kv_kinds("k_rope"), [(KVW, 1), (KVW, 1)], name="proj_kvb")
        (gates,) = proj(qn_a[l][None, :], col["gates"], q_kinds("gate"), [(PROJ_COLS, 2 * D // PROJ_COLS)], name="proj_gates")

        ya = _attention(qa, ka, va, None, B, T, C, last, window=False)
        yb = _attention(qb, kb, vb, sink_b[l], B, T, C, last, window=True)

        w_router = jnp.concatenate(
            [w_rg[l], w_re[l], jnp.zeros((D, ROUTER_LANES - N_GROUPS - N_EXPERTS), F32)], axis=1)
        w_router_hi = w_router.astype(BF16)
        w_router = jnp.concatenate([w_router_hi, (w_router - w_router_hi.astype(F32)).astype(BF16)], axis=1)
        b_router = jnp.concatenate(
            [b_rg[l], b_re[l], jnp.zeros((ROUTER_LANES - N_GROUPS - N_EXPERTS,), F32)])[None, :]
        x_new, h2, logits = _merge(ya, yb, gates, w_br_a[l].astype(BF16), w_br_b[l].astype(BF16),
                                   w_out[l].astype(BF16), x_all, mod[l], norm_ffn[l][None, :],
                                   w_router, b_router, rows)

        tok_rows = latent_rows if last else all_rows
        eid, wts = _route(logits[tok_rows] if last else logits)
        dest, *plan = _dispatch(eid, tok_rows, D // LANES)
        ybuf = _experts(h2, w1, w3, w2, l, *plan)
        ysel = ybuf[dest.reshape(-1, TOP_K).T.reshape(-1)]
        if last:
            out = _combine_final(x_new, ysel, wts, mod[l], norm_final[None, :], rows).reshape(B, S, D)
        else:
            x_all, h = _combine_next(x_new, ysel, wts, mod[l], norm_mix[l + 1][None, :], mod[l + 1], rows)
    return out
```

```python
import functools

import jax
import jax.numpy as jnp
from jax import lax
from jax.experimental import pallas as pl
from jax.experimental.pallas import tpu as pltpu
from jax.experimental.pallas import tpu_sc as plsc

F32 = jnp.float32
BF16 = jnp.bfloat16

GRID_W = 64
HEAD_DIM = 128
HEADS = 8
KV_HEADS = 2
GROUP = HEADS // KV_HEADS
WIDTH = HEADS * HEAD_DIM
KVW = KV_HEADS * HEAD_DIM
WINDOW = 128
ROPE_BASE = 10000.0
ATTN_SCALE = HEAD_DIM ** -0.5
LOG2E = 1.4426950408889634
N_GROUPS = 8
EXPERTS_PER_GROUP = 8
N_EXPERTS = N_GROUPS * EXPERTS_PER_GROUP
TOP_K = 2
EPS = 1e-6
NEG_INF = -1e30

ROW_TILE = 256
PROJ_ROWS = 768
PROJ_COLS = 512
MOE_ROWS = 128
SC_GATHER_ROWS = 128
SC_ROW_WORDS = 128
ROUTER_LANES = 128
DENSE_CHAIN_HEADS = 4
WINDOW_CHAIN_HEADS = 1
MOD_ROWS = 8
VMEM_LIMIT = 56 * 1024 * 1024


def _params(*sem):
    return pltpu.CompilerParams(dimension_semantics=sem, vmem_limit_bytes=VMEM_LIMIT)


def _rms(y):
    return y * lax.rsqrt(jnp.mean(y * y, axis=-1, keepdims=True) + EPS)


def _mod_kernel(c_ref, w_ref, b_ref, o_ref):
    c = c_ref[...]
    a = (c * jax.nn.sigmoid(c)).astype(BF16)
    o_ref[...] = jnp.dot(a, w_ref[...].astype(BF16), preferred_element_type=F32) + b_ref[...]


def _mod_vectors(cc, w_mod, b_mod):
    L, D, N = w_mod.shape
    tn = 1024
    return pl.pallas_call(
        _mod_kernel,
        grid=(L, N // tn),
        in_specs=[
            pl.BlockSpec((MOD_ROWS, D), lambda l, j: (0, 0)),
            pl.BlockSpec((None, D, tn), lambda l, j: (l, 0, j)),
            pl.BlockSpec((None, 1, tn), lambda l, j: (l, 0, j)),
        ],
        out_specs=pl.BlockSpec((None, MOD_ROWS, tn), lambda l, j: (l, 0, j)),
        out_shape=jax.ShapeDtypeStruct((L, MOD_ROWS, N), F32),
        compiler_params=_params("parallel", "parallel"),
        name="mod_vectors",
    )(cc, w_mod, b_mod.reshape(L, 1, N))


class _Rows:
    def __init__(self, batch, tiles_per_batch, latent_only):
        self.batch = batch
        self.tpb = tiles_per_batch
        self.latent_only = latent_only
        self.n = batch * (tiles_per_batch - 1 if latent_only else tiles_per_batch)

    def tile(self, i):
        if self.latent_only:
            per = self.tpb - 1
            return (i // per) * self.tpb + 1 + i % per
        return i

    def mod_row(self, i):
        if self.latent_only:
            return i // (self.tpb - 1)
        return jnp.where(i % self.tpb == 0, self.batch, i // self.tpb)

    def row_spec(self, width):
        return pl.BlockSpec((ROW_TILE, width), lambda i: (self.tile(i), 0))

    def mod_spec(self, d, k):
        return pl.BlockSpec((None, 1, d), lambda i: (self.mod_row(i) * 6 + k, 0, 0))


def _const_spec(shape):
    zeros = (0,) * len(shape)
    return pl.BlockSpec(shape, lambda *_: zeros, pipeline_mode=pl.Buffered(1))


def _norm_mod_kernel(x_ref, ctx_ref, nw_ref, sh_ref, sc_ref, xo_ref, h_ref):
    def emit(src_ref):
        x = src_ref[...]
        xo_ref[...] = x
        h_ref[...] = ((_rms(x) * nw_ref[...]) * (1.0 + sc_ref[...]) + sh_ref[...]).astype(BF16)

    is_ctx = pl.program_id(1) == 0
    pl.when(is_ctx)(lambda: emit(ctx_ref))
    pl.when(jnp.logical_not(is_ctx))(lambda: emit(x_ref))


def _norm_mod(x, ctx, nw, mod_l):
    B, S, D = x.shape
    tiles = 1 + S // ROW_TILE
    mod_spec = lambda k: pl.BlockSpec((None, 1, D), lambda b, t: (jnp.where(t == 0, B, b) * 6 + k, 0, 0))
    out_spec = pl.BlockSpec((ROW_TILE, D), lambda b, t: (b * tiles + t, 0))
    return pl.pallas_call(
        _norm_mod_kernel,
        grid=(B, tiles),
        in_specs=[pl.BlockSpec((None, ROW_TILE, D), lambda b, t: (b, jnp.maximum(t - 1, 0), 0)),
                  pl.BlockSpec((None, ROW_TILE, D), lambda b, t: (b, 0, 0)),
                  _const_spec((1, D)), mod_spec(0), mod_spec(1)],
        out_specs=[out_spec, out_spec],
        out_shape=[jax.ShapeDtypeStruct((B * tiles * ROW_TILE, D), F32),
                   jax.ShapeDtypeStruct((B * tiles * ROW_TILE, D), BF16)],
        compiler_params=_params("parallel", "parallel"),
        name="norm_mod",
    )(x, ctx, nw, mod_l, mod_l)


def _rope(y, cos, sin):
    lane = lax.broadcasted_iota(jnp.int32, y.shape, 1)
    quarter = HEAD_DIM // 4
    partner = jnp.where((lane & quarter) == 0,
                        pltpu.roll(y, HEAD_DIM - quarter, 1), pltpu.roll(y, quarter, 1))
    return y * cos + partner * sin


def _proj_kernel(h_ref, w_ref, cos_ref, sin_ref, gain_ref, *refs, kinds):
    *out_refs, wb_ref = refs

    @pl.when(pl.program_id(1) == 0)
    def _():
        wb_ref[...] = w_ref[...].astype(BF16)

    acc = jnp.dot(h_ref[...], wb_ref[...], preferred_element_type=F32)
    for c, (kind, out_idx, out_chunk) in enumerate(kinds):
        y = acc[:, c * HEAD_DIM:(c + 1) * HEAD_DIM]
        if kind in ("q_norm", "k_norm"):
            y = _rms(y) * gain_ref[...]
        if kind in ("q_norm", "k_norm", "q_rope", "k_rope"):
            y = _rope(y, cos_ref[...], sin_ref[...])
        if kind in ("q_norm", "q_rope"):
            y = y * (ATTN_SCALE * LOG2E)
        if kind == "gate":
            y = jax.nn.sigmoid(y)
        out_refs[out_idx][:, out_chunk * HEAD_DIM:(out_chunk + 1) * HEAD_DIM] = y.astype(BF16)


def _proj(h, w_in, layer, cos, sin, gain, col0, kinds, out_widths, tiles_per_batch, name):
    M, D = h.shape
    n_col_tiles = out_widths[0][1]
    cb0 = col0 // PROJ_COLS
    proj_tiles_per_batch = tiles_per_batch * ROW_TILE // PROJ_ROWS
    return pl.pallas_call(
        functools.partial(_proj_kernel, kinds=kinds),
        grid=(n_col_tiles, M // PROJ_ROWS),
        in_specs=[
            pl.BlockSpec((PROJ_ROWS, D), lambda j, i: (i, 0)),
            pl.BlockSpec((None, D, PROJ_COLS), lambda j, i: (layer, 0, cb0 + j)),
            pl.BlockSpec((PROJ_ROWS, HEAD_DIM), lambda j, i: (i % proj_tiles_per_batch, 0)),
            pl.BlockSpec((PROJ_ROWS, HEAD_DIM), lambda j, i: (i % proj_tiles_per_batch, 0)),
            pl.BlockSpec((1, HEAD_DIM), lambda j, i: (0, 0)),
        ],
        out_specs=[pl.BlockSpec((PROJ_ROWS, w), lambda j, i: (i, j)) for w, _ in out_widths],
        out_shape=[jax.ShapeDtypeStruct((M, w * n), BF16) for w, n in out_widths],
        scratch_shapes=[pltpu.VMEM((D, PROJ_COLS), BF16)],
        compiler_params=_params("parallel", "arbitrary"),
        name=name,
    )(h, w_in, cos, sin, gain)


def _stack_heads(q_ref, g0, n):
    return jnp.concatenate([q_ref[:, g * HEAD_DIM:(g + 1) * HEAD_DIM] for g in range(g0, g0 + n)], axis=0)


def _write_heads(o_ref, g0, n, o):
    rows = o_ref.shape[0]
    for j in range(n):
        o_ref[:, (g0 + j) * HEAD_DIM:(g0 + j + 1) * HEAD_DIM] = o[j * rows:(j + 1) * rows].astype(o_ref.dtype)


def _scores(q, k):
    return lax.dot_general(q, k, (((1,), (1,)), ((), ())), preferred_element_type=F32)


def _online_step(q, k, v, m, l, acc):
    s = _scores(q, k)
    m_new = jnp.maximum(m, jnp.max(s, axis=-1, keepdims=True))
    alpha = jnp.exp2(m - m_new)
    p = jnp.exp2(s - m_new)
    l = alpha * l + jnp.sum(p, axis=-1, keepdims=True)
    acc = alpha * acc + jnp.dot(p.astype(BF16), v, preferred_element_type=F32)
    return m_new, l, acc


def _dense_attn_kernel(q_ref, k_ref, v_ref, o_ref, *, ctx_len, total_len, key_chunk, ctx_tile):
    def run(n_keys, chunk):
        for g0 in range(0, GROUP, DENSE_CHAIN_HEADS):
            q = _stack_heads(q_ref, g0, DENSE_CHAIN_HEADS)
            r = q.shape[0]
            m = jnp.full((r, 1), -jnp.inf, F32)
            l = jnp.zeros((r, 1), F32)
            acc = jnp.zeros((r, HEAD_DIM), F32)
            for c in range(n_keys // chunk):
                m, l, acc = _online_step(q, k_ref[c * chunk:(c + 1) * chunk, :],
                                         v_ref[c * chunk:(c + 1) * chunk, :], m, l, acc)
            _write_heads(o_ref, g0, DENSE_CHAIN_HEADS, acc * (1.0 / l))

    if ctx_tile:
        is_ctx = pl.program_id(2) == 0
        pl.when(is_ctx)(lambda: run(ctx_len, ctx_len))
        pl.when(jnp.logical_not(is_ctx))(lambda: run(total_len, key_chunk))
    else:
        run(total_len, key_chunk)


def _window_attn_kernel(sink_ref, q_ref, k_ref, v_ref, o_ref, *, ctx_len, total_len, ctx_tile, q_off):
    kvh = pl.program_id(1)
    tq = q_ref.shape[0]
    span = tq + 2 * WINDOW
    n = WINDOW_CHAIN_HEADS

    def attend(parts):
        for g0 in range(0, GROUP, n):
            q = _stack_heads(q_ref, g0, n)
            sink = jnp.concatenate(
                [jnp.full((tq, 1), sink_ref[kvh * GROUP + g] * LOG2E, F32) for g in range(g0, g0 + n)], axis=0)
            scores = []
            m = sink
            for k, _, bias in parts:
                s = _scores(q, k)
                if bias is not None:
                    s = s + bias
                scores.append(s)
                m = jnp.maximum(m, jnp.max(s, axis=-1, keepdims=True))
            l = jnp.exp2(sink - m)
            o = jnp.zeros((n * tq, HEAD_DIM), F32)
            for s, (_, v, _) in zip(scores, parts):
                p = jnp.exp2(s - m)
                l = l + jnp.sum(p, axis=-1, keepdims=True)
                o = o + jnp.dot(p.astype(BF16), v, preferred_element_type=F32)
            _write_heads(o_ref, g0, n, o * (1.0 / l))

    def ctx_part():
        return k_ref[0:ctx_len, :], v_ref[0:ctx_len, :], None

    def run_latent():
        q0 = (pl.program_id(2) + q_off) * tq - ctx_len
        n_latent = total_len - ctx_len
        start = pl.multiple_of(jnp.clip(q0 - WINDOW, 0, n_latent - span), WINDOW)
        row = lax.broadcasted_iota(jnp.int32, (n * tq, span), 0) & (tq - 1)
        col = lax.broadcasted_iota(jnp.int32, (n * tq, span), 1)
        bias = jnp.where(jnp.abs(col - row + (start - q0)) <= WINDOW, 0.0, NEG_INF).astype(F32)
        attend([ctx_part(), (k_ref[pl.ds(ctx_len + start, span), :], v_ref[pl.ds(ctx_len + start, span), :], bias)])

    if ctx_tile:
        is_ctx = pl.program_id(2) == 0
        pl.when(is_ctx)(lambda: attend([ctx_part()]))
        pl.when(jnp.logical_not(is_ctx))(run_latent)
    else:
        run_latent()


def _attention(q, k, v, sink, batch, total_len, ctx_len, latent_only, window):
    M = q.shape[0]
    q3, k3, v3 = (a.reshape(batch, total_len, a.shape[1]) for a in (q, k, v))
    tiles = total_len // ROW_TILE
    q_off = 1 if latent_only else 0
    gw = GROUP * HEAD_DIM
    q_spec = pl.BlockSpec((None, ROW_TILE, gw), lambda b, h, i, *_: (b, i + q_off, h))
    kv_spec = pl.BlockSpec((None, total_len, HEAD_DIM), lambda b, h, i, *_: (b, 0, h))
    grid = (batch, KV_HEADS, tiles - q_off)
    out_shape = jax.ShapeDtypeStruct(q3.shape, BF16)
    if window:
        kern = functools.partial(_window_attn_kernel, ctx_len=ctx_len, total_len=total_len,
                                 ctx_tile=not latent_only, q_off=q_off)
        out = pl.pallas_call(
            kern,
            grid_spec=pltpu.PrefetchScalarGridSpec(
                num_scalar_prefetch=1, grid=grid,
                in_specs=[q_spec, kv_spec, kv_spec], out_specs=q_spec),
            out_shape=out_shape,
            compiler_params=_params("parallel", "parallel", "parallel"),
            name="window_attention",
        )(sink, q3, k3, v3)
    else:
        kern = functools.partial(_dense_attn_kernel, ctx_len=ctx_len, total_len=total_len,
                                 key_chunk=total_len // 3, ctx_tile=not latent_only)
        out = pl.pallas_call(
            kern, grid=grid, in_specs=[q_spec, kv_spec, kv_spec], out_specs=q_spec,
            out_shape=out_shape,
            compiler_params=_params("parallel", "parallel", "parallel"),
            name="dense_attention",
        )(q3, k3, v3)
    return out.reshape(M, q.shape[1])


def _merge_kernel(ya_ref, yb_ref, ga_ref, gb_ref, wa_ref, wb_ref, wo_ref, x_ref, g1_ref,
                  nw_ref, sh_ref, sc_ref, wr_ref, br_ref, xo_ref, h_ref, lg_ref):
    a = jnp.dot(ya_ref[...], wa_ref[...], preferred_element_type=F32)
    b = jnp.dot(yb_ref[...], wb_ref[...], preferred_element_type=F32)
    m = ga_ref[...].astype(F32) * a + gb_ref[...].astype(F32) * b
    o = jnp.dot(m.astype(BF16), wo_ref[...], preferred_element_type=F32)
    xn = x_ref[...] + g1_ref[...] * o
    xo_ref[...] = xn
    h = (_rms(xn) * nw_ref[...]) * (1.0 + sc_ref[...]) + sh_ref[...]
    h_hi = h.astype(BF16)
    bits = pltpu.bitcast(h_hi.astype(F32), jnp.uint32)
    half = h.shape[1] // 2
    words = (bits[:, :half] >> 16) | (bits[:, half:] & jnp.uint32(0xFFFF0000))
    pieces = half // SC_ROW_WORDS
    for q in range(pieces):
        h_ref[pl.ds(q, h.shape[0], stride=pieces), :] = words[:, q * SC_ROW_WORDS:(q + 1) * SC_ROW_WORDS]
    h_lo = (h - h_hi.astype(F32)).astype(BF16)
    lg = jnp.dot(h_hi, wr_ref[...], preferred_element_type=F32)
    lg_lo = jnp.dot(h_lo, wr_ref[:, :ROUTER_LANES], preferred_element_type=F32)
    lg_ref[...] = lg[:, :ROUTER_LANES] + (lg[:, ROUTER_LANES:] + lg_lo) + br_ref[...]


def _merge(ya, yb, gates, wa, wb, wo, x_all, mod_l, nw, wr, br, rows):
    M, D = x_all.shape
    W = ya.shape[1]
    pieces = D // 2 // SC_ROW_WORDS
    return pl.pallas_call(
        _merge_kernel,
        grid=(rows.n,),
        in_specs=[
            rows.row_spec(W), rows.row_spec(W),
            pl.BlockSpec((ROW_TILE, D), lambda i: (rows.tile(i), 0)),
            pl.BlockSpec((ROW_TILE, D), lambda i: (rows.tile(i), 1)),
            _const_spec((W, D)), _const_spec((W, D)), _const_spec((D, D)),
            rows.row_spec(D), rows.mod_spec(D, 2),
            _const_spec((1, D)), rows.mod_spec(D, 3), rows.mod_spec(D, 4),
            _const_spec((D, 2 * ROUTER_LANES)), _const_spec((1, ROUTER_LANES)),
        ],
        out_specs=[rows.row_spec(D),
                   pl.BlockSpec((ROW_TILE * pieces, SC_ROW_WORDS), lambda i: (rows.tile(i), 0)),
                   rows.row_spec(ROUTER_LANES)],
        out_shape=[jax.ShapeDtypeStruct((M, D), F32), jax.ShapeDtypeStruct((M * pieces, SC_ROW_WORDS), jnp.uint32),
                   jax.ShapeDtypeStruct((M, ROUTER_LANES), F32)],
        compiler_params=_params("parallel"),
        name="merge",
    )(ya, yb, gates, gates, wa, wb, wo, x_all, mod_l, nw, mod_l, mod_l, wr, br)


def _route(logits):
    n = logits.shape[0]
    gl = logits[:, :N_GROUPS]
    el = logits[:, N_GROUPS:N_GROUPS + N_EXPERTS].reshape(n, N_GROUPS, EXPERTS_PER_GROUP)
    g = jnp.argmax(gl, axis=-1).astype(jnp.int32)
    pg = 1.0 / jnp.sum(jnp.exp(gl - jnp.max(gl, axis=-1, keepdims=True)), axis=-1, keepdims=True)
    in_group = jnp.arange(N_GROUPS, dtype=jnp.int32)[None, :, None] == g[:, None, None]
    el_g = jnp.sum(jnp.where(in_group, el, 0.0), axis=1)
    tv, ti = lax.top_k(el_g, TOP_K)
    wts = jax.nn.softmax(tv, axis=-1) * pg
    eid = g[:, None] * EXPERTS_PER_GROUP + ti.astype(jnp.int32)
    return eid, wts


def _dispatch(eid, tok_rows):
    n = eid.shape[0]
    a = n * TOP_K
    e_flat = eid.reshape(a)
    onehot = (e_flat[:, None] == jnp.arange(N_EXPERTS, dtype=jnp.int32)[None, :]).astype(jnp.int32)
    csum = jnp.cumsum(onehot, axis=0)
    counts = csum[-1]
    padded = (counts + MOE_ROWS - 1) // MOE_ROWS * MOE_ROWS
    pad_end = jnp.cumsum(padded)
    pad_start = pad_end - padded
    dest = jnp.sum(onehot * (csum - 1 + pad_start[None, :]), axis=1)
    n_blk = a // MOE_ROWS + N_EXPERTS
    slot_row = jnp.zeros((n_blk * MOE_ROWS,), jnp.int32).at[dest].set(jnp.repeat(tok_rows, TOP_K))
    blk = jnp.arange(n_blk, dtype=jnp.int32)
    n_valid = (pad_end[-1] // MOE_ROWS).astype(jnp.int32)
    blk_c = jnp.minimum(blk, n_valid - 1)
    blk_e = jnp.clip(jnp.searchsorted(pad_end, blk_c * MOE_ROWS, side="right"), 0, N_EXPERTS - 1).astype(jnp.int32)
    experts = jnp.arange(N_EXPERTS, dtype=jnp.int32)
    used = counts > 0
    first_used_from = lax.cummin(jnp.where(used, experts, N_EXPERTS), reverse=True)
    next_used = jnp.concatenate([first_used_from[1:], jnp.full((1,), N_EXPERTS, jnp.int32)])
    next_used = jnp.where(next_used == N_EXPERTS, -1, next_used)
    parity = (jnp.cumsum(used.astype(jnp.int32)) - 1) % 2
    return (dest, slot_row, blk_e, next_used[blk_e].astype(jnp.int32), parity[blk_e].astype(jnp.int32),
            n_valid.reshape(1))


def _sc_row_gather(table, idx):
    P = idx.shape[0]
    W = table.shape[1]
    mesh = plsc.VectorSubcoreMesh(core_axis_name="core", subcore_axis_name="subcore")

    @functools.partial(pl.kernel, out_type=jax.ShapeDtypeStruct((P, W), table.dtype), mesh=mesh)
    def gather(table_hbm, idx_hbm, out_hbm):
        def body(idx_vmem, out_vmem):
            pltpu.sync_copy(table_hbm.at[idx_vmem.at[0]], out_vmem)

        pltpu.emit_pipeline(
            body,
            grid=(P // SC_GATHER_ROWS,),
            in_specs=[pl.BlockSpec((1, SC_GATHER_ROWS), index_map=lambda i: (0, i))],
            out_specs=[pl.BlockSpec((SC_GATHER_ROWS, W), index_map=lambda i: (i, 0))],
            core_axis_name=("core", "subcore"),
            dimension_semantics=(pltpu.PARALLEL,),
        )(idx_hbm, out_hbm)

    return gather(table, idx.reshape(1, P))


def _experts_kernel(blk_e, blk_next, blk_par, n_valid, x_ref, w1_hbm, w3_hbm, w2_hbm, o_ref,
                    ws1, ws3, ws2, wsem, w1b, w3b, w2b, *, layer):
    i = pl.program_id(0)
    e = blk_e[i]
    par = blk_par[i]
    fresh = jnp.logical_or(i == 0, e != blk_e[jnp.maximum(i - 1, 0)])

    def weight_copies(expert, p):
        return (pltpu.make_async_copy(w1_hbm.at[layer, expert], ws1.at[p], wsem.at[p, 0]),
                pltpu.make_async_copy(w3_hbm.at[layer, expert], ws3.at[p], wsem.at[p, 1]),
                pltpu.make_async_copy(w2_hbm.at[layer, expert], ws2.at[p], wsem.at[p, 2]))

    @pl.when(i == 0)
    def _():
        for cp in weight_copies(e, par):
            cp.start()

    @pl.when(fresh)
    def _():
        for cp in weight_copies(e, par):
            cp.wait()
        nxt = blk_next[i]

        @pl.when(nxt >= 0)
        def _():
            for cp in weight_copies(nxt, 1 - par):
                cp.start()

        w1b[...] = ws1[par].astype(BF16)
        w3b[...] = ws3[par].astype(BF16)
        w2b[...] = ws2[par].astype(BF16)

    @pl.when(i < n_valid[0])
    def _():
        pieces = x_ref.shape[0] // MOE_ROWS
        words = jnp.concatenate([x_ref[pl.ds(q, MOE_ROWS, stride=pieces), :] for q in range(pieces)], axis=1)
        lo = pltpu.bitcast(words << 16, F32).astype(BF16)
        hi = pltpu.bitcast(words & jnp.uint32(0xFFFF0000), F32).astype(BF16)
        x = jnp.concatenate([lo, hi], axis=1)
        a = jnp.dot(x, w1b[...], preferred_element_type=F32)
        b = jnp.dot(x, w3b[...], preferred_element_type=F32)
        hid = (a * jax.nn.sigmoid(a)) * b
        o_ref[...] = jnp.dot(hid.astype(BF16), w2b[...], preferred_element_type=F32).astype(o_ref.dtype)


def _experts(xbuf, w1, w3, w2, layer, blk_e, blk_next, blk_par, n_valid):
    D, de = w1.shape[-2:]
    n_blk = blk_e.shape[0]
    pieces = D // 2 // SC_ROW_WORDS
    any_spec = pl.BlockSpec(memory_space=pl.ANY)
    row_block = lambda rows, width: pl.BlockSpec((rows, width), lambda i, *s: (jnp.minimum(i, s[3][0] - 1), 0))
    return pl.pallas_call(
        functools.partial(_experts_kernel, layer=layer),
        grid_spec=pltpu.PrefetchScalarGridSpec(
            num_scalar_prefetch=4, grid=(n_blk,),
            in_specs=[row_block(MOE_ROWS * pieces, SC_ROW_WORDS), any_spec, any_spec, any_spec],
            out_specs=row_block(MOE_ROWS, D),
            scratch_shapes=[
                pltpu.VMEM((2, D, de), F32), pltpu.VMEM((2, D, de), F32), pltpu.VMEM((2, de, D), F32),
                pltpu.SemaphoreType.DMA((2, 3)),
                pltpu.VMEM((D, de), BF16), pltpu.VMEM((D, de), BF16), pltpu.VMEM((de, D), BF16),
            ],
        ),
        out_shape=jax.ShapeDtypeStruct((n_blk * MOE_ROWS, D), BF16),
        compiler_params=_params("arbitrary"),
        name="experts",
    )(blk_e, blk_next, blk_par, n_valid, xbuf, w1, w3, w2)


def _moe_mix(x_ref, y0_ref, y1_ref, w_ref, g2_ref):
    w = w_ref[...]
    y = w[:, 0:1] * y0_ref[...].astype(F32) + w[:, 1:2] * y1_ref[...].astype(F32)
    return x_ref[...] + g2_ref[...] * y


def _combine_next_kernel(x_ref, y0_ref, y1_ref, w_ref, g2_ref, nw_ref, sh_ref, sc_ref, xo_ref, h_ref):
    xn = _moe_mix(x_ref, y0_ref, y1_ref, w_ref, g2_ref)
    xo_ref[...] = xn
    h_ref[...] = ((_rms(xn) * nw_ref[...]) * (1.0 + sc_ref[...]) + sh_ref[...]).astype(BF16)


def _combine_final_kernel(x_ref, y0_ref, y1_ref, w_ref, g2_ref, nf_ref, o_ref):
    o_ref[...] = _rms(_moe_mix(x_ref, y0_ref, y1_ref, w_ref, g2_ref)) * nf_ref[...]


def _token_specs(n_tokens, d):
    tiles = n_tokens // ROW_TILE
    return [pl.BlockSpec((ROW_TILE, d), lambda i: (i, 0)), pl.BlockSpec((ROW_TILE, d), lambda i: (i + tiles, 0)),
            pl.BlockSpec((ROW_TILE, TOP_K), lambda i: (i, 0))]


def _combine_next(x_new, ysel, wts, mod_l, nw_next, mod_next, rows):
    M, D = x_new.shape
    return pl.pallas_call(
        _combine_next_kernel,
        grid=(rows.n,),
        in_specs=[rows.row_spec(D), *_token_specs(M, D), rows.mod_spec(D, 5),
                  _const_spec((1, D)), rows.mod_spec(D, 0), rows.mod_spec(D, 1)],
        out_specs=[rows.row_spec(D), rows.row_spec(D)],
        out_shape=[jax.ShapeDtypeStruct((M, D), F32), jax.ShapeDtypeStruct((M, D), BF16)],
        compiler_params=_params("parallel"),
        name="combine_next",
    )(x_new, ysel, ysel, wts, mod_l, nw_next, mod_next, mod_next)


def _combine_final(x_new, ysel, wts, mod_l, nf, rows):
    M, D = x_new.shape
    n = wts.shape[0]
    return pl.pallas_call(
        _combine_final_kernel,
        grid=(rows.n,),
        in_specs=[rows.row_spec(D), *_token_specs(n, D), rows.mod_spec(D, 5), _const_spec((1, D))],
        out_specs=pl.BlockSpec((ROW_TILE, D), lambda i: (i, 0)),
        out_shape=jax.ShapeDtypeStruct((n, D), F32),
        compiler_params=_params("parallel"),
        name="combine_final",
    )(x_new, ysel, ysel, wts, mod_l, nf)


def _rope_tables(ctx_len, seq):
    quarter = HEAD_DIM // 4
    freqs = ROPE_BASE ** (-jnp.arange(quarter, dtype=F32) / quarter)
    pos = jnp.arange(seq, dtype=jnp.int32)
    row = (pos // GRID_W).astype(F32)[:, None] * freqs[None, :]
    col = (pos % GRID_W).astype(F32)[:, None] * freqs[None, :]
    cos = jnp.concatenate([jnp.cos(row), jnp.cos(row), jnp.cos(col), jnp.cos(col)], axis=-1)
    sin = jnp.concatenate([-jnp.sin(row), jnp.sin(row), -jnp.sin(col), jnp.sin(col)], axis=-1)
    cos = jnp.concatenate([jnp.ones((ctx_len, HEAD_DIM), F32), cos], axis=0)
    sin = jnp.concatenate([jnp.zeros((ctx_len, HEAD_DIM), F32), sin], axis=0)
    return cos, sin


def kernel(x, c, ctx, c_ctx, w_mod, b_mod, norm_mix, norm_ffn, w_in, qn_a, kn_a, sink_b, w_br_a, w_br_b, w_out, w_rg, b_rg, w_re, b_re, w1, w3, w2, norm_final):
    B, S, D = x.shape
    C = ctx.shape[1]
    L = w_mod.shape[0]
    T = C + S
    M = B * T
    assert C == ROW_TILE and S % ROW_TILE == 0 and T % PROJ_ROWS == 0 and B < MOD_ROWS
    tiles_per_batch = T // ROW_TILE

    cc = jnp.concatenate([c, c_ctx[None, :], jnp.zeros((MOD_ROWS - B - 1, D), F32)], axis=0)
    mod = _mod_vectors(cc, w_mod, b_mod).reshape(L, MOD_ROWS * 6, 1, D)
    cos, sin = _rope_tables(C, S)
    latent_rows = (jnp.arange(B, dtype=jnp.int32)[:, None] * T + C
                   + jnp.arange(S, dtype=jnp.int32)[None, :]).reshape(B * S)
    all_rows = jnp.arange(M, dtype=jnp.int32)

    q_kinds = lambda kind: tuple((kind, 0, k) for k in range(PROJ_COLS // HEAD_DIM))
    kv_kinds = lambda kind: ((kind, 0, 0), (kind, 0, 1), ("v", 1, 0), ("v", 1, 1))
    col = {"qa": 0, "kva": WIDTH, "qb": WIDTH + 2 * KVW, "kvb": 2 * WIDTH + 2 * KVW, "gates": 2 * WIDTH + 4 * KVW}

    every = _Rows(B, tiles_per_batch, latent_only=False)
    latent = _Rows(B, tiles_per_batch, latent_only=True)
    x_all, h = _norm_mod(x, ctx, norm_mix[0][None, :], mod[0])
    out = None
    for l in range(L):
        last = l == L - 1
        rows = latent if last else every
        proj = functools.partial(_proj, h, w_in, l, cos, sin, tiles_per_batch=tiles_per_batch)
        (qa,) = proj(qn_a[l][None, :], col["qa"], q_kinds("q_norm"), [(PROJ_COLS, WIDTH // PROJ_COLS)], name="proj_qa")
        ka, va = proj(kn_a[l][None, :], col["kva"], kv_kinds("k_norm"), [(KVW, 1), (KVW, 1)], name="proj_kva")
        (qb,) = proj(qn_a[l][None, :], col["qb"], q_kinds("q_rope"), [(PROJ_COLS, WIDTH // PROJ_COLS)], name="proj_qb")
        kb, vb = proj(kn_a[l][None, :], col["kvb"], kv_kinds("k_rope"), [(KVW, 1), (KVW, 1)], name="proj_kvb")
        (gates,) = proj(qn_a[l][None, :], col["gates"], q_kinds("gate"), [(PROJ_COLS, 2 * D // PROJ_COLS)], name="proj_gates")

        ya = _attention(qa, ka, va, None, B, T, C, last, window=False)
        yb = _attention(qb, kb, vb, sink_b[l], B, T, C, last, window=True)

        w_router = jnp.concatenate(
            [w_rg[l], w_re[l], jnp.zeros((D, ROUTER_LANES - N_GROUPS - N_EXPERTS), F32)], axis=1)
        w_router_hi = w_router.astype(BF16)
        w_router = jnp.concatenate([w_router_hi, (w_router - w_router_hi.astype(F32)).astype(BF16)], axis=1)
        b_router = jnp.concatenate(
            [b_rg[l], b_re[l], jnp.zeros((ROUTER_LANES - N_GROUPS - N_EXPERTS,), F32)])[None, :]
        x_new, h2, logits = _merge(ya, yb, gates, w_br_a[l].astype(BF16), w_br_b[l].astype(BF16),
                                   w_out[l].astype(BF16), x_all, mod[l], norm_ffn[l][None, :],
                                   w_router, b_router, rows)

        tok_rows = latent_rows if last else all_rows
        eid, wts = _route(logits[tok_rows] if last else logits)
        dest, slot_row, *plan = _dispatch(eid, tok_rows)
        pieces = h2.shape[0] // M
        piece_rows = (slot_row[:, None] * pieces + jnp.arange(pieces, dtype=jnp.int32)[None, :]).reshape(-1)
        ybuf = _experts(_sc_row_gather(h2, piece_rows), w1, w3, w2, l, *plan)
        ysel = ybuf[dest.reshape(-1, TOP_K).T.reshape(-1)]
        if last:
            out = _combine_final(x_new, ysel, wts, mod[l], norm_final[None, :], rows).reshape(B, S, D)
        else:
            x_all, h = _combine_next(x_new, ysel, wts, mod[l], norm_mix[l + 1][None, :], mod[l + 1], rows)
    return out
```

```python
import functools

import jax
import jax.numpy as jnp
from jax import lax
from jax.experimental import pallas as pl
from jax.experimental.pallas import tpu as pltpu
from jax.experimental.pallas import tpu_sc as plsc

F32 = jnp.float32
BF16 = jnp.bfloat16

GRID_W = 64
HEAD_DIM = 128
HEADS = 8
KV_HEADS = 2
GROUP = HEADS // KV_HEADS
WIDTH = HEADS * HEAD_DIM
KVW = KV_HEADS * HEAD_DIM
WINDOW = 128
ROPE_BASE = 10000.0
ATTN_SCALE = HEAD_DIM ** -0.5
LOG2E = 1.4426950408889634
N_GROUPS = 8
EXPERTS_PER_GROUP = 8
N_EXPERTS = N_GROUPS * EXPERTS_PER_GROUP
TOP_K = 2
EPS = 1e-6
NEG_INF = -1e30

ROW_TILE = 256
PROJ_ROWS = 768
PROJ_COLS = 512
MOE_ROWS = 128
SC_GATHER_ROWS = 32
SC_INDEX_LANES = 128
ROUTER_LANES = 128
DENSE_CHAIN_HEADS = 4
WINDOW_CHAIN_HEADS = 1
MOD_ROWS = 8
VMEM_LIMIT = 56 * 1024 * 1024


def _params(*sem):
    return pltpu.CompilerParams(dimension_semantics=sem, vmem_limit_bytes=VMEM_LIMIT)


def _rms(y):
    return y * lax.rsqrt(jnp.mean(y * y, axis=-1, keepdims=True) + EPS)


def _mod_kernel(c_ref, w_ref, b_ref, o_ref):
    c = c_ref[...]
    a = (c * jax.nn.sigmoid(c)).astype(BF16)
    o_ref[...] = jnp.dot(a, w_ref[...].astype(BF16), preferred_element_type=F32) + b_ref[...]


def _mod_vectors(cc, w_mod, b_mod):
    L, D, N = w_mod.shape
    tn = 1024
    return pl.pallas_call(
        _mod_kernel,
        grid=(L, N // tn),
        in_specs=[
            pl.BlockSpec((MOD_ROWS, D), lambda l, j: (0, 0)),
            pl.BlockSpec((None, D, tn), lambda l, j: (l, 0, j)),
            pl.BlockSpec((None, 1, tn), lambda l, j: (l, 0, j)),
        ],
        out_specs=pl.BlockSpec((None, MOD_ROWS, tn), lambda l, j: (l, 0, j)),
        out_shape=jax.ShapeDtypeStruct((L, MOD_ROWS, N), F32),
        compiler_params=_params("parallel", "parallel"),
        name="mod_vectors",
    )(cc, w_mod, b_mod.reshape(L, 1, N))


class _Rows:
    def __init__(self, batch, tiles_per_batch, latent_only):
        self.batch = batch
        self.tpb = tiles_per_batch
        self.latent_only = latent_only
        self.n = batch * (tiles_per_batch - 1 if latent_only else tiles_per_batch)

    def tile(self, i):
        if self.latent_only:
            per = self.tpb - 1
            return (i // per) * self.tpb + 1 + i % per
        return i

    def mod_row(self, i):
        if self.latent_only:
            return i // (self.tpb - 1)
        return jnp.where(i % self.tpb == 0, self.batch, i // self.tpb)

    def row_spec(self, width):
        return pl.BlockSpec((ROW_TILE, width), lambda i: (self.tile(i), 0))

    def mod_spec(self, d, k):
        return pl.BlockSpec((None, 1, d), lambda i: (self.mod_row(i) * 6 + k, 0, 0))


def _const_spec(shape):
    zeros = (0,) * len(shape)
    return pl.BlockSpec(shape, lambda *_: zeros, pipeline_mode=pl.Buffered(1))


def _norm_mod_kernel(x_ref, ctx_ref, nw_ref, sh_ref, sc_ref, xo_ref, h_ref):
    def emit(src_ref):
        x = src_ref[...]
        xo_ref[...] = x
        h_ref[...] = ((_rms(x) * nw_ref[...]) * (1.0 + sc_ref[...]) + sh_ref[...]).astype(BF16)

    is_ctx = pl.program_id(1) == 0
    pl.when(is_ctx)(lambda: emit(ctx_ref))
    pl.when(jnp.logical_not(is_ctx))(lambda: emit(x_ref))


def _norm_mod(x, ctx, nw, mod_l):
    B, S, D = x.shape
    tiles = 1 + S // ROW_TILE
    mod_spec = lambda k: pl.BlockSpec((None, 1, D), lambda b, t: (jnp.where(t == 0, B, b) * 6 + k, 0, 0))
    out_spec = pl.BlockSpec((ROW_TILE, D), lambda b, t: (b * tiles + t, 0))
    return pl.pallas_call(
        _norm_mod_kernel,
        grid=(B, tiles),
        in_specs=[pl.BlockSpec((None, ROW_TILE, D), lambda b, t: (b, jnp.maximum(t - 1, 0), 0)),
                  pl.BlockSpec((None, ROW_TILE, D), lambda b, t: (b, 0, 0)),
                  _const_spec((1, D)), mod_spec(0), mod_spec(1)],
        out_specs=[out_spec, out_spec],
        out_shape=[jax.ShapeDtypeStruct((B * tiles * ROW_TILE, D), F32),
                   jax.ShapeDtypeStruct((B * tiles * ROW_TILE, D), BF16)],
        compiler_params=_params("parallel", "parallel"),
        name="norm_mod",
    )(x, ctx, nw, mod_l, mod_l)


def _rope(y, cos, sin):
    lane = lax.broadcasted_iota(jnp.int32, y.shape, 1)
    quarter = HEAD_DIM // 4
    partner = jnp.where((lane & quarter) == 0,
                        pltpu.roll(y, HEAD_DIM - quarter, 1), pltpu.roll(y, quarter, 1))
    return y * cos + partner * sin


def _proj_kernel(h_ref, w_ref, cos_ref, sin_ref, gain_ref, *refs, kinds):
    *out_refs, wb_ref = refs

    @pl.when(pl.program_id(1) == 0)
    def _():
        wb_ref[...] = w_ref[...].astype(BF16)

    acc = jnp.dot(h_ref[...], wb_ref[...], preferred_element_type=F32)
    for c, (kind, out_idx, out_chunk) in enumerate(kinds):
        y = acc[:, c * HEAD_DIM:(c + 1) * HEAD_DIM]
        if kind in ("q_norm", "k_norm"):
            y = _rms(y) * gain_ref[...]
        if kind in ("q_norm", "k_norm", "q_rope", "k_rope"):
            y = _rope(y, cos_ref[...], sin_ref[...])
        if kind in ("q_norm", "q_rope"):
            y = y * (ATTN_SCALE * LOG2E)
        if kind == "gate":
            y = jax.nn.sigmoid(y)
        out_refs[out_idx][:, out_chunk * HEAD_DIM:(out_chunk + 1) * HEAD_DIM] = y.astype(BF16)


def _proj(h, w_in, layer, cos, sin, gain, col0, kinds, out_widths, tiles_per_batch, name):
    M, D = h.shape
    n_col_tiles = out_widths[0][1]
    cb0 = col0 // PROJ_COLS
    proj_tiles_per_batch = tiles_per_batch * ROW_TILE // PROJ_ROWS
    return pl.pallas_call(
        functools.partial(_proj_kernel, kinds=kinds),
        grid=(n_col_tiles, M // PROJ_ROWS),
        in_specs=[
            pl.BlockSpec((PROJ_ROWS, D), lambda j, i: (i, 0)),
            pl.BlockSpec((None, D, PROJ_COLS), lambda j, i: (layer, 0, cb0 + j)),
            pl.BlockSpec((PROJ_ROWS, HEAD_DIM), lambda j, i: (i % proj_tiles_per_batch, 0)),
            pl.BlockSpec((PROJ_ROWS, HEAD_DIM), lambda j, i: (i % proj_tiles_per_batch, 0)),
            pl.BlockSpec((1, HEAD_DIM), lambda j, i: (0, 0)),
        ],
        out_specs=[pl.BlockSpec((PROJ_ROWS, w), lambda j, i: (i, j)) for w, _ in out_widths],
        out_shape=[jax.ShapeDtypeStruct((M, w * n), BF16) for w, n in out_widths],
        scratch_shapes=[pltpu.VMEM((D, PROJ_COLS), BF16)],
        compiler_params=_params("parallel", "arbitrary"),
        name=name,
    )(h, w_in, cos, sin, gain)


def _stack_heads(q_ref, g0, n):
    return jnp.concatenate([q_ref[:, g * HEAD_DIM:(g + 1) * HEAD_DIM] for g in range(g0, g0 + n)], axis=0)


def _write_heads(o_ref, g0, n, o):
    rows = o_ref.shape[0]
    for j in range(n):
        o_ref[:, (g0 + j) * HEAD_DIM:(g0 + j + 1) * HEAD_DIM] = o[j * rows:(j + 1) * rows].astype(o_ref.dtype)


def _scores(q, k):
    return lax.dot_general(q, k, (((1,), (1,)), ((), ())), preferred_element_type=F32)


def _online_step(q, k, v, m, l, acc):
    s = _scores(q, k)
    m_new = jnp.maximum(m, jnp.max(s, axis=-1, keepdims=True))
    alpha = jnp.exp2(m - m_new)
    p = jnp.exp2(s - m_new)
    l = alpha * l + jnp.sum(p, axis=-1, keepdims=True)
    acc = alpha * acc + jnp.dot(p.astype(BF16), v, preferred_element_type=F32)
    return m_new, l, acc


def _dense_attn_kernel(q_ref, k_ref, v_ref, o_ref, *, ctx_len, total_len, key_chunk, ctx_tile):
    def run(n_keys, chunk):
        for g0 in range(0, GROUP, DENSE_CHAIN_HEADS):
            q = _stack_heads(q_ref, g0, DENSE_CHAIN_HEADS)
            r = q.shape[0]
            m = jnp.full((r, 1), -jnp.inf, F32)
            l = jnp.zeros((r, 1), F32)
            acc = jnp.zeros((r, HEAD_DIM), F32)
            for c in range(n_keys // chunk):
                m, l, acc = _online_step(q, k_ref[c * chunk:(c + 1) * chunk, :],
                                         v_ref[c * chunk:(c + 1) * chunk, :], m, l, acc)
            _write_heads(o_ref, g0, DENSE_CHAIN_HEADS, acc * (1.0 / l))

    if ctx_tile:
        is_ctx = pl.program_id(2) == 0
        pl.when(is_ctx)(lambda: run(ctx_len, ctx_len))
        pl.when(jnp.logical_not(is_ctx))(lambda: run(total_len, key_chunk))
    else:
        run(total_len, key_chunk)


def _window_attn_kernel(sink_ref, q_ref, k_ref, v_ref, o_ref, *, ctx_len, total_len, ctx_tile, q_off):
    kvh = pl.program_id(1)
    tq = q_ref.shape[0]
    span = tq + 2 * WINDOW
    n = WINDOW_CHAIN_HEADS

    def attend(parts):
        for g0 in range(0, GROUP, n):
            q = _stack_heads(q_ref, g0, n)
            sink = jnp.concatenate(
                [jnp.full((tq, 1), sink_ref[kvh * GROUP + g] * LOG2E, F32) for g in range(g0, g0 + n)], axis=0)
            scores = []
            m = sink
            for k, _, bias in parts:
                s = _scores(q, k)
                if bias is not None:
                    s = s + bias
                scores.append(s)
                m = jnp.maximum(m, jnp.max(s, axis=-1, keepdims=True))
            l = jnp.exp2(sink - m)
            o = jnp.zeros((n * tq, HEAD_DIM), F32)
            for s, (_, v, _) in zip(scores, parts):
                p = jnp.exp2(s - m)
                l = l + jnp.sum(p, axis=-1, keepdims=True)
                o = o + jnp.dot(p.astype(BF16), v, preferred_element_type=F32)
            _write_heads(o_ref, g0, n, o * (1.0 / l))

    def ctx_part():
        return k_ref[0:ctx_len, :], v_ref[0:ctx_len, :], None

    def run_latent():
        q0 = (pl.program_id(2) + q_off) * tq - ctx_len
        n_latent = total_len - ctx_len
        start = pl.multiple_of(jnp.clip(q0 - WINDOW, 0, n_latent - span), WINDOW)
        row = lax.broadcasted_iota(jnp.int32, (n * tq, span), 0) & (tq - 1)
        col = lax.broadcasted_iota(jnp.int32, (n * tq, span), 1)
        bias = jnp.where(jnp.abs(col - row + (start - q0)) <= WINDOW, 0.0, NEG_INF).astype(F32)
        attend([ctx_part(), (k_ref[pl.ds(ctx_len + start, span), :], v_ref[pl.ds(ctx_len + start, span), :], bias)])

    if ctx_tile:
        is_ctx = pl.program_id(2) == 0
        pl.when(is_ctx)(lambda: attend([ctx_part()]))
        pl.when(jnp.logical_not(is_ctx))(run_latent)
    else:
        run_latent()


def _attention(q, k, v, sink, batch, total_len, ctx_len, latent_only, window):
    M = q.shape[0]
    q3, k3, v3 = (a.reshape(batch, total_len, a.shape[1]) for a in (q, k, v))
    tiles = total_len // ROW_TILE
    q_off = 1 if latent_only else 0
    gw = GROUP * HEAD_DIM
    q_spec = pl.BlockSpec((None, ROW_TILE, gw), lambda b, h, i, *_: (b, i + q_off, h))
    kv_spec = pl.BlockSpec((None, total_len, HEAD_DIM), lambda b, h, i, *_: (b, 0, h))
    grid = (batch, KV_HEADS, tiles - q_off)
    out_shape = jax.ShapeDtypeStruct(q3.shape, BF16)
    if window:
        kern = functools.partial(_window_attn_kernel, ctx_len=ctx_len, total_len=total_len,
                                 ctx_tile=not latent_only, q_off=q_off)
        out = pl.pallas_call(
            kern,
            grid_spec=pltpu.PrefetchScalarGridSpec(
                num_scalar_prefetch=1, grid=grid,
                in_specs=[q_spec, kv_spec, kv_spec], out_specs=q_spec),
            out_shape=out_shape,
            compiler_params=_params("parallel", "parallel", "parallel"),
            name="window_attention",
        )(sink, q3, k3, v3)
    else:
        kern = functools.partial(_dense_attn_kernel, ctx_len=ctx_len, total_len=total_len,
                                 key_chunk=total_len // 3, ctx_tile=not latent_only)
        out = pl.pallas_call(
            kern, grid=grid, in_specs=[q_spec, kv_spec, kv_spec], out_specs=q_spec,
            out_shape=out_shape,
            compiler_params=_params("parallel", "parallel", "parallel"),
            name="dense_attention",
        )(q3, k3, v3)
    return out.reshape(M, q.shape[1])


def _merge_kernel(ya_ref, yb_ref, ga_ref, gb_ref, wa_ref, wb_ref, wo_ref, x_ref, g1_ref,
                  nw_ref, sh_ref, sc_ref, wr_ref, br_ref, xo_ref, h_ref, lg_ref):
    a = jnp.dot(ya_ref[...], wa_ref[...], preferred_element_type=F32)
    b = jnp.dot(yb_ref[...], wb_ref[...], preferred_element_type=F32)
    m = ga_ref[...].astype(F32) * a + gb_ref[...].astype(F32) * b
    o = jnp.dot(m.astype(BF16), wo_ref[...], preferred_element_type=F32)
    xn = x_ref[...] + g1_ref[...] * o
    xo_ref[...] = xn
    h = (_rms(xn) * nw_ref[...]) * (1.0 + sc_ref[...]) + sh_ref[...]
    h_hi = h.astype(BF16)
    bits = pltpu.bitcast(h_hi.astype(F32), jnp.uint32)
    half = h.shape[1] // 2
    h_ref[...] = (bits[:, :half] >> 16) | (bits[:, half:] & jnp.uint32(0xFFFF0000))
    h_lo = (h - h_hi.astype(F32)).astype(BF16)
    lg = jnp.dot(h_hi, wr_ref[...], preferred_element_type=F32)
    lg_lo = jnp.dot(h_lo, wr_ref[:, :ROUTER_LANES], preferred_element_type=F32)
    lg_ref[...] = lg[:, :ROUTER_LANES] + (lg[:, ROUTER_LANES:] + lg_lo) + br_ref[...]


def _merge(ya, yb, gates, wa, wb, wo, x_all, mod_l, nw, wr, br, rows):
    M, D = x_all.shape
    W = ya.shape[1]
    return pl.pallas_call(
        _merge_kernel,
        grid=(rows.n,),
        in_specs=[
            rows.row_spec(W), rows.row_spec(W),
            pl.BlockSpec((ROW_TILE, D), lambda i: (rows.tile(i), 0)),
            pl.BlockSpec((ROW_TILE, D), lambda i: (rows.tile(i), 1)),
            _const_spec((W, D)), _const_spec((W, D)), _const_spec((D, D)),
            rows.row_spec(D), rows.mod_spec(D, 2),
            _const_spec((1, D)), rows.mod_spec(D, 3), rows.mod_spec(D, 4),
            _const_spec((D, 2 * ROUTER_LANES)), _const_spec((1, ROUTER_LANES)),
        ],
        out_specs=[rows.row_spec(D), rows.row_spec(D // 2), rows.row_spec(ROUTER_LANES)],
        out_shape=[jax.ShapeDtypeStruct((M, D), F32), jax.ShapeDtypeStruct((M, D // 2), jnp.uint32),
                   jax.ShapeDtypeStruct((M, ROUTER_LANES), F32)],
        compiler_params=_params("parallel"),
        name="merge",
    )(ya, yb, gates, gates, wa, wb, wo, x_all, mod_l, nw, mod_l, mod_l, wr, br)


def _route(logits):
    n = logits.shape[0]
    gl = logits[:, :N_GROUPS]
    el = logits[:, N_GROUPS:N_GROUPS + N_EXPERTS].reshape(n, N_GROUPS, EXPERTS_PER_GROUP)
    g = jnp.argmax(gl, axis=-1).astype(jnp.int32)
    pg = 1.0 / jnp.sum(jnp.exp(gl - jnp.max(gl, axis=-1, keepdims=True)), axis=-1, keepdims=True)
    in_group = jnp.arange(N_GROUPS, dtype=jnp.int32)[None, :, None] == g[:, None, None]
    el_g = jnp.sum(jnp.where(in_group, el, 0.0), axis=1)
    tv, ti = lax.top_k(el_g, TOP_K)
    wts = jax.nn.softmax(tv, axis=-1) * pg
    eid = g[:, None] * EXPERTS_PER_GROUP + ti.astype(jnp.int32)
    return eid, wts


def _dispatch(eid, tok_rows):
    n = eid.shape[0]
    a = n * TOP_K
    e_flat = eid.reshape(a)
    onehot = (e_flat[:, None] == jnp.arange(N_EXPERTS, dtype=jnp.int32)[None, :]).astype(jnp.int32)
    csum = jnp.cumsum(onehot, axis=0)
    counts = csum[-1]
    padded = (counts + MOE_ROWS - 1) // MOE_ROWS * MOE_ROWS
    pad_end = jnp.cumsum(padded)
    pad_start = pad_end - padded
    dest = jnp.sum(onehot * (csum - 1 + pad_start[None, :]), axis=1)
    n_blk = a // MOE_ROWS + N_EXPERTS
    slot_row = jnp.zeros((n_blk * MOE_ROWS,), jnp.int32).at[dest].set(jnp.repeat(tok_rows, TOP_K))
    blk = jnp.arange(n_blk, dtype=jnp.int32)
    n_valid = (pad_end[-1] // MOE_ROWS).astype(jnp.int32)
    blk_c = jnp.minimum(blk, n_valid - 1)
    blk_e = jnp.clip(jnp.searchsorted(pad_end, blk_c * MOE_ROWS, side="right"), 0, N_EXPERTS - 1).astype(jnp.int32)
    experts = jnp.arange(N_EXPERTS, dtype=jnp.int32)
    used = counts > 0
    first_used_from = lax.cummin(jnp.where(used, experts, N_EXPERTS), reverse=True)
    next_used = jnp.concatenate([first_used_from[1:], jnp.full((1,), N_EXPERTS, jnp.int32)])
    next_used = jnp.where(next_used == N_EXPERTS, -1, next_used)
    parity = (jnp.cumsum(used.astype(jnp.int32)) - 1) % 2
    return (dest, slot_row, blk_e, next_used[blk_e].astype(jnp.int32), parity[blk_e].astype(jnp.int32),
            n_valid.reshape(1))


def _sc_row_gather(table, idx):
    P = idx.shape[0]
    W = table.shape[1]
    steps = P // SC_GATHER_ROWS
    idx_steps = jnp.pad(idx.reshape(steps, SC_GATHER_ROWS), ((0, 0), (0, SC_INDEX_LANES - SC_GATHER_ROWS)))
    mesh = plsc.VectorSubcoreMesh(core_axis_name="core", subcore_axis_name="subcore")

    @functools.partial(pl.kernel, out_type=jax.ShapeDtypeStruct((P, W), table.dtype), mesh=mesh)
    def gather(table_hbm, idx_hbm, out_hbm):
        def body(idx_vmem, out_vmem):
            pltpu.sync_copy(table_hbm.at[idx_vmem.at[0, pl.ds(0, SC_GATHER_ROWS)]], out_vmem)

        pltpu.emit_pipeline(
            body,
            grid=(steps,),
            in_specs=[pl.BlockSpec((1, SC_INDEX_LANES), index_map=lambda i: (i, 0))],
            out_specs=[pl.BlockSpec((SC_GATHER_ROWS, W), index_map=lambda i: (i, 0))],
            core_axis_name=("core", "subcore"),
            dimension_semantics=(pltpu.PARALLEL,),
        )(idx_hbm, out_hbm)

    return gather(table, idx_steps)


def _experts_kernel(blk_e, blk_next, blk_par, n_valid, x_ref, w1_hbm, w3_hbm, w2_hbm, o_ref,
                    ws1, ws3, ws2, wsem, w1b, w3b, w2b, *, layer):
    i = pl.program_id(0)
    e = blk_e[i]
    par = blk_par[i]
    fresh = jnp.logical_or(i == 0, e != blk_e[jnp.maximum(i - 1, 0)])

    def weight_copies(expert, p):
        return (pltpu.make_async_copy(w1_hbm.at[layer, expert], ws1.at[p], wsem.at[p, 0]),
                pltpu.make_async_copy(w3_hbm.at[layer, expert], ws3.at[p], wsem.at[p, 1]),
                pltpu.make_async_copy(w2_hbm.at[layer, expert], ws2.at[p], wsem.at[p, 2]))

    @pl.when(i == 0)
    def _():
        for cp in weight_copies(e, par):
            cp.start()

    @pl.when(fresh)
    def _():
        for cp in weight_copies(e, par):
            cp.wait()
        nxt = blk_next[i]

        @pl.when(nxt >= 0)
        def _():
            for cp in weight_copies(nxt, 1 - par):
                cp.start()

        w1b[...] = ws1[par].astype(BF16)
        w3b[...] = ws3[par].astype(BF16)
        w2b[...] = ws2[par].astype(BF16)

    @pl.when(i < n_valid[0])
    def _():
        words = x_ref[...]
        lo = pltpu.bitcast(words << 16, F32).astype(BF16)
        hi = pltpu.bitcast(words & jnp.uint32(0xFFFF0000), F32).astype(BF16)
        x = jnp.concatenate([lo, hi], axis=1)
        a = jnp.dot(x, w1b[...], preferred_element_type=F32)
        b = jnp.dot(x, w3b[...], preferred_element_type=F32)
        hid = (a * jax.nn.sigmoid(a)) * b
        o_ref[...] = jnp.dot(hid.astype(BF16), w2b[...], preferred_element_type=F32).astype(o_ref.dtype)


def _experts(xbuf, w1, w3, w2, layer, blk_e, blk_next, blk_par, n_valid):
    D, de = w1.shape[-2:]
    n_blk = blk_e.shape[0]
    any_spec = pl.BlockSpec(memory_space=pl.ANY)
    row_block = lambda rows, width: pl.BlockSpec((rows, width), lambda i, *s: (jnp.minimum(i, s[3][0] - 1), 0))
    return pl.pallas_call(
        functools.partial(_experts_kernel, layer=layer),
        grid_spec=pltpu.PrefetchScalarGridSpec(
            num_scalar_prefetch=4, grid=(n_blk,),
            in_specs=[row_block(MOE_ROWS, D // 2), any_spec, any_spec, any_spec],
            out_specs=row_block(MOE_ROWS, D),
            scratch_shapes=[
                pltpu.VMEM((2, D, de), F32), pltpu.VMEM((2, D, de), F32), pltpu.VMEM((2, de, D), F32),
                pltpu.SemaphoreType.DMA((2, 3)),
                pltpu.VMEM((D, de), BF16), pltpu.VMEM((D, de), BF16), pltpu.VMEM((de, D), BF16),
            ],
        ),
        out_shape=jax.ShapeDtypeStruct((n_blk * MOE_ROWS, D), BF16),
        compiler_params=_params("arbitrary"),
        name="experts",
    )(blk_e, blk_next, blk_par, n_valid, xbuf, w1, w3, w2)


def _moe_mix(x_ref, y0_ref, y1_ref, w_ref, g2_ref):
    w = w_ref[...]
    y = w[:, 0:1] * y0_ref[...].astype(F32) + w[:, 1:2] * y1_ref[...].astype(F32)
    return x_ref[...] + g2_ref[...] * y


def _combine_next_kernel(x_ref, y0_ref, y1_ref, w_ref, g2_ref, nw_ref, sh_ref, sc_ref, xo_ref, h_ref):
    xn = _moe_mix(x_ref, y0_ref, y1_ref, w_ref, g2_ref)
    xo_ref[...] = xn
    h_ref[...] = ((_rms(xn) * nw_ref[...]) * (1.0 + sc_ref[...]) + sh_ref[...]).astype(BF16)


def _combine_final_kernel(x_ref, y0_ref, y1_ref, w_ref, g2_ref, nf_ref, o_ref):
    o_ref[...] = _rms(_moe_mix(x_ref, y0_ref, y1_ref, w_ref, g2_ref)) * nf_ref[...]


def _token_specs(n_tokens, d):
    tiles = n_tokens // ROW_TILE
    return [pl.BlockSpec((ROW_TILE, d), lambda i: (i, 0)), pl.BlockSpec((ROW_TILE, d), lambda i: (i + tiles, 0)),
            pl.BlockSpec((ROW_TILE, TOP_K), lambda i: (i, 0))]


def _combine_next(x_new, ysel, wts, mod_l, nw_next, mod_next, rows):
    M, D = x_new.shape
    return pl.pallas_call(
        _combine_next_kernel,
        grid=(rows.n,),
        in_specs=[rows.row_spec(D), *_token_specs(M, D), rows.mod_spec(D, 5),
                  _const_spec((1, D)), rows.mod_spec(D, 0), rows.mod_spec(D, 1)],
        out_specs=[rows.row_spec(D), rows.row_spec(D)],
        out_shape=[jax.ShapeDtypeStruct((M, D), F32), jax.ShapeDtypeStruct((M, D), BF16)],
        compiler_params=_params("parallel"),
        name="combine_next",
    )(x_new, ysel, ysel, wts, mod_l, nw_next, mod_next, mod_next)


def _combine_final(x_new, ysel, wts, mod_l, nf, rows):
    M, D = x_new.shape
    n = wts.shape[0]
    return pl.pallas_call(
        _combine_final_kernel,
        grid=(rows.n,),
        in_specs=[rows.row_spec(D), *_token_specs(n, D), rows.mod_spec(D, 5), _const_spec((1, D))],
        out_specs=pl.BlockSpec((ROW_TILE, D), lambda i: (i, 0)),
        out_shape=jax.ShapeDtypeStruct((n, D), F32),
        compiler_params=_params("parallel"),
        name="combine_final",
    )(x_new, ysel, ysel, wts, mod_l, nf)


def _rope_tables(ctx_len, seq):
    quarter = HEAD_DIM // 4
    freqs = ROPE_BASE ** (-jnp.arange(quarter, dtype=F32) / quarter)
    pos = jnp.arange(seq, dtype=jnp.int32)
    row = (pos // GRID_W).astype(F32)[:, None] * freqs[None, :]
    col = (pos % GRID_W).astype(F32)[:, None] * freqs[None, :]
    cos = jnp.concatenate([jnp.cos(row), jnp.cos(row), jnp.cos(col), jnp.cos(col)], axis=-1)
    sin = jnp.concatenate([-jnp.sin(row), jnp.sin(row), -jnp.sin(col), jnp.sin(col)], axis=-1)
    cos = jnp.concatenate([jnp.ones((ctx_len, HEAD_DIM), F32), cos], axis=0)
    sin = jnp.concatenate([jnp.zeros((ctx_len, HEAD_DIM), F32), sin], axis=0)
    return cos, sin


def kernel(x, c, ctx, c_ctx, w_mod, b_mod, norm_mix, norm_ffn, w_in, qn_a, kn_a, sink_b, w_br_a, w_br_b, w_out, w_rg, b_rg, w_re, b_re, w1, w3, w2, norm_final):
    B, S, D = x.shape
    C = ctx.shape[1]
    L = w_mod.shape[0]
    T = C + S
    M = B * T
    assert C == ROW_TILE and S % ROW_TILE == 0 and T % PROJ_ROWS == 0 and B < MOD_ROWS
    tiles_per_batch = T // ROW_TILE

    cc = jnp.concatenate([c, c_ctx[None, :], jnp.zeros((MOD_ROWS - B - 1, D), F32)], axis=0)
    mod = _mod_vectors(cc, w_mod, b_mod).reshape(L, MOD_ROWS * 6, 1, D)
    cos, sin = _rope_tables(C, S)
    latent_rows = (jnp.arange(B, dtype=jnp.int32)[:, None] * T + C
                   + jnp.arange(S, dtype=jnp.int32)[None, :]).reshape(B * S)
    all_rows = jnp.arange(M, dtype=jnp.int32)

    q_kinds = lambda kind: tuple((kind, 0, k) for k in range(PROJ_COLS // HEAD_DIM))
    kv_kinds = lambda kind: ((kind, 0, 0), (kind, 0, 1), ("v", 1, 0), ("v", 1, 1))
    col = {"qa": 0, "kva": WIDTH, "qb": WIDTH + 2 * KVW, "kvb": 2 * WIDTH + 2 * KVW, "gates": 2 * WIDTH + 4 * KVW}

    every = _Rows(B, tiles_per_batch, latent_only=False)
    latent = _Rows(B, tiles_per_batch, latent_only=True)
    x_all, h = _norm_mod(x, ctx, norm_mix[0][None, :], mod[0])
    out = None
    for l in range(L):
        last = l == L - 1
        rows = latent if last else every
        proj = functools.partial(_proj, h, w_in, l, cos, sin, tiles_per_batch=tiles_per_batch)
        (qa,) = proj(qn_a[l][None, :], col["qa"], q_kinds("q_norm"), [(PROJ_COLS, WIDTH // PROJ_COLS)], name="proj_qa")
        ka, va = proj(kn_a[l][None, :], col["kva"], kv_kinds("k_norm"), [(KVW, 1), (KVW, 1)], name="proj_kva")
        (qb,) = proj(qn_a[l][None, :], col["qb"], q_kinds("q_rope"), [(PROJ_COLS, WIDTH // PROJ_COLS)], name="proj_qb")
        kb, vb = proj(kn_a[l][None, :], col["kvb"], kv_kinds("k_rope"), [(KVW, 1), (KVW, 1)], name="proj_kvb")
        (gates,) = proj(qn_a[l][None, :], col["gates"], q_kinds("gate"), [(PROJ_COLS, 2 * D // PROJ_COLS)], name="proj_gates")

        ya = _attention(qa, ka, va, None, B, T, C, last, window=False)
        yb = _attention(qb, kb, vb, sink_b[l], B, T, C, last, window=True)

        w_router = jnp.concatenate(
            [w_rg[l], w_re[l], jnp.zeros((D, ROUTER_LANES - N_GROUPS - N_EXPERTS), F32)], axis=1)
        w_router_hi = w_router.astype(BF16)
        w_router = jnp.concatenate([w_router_hi, (w_router - w_router_hi.astype(F32)).astype(BF16)], axis=1)
        b_router = jnp.concatenate(
            [b_rg[l], b_re[l], jnp.zeros((ROUTER_LANES - N_GROUPS - N_EXPERTS,), F32)])[None, :]
        x_new, h2, logits = _merge(ya, yb, gates, w_br_a[l].astype(BF16), w_br_b[l].astype(BF16),
                                   w_out[l].astype(BF16), x_all, mod[l], norm_ffn[l][None, :],
                                   w_router, b_router, rows)

        tok_rows = latent_rows if last else all_rows
        eid, wts = _route(logits[tok_rows] if last else logits)
        dest, slot_row, *plan = _dispatch(eid, tok_rows)
        ybuf = _experts(_sc_row_gather(h2, slot_row), w1, w3, w2, l, *plan)
        ysel = ybuf[dest.reshape(-1, TOP_K).T.reshape(-1)]
        if last:
            out = _combine_final(x_new, ysel, wts, mod[l], norm_final[None, :], rows).reshape(B, S, D)
        else:
            x_all, h = _combine_next(x_new, ysel, wts, mod[l], norm_mix[l + 1][None, :], mod[l + 1], rows)
    return out
```

```python
import functools

import jax
import jax.numpy as jnp
from jax import lax
from jax.experimental import pallas as pl
from jax.experimental.pallas import tpu as pltpu

F32 = jnp.float32
BF16 = jnp.bfloat16

GRID_W = 64
HEAD_DIM = 128
LANES = 128
HEADS = 8
KV_HEADS = 2
GROUP = HEADS // KV_HEADS
WIDTH = HEADS * HEAD_DIM
KVW = KV_HEADS * HEAD_DIM
WINDOW = 128
ROPE_BASE = 10000.0
ATTN_SCALE = HEAD_DIM ** -0.5
LOG2E = 1.4426950408889634
N_GROUPS = 8
EXPERTS_PER_GROUP = 8
N_EXPERTS = N_GROUPS * EXPERTS_PER_GROUP
TOP_K = 2
EPS = 1e-6
NEG_INF = -1e30

ROW_TILE = 256
PROJ_ROWS = 768
PROJ_COLS = 512
MOE_ROWS = 128
GATHER_AHEAD = 2
WEIGHT_DMA_PRIORITY = 1
ROUTER_LANES = 128
DENSE_CHAIN_HEADS = 4
WINDOW_CHAIN_HEADS = 1
MOD_ROWS = 8
VMEM_LIMIT = 56 * 1024 * 1024


def _params(*sem):
    return pltpu.CompilerParams(dimension_semantics=sem, vmem_limit_bytes=VMEM_LIMIT)


def _rms(y):
    return y * lax.rsqrt(jnp.mean(y * y, axis=-1, keepdims=True) + EPS)


def _mod_kernel(c_ref, w_ref, b_ref, o_ref):
    c = c_ref[...]
    a = (c * jax.nn.sigmoid(c)).astype(BF16)
    o_ref[...] = jnp.dot(a, w_ref[...].astype(BF16), preferred_element_type=F32) + b_ref[...]


def _mod_vectors(cc, w_mod, b_mod):
    L, D, N = w_mod.shape
    tn = 1024
    return pl.pallas_call(
        _mod_kernel,
        grid=(L, N // tn),
        in_specs=[
            pl.BlockSpec((MOD_ROWS, D), lambda l, j: (0, 0)),
            pl.BlockSpec((None, D, tn), lambda l, j: (l, 0, j)),
            pl.BlockSpec((None, 1, tn), lambda l, j: (l, 0, j)),
        ],
        out_specs=pl.BlockSpec((None, MOD_ROWS, tn), lambda l, j: (l, 0, j)),
        out_shape=jax.ShapeDtypeStruct((L, MOD_ROWS, N), F32),
        compiler_params=_params("parallel", "parallel"),
        name="mod_vectors",
    )(cc, w_mod, b_mod.reshape(L, 1, N))


class _Rows:
    def __init__(self, batch, tiles_per_batch, latent_only):
        self.batch = batch
        self.tpb = tiles_per_batch
        self.latent_only = latent_only
        self.n = batch * (tiles_per_batch - 1 if latent_only else tiles_per_batch)

    def tile(self, i):
        if self.latent_only:
            per = self.tpb - 1
            return (i // per) * self.tpb + 1 + i % per
        return i

    def mod_row(self, i):
        if self.latent_only:
            return i // (self.tpb - 1)
        return jnp.where(i % self.tpb == 0, self.batch, i // self.tpb)

    def row_spec(self, width):
        return pl.BlockSpec((ROW_TILE, width), lambda i: (self.tile(i), 0))

    def mod_spec(self, d, k):
        return pl.BlockSpec((None, 1, d), lambda i: (self.mod_row(i) * 6 + k, 0, 0))


def _const_spec(shape):
    zeros = (0,) * len(shape)
    return pl.BlockSpec(shape, lambda *_: zeros, pipeline_mode=pl.Buffered(1))


def _norm_mod_kernel(x_ref, ctx_ref, nw_ref, sh_ref, sc_ref, xo_ref, h_ref):
    def emit(src_ref):
        x = src_ref[...]
        xo_ref[...] = x
        h_ref[...] = ((_rms(x) * nw_ref[...]) * (1.0 + sc_ref[...]) + sh_ref[...]).astype(BF16)

    is_ctx = pl.program_id(1) == 0
    pl.when(is_ctx)(lambda: emit(ctx_ref))
    pl.when(jnp.logical_not(is_ctx))(lambda: emit(x_ref))


def _norm_mod(x, ctx, nw, mod_l):
    B, S, D = x.shape
    tiles = 1 + S // ROW_TILE
    mod_spec = lambda k: pl.BlockSpec((None, 1, D), lambda b, t: (jnp.where(t == 0, B, b) * 6 + k, 0, 0))
    out_spec = pl.BlockSpec((ROW_TILE, D), lambda b, t: (b * tiles + t, 0))
    return pl.pallas_call(
        _norm_mod_kernel,
        grid=(B, tiles),
        in_specs=[pl.BlockSpec((None, ROW_TILE, D), lambda b, t: (b, jnp.maximum(t - 1, 0), 0)),
                  pl.BlockSpec((None, ROW_TILE, D), lambda b, t: (b, 0, 0)),
                  _const_spec((1, D)), mod_spec(0), mod_spec(1)],
        out_specs=[out_spec, out_spec],
        out_shape=[jax.ShapeDtypeStruct((B * tiles * ROW_TILE, D), F32),
                   jax.ShapeDtypeStruct((B * tiles * ROW_TILE, D), BF16)],
        compiler_params=_params("parallel", "parallel"),
        name="norm_mod",
    )(x, ctx, nw, mod_l, mod_l)


ROTARY_KINDS = ("q_norm", "k_norm", "q_rope", "k_rope")


def _pair_halves(a):
    q = HEAD_DIM // 4
    return jnp.concatenate([a[..., 0:q], a[..., 2 * q:3 * q], a[..., q:2 * q], a[..., 3 * q:]], axis=-1)


def _rope(y, cos, sin):
    return y * cos + pltpu.roll(y, HEAD_DIM // 2, 1) * sin


def _proj_kernel(h_ref, w_ref, cos_ref, sin_ref, gain_ref, *refs, kinds):
    *out_refs, wb_ref = refs

    @pl.when(pl.program_id(1) == 0)
    def _():
        for c, (kind, _, _) in enumerate(kinds):
            w = w_ref[:, c * HEAD_DIM:(c + 1) * HEAD_DIM]
            wb_ref[:, c * HEAD_DIM:(c + 1) * HEAD_DIM] = (_pair_halves(w) if kind in ROTARY_KINDS else w).astype(BF16)

    acc = jnp.dot(h_ref[...], wb_ref[...], preferred_element_type=F32)
    for c, (kind, out_idx, out_chunk) in enumerate(kinds):
        y = acc[:, c * HEAD_DIM:(c + 1) * HEAD_DIM]
        if kind in ("q_norm", "k_norm"):
            y = _rms(y) * gain_ref[...]
        if kind in ROTARY_KINDS:
            y = _rope(y, cos_ref[...], sin_ref[...])
        if kind in ("q_norm", "q_rope"):
            y = y * (ATTN_SCALE * LOG2E)
        if kind == "gate":
            y = jax.nn.sigmoid(y)
        out_refs[out_idx][:, out_chunk * HEAD_DIM:(out_chunk + 1) * HEAD_DIM] = y.astype(BF16)


def _proj(h, w_in, layer, cos, sin, gain, col0, kinds, out_widths, tiles_per_batch, name):
    M, D = h.shape
    n_col_tiles = out_widths[0][1]
    cb0 = col0 // PROJ_COLS
    proj_tiles_per_batch = tiles_per_batch * ROW_TILE // PROJ_ROWS
    return pl.pallas_call(
        functools.partial(_proj_kernel, kinds=kinds),
        grid=(n_col_tiles, M // PROJ_ROWS),
        in_specs=[
            pl.BlockSpec((PROJ_ROWS, D), lambda j, i: (i, 0)),
            pl.BlockSpec((None, D, PROJ_COLS), lambda j, i: (layer, 0, cb0 + j)),
            pl.BlockSpec((PROJ_ROWS, HEAD_DIM), lambda j, i: (i % proj_tiles_per_batch, 0)),
            pl.BlockSpec((PROJ_ROWS, HEAD_DIM), lambda j, i: (i % proj_tiles_per_batch, 0)),
            pl.BlockSpec((1, HEAD_DIM), lambda j, i: (0, 0)),
        ],
        out_specs=[pl.BlockSpec((PROJ_ROWS, w), lambda j, i: (i, j)) for w, _ in out_widths],
        out_shape=[jax.ShapeDtypeStruct((M, w * n), BF16) for w, n in out_widths],
        scratch_shapes=[pltpu.VMEM((D, PROJ_COLS), BF16)],
        compiler_params=_params("parallel", "arbitrary"),
        name=name,
    )(h, w_in, cos, sin, gain)


def _stack_heads(q_ref, g0, n):
    return jnp.concatenate([q_ref[:, g * HEAD_DIM:(g + 1) * HEAD_DIM] for g in range(g0, g0 + n)], axis=0)


def _write_heads(o_ref, g0, n, o):
    rows = o_ref.shape[0]
    for j in range(n):
        o_ref[:, (g0 + j) * HEAD_DIM:(g0 + j + 1) * HEAD_DIM] = o[j * rows:(j + 1) * rows].astype(o_ref.dtype)


def _scores(q, k):
    return lax.dot_general(q, k, (((1,), (1,)), ((), ())), preferred_element_type=F32)


def _online_step(q, k, v, m, l, acc):
    s = _scores(q, k)
    m_new = jnp.maximum(m, jnp.max(s, axis=-1, keepdims=True))
    alpha = jnp.exp2(m - m_new)
    p = jnp.exp2(s - m_new)
    l = alpha * l + jnp.sum(p, axis=-1, keepdims=True)
    acc = alpha * acc + jnp.dot(p.astype(BF16), v, preferred_element_type=F32)
    return m_new, l, acc


def _dense_attn_kernel(q_ref, k_ref, v_ref, o_ref, *, ctx_len, total_len, key_chunk, ctx_tile):
    def run(n_keys, chunk):
        for g0 in range(0, GROUP, DENSE_CHAIN_HEADS):
            q = _stack_heads(q_ref, g0, DENSE_CHAIN_HEADS)
            r = q.shape[0]
            m = jnp.full((r, 1), -jnp.inf, F32)
            l = jnp.zeros((r, 1), F32)
            acc = jnp.zeros((r, HEAD_DIM), F32)
            for c in range(n_keys // chunk):
                m, l, acc = _online_step(q, k_ref[c * chunk:(c + 1) * chunk, :],
                                         v_ref[c * chunk:(c + 1) * chunk, :], m, l, acc)
            _write_heads(o_ref, g0, DENSE_CHAIN_HEADS, acc * (1.0 / l))

    if ctx_tile:
        is_ctx = pl.program_id(2) == 0
        pl.when(is_ctx)(lambda: run(ctx_len, ctx_len))
        pl.when(jnp.logical_not(is_ctx))(lambda: run(total_len, key_chunk))
    else:
        run(total_len, key_chunk)


def _window_attn_kernel(sink_ref, q_ref, k_ref, v_ref, o_ref, *, ctx_len, total_len, ctx_tile, q_off):
    kvh = pl.program_id(1)
    tq = q_ref.shape[0]
    span = tq + 2 * WINDOW
    n = WINDOW_CHAIN_HEADS

    def attend(parts):
        for g0 in range(0, GROUP, n):
            q = _stack_heads(q_ref, g0, n)
            sink = jnp.concatenate(
                [jnp.full((tq, 1), sink_ref[kvh * GROUP + g] * LOG2E, F32) for g in range(g0, g0 + n)], axis=0)
            scores = []
            m = sink
            for k, _, bias in parts:
                s = _scores(q, k)
                if bias is not None:
                    s = s + bias
                scores.append(s)
                m = jnp.maximum(m, jnp.max(s, axis=-1, keepdims=True))
            l = jnp.exp2(sink - m)
            o = jnp.zeros((n * tq, HEAD_DIM), F32)
            for s, (_, v, _) in zip(scores, parts):
                p = jnp.exp2(s - m)
                l = l + jnp.sum(p, axis=-1, keepdims=True)
                o = o + jnp.dot(p.astype(BF16), v, preferred_element_type=F32)
            _write_heads(o_ref, g0, n, o * (1.0 / l))

    def ctx_part():
        return k_ref[0:ctx_len, :], v_ref[0:ctx_len, :], None

    def run_latent():
        q0 = (pl.program_id(2) + q_off) * tq - ctx_len
        n_latent = total_len - ctx_len
        start = pl.multiple_of(jnp.clip(q0 - WINDOW, 0, n_latent - span), WINDOW)
        row = lax.broadcasted_iota(jnp.int32, (n * tq, span), 0) & (tq - 1)
        col = lax.broadcasted_iota(jnp.int32, (n * tq, span), 1)
        bias = jnp.where(jnp.abs(col - row + (start - q0)) <= WINDOW, 0.0, NEG_INF).astype(F32)
        attend([ctx_part(), (k_ref[pl.ds(ctx_len + start, span), :], v_ref[pl.ds(ctx_len + start, span), :], bias)])

    if ctx_tile:
        is_ctx = pl.program_id(2) == 0
        pl.when(is_ctx)(lambda: attend([ctx_part()]))
        pl.when(jnp.logical_not(is_ctx))(run_latent)
    else:
        run_latent()


def _attention(q, k, v, sink, batch, total_len, ctx_len, latent_only, window):
    M = q.shape[0]
    q3, k3, v3 = (a.reshape(batch, total_len, a.shape[1]) for a in (q, k, v))
    tiles = total_len // ROW_TILE
    q_off = 1 if latent_only else 0
    gw = GROUP * HEAD_DIM
    q_spec = pl.BlockSpec((None, ROW_TILE, gw), lambda b, h, i, *_: (b, i + q_off, h))
    kv_spec = pl.BlockSpec((None, total_len, HEAD_DIM), lambda b, h, i, *_: (b, 0, h))
    grid = (batch, KV_HEADS, tiles - q_off)
    out_shape = jax.ShapeDtypeStruct(q3.shape, BF16)
    if window:
        kern = functools.partial(_window_attn_kernel, ctx_len=ctx_len, total_len=total_len,
                                 ctx_tile=not latent_only, q_off=q_off)
        out = pl.pallas_call(
            kern,
            grid_spec=pltpu.PrefetchScalarGridSpec(
                num_scalar_prefetch=1, grid=grid,
                in_specs=[q_spec, kv_spec, kv_spec], out_specs=q_spec),
            out_shape=out_shape,
            compiler_params=_params("parallel", "parallel", "parallel"),
            name="window_attention",
        )(sink, q3, k3, v3)
    else:
        kern = functools.partial(_dense_attn_kernel, ctx_len=ctx_len, total_len=total_len,
                                 key_chunk=total_len // 3, ctx_tile=not latent_only)
        out = pl.pallas_call(
            kern, grid=grid, in_specs=[q_spec, kv_spec, kv_spec], out_specs=q_spec,
            out_shape=out_shape,
            compiler_params=_params("parallel", "parallel", "parallel"),
            name="dense_attention",
        )(q3, k3, v3)
    return out.reshape(M, q.shape[1])


def _merge_kernel(ya_ref, yb_ref, ga_ref, gb_ref, wa_ref, wb_ref, wo_ref, x_ref, g1_ref,
                  nw_ref, sh_ref, sc_ref, wr_ref, br_ref, xo_ref, h_ref, lg_ref):
    a = jnp.dot(ya_ref[...], wa_ref[...], preferred_element_type=F32)
    b = jnp.dot(yb_ref[...], wb_ref[...], preferred_element_type=F32)
    m = ga_ref[...].astype(F32) * a + gb_ref[...].astype(F32) * b
    o = jnp.dot(m.astype(BF16), wo_ref[...], preferred_element_type=F32)
    xn = x_ref[...] + g1_ref[...] * o
    xo_ref[...] = xn
    h = (_rms(xn) * nw_ref[...]) * (1.0 + sc_ref[...]) + sh_ref[...]
    h_hi = h.astype(BF16)
    bits = pltpu.bitcast(h_hi.astype(F32), jnp.uint32)
    half = h.shape[1] // 2
    words = (bits[:, :half] >> 16) | (bits[:, half:] & jnp.uint32(0xFFFF0000))
    chunks = half // LANES
    for c in range(chunks):
        h_ref[pl.ds(c, h.shape[0], stride=chunks), :] = words[:, c * LANES:(c + 1) * LANES]
    h_lo = (h - h_hi.astype(F32)).astype(BF16)
    lg = jnp.dot(h_hi, wr_ref[...], preferred_element_type=F32)
    lg_lo = jnp.dot(h_lo, wr_ref[:, :ROUTER_LANES], preferred_element_type=F32)
    lg_ref[...] = lg[:, :ROUTER_LANES] + (lg[:, ROUTER_LANES:] + lg_lo) + br_ref[...]


def _merge(ya, yb, gates, wa, wb, wo, x_all, mod_l, nw, wr, br, rows):
    M, D = x_all.shape
    W = ya.shape[1]
    chunks = D // 2 // LANES
    return pl.pallas_call(
        _merge_kernel,
        grid=(rows.n,),
        in_specs=[
            rows.row_spec(W), rows.row_spec(W),
            pl.BlockSpec((ROW_TILE, D), lambda i: (rows.tile(i), 0)),
            pl.BlockSpec((ROW_TILE, D), lambda i: (rows.tile(i), 1)),
            _const_spec((W, D)), _const_spec((W, D)), _const_spec((D, D)),
            rows.row_spec(D), rows.mod_spec(D, 2),
            _const_spec((1, D)), rows.mod_spec(D, 3), rows.mod_spec(D, 4),
            _const_spec((D, 2 * ROUTER_LANES)), _const_spec((1, ROUTER_LANES)),
        ],
        out_specs=[rows.row_spec(D),
                   pl.BlockSpec((ROW_TILE * chunks, LANES), lambda i: (rows.tile(i), 0)),
                   rows.row_spec(ROUTER_LANES)],
        out_shape=[jax.ShapeDtypeStruct((M, D), F32), jax.ShapeDtypeStruct((M * chunks, LANES), jnp.uint32),
                   jax.ShapeDtypeStruct((M, ROUTER_LANES), F32)],
        compiler_params=_params("parallel"),
        name="merge",
    )(ya, yb, gates, gates, wa, wb, wo, x_all, mod_l, nw, mod_l, mod_l, wr, br)


def _route(logits):
    n = logits.shape[0]
    gl = logits[:, :N_GROUPS]
    el = logits[:, N_GROUPS:N_GROUPS + N_EXPERTS].reshape(n, N_GROUPS, EXPERTS_PER_GROUP)
    g = jnp.argmax(gl, axis=-1).astype(jnp.int32)
    pg = 1.0 / jnp.sum(jnp.exp(gl - jnp.max(gl, axis=-1, keepdims=True)), axis=-1, keepdims=True)
    in_group = jnp.arange(N_GROUPS, dtype=jnp.int32)[None, :, None] == g[:, None, None]
    el_g = jnp.sum(jnp.where(in_group, el, 0.0), axis=1)
    tv, ti = lax.top_k(el_g, TOP_K)
    wts = jax.nn.softmax(tv, axis=-1) * pg
    eid = g[:, None] * EXPERTS_PER_GROUP + ti.astype(jnp.int32)
    return eid, wts


def _dispatch(eid, tok_rows, row_scale):
    n = eid.shape[0]
    a = n * TOP_K
    e_flat = eid.reshape(a)
    onehot = (e_flat[:, None] == jnp.arange(N_EXPERTS, dtype=jnp.int32)[None, :]).astype(jnp.int32)
    csum = jnp.cumsum(onehot, axis=0)
    counts = csum[-1]
    padded = (counts + MOE_ROWS - 1) // MOE_ROWS * MOE_ROWS
    pad_end = jnp.cumsum(padded)
    pad_start = pad_end - padded
    dest = jnp.sum(onehot * (csum - 1 + pad_start[None, :]), axis=1)
    n_grid = a // MOE_ROWS + N_EXPERTS - 1 + GATHER_AHEAD
    slot_row = jnp.zeros((n_grid * MOE_ROWS,), jnp.int32).at[dest].set(jnp.repeat(tok_rows * row_scale, TOP_K))
    blk = jnp.arange(n_grid, dtype=jnp.int32)
    n_valid = (pad_end[-1] // MOE_ROWS).astype(jnp.int32)
    blk_c = jnp.minimum(blk, n_valid - 1)
    blk_e = jnp.clip(jnp.searchsorted(pad_end, blk_c * MOE_ROWS, side="right"), 0, N_EXPERTS - 1).astype(jnp.int32)
    experts = jnp.arange(N_EXPERTS, dtype=jnp.int32)
    used = counts > 0
    first_used_from = lax.cummin(jnp.where(used, experts, N_EXPERTS), reverse=True)
    next_used = jnp.concatenate([first_used_from[1:], jnp.full((1,), N_EXPERTS, jnp.int32)])
    next_used = jnp.where(next_used == N_EXPERTS, -1, next_used)
    parity = (jnp.cumsum(used.astype(jnp.int32)) - 1) % 2
    return (dest, blk_e, next_used[blk_e].astype(jnp.int32), parity[blk_e].astype(jnp.int32), slot_row,
            n_valid.reshape(1))


def _experts_kernel(blk_e, blk_next, blk_par, slot_row, n_valid,
                    h_hbm, w1_hbm, w3_hbm, w2_hbm, o_ref,
                    xg, xsem, ws1, ws3, ws2, wsem, w1b, w3b, w2b, *, layer):
    i = pl.program_id(0)
    nv = n_valid[0]
    chunks = xg.shape[1] // MOE_ROWS
    e = blk_e[i]
    par = blk_par[i]
    fresh = jnp.logical_or(i == 0, e != blk_e[jnp.maximum(i - 1, 0)])

    def weight_copies(expert, p):
        return (pltpu.make_async_copy(w1_hbm.at[layer, expert], ws1.at[p], wsem.at[p, 0]),
                pltpu.make_async_copy(w3_hbm.at[layer, expert], ws3.at[p], wsem.at[p, 1]),
                pltpu.make_async_copy(w2_hbm.at[layer, expert], ws2.at[p], wsem.at[p, 2]))

    def for_rows(blk, fn):
        buf = lax.rem(blk, GATHER_AHEAD + 1)
        base = blk * MOE_ROWS
        for j in range(MOE_ROWS):
            r = pl.multiple_of(slot_row[base + j], chunks)
            fn(pltpu.make_async_copy(h_hbm.at[pl.ds(r, chunks), :], xg.at[buf, pl.ds(j * chunks, chunks), :],
                                     xsem.at[buf]))

    @pl.when(i == 0)
    def _():
        for cp in weight_copies(e, par):
            cp.start(priority=WEIGHT_DMA_PRIORITY)
        for blk in range(GATHER_AHEAD):
            for_rows(blk, lambda cp: cp.start())

    @pl.when(jnp.logical_and(i >= nv, i < nv + GATHER_AHEAD))
    def _():
        for_rows(i, lambda cp: cp.wait())

    @pl.when(fresh)
    def _():
        for cp in weight_copies(e, par):
            cp.wait()
        nxt = blk_next[i]

        @pl.when(nxt >= 0)
        def _():
            for cp in weight_copies(nxt, 1 - par):
                cp.start(priority=WEIGHT_DMA_PRIORITY)

        w1b[...] = ws1[par].astype(BF16)
        w3b[...] = ws3[par].astype(BF16)
        w2b[...] = ws2[par].astype(BF16)

    @pl.when(i < nv)
    def _():
        for_rows(i, lambda cp: cp.wait())
        words = jnp.concatenate(
            [xg[lax.rem(i, GATHER_AHEAD + 1), pl.ds(c, MOE_ROWS, stride=chunks), :] for c in range(chunks)],
            axis=1)
        x = jnp.concatenate([pltpu.bitcast(words << 16, F32).astype(BF16),
                             pltpu.bitcast(words & jnp.uint32(0xFFFF0000), F32).astype(BF16)], axis=1)
        for_rows(i + GATHER_AHEAD, lambda cp: cp.start())
        a = jnp.dot(x, w1b[...], preferred_element_type=F32)
        b = jnp.dot(x, w3b[...], preferred_element_type=F32)
        hid = (a * jax.nn.sigmoid(a)) * b
        o_ref[...] = jnp.dot(hid.astype(BF16), w2b[...], preferred_element_type=F32).astype(o_ref.dtype)


def _experts(h2, w1, w3, w2, layer, blk_e, blk_next, blk_par, slot_row, n_valid):
    D, de = w1.shape[-2:]
    chunks = D // 2 // LANES
    n_blk = blk_e.shape[0]
    any_spec = pl.BlockSpec(memory_space=pl.ANY)
    return pl.pallas_call(
        functools.partial(_experts_kernel, layer=layer),
        grid_spec=pltpu.PrefetchScalarGridSpec(
            num_scalar_prefetch=5, grid=(n_blk,),
            in_specs=[any_spec, any_spec, any_spec, any_spec],
            out_specs=pl.BlockSpec((MOE_ROWS, D), lambda i, *s: (jnp.minimum(i, s[4][0] - 1), 0)),
            scratch_shapes=[
                pltpu.VMEM((GATHER_AHEAD + 1, MOE_ROWS * chunks, LANES), jnp.uint32),
                pltpu.SemaphoreType.DMA((GATHER_AHEAD + 1,)),
                pltpu.VMEM((2, D, de), F32), pltpu.VMEM((2, D, de), F32), pltpu.VMEM((2, de, D), F32),
                pltpu.SemaphoreType.DMA((2, 3)),
                pltpu.VMEM((D, de), BF16), pltpu.VMEM((D, de), BF16), pltpu.VMEM((de, D), BF16),
            ],
        ),
        out_shape=jax.ShapeDtypeStruct((n_blk * MOE_ROWS, D), BF16),
        compiler_params=_params("arbitrary"),
        name="experts",
    )(blk_e, blk_next, blk_par, slot_row, n_valid, h2, w1, w3, w2)


def _moe_mix(x_ref, y0_ref, y1_ref, w_ref, g2_ref):
    w = w_ref[...]
    y = w[:, 0:1] * y0_ref[...].astype(F32) + w[:, 1:2] * y1_ref[...].astype(F32)
    return x_ref[...] + g2_ref[...] * y


def _combine_next_kernel(x_ref, y0_ref, y1_ref, w_ref, g2_ref, nw_ref, sh_ref, sc_ref, xo_ref, h_ref):
    xn = _moe_mix(x_ref, y0_ref, y1_ref, w_ref, g2_ref)
    xo_ref[...] = xn
    h_ref[...] = ((_rms(xn) * nw_ref[...]) * (1.0 + sc_ref[...]) + sh_ref[...]).astype(BF16)


def _combine_final_kernel(x_ref, y0_ref, y1_ref, w_ref, g2_ref, nf_ref, o_ref):
    o_ref[...] = _rms(_moe_mix(x_ref, y0_ref, y1_ref, w_ref, g2_ref)) * nf_ref[...]


def _token_specs(n_tokens, d):
    tiles = n_tokens // ROW_TILE
    return [pl.BlockSpec((ROW_TILE, d), lambda i: (i, 0)), pl.BlockSpec((ROW_TILE, d), lambda i: (i + tiles, 0)),
            pl.BlockSpec((ROW_TILE, TOP_K), lambda i: (i, 0))]


def _combine_next(x_new, ysel, wts, mod_l, nw_next, mod_next, rows):
    M, D = x_new.shape
    return pl.pallas_call(
        _combine_next_kernel,
        grid=(rows.n,),
        in_specs=[rows.row_spec(D), *_token_specs(M, D), rows.mod_spec(D, 5),
                  _const_spec((1, D)), rows.mod_spec(D, 0), rows.mod_spec(D, 1)],
        out_specs=[rows.row_spec(D), rows.row_spec(D)],
        out_shape=[jax.ShapeDtypeStruct((M, D), F32), jax.ShapeDtypeStruct((M, D), BF16)],
        compiler_params=_params("parallel"),
        name="combine_next",
    )(x_new, ysel, ysel, wts, mod_l, nw_next, mod_next, mod_next)


def _combine_final(x_new, ysel, wts, mod_l, nf, rows):
    M, D = x_new.shape
    n = wts.shape[0]
    return pl.pallas_call(
        _combine_final_kernel,
        grid=(rows.n,),
        in_specs=[rows.row_spec(D), *_token_specs(n, D), rows.mod_spec(D, 5), _const_spec((1, D))],
        out_specs=pl.BlockSpec((ROW_TILE, D), lambda i: (i, 0)),
        out_shape=jax.ShapeDtypeStruct((n, D), F32),
        compiler_params=_params("parallel"),
        name="combine_final",
    )(x_new, ysel, ysel, wts, mod_l, nf)


def _rope_tables(ctx_len, seq):
    quarter = HEAD_DIM // 4
    freqs = ROPE_BASE ** (-jnp.arange(quarter, dtype=F32) / quarter)
    pos = jnp.arange(seq, dtype=jnp.int32)
    row = (pos // GRID_W).astype(F32)[:, None] * freqs[None, :]
    col = (pos % GRID_W).astype(F32)[:, None] * freqs[None, :]
    cos = jnp.concatenate([jnp.cos(row), jnp.cos(col), jnp.cos(row), jnp.cos(col)], axis=-1)
    sin = jnp.concatenate([-jnp.sin(row), -jnp.sin(col), jnp.sin(row), jnp.sin(col)], axis=-1)
    cos = jnp.concatenate([jnp.ones((ctx_len, HEAD_DIM), F32), cos], axis=0)
    sin = jnp.concatenate([jnp.zeros((ctx_len, HEAD_DIM), F32), sin], axis=0)
    return cos, sin


def kernel(x, c, ctx, c_ctx, w_mod, b_mod, norm_mix, norm_ffn, w_in, qn_a, kn_a, sink_b, w_br_a, w_br_b, w_out, w_rg, b_rg, w_re, b_re, w1, w3, w2, norm_final):
    B, S, D = x.shape
    C = ctx.shape[1]
    L = w_mod.shape[0]
    T = C + S
    M = B * T
    assert C == ROW_TILE and S % ROW_TILE == 0 and T % PROJ_ROWS == 0 and B < MOD_ROWS
    tiles_per_batch = T // ROW_TILE

    cc = jnp.concatenate([c, c_ctx[None, :], jnp.zeros((MOD_ROWS - B - 1, D), F32)], axis=0)
    mod = _mod_vectors(cc, w_mod, b_mod).reshape(L, MOD_ROWS * 6, 1, D)
    cos, sin = _rope_tables(C, S)
    latent_rows = (jnp.arange(B, dtype=jnp.int32)[:, None] * T + C
                   + jnp.arange(S, dtype=jnp.int32)[None, :]).reshape(B * S)
    all_rows = jnp.arange(M, dtype=jnp.int32)

    q_kinds = lambda kind: tuple((kind, 0, k) for k in range(PROJ_COLS // HEAD_DIM))
    kv_kinds = lambda kind: ((kind, 0, 0), (kind, 0, 1), ("v", 1, 0), ("v", 1, 1))
    col = {"qa": 0, "kva": WIDTH, "qb": WIDTH + 2 * KVW, "kvb": 2 * WIDTH + 2 * KVW, "gates": 2 * WIDTH + 4 * KVW}

    every = _Rows(B, tiles_per_batch, latent_only=False)
    latent = _Rows(B, tiles_per_batch, latent_only=True)
    x_all, h = _norm_mod(x, ctx, norm_mix[0][None, :], mod[0])
    out = None
    for l in range(L):
        last = l == L - 1
        rows = latent if last else every
        proj = functools.partial(_proj, h, w_in, l, cos, sin, tiles_per_batch=tiles_per_batch)
        q_gain, k_gain = _pair_halves(qn_a[l])[None, :], _pair_halves(kn_a[l])[None, :]
        (qa,) = proj(q_gain,col["qa"], q_kinds("q_norm"), [(PROJ_COLS, WIDTH // PROJ_COLS)], name="proj_qa")
        ka, va = proj(k_gain,col["kva"], kv_kinds("k_norm"), [(KVW, 1), (KVW, 1)], name="proj_kva")
        (qb,) = proj(q_gain,col["qb"], q_kinds("q_rope"), [(PROJ_COLS, WIDTH // PROJ_COLS)], name="proj_qb")
        kb, vb = proj(k_gain,col["kvb"], kv_kinds("k_rope"), [(KVW, 1), (KVW, 1)], name="proj_kvb")
        (gates,) = proj(q_gain,col["gates"], q_kinds("gate"), [(PROJ_COLS, 2 * D // PROJ_COLS)], name="proj_gates")

        ya = _attention(qa, ka, va, None, B, T, C, last, window=False)
        yb = _attention(qb, kb, vb, sink_b[l], B, T, C, last, window=True)

        w_router = jnp.concatenate(
            [w_rg[l], w_re[l], jnp.zeros((D, ROUTER_LANES - N_GROUPS - N_EXPERTS), F32)], axis=1)
        w_router_hi = w_router.astype(BF16)
        w_router = jnp.concatenate([w_router_hi, (w_router - w_router_hi.astype(F32)).astype(BF16)], axis=1)
        b_router = jnp.concatenate(
            [b_rg[l], b_re[l], jnp.zeros((ROUTER_LANES - N_GROUPS - N_EXPERTS,), F32)])[None, :]
        x_new, h2, logits = _merge(ya, yb, gates, w_br_a[l].astype(BF16), w_br_b[l].astype(BF16),
                                   w_out[l].astype(BF16), x_all, mod[l], norm_ffn[l][None, :],
                                   w_router, b_router, rows)

        tok_rows = latent_rows if last else all_rows
        eid, wts = _route(logits[tok_rows] if last else logits)
        dest, *plan = _dispatch(eid, tok_rows, D // 2 // LANES)
        ybuf = _experts(h2, w1, w3, w2, l, *plan)
        ysel = ybuf[dest.reshape(-1, TOP_K).T.reshape(-1)]
        if last:
            out = _combine_final(x_new, ysel, wts, mod[l], norm_final[None, :], rows).reshape(B, S, D)
        else:
            x_all, h = _combine_next(x_new, ysel, wts, mod[l], norm_mix[l + 1][None, :], mod[l + 1], rows)
    return out
```

```python
import functools

import jax
import jax.numpy as jnp
from jax import lax
from jax.experimental import pallas as pl
from jax.experimental.pallas import tpu as pltpu

F32 = jnp.float32
BF16 = jnp.bfloat16

GRID_W = 64
HEAD_DIM = 128
LANES = 128
HEADS = 8
KV_HEADS = 2
GROUP = HEADS // KV_HEADS
WIDTH = HEADS * HEAD_DIM
KVW = KV_HEADS * HEAD_DIM
WINDOW = 128
ROPE_BASE = 10000.0
ATTN_SCALE = HEAD_DIM ** -0.5
LOG2E = 1.4426950408889634
N_GROUPS = 8
EXPERTS_PER_GROUP = 8
N_EXPERTS = N_GROUPS * EXPERTS_PER_GROUP
TOP_K = 2
EPS = 1e-6
NEG_INF = -1e30

ROW_TILE = 256
PROJ_ROWS = 768
PROJ_COLS = 512
MOE_ROWS = 128
GATHER_AHEAD = 2
WEIGHT_DMA_PRIORITY = 1
ROUTER_LANES = 128
DENSE_CHAIN_HEADS = 4
WINDOW_CHAIN_HEADS = 1
MOD_ROWS = 8
VMEM_LIMIT = 56 * 1024 * 1024


def _params(*sem):
    return pltpu.CompilerParams(dimension_semantics=sem, vmem_limit_bytes=VMEM_LIMIT)


def _rms(y):
    return y * lax.rsqrt(jnp.mean(y * y, axis=-1, keepdims=True) + EPS)


def _mod_kernel(c_ref, w_ref, b_ref, o_ref):
    c = c_ref[...]
    a = (c * jax.nn.sigmoid(c)).astype(BF16)
    o_ref[...] = jnp.dot(a, w_ref[...].astype(BF16), preferred_element_type=F32) + b_ref[...]


def _mod_vectors(cc, w_mod, b_mod):
    L, D, N = w_mod.shape
    tn = 1024
    return pl.pallas_call(
        _mod_kernel,
        grid=(L, N // tn),
        in_specs=[
            pl.BlockSpec((MOD_ROWS, D), lambda l, j: (0, 0)),
            pl.BlockSpec((None, D, tn), lambda l, j: (l, 0, j)),
            pl.BlockSpec((None, 1, tn), lambda l, j: (l, 0, j)),
        ],
        out_specs=pl.BlockSpec((None, MOD_ROWS, tn), lambda l, j: (l, 0, j)),
        out_shape=jax.ShapeDtypeStruct((L, MOD_ROWS, N), F32),
        compiler_params=_params("parallel", "parallel"),
        name="mod_vectors",
    )(cc, w_mod, b_mod.reshape(L, 1, N))


class _Rows:
    def __init__(self, batch, tiles_per_batch, latent_only):
        self.batch = batch
        self.tpb = tiles_per_batch
        self.latent_only = latent_only
        self.n = batch * (tiles_per_batch - 1 if latent_only else tiles_per_batch)

    def tile(self, i):
        if self.latent_only:
            per = self.tpb - 1
            return (i // per) * self.tpb + 1 + i % per
        return i

    def mod_row(self, i):
        if self.latent_only:
            return i // (self.tpb - 1)
        return jnp.where(i % self.tpb == 0, self.batch, i // self.tpb)

    def row_spec(self, width):
        return pl.BlockSpec((ROW_TILE, width), lambda i: (self.tile(i), 0))

    def mod_spec(self, d, k):
        return pl.BlockSpec((None, 1, d), lambda i: (self.mod_row(i) * 6 + k, 0, 0))


def _const_spec(shape):
    zeros = (0,) * len(shape)
    return pl.BlockSpec(shape, lambda *_: zeros, pipeline_mode=pl.Buffered(1))


def _norm_mod_kernel(x_ref, ctx_ref, nw_ref, sh_ref, sc_ref, xo_ref, h_ref):
    def emit(src_ref):
        x = src_ref[...]
        xo_ref[...] = x
        h_ref[...] = ((_rms(x) * nw_ref[...]) * (1.0 + sc_ref[...]) + sh_ref[...]).astype(BF16)

    is_ctx = pl.program_id(1) == 0
    pl.when(is_ctx)(lambda: emit(ctx_ref))
    pl.when(jnp.logical_not(is_ctx))(lambda: emit(x_ref))


def _norm_mod(x, ctx, nw, mod_l):
    B, S, D = x.shape
    tiles = 1 + S // ROW_TILE
    mod_spec = lambda k: pl.BlockSpec((None, 1, D), lambda b, t: (jnp.where(t == 0, B, b) * 6 + k, 0, 0))
    out_spec = pl.BlockSpec((ROW_TILE, D), lambda b, t: (b * tiles + t, 0))
    return pl.pallas_call(
        _norm_mod_kernel,
        grid=(B, tiles),
        in_specs=[pl.BlockSpec((None, ROW_TILE, D), lambda b, t: (b, jnp.maximum(t - 1, 0), 0)),
                  pl.BlockSpec((None, ROW_TILE, D), lambda b, t: (b, 0, 0)),
                  _const_spec((1, D)), mod_spec(0), mod_spec(1)],
        out_specs=[out_spec, out_spec],
        out_shape=[jax.ShapeDtypeStruct((B * tiles * ROW_TILE, D), F32),
                   jax.ShapeDtypeStruct((B * tiles * ROW_TILE, D), BF16)],
        compiler_params=_params("parallel", "parallel"),
        name="norm_mod",
    )(x, ctx, nw, mod_l, mod_l)


ROTARY_KINDS = ("q_norm", "k_norm", "q_rope", "k_rope")


def _pair_halves(a):
    q = HEAD_DIM // 4
    return jnp.concatenate([a[..., 0:q], a[..., 2 * q:3 * q], a[..., q:2 * q], a[..., 3 * q:]], axis=-1)


def _rope(y, cos, sin):
    return y * cos + pltpu.roll(y, HEAD_DIM // 2, 1) * sin


def _proj_kernel(h_ref, w_ref, cos_ref, sin_ref, gain_ref, *refs, kinds):
    *out_refs, wb_ref = refs

    @pl.when(pl.program_id(1) == 0)
    def _():
        for c, (kind, _, _) in enumerate(kinds):
            w = w_ref[:, c * HEAD_DIM:(c + 1) * HEAD_DIM]
            wb_ref[:, c * HEAD_DIM:(c + 1) * HEAD_DIM] = (_pair_halves(w) if kind in ROTARY_KINDS else w).astype(BF16)

    acc = jnp.dot(h_ref[...], wb_ref[...], preferred_element_type=F32)
    for c, (kind, out_idx, out_chunk) in enumerate(kinds):
        y = acc[:, c * HEAD_DIM:(c + 1) * HEAD_DIM]
        if kind in ("q_norm", "k_norm"):
            y = _rms(y) * gain_ref[...]
        if kind in ROTARY_KINDS:
            y = _rope(y, cos_ref[...], sin_ref[...])
        if kind in ("q_norm", "q_rope"):
            y = y * (ATTN_SCALE * LOG2E)
        out_refs[out_idx][:, out_chunk * HEAD_DIM:(out_chunk + 1) * HEAD_DIM] = y.astype(BF16)


def _proj(h, w_in, layer, cos, sin, gain, col0, kinds, out_widths, tiles_per_batch, name):
    M, D = h.shape
    n_col_tiles = out_widths[0][1]
    cb0 = col0 // PROJ_COLS
    proj_tiles_per_batch = tiles_per_batch * ROW_TILE // PROJ_ROWS
    return pl.pallas_call(
        functools.partial(_proj_kernel, kinds=kinds),
        grid=(n_col_tiles, M // PROJ_ROWS),
        in_specs=[
            pl.BlockSpec((PROJ_ROWS, D), lambda j, i: (i, 0)),
            pl.BlockSpec((None, D, PROJ_COLS), lambda j, i: (layer, 0, cb0 + j)),
            pl.BlockSpec((PROJ_ROWS, HEAD_DIM), lambda j, i: (i % proj_tiles_per_batch, 0)),
            pl.BlockSpec((PROJ_ROWS, HEAD_DIM), lambda j, i: (i % proj_tiles_per_batch, 0)),
            pl.BlockSpec((1, HEAD_DIM), lambda j, i: (0, 0)),
        ],
        out_specs=[pl.BlockSpec((PROJ_ROWS, w), lambda j, i: (i, j)) for w, _ in out_widths],
        out_shape=[jax.ShapeDtypeStruct((M, w * n), BF16) for w, n in out_widths],
        scratch_shapes=[pltpu.VMEM((D, PROJ_COLS), BF16)],
        compiler_params=_params("parallel", "arbitrary"),
        name=name,
    )(h, w_in, cos, sin, gain)


def _stack_heads(q_ref, g0, n):
    return jnp.concatenate([q_ref[:, g * HEAD_DIM:(g + 1) * HEAD_DIM] for g in range(g0, g0 + n)], axis=0)


def _write_heads(o_ref, g0, n, o):
    rows = o_ref.shape[0]
    for j in range(n):
        o_ref[:, (g0 + j) * HEAD_DIM:(g0 + j + 1) * HEAD_DIM] = o[j * rows:(j + 1) * rows].astype(o_ref.dtype)


def _scores(q, k):
    return lax.dot_general(q, k, (((1,), (1,)), ((), ())), preferred_element_type=F32)


def _online_step(q, k, v, m, l, acc):
    s = _scores(q, k)
    m_new = jnp.maximum(m, jnp.max(s, axis=-1, keepdims=True))
    alpha = jnp.exp2(m - m_new)
    p = jnp.exp2(s - m_new)
    l = alpha * l + jnp.sum(p, axis=-1, keepdims=True)
    acc = alpha * acc + jnp.dot(p.astype(BF16), v, preferred_element_type=F32)
    return m_new, l, acc


def _dense_attn_kernel(q_ref, k_ref, v_ref, o_ref, *, ctx_len, total_len, key_chunk, ctx_tile):
    def run(n_keys, chunk):
        for g0 in range(0, GROUP, DENSE_CHAIN_HEADS):
            q = _stack_heads(q_ref, g0, DENSE_CHAIN_HEADS)
            r = q.shape[0]
            m = jnp.full((r, 1), -jnp.inf, F32)
            l = jnp.zeros((r, 1), F32)
            acc = jnp.zeros((r, HEAD_DIM), F32)
            for c in range(n_keys // chunk):
                m, l, acc = _online_step(q, k_ref[c * chunk:(c + 1) * chunk, :],
                                         v_ref[c * chunk:(c + 1) * chunk, :], m, l, acc)
            _write_heads(o_ref, g0, DENSE_CHAIN_HEADS, acc * (1.0 / l))

    if ctx_tile:
        is_ctx = pl.program_id(2) == 0
        pl.when(is_ctx)(lambda: run(ctx_len, ctx_len))
        pl.when(jnp.logical_not(is_ctx))(lambda: run(total_len, key_chunk))
    else:
        run(total_len, key_chunk)


def _window_attn_kernel(sink_ref, q_ref, k_ref, v_ref, o_ref, *, ctx_len, total_len, ctx_tile, q_off):
    kvh = pl.program_id(1)
    tq = q_ref.shape[0]
    span = tq + 2 * WINDOW
    n = WINDOW_CHAIN_HEADS

    def attend(parts):
        for g0 in range(0, GROUP, n):
            q = _stack_heads(q_ref, g0, n)
            sink = jnp.concatenate(
                [jnp.full((tq, 1), sink_ref[kvh * GROUP + g] * LOG2E, F32) for g in range(g0, g0 + n)], axis=0)
            scores = []
            m = sink
            for k, _, bias in parts:
                s = _scores(q, k)
                if bias is not None:
                    s = s + bias
                scores.append(s)
                m = jnp.maximum(m, jnp.max(s, axis=-1, keepdims=True))
            l = jnp.exp2(sink - m)
            o = jnp.zeros((n * tq, HEAD_DIM), F32)
            for s, (_, v, _) in zip(scores, parts):
                p = jnp.exp2(s - m)
                l = l + jnp.sum(p, axis=-1, keepdims=True)
                o = o + jnp.dot(p.astype(BF16), v, preferred_element_type=F32)
            _write_heads(o_ref, g0, n, o * (1.0 / l))

    def ctx_part():
        return k_ref[0:ctx_len, :], v_ref[0:ctx_len, :], None

    def run_latent():
        q0 = (pl.program_id(2) + q_off) * tq - ctx_len
        n_latent = total_len - ctx_len
        start = pl.multiple_of(jnp.clip(q0 - WINDOW, 0, n_latent - span), WINDOW)
        row = lax.broadcasted_iota(jnp.int32, (n * tq, span), 0) & (tq - 1)
        col = lax.broadcasted_iota(jnp.int32, (n * tq, span), 1)
        bias = jnp.where(jnp.abs(col - row + (start - q0)) <= WINDOW, 0.0, NEG_INF).astype(F32)
        attend([ctx_part(), (k_ref[pl.ds(ctx_len + start, span), :], v_ref[pl.ds(ctx_len + start, span), :], bias)])

    if ctx_tile:
        is_ctx = pl.program_id(2) == 0
        pl.when(is_ctx)(lambda: attend([ctx_part()]))
        pl.when(jnp.logical_not(is_ctx))(run_latent)
    else:
        run_latent()


def _attention(q, k, v, sink, batch, total_len, ctx_len, latent_only, window):
    M = q.shape[0]
    q3, k3, v3 = (a.reshape(batch, total_len, a.shape[1]) for a in (q, k, v))
    tiles = total_len // ROW_TILE
    q_off = 1 if latent_only else 0
    gw = GROUP * HEAD_DIM
    q_spec = pl.BlockSpec((None, ROW_TILE, gw), lambda b, h, i, *_: (b, i + q_off, h))
    kv_spec = pl.BlockSpec((None, total_len, HEAD_DIM), lambda b, h, i, *_: (b, 0, h))
    grid = (batch, KV_HEADS, tiles - q_off)
    out_shape = jax.ShapeDtypeStruct(q3.shape, BF16)
    if window:
        kern = functools.partial(_window_attn_kernel, ctx_len=ctx_len, total_len=total_len,
                                 ctx_tile=not latent_only, q_off=q_off)
        out = pl.pallas_call(
            kern,
            grid_spec=pltpu.PrefetchScalarGridSpec(
                num_scalar_prefetch=1, grid=grid,
                in_specs=[q_spec, kv_spec, kv_spec], out_specs=q_spec),
            out_shape=out_shape,
            compiler_params=_params("parallel", "parallel", "parallel"),
            name="window_attention",
        )(sink, q3, k3, v3)
    else:
        kern = functools.partial(_dense_attn_kernel, ctx_len=ctx_len, total_len=total_len,
                                 key_chunk=total_len // 3, ctx_tile=not latent_only)
        out = pl.pallas_call(
            kern, grid=grid, in_specs=[q_spec, kv_spec, kv_spec], out_specs=q_spec,
            out_shape=out_shape,
            compiler_params=_params("parallel", "parallel", "parallel"),
            name="dense_attention",
        )(q3, k3, v3)
    return out.reshape(M, q.shape[1])


def _merge_kernel(ya_ref, yb_ref, ga_ref, gb_ref, wa_ref, wb_ref, wo_ref, x_ref, g1_ref,
                  nw_ref, sh_ref, sc_ref, wr_ref, br_ref, xo_ref, h_ref, lg_ref):
    a = jnp.dot(ya_ref[...], wa_ref[...], preferred_element_type=F32)
    b = jnp.dot(yb_ref[...], wb_ref[...], preferred_element_type=F32)
    m = jax.nn.sigmoid(ga_ref[...].astype(F32)) * a + jax.nn.sigmoid(gb_ref[...].astype(F32)) * b
    o = jnp.dot(m.astype(BF16), wo_ref[...], preferred_element_type=F32)
    xn = x_ref[...] + g1_ref[...] * o
    xo_ref[...] = xn
    h = (_rms(xn) * nw_ref[...]) * (1.0 + sc_ref[...]) + sh_ref[...]
    h_hi = h.astype(BF16)
    bits = pltpu.bitcast(h_hi.astype(F32), jnp.uint32)
    half = h.shape[1] // 2
    words = (bits[:, :half] >> 16) | (bits[:, half:] & jnp.uint32(0xFFFF0000))
    chunks = half // LANES
    for c in range(chunks):
        h_ref[pl.ds(c, h.shape[0], stride=chunks), :] = words[:, c * LANES:(c + 1) * LANES]
    h_lo = (h - h_hi.astype(F32)).astype(BF16)
    lg = jnp.dot(h_hi, wr_ref[...], preferred_element_type=F32)
    lg_lo = jnp.dot(h_lo, wr_ref[:, :ROUTER_LANES], preferred_element_type=F32)
    lg_ref[...] = lg[:, :ROUTER_LANES] + (lg[:, ROUTER_LANES:] + lg_lo) + br_ref[...]


def _merge(ya, yb, gates, wa, wb, wo, x_all, mod_l, nw, wr, br, rows):
    M, D = x_all.shape
    W = ya.shape[1]
    chunks = D // 2 // LANES
    return pl.pallas_call(
        _merge_kernel,
        grid=(rows.n,),
        in_specs=[
            rows.row_spec(W), rows.row_spec(W),
            pl.BlockSpec((ROW_TILE, D), lambda i: (rows.tile(i), 0)),
            pl.BlockSpec((ROW_TILE, D), lambda i: (rows.tile(i), 1)),
            _const_spec((W, D)), _const_spec((W, D)), _const_spec((D, D)),
            rows.row_spec(D), rows.mod_spec(D, 2),
            _const_spec((1, D)), rows.mod_spec(D, 3), rows.mod_spec(D, 4),
            _const_spec((D, 2 * ROUTER_LANES)), _const_spec((1, ROUTER_LANES)),
        ],
        out_specs=[rows.row_spec(D),
                   pl.BlockSpec((ROW_TILE * chunks, LANES), lambda i: (rows.tile(i), 0)),
                   rows.row_spec(ROUTER_LANES)],
        out_shape=[jax.ShapeDtypeStruct((M, D), F32), jax.ShapeDtypeStruct((M * chunks, LANES), jnp.uint32),
                   jax.ShapeDtypeStruct((M, ROUTER_LANES), F32)],
        compiler_params=_params("parallel"),
        name="merge",
    )(ya, yb, gates, gates, wa, wb, wo, x_all, mod_l, nw, mod_l, mod_l, wr, br)


def _route(logits):
    n = logits.shape[0]
    gl = logits[:, :N_GROUPS]
    el = logits[:, N_GROUPS:N_GROUPS + N_EXPERTS].reshape(n, N_GROUPS, EXPERTS_PER_GROUP)
    g = jnp.argmax(gl, axis=-1).astype(jnp.int32)
    pg = 1.0 / jnp.sum(jnp.exp(gl - jnp.max(gl, axis=-1, keepdims=True)), axis=-1, keepdims=True)
    in_group = jnp.arange(N_GROUPS, dtype=jnp.int32)[None, :, None] == g[:, None, None]
    el_g = jnp.sum(jnp.where(in_group, el, 0.0), axis=1)
    tv, ti = lax.top_k(el_g, TOP_K)
    wts = jax.nn.softmax(tv, axis=-1) * pg
    eid = g[:, None] * EXPERTS_PER_GROUP + ti.astype(jnp.int32)
    return eid, wts


def _dispatch(eid, tok_rows, row_scale):
    n = eid.shape[0]
    a = n * TOP_K
    e_flat = eid.reshape(a)
    onehot = (e_flat[:, None] == jnp.arange(N_EXPERTS, dtype=jnp.int32)[None, :]).astype(jnp.int32)
    csum = jnp.cumsum(onehot, axis=0)
    counts = csum[-1]
    padded = (counts + MOE_ROWS - 1) // MOE_ROWS * MOE_ROWS
    pad_end = jnp.cumsum(padded)
    pad_start = pad_end - padded
    dest = jnp.sum(onehot * (csum - 1 + pad_start[None, :]), axis=1)
    n_grid = a // MOE_ROWS + N_EXPERTS - 1 + GATHER_AHEAD
    slot_row = jnp.zeros((n_grid * MOE_ROWS,), jnp.int32).at[dest].set(jnp.repeat(tok_rows * row_scale, TOP_K))
    blk = jnp.arange(n_grid, dtype=jnp.int32)
    n_valid = (pad_end[-1] // MOE_ROWS).astype(jnp.int32)
    blk_c = jnp.minimum(blk, n_valid - 1)
    blk_e = jnp.clip(jnp.searchsorted(pad_end, blk_c * MOE_ROWS, side="right"), 0, N_EXPERTS - 1).astype(jnp.int32)
    experts = jnp.arange(N_EXPERTS, dtype=jnp.int32)
    used = counts > 0
    first_used_from = lax.cummin(jnp.where(used, experts, N_EXPERTS), reverse=True)
    next_used = jnp.concatenate([first_used_from[1:], jnp.full((1,), N_EXPERTS, jnp.int32)])
    next_used = jnp.where(next_used == N_EXPERTS, -1, next_used)
    parity = (jnp.cumsum(used.astype(jnp.int32)) - 1) % 2
    return (dest, blk_e, next_used[blk_e].astype(jnp.int32), parity[blk_e].astype(jnp.int32), slot_row,
            n_valid.reshape(1))


def _experts_kernel(blk_e, blk_next, blk_par, slot_row, n_valid,
                    h_hbm, w1_hbm, w3_hbm, w2_hbm, o_ref,
                    xg, xsem, ws1, ws3, ws2, wsem, w1b, w3b, w2b, *, layer):
    i = pl.program_id(0)
    nv = n_valid[0]
    chunks = xg.shape[1] // MOE_ROWS
    e = blk_e[i]
    par = blk_par[i]
    fresh = jnp.logical_or(i == 0, e != blk_e[jnp.maximum(i - 1, 0)])

    def weight_copies(expert, p):
        return (pltpu.make_async_copy(w1_hbm.at[layer, expert], ws1.at[p], wsem.at[p, 0]),
                pltpu.make_async_copy(w3_hbm.at[layer, expert], ws3.at[p], wsem.at[p, 1]),
                pltpu.make_async_copy(w2_hbm.at[layer, expert], ws2.at[p], wsem.at[p, 2]))

    def for_rows(blk, fn):
        buf = lax.rem(blk, GATHER_AHEAD + 1)
        base = blk * MOE_ROWS
        for j in range(MOE_ROWS):
            r = pl.multiple_of(slot_row[base + j], chunks)
            fn(pltpu.make_async_copy(h_hbm.at[pl.ds(r, chunks), :], xg.at[buf, pl.ds(j * chunks, chunks), :],
                                     xsem.at[buf]))

    @pl.when(i == 0)
    def _():
        for cp in weight_copies(e, par):
            cp.start(priority=WEIGHT_DMA_PRIORITY)
        for blk in range(GATHER_AHEAD):
            for_rows(blk, lambda cp: cp.start())

    @pl.when(jnp.logical_and(i >= nv, i < nv + GATHER_AHEAD))
    def _():
        for_rows(i, lambda cp: cp.wait())

    @pl.when(fresh)
    def _():
        for cp in weight_copies(e, par):
            cp.wait()
        nxt = blk_next[i]

        @pl.when(nxt >= 0)
        def _():
            for cp in weight_copies(nxt, 1 - par):
                cp.start(priority=WEIGHT_DMA_PRIORITY)

        w1b[...] = ws1[par].astype(BF16)
        w3b[...] = ws3[par].astype(BF16)
        w2b[...] = ws2[par].astype(BF16)

    @pl.when(i < nv)
    def _():
        for_rows(i, lambda cp: cp.wait())
        words = jnp.concatenate(
            [xg[lax.rem(i, GATHER_AHEAD + 1), pl.ds(c, MOE_ROWS, stride=chunks), :] for c in range(chunks)],
            axis=1)
        x = jnp.concatenate([pltpu.bitcast(words << 16, F32).astype(BF16),
                             pltpu.bitcast(words & jnp.uint32(0xFFFF0000), F32).astype(BF16)], axis=1)
        for_rows(i + GATHER_AHEAD, lambda cp: cp.start())
        a = jnp.dot(x, w1b[...], preferred_element_type=F32)
        b = jnp.dot(x, w3b[...], preferred_element_type=F32)
        hid = (a * jax.nn.sigmoid(a)) * b
        o_ref[...] = jnp.dot(hid.astype(BF16), w2b[...], preferred_element_type=F32).astype(o_ref.dtype)


def _experts(h2, w1, w3, w2, layer, blk_e, blk_next, blk_par, slot_row, n_valid):
    D, de = w1.shape[-2:]
    chunks = D // 2 // LANES
    n_blk = blk_e.shape[0]
    any_spec = pl.BlockSpec(memory_space=pl.ANY)
    return pl.pallas_call(
        functools.partial(_experts_kernel, layer=layer),
        grid_spec=pltpu.PrefetchScalarGridSpec(
            num_scalar_prefetch=5, grid=(n_blk,),
            in_specs=[any_spec, any_spec, any_spec, any_spec],
            out_specs=pl.BlockSpec((MOE_ROWS, D), lambda i, *s: (jnp.minimum(i, s[4][0] - 1), 0)),
            scratch_shapes=[
                pltpu.VMEM((GATHER_AHEAD + 1, MOE_ROWS * chunks, LANES), jnp.uint32),
                pltpu.SemaphoreType.DMA((GATHER_AHEAD + 1,)),
                pltpu.VMEM((2, D, de), F32), pltpu.VMEM((2, D, de), F32), pltpu.VMEM((2, de, D), F32),
                pltpu.SemaphoreType.DMA((2, 3)),
                pltpu.VMEM((D, de), BF16), pltpu.VMEM((D, de), BF16), pltpu.VMEM((de, D), BF16),
            ],
        ),
        out_shape=jax.ShapeDtypeStruct((n_blk * MOE_ROWS, D), BF16),
        compiler_params=_params("arbitrary"),
        name="experts",
    )(blk_e, blk_next, blk_par, slot_row, n_valid, h2, w1, w3, w2)


def _moe_mix(x_ref, y0_ref, y1_ref, w_ref, g2_ref):
    w = w_ref[...]
    y = w[:, 0:1] * y0_ref[...].astype(F32) + w[:, 1:2] * y1_ref[...].astype(F32)
    return x_ref[...] + g2_ref[...] * y


def _combine_next_kernel(x_ref, y0_ref, y1_ref, w_ref, g2_ref, nw_ref, sh_ref, sc_ref, xo_ref, h_ref):
    xn = _moe_mix(x_ref, y0_ref, y1_ref, w_ref, g2_ref)
    xo_ref[...] = xn
    h_ref[...] = ((_rms(xn) * nw_ref[...]) * (1.0 + sc_ref[...]) + sh_ref[...]).astype(BF16)


def _combine_final_kernel(x_ref, y0_ref, y1_ref, w_ref, g2_ref, nf_ref, o_ref):
    o_ref[...] = _rms(_moe_mix(x_ref, y0_ref, y1_ref, w_ref, g2_ref)) * nf_ref[...]


def _token_specs(n_tokens, d):
    tiles = n_tokens // ROW_TILE
    return [pl.BlockSpec((ROW_TILE, d), lambda i: (i, 0)), pl.BlockSpec((ROW_TILE, d), lambda i: (i + tiles, 0)),
            pl.BlockSpec((ROW_TILE, TOP_K), lambda i: (i, 0))]


def _combine_next(x_new, ysel, wts, mod_l, nw_next, mod_next, rows):
    M, D = x_new.shape
    return pl.pallas_call(
        _combine_next_kernel,
        grid=(rows.n,),
        in_specs=[rows.row_spec(D), *_token_specs(M, D), rows.mod_spec(D, 5),
                  _const_spec((1, D)), rows.mod_spec(D, 0), rows.mod_spec(D, 1)],
        out_specs=[rows.row_spec(D), rows.row_spec(D)],
        out_shape=[jax.ShapeDtypeStruct((M, D), F32), jax.ShapeDtypeStruct((M, D), BF16)],
        compiler_params=_params("parallel"),
        name="combine_next",
    )(x_new, ysel, ysel, wts, mod_l, nw_next, mod_next, mod_next)


def _combine_final(x_new, ysel, wts, mod_l, nf, rows):
    M, D = x_new.shape
    n = wts.shape[0]
    return pl.pallas_call(
        _combine_final_kernel,
        grid=(rows.n,),
        in_specs=[rows.row_spec(D), *_token_specs(n, D), rows.mod_spec(D, 5), _const_spec((1, D))],
        out_specs=pl.BlockSpec((ROW_TILE, D), lambda i: (i, 0)),
        out_shape=jax.ShapeDtypeStruct((n, D), F32),
        compiler_params=_params("parallel"),
        name="combine_final",
    )(x_new, ysel, ysel, wts, mod_l, nf)


def _rope_tables(ctx_len, seq):
    quarter = HEAD_DIM // 4
    freqs = ROPE_BASE ** (-jnp.arange(quarter, dtype=F32) / quarter)
    pos = jnp.arange(seq, dtype=jnp.int32)
    row = (pos // GRID_W).astype(F32)[:, None] * freqs[None, :]
    col = (pos % GRID_W).astype(F32)[:, None] * freqs[None, :]
    cos = jnp.concatenate([jnp.cos(row), jnp.cos(col), jnp.cos(row), jnp.cos(col)], axis=-1)
    sin = jnp.concatenate([-jnp.sin(row), -jnp.sin(col), jnp.sin(row), jnp.sin(col)], axis=-1)
    cos = jnp.concatenate([jnp.ones((ctx_len, HEAD_DIM), F32), cos], axis=0)
    sin = jnp.concatenate([jnp.zeros((ctx_len, HEAD_DIM), F32), sin], axis=0)
    return cos, sin


def kernel(x, c, ctx, c_ctx, w_mod, b_mod, norm_mix, norm_ffn, w_in, qn_a, kn_a, sink_b, w_br_a, w_br_b, w_out, w_rg, b_rg, w_re, b_re, w1, w3, w2, norm_final):
    B, S, D = x.shape
    C = ctx.shape[1]
    L = w_mod.shape[0]
    T = C + S
    M = B * T
    assert C == ROW_TILE and S % ROW_TILE == 0 and T % PROJ_ROWS == 0 and B < MOD_ROWS
    tiles_per_batch = T // ROW_TILE

    cc = jnp.concatenate([c, c_ctx[None, :], jnp.zeros((MOD_ROWS - B - 1, D), F32)], axis=0)
    mod = _mod_vectors(cc, w_mod, b_mod).reshape(L, MOD_ROWS * 6, 1, D)
    cos, sin = _rope_tables(C, S)
    latent_rows = (jnp.arange(B, dtype=jnp.int32)[:, None] * T + C
                   + jnp.arange(S, dtype=jnp.int32)[None, :]).reshape(B * S)
    all_rows = jnp.arange(M, dtype=jnp.int32)

    q_kinds = lambda kind: tuple((kind, 0, k) for k in range(PROJ_COLS // HEAD_DIM))
    kv_kinds = lambda kind: ((kind, 0, 0), (kind, 0, 1), ("v", 1, 0), ("v", 1, 1))
    col = {"qa": 0, "kva": WIDTH, "qb": WIDTH + 2 * KVW, "kvb": 2 * WIDTH + 2 * KVW, "gates": 2 * WIDTH + 4 * KVW}

    every = _Rows(B, tiles_per_batch, latent_only=False)
    latent = _Rows(B, tiles_per_batch, latent_only=True)
    x_all, h = _norm_mod(x, ctx, norm_mix[0][None, :], mod[0])
    out = None
    for l in range(L):
        last = l == L - 1
        rows = latent if last else every
        proj = functools.partial(_proj, h, w_in, l, cos, sin, tiles_per_batch=tiles_per_batch)
        q_gain, k_gain = _pair_halves(qn_a[l])[None, :], _pair_halves(kn_a[l])[None, :]
        (qa,) = proj(q_gain,col["qa"], q_kinds("q_norm"), [(PROJ_COLS, WIDTH // PROJ_COLS)], name="proj_qa")
        ka, va = proj(k_gain,col["kva"], kv_kinds("k_norm"), [(KVW, 1), (KVW, 1)], name="proj_kva")
        (qb,) = proj(q_gain,col["qb"], q_kinds("q_rope"), [(PROJ_COLS, WIDTH // PROJ_COLS)], name="proj_qb")
        kb, vb = proj(k_gain,col["kvb"], kv_kinds("k_rope"), [(KVW, 1), (KVW, 1)], name="proj_kvb")
        (gates,) = proj(q_gain,col["gates"], q_kinds("gate"), [(PROJ_COLS, 2 * D // PROJ_COLS)], name="proj_gates")

        ya = _attention(qa, ka, va, None, B, T, C, last, window=False)
        yb = _attention(qb, kb, vb, sink_b[l], B, T, C, last, window=True)

        w_router = jnp.concatenate(
            [w_rg[l], w_re[l], jnp.zeros((D, ROUTER_LANES - N_GROUPS - N_EXPERTS), F32)], axis=1)
        w_router_hi = w_router.astype(BF16)
        w_router = jnp.concatenate([w_router_hi, (w_router - w_router_hi.astype(F32)).astype(BF16)], axis=1)
        b_router = jnp.concatenate(
            [b_rg[l], b_re[l], jnp.zeros((ROUTER_LANES - N_GROUPS - N_EXPERTS,), F32)])[None, :]
        x_new, h2, logits = _merge(ya, yb, gates, w_br_a[l].astype(BF16), w_br_b[l].astype(BF16),
                                   w_out[l].astype(BF16), x_all, mod[l], norm_ffn[l][None, :],
                                   w_router, b_router, rows)

        tok_rows = latent_rows if last else all_rows
        eid, wts = _route(logits[tok_rows] if last else logits)
        dest, *plan = _dispatch(eid, tok_rows, D // 2 // LANES)
        ybuf = _experts(h2, w1, w3, w2, l, *plan)
        ysel = ybuf[dest.reshape(-1, TOP_K).T.reshape(-1)]
        if last:
            out = _combine_final(x_new, ysel, wts, mod[l], norm_final[None, :], rows).reshape(B, S, D)
        else:
            x_all, h = _combine_next(x_new, ysel, wts, mod[l], norm_mix[l + 1][None, :], mod[l + 1], rows)
    return out
```

```python
import functools

import jax
import jax.numpy as jnp
from jax import lax
from jax.experimental import pallas as pl
from jax.experimental.pallas import tpu as pltpu

F32 = jnp.float32
BF16 = jnp.bfloat16

GRID_W = 64
HEAD_DIM = 128
LANES = 128
HEADS = 8
KV_HEADS = 2
GROUP = HEADS // KV_HEADS
WIDTH = HEADS * HEAD_DIM
KVW = KV_HEADS * HEAD_DIM
WINDOW = 128
ROPE_BASE = 10000.0
ATTN_SCALE = HEAD_DIM ** -0.5
LOG2E = 1.4426950408889634
N_GROUPS = 8
EXPERTS_PER_GROUP = 8
N_EXPERTS = N_GROUPS * EXPERTS_PER_GROUP
TOP_K = 2
EPS = 1e-6
NEG_INF = -1e30

ROW_TILE = 256
PROJ_ROWS = 768
PROJ_COLS = 512
GATE_COLS = 1024
MOE_ROWS = 128
GATHER_AHEAD = 2
WEIGHT_DMA_PRIORITY = 1
ROUTER_LANES = 128
DENSE_CHAIN_HEADS = 4
WINDOW_CHAIN_HEADS = 1
MOD_ROWS = 8
VMEM_LIMIT = 56 * 1024 * 1024


def _params(*sem):
    return pltpu.CompilerParams(dimension_semantics=sem, vmem_limit_bytes=VMEM_LIMIT)


def _rms(y):
    return y * lax.rsqrt(jnp.mean(y * y, axis=-1, keepdims=True) + EPS)


def _mod_kernel(c_ref, w_ref, b_ref, o_ref):
    c = c_ref[...]
    a = (c * jax.nn.sigmoid(c)).astype(BF16)
    o_ref[...] = jnp.dot(a, w_ref[...].astype(BF16), preferred_element_type=F32) + b_ref[...]


def _mod_vectors(cc, w_mod, b_mod):
    L, D, N = w_mod.shape
    tn = 1024
    return pl.pallas_call(
        _mod_kernel,
        grid=(L, N // tn),
        in_specs=[
            pl.BlockSpec((MOD_ROWS, D), lambda l, j: (0, 0)),
            pl.BlockSpec((None, D, tn), lambda l, j: (l, 0, j)),
            pl.BlockSpec((None, 1, tn), lambda l, j: (l, 0, j)),
        ],
        out_specs=pl.BlockSpec((None, MOD_ROWS, tn), lambda l, j: (l, 0, j)),
        out_shape=jax.ShapeDtypeStruct((L, MOD_ROWS, N), F32),
        compiler_params=_params("parallel", "parallel"),
        name="mod_vectors",
    )(cc, w_mod, b_mod.reshape(L, 1, N))


class _Rows:
    def __init__(self, batch, tiles_per_batch, latent_only):
        self.batch = batch
        self.tpb = tiles_per_batch
        self.latent_only = latent_only
        self.n = batch * (tiles_per_batch - 1 if latent_only else tiles_per_batch)

    def tile(self, i):
        if self.latent_only:
            per = self.tpb - 1
            return (i // per) * self.tpb + 1 + i % per
        return i

    def mod_row(self, i):
        if self.latent_only:
            return i // (self.tpb - 1)
        return jnp.where(i % self.tpb == 0, self.batch, i // self.tpb)

    def row_spec(self, width):
        return pl.BlockSpec((ROW_TILE, width), lambda i: (self.tile(i), 0))

    def mod_spec(self, d, k):
        return pl.BlockSpec((None, 1, d), lambda i: (self.mod_row(i) * 6 + k, 0, 0))


def _const_spec(shape):
    zeros = (0,) * len(shape)
    return pl.BlockSpec(shape, lambda *_: zeros, pipeline_mode=pl.Buffered(1))


def _norm_mod_kernel(x_ref, ctx_ref, nw_ref, sh_ref, sc_ref, xo_ref, h_ref):
    def emit(src_ref):
        x = src_ref[...]
        xo_ref[...] = x
        h_ref[...] = ((_rms(x) * nw_ref[...]) * (1.0 + sc_ref[...]) + sh_ref[...]).astype(BF16)

    is_ctx = pl.program_id(1) == 0
    pl.when(is_ctx)(lambda: emit(ctx_ref))
    pl.when(jnp.logical_not(is_ctx))(lambda: emit(x_ref))


def _norm_mod(x, ctx, nw, mod_l):
    B, S, D = x.shape
    tiles = 1 + S // ROW_TILE
    mod_spec = lambda k: pl.BlockSpec((None, 1, D), lambda b, t: (jnp.where(t == 0, B, b) * 6 + k, 0, 0))
    out_spec = pl.BlockSpec((ROW_TILE, D), lambda b, t: (b * tiles + t, 0))
    return pl.pallas_call(
        _norm_mod_kernel,
        grid=(B, tiles),
        in_specs=[pl.BlockSpec((None, ROW_TILE, D), lambda b, t: (b, jnp.maximum(t - 1, 0), 0)),
                  pl.BlockSpec((None, ROW_TILE, D), lambda b, t: (b, 0, 0)),
                  _const_spec((1, D)), mod_spec(0), mod_spec(1)],
        out_specs=[out_spec, out_spec],
        out_shape=[jax.ShapeDtypeStruct((B * tiles * ROW_TILE, D), F32),
                   jax.ShapeDtypeStruct((B * tiles * ROW_TILE, D), BF16)],
        compiler_params=_params("parallel", "parallel"),
        name="norm_mod",
    )(x, ctx, nw, mod_l, mod_l)


ROTARY_KINDS = ("q_norm", "k_norm", "q_rope", "k_rope")


def _pair_halves(a):
    q = HEAD_DIM // 4
    return jnp.concatenate([a[..., 0:q], a[..., 2 * q:3 * q], a[..., q:2 * q], a[..., 3 * q:]], axis=-1)


def _rope(y, cos, sin):
    return y * cos + pltpu.roll(y, HEAD_DIM // 2, 1) * sin


def _proj_kernel(h_ref, w_ref, cos_ref, sin_ref, gain_ref, *refs, kinds):
    *out_refs, wb_ref = refs

    @pl.when(pl.program_id(1) == 0)
    def _():
        for c, (kind, _, _) in enumerate(kinds):
            w = w_ref[:, c * HEAD_DIM:(c + 1) * HEAD_DIM]
            wb_ref[:, c * HEAD_DIM:(c + 1) * HEAD_DIM] = (_pair_halves(w) if kind in ROTARY_KINDS else w).astype(BF16)

    acc = jnp.dot(h_ref[...], wb_ref[...], preferred_element_type=F32)
    for c, (kind, out_idx, out_chunk) in enumerate(kinds):
        y = acc[:, c * HEAD_DIM:(c + 1) * HEAD_DIM]
        if kind in ("q_norm", "k_norm"):
            y = _rms(y) * gain_ref[...]
        if kind in ROTARY_KINDS:
            y = _rope(y, cos_ref[...], sin_ref[...])
        if kind in ("q_norm", "q_rope"):
            y = y * (ATTN_SCALE * LOG2E)
        out_refs[out_idx][:, out_chunk * HEAD_DIM:(out_chunk + 1) * HEAD_DIM] = y.astype(BF16)


def _proj(h, w_in, layer, cos, sin, gain, col0, kinds, out_widths, tiles_per_batch, name):
    M, D = h.shape
    n_col_tiles = out_widths[0][1]
    cols = len(kinds) * HEAD_DIM
    cb0 = col0 // cols
    proj_tiles_per_batch = tiles_per_batch * ROW_TILE // PROJ_ROWS
    return pl.pallas_call(
        functools.partial(_proj_kernel, kinds=kinds),
        grid=(n_col_tiles, M // PROJ_ROWS),
        in_specs=[
            pl.BlockSpec((PROJ_ROWS, D), lambda j, i: (i, 0)),
            pl.BlockSpec((None, D, cols), lambda j, i: (layer, 0, cb0 + j)),
            pl.BlockSpec((PROJ_ROWS, HEAD_DIM), lambda j, i: (i % proj_tiles_per_batch, 0)),
            pl.BlockSpec((PROJ_ROWS, HEAD_DIM), lambda j, i: (i % proj_tiles_per_batch, 0)),
            pl.BlockSpec((1, HEAD_DIM), lambda j, i: (0, 0)),
        ],
        out_specs=[pl.BlockSpec((PROJ_ROWS, w), lambda j, i: (i, j)) for w, _ in out_widths],
        out_shape=[jax.ShapeDtypeStruct((M, w * n), BF16) for w, n in out_widths],
        scratch_shapes=[pltpu.VMEM((D, cols), BF16)],
        compiler_params=_params("parallel", "arbitrary"),
        name=name,
    )(h, w_in, cos, sin, gain)


def _stack_heads(q_ref, g0, n):
    return jnp.concatenate([q_ref[:, g * HEAD_DIM:(g + 1) * HEAD_DIM] for g in range(g0, g0 + n)], axis=0)


def _write_heads(o_ref, g0, n, o):
    rows = o_ref.shape[0]
    for j in range(n):
        o_ref[:, (g0 + j) * HEAD_DIM:(g0 + j + 1) * HEAD_DIM] = o[j * rows:(j + 1) * rows].astype(o_ref.dtype)


def _scores(q, k):
    return lax.dot_general(q, k, (((1,), (1,)), ((), ())), preferred_element_type=F32)


def _online_step(q, k, v, m, l, acc):
    s = _scores(q, k)
    m_new = jnp.maximum(m, jnp.max(s, axis=-1, keepdims=True))
    alpha = jnp.exp2(m - m_new)
    p = jnp.exp2(s - m_new)
    l = alpha * l + jnp.sum(p, axis=-1, keepdims=True)
    acc = alpha * acc + jnp.dot(p.astype(BF16), v, preferred_element_type=F32)
    return m_new, l, acc


def _dense_attn_kernel(q_ref, k_ref, v_ref, o_ref, *, ctx_len, total_len, key_chunk, ctx_tile):
    def run(n_keys, chunk):
        for g0 in range(0, GROUP, DENSE_CHAIN_HEADS):
            q = _stack_heads(q_ref, g0, DENSE_CHAIN_HEADS)
            r = q.shape[0]
            m = jnp.full((r, 1), -jnp.inf, F32)
            l = jnp.zeros((r, 1), F32)
            acc = jnp.zeros((r, HEAD_DIM), F32)
            for c in range(n_keys // chunk):
                m, l, acc = _online_step(q, k_ref[c * chunk:(c + 1) * chunk, :],
                                         v_ref[c * chunk:(c + 1) * chunk, :], m, l, acc)
            _write_heads(o_ref, g0, DENSE_CHAIN_HEADS, acc * (1.0 / l))

    if ctx_tile:
        is_ctx = pl.program_id(2) == 0
        pl.when(is_ctx)(lambda: run(ctx_len, ctx_len))
        pl.when(jnp.logical_not(is_ctx))(lambda: run(total_len, key_chunk))
    else:
        run(total_len, key_chunk)


def _window_attn_kernel(sink_ref, q_ref, k_ref, v_ref, o_ref, *, ctx_len, total_len, ctx_tile, q_off):
    kvh = pl.program_id(1)
    tq = q_ref.shape[0]
    span = tq + 2 * WINDOW
    n = WINDOW_CHAIN_HEADS

    def attend(parts):
        for g0 in range(0, GROUP, n):
            q = _stack_heads(q_ref, g0, n)
            sink = jnp.concatenate(
                [jnp.full((tq, 1), sink_ref[kvh * GROUP + g] * LOG2E, F32) for g in range(g0, g0 + n)], axis=0)
            scores = []
            m = sink
            for k, _, bias in parts:
                s = _scores(q, k)
                if bias is not None:
                    s = s + bias
                scores.append(s)
                m = jnp.maximum(m, jnp.max(s, axis=-1, keepdims=True))
            l = jnp.exp2(sink - m)
            o = jnp.zeros((n * tq, HEAD_DIM), F32)
            for s, (_, v, _) in zip(scores, parts):
                p = jnp.exp2(s - m)
                l = l + jnp.sum(p, axis=-1, keepdims=True)
                o = o + jnp.dot(p.astype(BF16), v, preferred_element_type=F32)
            _write_heads(o_ref, g0, n, o * (1.0 / l))

    def ctx_part():
        return k_ref[0:ctx_len, :], v_ref[0:ctx_len, :], None

    def run_latent():
        q0 = (pl.program_id(2) + q_off) * tq - ctx_len
        n_latent = total_len - ctx_len
        start = pl.multiple_of(jnp.clip(q0 - WINDOW, 0, n_latent - span), WINDOW)
        row = lax.broadcasted_iota(jnp.int32, (n * tq, span), 0) & (tq - 1)
        col = lax.broadcasted_iota(jnp.int32, (n * tq, span), 1)
        bias = jnp.where(jnp.abs(col - row + (start - q0)) <= WINDOW, 0.0, NEG_INF).astype(F32)
        attend([ctx_part(), (k_ref[pl.ds(ctx_len + start, span), :], v_ref[pl.ds(ctx_len + start, span), :], bias)])

    if ctx_tile:
        is_ctx = pl.program_id(2) == 0
        pl.when(is_ctx)(lambda: attend([ctx_part()]))
        pl.when(jnp.logical_not(is_ctx))(run_latent)
    else:
        run_latent()


def _attention(q, k, v, sink, batch, total_len, ctx_len, latent_only, window):
    M = q.shape[0]
    q3, k3, v3 = (a.reshape(batch, total_len, a.shape[1]) for a in (q, k, v))
    tiles = total_len // ROW_TILE
    q_off = 1 if latent_only else 0
    gw = GROUP * HEAD_DIM
    q_spec = pl.BlockSpec((None, ROW_TILE, gw), lambda b, h, i, *_: (b, i + q_off, h))
    kv_spec = pl.BlockSpec((None, total_len, HEAD_DIM), lambda b, h, i, *_: (b, 0, h))
    grid = (batch, KV_HEADS, tiles - q_off)
    out_shape = jax.ShapeDtypeStruct(q3.shape, BF16)
    if window:
        kern = functools.partial(_window_attn_kernel, ctx_len=ctx_len, total_len=total_len,
                                 ctx_tile=not latent_only, q_off=q_off)
        out = pl.pallas_call(
            kern,
            grid_spec=pltpu.PrefetchScalarGridSpec(
                num_scalar_prefetch=1, grid=grid,
                in_specs=[q_spec, kv_spec, kv_spec], out_specs=q_spec),
            out_shape=out_shape,
            compiler_params=_params("parallel", "parallel", "parallel"),
            name="window_attention",
        )(sink, q3, k3, v3)
    else:
        kern = functools.partial(_dense_attn_kernel, ctx_len=ctx_len, total_len=total_len,
                                 key_chunk=total_len // 3, ctx_tile=not latent_only)
        out = pl.pallas_call(
            kern, grid=grid, in_specs=[q_spec, kv_spec, kv_spec], out_specs=q_spec,
            out_shape=out_shape,
            compiler_params=_params("parallel", "parallel", "parallel"),
            name="dense_attention",
        )(q3, k3, v3)
    return out.reshape(M, q.shape[1])


def _merge_kernel(ya_ref, yb_ref, ga_ref, gb_ref, wa_ref, wb_ref, wo_ref, x_ref, g1_ref,
                  nw_ref, sh_ref, sc_ref, wr_ref, br_ref, xo_ref, h_ref, lg_ref):
    a = jnp.dot(ya_ref[...], wa_ref[...], preferred_element_type=F32)
    b = jnp.dot(yb_ref[...], wb_ref[...], preferred_element_type=F32)
    m = jax.nn.sigmoid(ga_ref[...].astype(F32)) * a + jax.nn.sigmoid(gb_ref[...].astype(F32)) * b
    o = jnp.dot(m.astype(BF16), wo_ref[...], preferred_element_type=F32)
    xn = x_ref[...] + g1_ref[...] * o
    xo_ref[...] = xn
    h = (_rms(xn) * nw_ref[...]) * (1.0 + sc_ref[...]) + sh_ref[...]
    h_hi = h.astype(BF16)
    bits = pltpu.bitcast(h_hi.astype(F32), jnp.uint32)
    half = h.shape[1] // 2
    words = (bits[:, :half] >> 16) | (bits[:, half:] & jnp.uint32(0xFFFF0000))
    chunks = half // LANES
    for c in range(chunks):
        h_ref[pl.ds(c, h.shape[0], stride=chunks), :] = words[:, c * LANES:(c + 1) * LANES]
    h_lo = (h - h_hi.astype(F32)).astype(BF16)
    lg = jnp.dot(h_hi, wr_ref[...], preferred_element_type=F32)
    lg_lo = jnp.dot(h_lo, wr_ref[:, :ROUTER_LANES], preferred_element_type=F32)
    lg_ref[...] = lg[:, :ROUTER_LANES] + (lg[:, ROUTER_LANES:] + lg_lo) + br_ref[...]


def _merge(ya, yb, gates, wa, wb, wo, x_all, mod_l, nw, wr, br, rows):
    M, D = x_all.shape
    W = ya.shape[1]
    chunks = D // 2 // LANES
    return pl.pallas_call(
        _merge_kernel,
        grid=(rows.n,),
        in_specs=[
            rows.row_spec(W), rows.row_spec(W),
            pl.BlockSpec((ROW_TILE, D), lambda i: (rows.tile(i), 0)),
            pl.BlockSpec((ROW_TILE, D), lambda i: (rows.tile(i), 1)),
            _const_spec((W, D)), _const_spec((W, D)), _const_spec((D, D)),
            rows.row_spec(D), rows.mod_spec(D, 2),
            _const_spec((1, D)), rows.mod_spec(D, 3), rows.mod_spec(D, 4),
            _const_spec((D, 2 * ROUTER_LANES)), _const_spec((1, ROUTER_LANES)),
        ],
        out_specs=[rows.row_spec(D),
                   pl.BlockSpec((ROW_TILE * chunks, LANES), lambda i: (rows.tile(i), 0)),
                   rows.row_spec(ROUTER_LANES)],
        out_shape=[jax.ShapeDtypeStruct((M, D), F32), jax.ShapeDtypeStruct((M * chunks, LANES), jnp.uint32),
                   jax.ShapeDtypeStruct((M, ROUTER_LANES), F32)],
        compiler_params=_params("parallel"),
        name="merge",
    )(ya, yb, gates, gates, wa, wb, wo, x_all, mod_l, nw, mod_l, mod_l, wr, br)


def _route(logits):
    n = logits.shape[0]
    gl = logits[:, :N_GROUPS]
    el = logits[:, N_GROUPS:N_GROUPS + N_EXPERTS].reshape(n, N_GROUPS, EXPERTS_PER_GROUP)
    g = jnp.argmax(gl, axis=-1).astype(jnp.int32)
    pg = 1.0 / jnp.sum(jnp.exp(gl - jnp.max(gl, axis=-1, keepdims=True)), axis=-1, keepdims=True)
    in_group = jnp.arange(N_GROUPS, dtype=jnp.int32)[None, :, None] == g[:, None, None]
    el_g = jnp.sum(jnp.where(in_group, el, 0.0), axis=1)
    tv, ti = lax.top_k(el_g, TOP_K)
    wts = jax.nn.softmax(tv, axis=-1) * pg
    eid = g[:, None] * EXPERTS_PER_GROUP + ti.astype(jnp.int32)
    return eid, wts


def _dispatch(eid, tok_rows, row_scale):
    n = eid.shape[0]
    a = n * TOP_K
    e_flat = eid.reshape(a)
    onehot = (e_flat[:, None] == jnp.arange(N_EXPERTS, dtype=jnp.int32)[None, :]).astype(jnp.int32)
    csum = jnp.cumsum(onehot, axis=0)
    counts = csum[-1]
    padded = (counts + MOE_ROWS - 1) // MOE_ROWS * MOE_ROWS
    pad_end = jnp.cumsum(padded)
    pad_start = pad_end - padded
    dest = jnp.sum(onehot * (csum - 1 + pad_start[None, :]), axis=1)
    n_grid = a // MOE_ROWS + N_EXPERTS - 1 + GATHER_AHEAD
    slot_row = jnp.zeros((n_grid * MOE_ROWS,), jnp.int32).at[dest].set(jnp.repeat(tok_rows * row_scale, TOP_K))
    blk = jnp.arange(n_grid, dtype=jnp.int32)
    n_valid = (pad_end[-1] // MOE_ROWS).astype(jnp.int32)
    blk_c = jnp.minimum(blk, n_valid - 1)
    blk_e = jnp.clip(jnp.searchsorted(pad_end, blk_c * MOE_ROWS, side="right"), 0, N_EXPERTS - 1).astype(jnp.int32)
    experts = jnp.arange(N_EXPERTS, dtype=jnp.int32)
    used = counts > 0
    first_used_from = lax.cummin(jnp.where(used, experts, N_EXPERTS), reverse=True)
    next_used = jnp.concatenate([first_used_from[1:], jnp.full((1,), N_EXPERTS, jnp.int32)])
    next_used = jnp.where(next_used == N_EXPERTS, -1, next_used)
    parity = (jnp.cumsum(used.astype(jnp.int32)) - 1) % 2
    return (dest, blk_e, next_used[blk_e].astype(jnp.int32), parity[blk_e].astype(jnp.int32), slot_row,
            n_valid.reshape(1))


def _experts_kernel(blk_e, blk_next, blk_par, slot_row, n_valid,
                    h_hbm, w1_hbm, w3_hbm, w2_hbm, o_ref,
                    xg, xsem, ws1, ws3, ws2, wsem, w1b, w3b, w2b, *, layer):
    i = pl.program_id(0)
    nv = n_valid[0]
    chunks = xg.shape[1] // MOE_ROWS
    e = blk_e[i]
    par = blk_par[i]
    fresh = jnp.logical_or(i == 0, e != blk_e[jnp.maximum(i - 1, 0)])

    def weight_copies(expert, p):
        return (pltpu.make_async_copy(w1_hbm.at[layer, expert], ws1.at[p], wsem.at[p, 0]),
                pltpu.make_async_copy(w3_hbm.at[layer, expert], ws3.at[p], wsem.at[p, 1]),
                pltpu.make_async_copy(w2_hbm.at[layer, expert], ws2.at[p], wsem.at[p, 2]))

    def for_rows(blk, fn):
        buf = lax.rem(blk, GATHER_AHEAD + 1)
        base = blk * MOE_ROWS
        for j in range(MOE_ROWS):
            r = pl.multiple_of(slot_row[base + j], chunks)
            fn(pltpu.make_async_copy(h_hbm.at[pl.ds(r, chunks), :], xg.at[buf, pl.ds(j * chunks, chunks), :],
                                     xsem.at[buf]))

    @pl.when(i == 0)
    def _():
        for cp in weight_copies(e, par):
            cp.start(priority=WEIGHT_DMA_PRIORITY)
        for blk in range(GATHER_AHEAD):
            for_rows(blk, lambda cp: cp.start())

    @pl.when(jnp.logical_and(i >= nv, i < nv + GATHER_AHEAD))
    def _():
        for_rows(i, lambda cp: cp.wait())

    @pl.when(fresh)
    def _():
        for cp in weight_copies(e, par):
            cp.wait()
        nxt = blk_next[i]

        @pl.when(nxt >= 0)
        def _():
            for cp in weight_copies(nxt, 1 - par):
                cp.start(priority=WEIGHT_DMA_PRIORITY)

        w1b[...] = ws1[par].astype(BF16)
        w3b[...] = ws3[par].astype(BF16)
        w2b[...] = ws2[par].astype(BF16)

    @pl.when(i < nv)
    def _():
        for_rows(i, lambda cp: cp.wait())
        words = jnp.concatenate(
            [xg[lax.rem(i, GATHER_AHEAD + 1), pl.ds(c, MOE_ROWS, stride=chunks), :] for c in range(chunks)],
            axis=1)
        x = jnp.concatenate([pltpu.bitcast(words << 16, F32).astype(BF16),
                             pltpu.bitcast(words & jnp.uint32(0xFFFF0000), F32).astype(BF16)], axis=1)
        for_rows(i + GATHER_AHEAD, lambda cp: cp.start())
        a = jnp.dot(x, w1b[...], preferred_element_type=F32)
        b = jnp.dot(x, w3b[...], preferred_element_type=F32)
        hid = (a * jax.nn.sigmoid(a)) * b
        o_ref[...] = jnp.dot(hid.astype(BF16), w2b[...], preferred_element_type=F32).astype(o_ref.dtype)


def _experts(h2, w1, w3, w2, layer, blk_e, blk_next, blk_par, slot_row, n_valid):
    D, de = w1.shape[-2:]
    chunks = D // 2 // LANES
    n_blk = blk_e.shape[0]
    any_spec = pl.BlockSpec(memory_space=pl.ANY)
    return pl.pallas_call(
        functools.partial(_experts_kernel, layer=layer),
        grid_spec=pltpu.PrefetchScalarGridSpec(
            num_scalar_prefetch=5, grid=(n_blk,),
            in_specs=[any_spec, any_spec, any_spec, any_spec],
            out_specs=pl.BlockSpec((MOE_ROWS, D), lambda i, *s: (jnp.minimum(i, s[4][0] - 1), 0)),
            scratch_shapes=[
                pltpu.VMEM((GATHER_AHEAD + 1, MOE_ROWS * chunks, LANES), jnp.uint32),
                pltpu.SemaphoreType.DMA((GATHER_AHEAD + 1,)),
                pltpu.VMEM((2, D, de), F32), pltpu.VMEM((2, D, de), F32), pltpu.VMEM((2, de, D), F32),
                pltpu.SemaphoreType.DMA((2, 3)),
                pltpu.VMEM((D, de), BF16), pltpu.VMEM((D, de), BF16), pltpu.VMEM((de, D), BF16),
            ],
        ),
        out_shape=jax.ShapeDtypeStruct((n_blk * MOE_ROWS, D), BF16),
        compiler_params=_params("arbitrary"),
        name="experts",
    )(blk_e, blk_next, blk_par, slot_row, n_valid, h2, w1, w3, w2)


def _moe_mix(x_ref, y0_ref, y1_ref, w_ref, g2_ref):
    w = w_ref[...]
    y = w[:, 0:1] * y0_ref[...].astype(F32) + w[:, 1:2] * y1_ref[...].astype(F32)
    return x_ref[...] + g2_ref[...] * y


def _combine_next_kernel(x_ref, y0_ref, y1_ref, w_ref, g2_ref, nw_ref, sh_ref, sc_ref, xo_ref, h_ref):
    xn = _moe_mix(x_ref, y0_ref, y1_ref, w_ref, g2_ref)
    xo_ref[...] = xn
    h_ref[...] = ((_rms(xn) * nw_ref[...]) * (1.0 + sc_ref[...]) + sh_ref[...]).astype(BF16)


def _combine_final_kernel(x_ref, y0_ref, y1_ref, w_ref, g2_ref, nf_ref, o_ref):
    o_ref[...] = _rms(_moe_mix(x_ref, y0_ref, y1_ref, w_ref, g2_ref)) * nf_ref[...]


def _token_specs(n_tokens, d):
    tiles = n_tokens // ROW_TILE
    return [pl.BlockSpec((ROW_TILE, d), lambda i: (i, 0)), pl.BlockSpec((ROW_TILE, d), lambda i: (i + tiles, 0)),
            pl.BlockSpec((ROW_TILE, TOP_K), lambda i: (i, 0))]


def _combine_next(x_new, ysel, wts, mod_l, nw_next, mod_next, rows):
    M, D = x_new.shape
    return pl.pallas_call(
        _combine_next_kernel,
        grid=(rows.n,),
        in_specs=[rows.row_spec(D), *_token_specs(M, D), rows.mod_spec(D, 5),
                  _const_spec((1, D)), rows.mod_spec(D, 0), rows.mod_spec(D, 1)],
        out_specs=[rows.row_spec(D), rows.row_spec(D)],
        out_shape=[jax.ShapeDtypeStruct((M, D), F32), jax.ShapeDtypeStruct((M, D), BF16)],
        compiler_params=_params("parallel"),
        name="combine_next",
    )(x_new, ysel, ysel, wts, mod_l, nw_next, mod_next, mod_next)


def _combine_final(x_new, ysel, wts, mod_l, nf, rows):
    M, D = x_new.shape
    n = wts.shape[0]
    return pl.pallas_call(
        _combine_final_kernel,
        grid=(rows.n,),
        in_specs=[rows.row_spec(D), *_token_specs(n, D), rows.mod_spec(D, 5), _const_spec((1, D))],
        out_specs=pl.BlockSpec((ROW_TILE, D), lambda i: (i, 0)),
        out_shape=jax.ShapeDtypeStruct((n, D), F32),
        compiler_params=_params("parallel"),
        name="combine_final",
    )(x_new, ysel, ysel, wts, mod_l, nf)


def _rope_tables(ctx_len, seq):
    quarter = HEAD_DIM // 4
    freqs = ROPE_BASE ** (-jnp.arange(quarter, dtype=F32) / quarter)
    pos = jnp.arange(seq, dtype=jnp.int32)
    row = (pos // GRID_W).astype(F32)[:, None] * freqs[None, :]
    col = (pos % GRID_W).astype(F32)[:, None] * freqs[None, :]
    cos = jnp.concatenate([jnp.cos(row), jnp.cos(col), jnp.cos(row), jnp.cos(col)], axis=-1)
    sin = jnp.concatenate([-jnp.sin(row), -jnp.sin(col), jnp.sin(row), jnp.sin(col)], axis=-1)
    cos = jnp.concatenate([jnp.ones((ctx_len, HEAD_DIM), F32), cos], axis=0)
    sin = jnp.concatenate([jnp.zeros((ctx_len, HEAD_DIM), F32), sin], axis=0)
    return cos, sin


def kernel(x, c, ctx, c_ctx, w_mod, b_mod, norm_mix, norm_ffn, w_in, qn_a, kn_a, sink_b, w_br_a, w_br_b, w_out, w_rg, b_rg, w_re, b_re, w1, w3, w2, norm_final):
    B, S, D = x.shape
    C = ctx.shape[1]
    L = w_mod.shape[0]
    T = C + S
    M = B * T
    assert C == ROW_TILE and S % ROW_TILE == 0 and T % PROJ_ROWS == 0 and B < MOD_ROWS
    tiles_per_batch = T // ROW_TILE

    cc = jnp.concatenate([c, c_ctx[None, :], jnp.zeros((MOD_ROWS - B - 1, D), F32)], axis=0)
    mod = _mod_vectors(cc, w_mod, b_mod).reshape(L, MOD_ROWS * 6, 1, D)
    cos, sin = _rope_tables(C, S)
    latent_rows = (jnp.arange(B, dtype=jnp.int32)[:, None] * T + C
                   + jnp.arange(S, dtype=jnp.int32)[None, :]).reshape(B * S)
    all_rows = jnp.arange(M, dtype=jnp.int32)

    q_kinds = lambda kind, cols=PROJ_COLS: tuple((kind, 0, k) for k in range(cols // HEAD_DIM))
    kv_kinds = lambda kind: ((kind, 0, 0), (kind, 0, 1), ("v", 1, 0), ("v", 1, 1))
    col = {"qa": 0, "kva": WIDTH, "qb": WIDTH + 2 * KVW, "kvb": 2 * WIDTH + 2 * KVW, "gates": 2 * WIDTH + 4 * KVW}

    every = _Rows(B, tiles_per_batch, latent_only=False)
    latent = _Rows(B, tiles_per_batch, latent_only=True)
    x_all, h = _norm_mod(x, ctx, norm_mix[0][None, :], mod[0])
    out = None
    for l in range(L):
        last = l == L - 1
        rows = latent if last else every
        proj = functools.partial(_proj, h, w_in, l, cos, sin, tiles_per_batch=tiles_per_batch)
        q_gain, k_gain = _pair_halves(qn_a[l])[None, :], _pair_halves(kn_a[l])[None, :]
        (qa,) = proj(q_gain,col["qa"], q_kinds("q_norm"), [(PROJ_COLS, WIDTH // PROJ_COLS)], name="proj_qa")
        ka, va = proj(k_gain,col["kva"], kv_kinds("k_norm"), [(KVW, 1), (KVW, 1)], name="proj_kva")
        (qb,) = proj(q_gain,col["qb"], q_kinds("q_rope"), [(PROJ_COLS, WIDTH // PROJ_COLS)], name="proj_qb")
        kb, vb = proj(k_gain,col["kvb"], kv_kinds("k_rope"), [(KVW, 1), (KVW, 1)], name="proj_kvb")
        (gates,) = proj(q_gain, col["gates"], q_kinds("gate", GATE_COLS), [(GATE_COLS, 2 * D // GATE_COLS)], name="proj_gates")

        ya = _attention(qa, ka, va, None, B, T, C, last, window=False)
        yb = _attention(qb, kb, vb, sink_b[l], B, T, C, last, window=True)

        w_router = jnp.concatenate(
            [w_rg[l], w_re[l], jnp.zeros((D, ROUTER_LANES - N_GROUPS - N_EXPERTS), F32)], axis=1)
        w_router_hi = w_router.astype(BF16)
        w_router = jnp.concatenate([w_router_hi, (w_router - w_router_hi.astype(F32)).astype(BF16)], axis=1)
        b_router = jnp.concatenate(
            [b_rg[l], b_re[l], jnp.zeros((ROUTER_LANES - N_GROUPS - N_EXPERTS,), F32)])[None, :]
        x_new, h2, logits = _merge(ya, yb, gates, w_br_a[l].astype(BF16), w_br_b[l].astype(BF16),
                                   w_out[l].astype(BF16), x_all, mod[l], norm_ffn[l][None, :],
                                   w_router, b_router, rows)

        tok_rows = latent_rows if last else all_rows
        eid, wts = _route(logits[tok_rows] if last else logits)
        dest, *plan = _dispatch(eid, tok_rows, D // 2 // LANES)
        ybuf = _experts(h2, w1, w3, w2, l, *plan)
        ysel = ybuf[dest.reshape(-1, TOP_K).T.reshape(-1)]
        if last:
            out = _combine_final(x_new, ysel, wts, mod[l], norm_final[None, :], rows).reshape(B, S, D)
        else:
            x_all, h = _combine_next(x_new, ysel, wts, mod[l], norm_mix[l + 1][None, :], mod[l + 1], rows)
    return out
```

```python
import functools

import jax
import jax.numpy as jnp
from jax import lax
from jax.experimental import pallas as pl
from jax.experimental.pallas import tpu as pltpu

F32 = jnp.float32
BF16 = jnp.bfloat16

GRID_W = 64
HEAD_DIM = 128
LANES = 128
HEADS = 8
KV_HEADS = 2
GROUP = HEADS // KV_HEADS
WIDTH = HEADS * HEAD_DIM
KVW = KV_HEADS * HEAD_DIM
WINDOW = 128
ROPE_BASE = 10000.0
ATTN_SCALE = HEAD_DIM ** -0.5
LOG2E = 1.4426950408889634
N_GROUPS = 8
EXPERTS_PER_GROUP = 8
N_EXPERTS = N_GROUPS * EXPERTS_PER_GROUP
TOP_K = 2
EPS = 1e-6
NEG_INF = -1e30

ROW_TILE = 256
PROJ_ROWS = 768
GATE_COLS = 1024
MOE_ROWS = 128
GATHER_AHEAD = 2
WEIGHT_DMA_PRIORITY = 1
ROUTER_LANES = 128
DENSE_CHAIN_HEADS = 4
WINDOW_CHAIN_HEADS = 1
MOD_ROWS = 8
VMEM_LIMIT = 56 * 1024 * 1024


def _params(*sem):
    return pltpu.CompilerParams(dimension_semantics=sem, vmem_limit_bytes=VMEM_LIMIT)


def _rms(y):
    return y * lax.rsqrt(jnp.mean(y * y, axis=-1, keepdims=True) + EPS)


def _mod_kernel(c_ref, w_ref, b_ref, o_ref):
    c = c_ref[...]
    a = (c * jax.nn.sigmoid(c)).astype(BF16)
    o_ref[...] = jnp.dot(a, w_ref[...].astype(BF16), preferred_element_type=F32) + b_ref[...]


def _mod_vectors(cc, w_mod, b_mod):
    L, D, N = w_mod.shape
    tn = 1024
    return pl.pallas_call(
        _mod_kernel,
        grid=(L, N // tn),
        in_specs=[
            pl.BlockSpec((MOD_ROWS, D), lambda l, j: (0, 0)),
            pl.BlockSpec((None, D, tn), lambda l, j: (l, 0, j)),
            pl.BlockSpec((None, 1, tn), lambda l, j: (l, 0, j)),
        ],
        out_specs=pl.BlockSpec((None, MOD_ROWS, tn), lambda l, j: (l, 0, j)),
        out_shape=jax.ShapeDtypeStruct((L, MOD_ROWS, N), F32),
        compiler_params=_params("parallel", "parallel"),
        name="mod_vectors",
    )(cc, w_mod, b_mod.reshape(L, 1, N))


class _Rows:
    def __init__(self, batch, tiles_per_batch, latent_only):
        self.batch = batch
        self.tpb = tiles_per_batch
        self.latent_only = latent_only
        self.n = batch * (tiles_per_batch - 1 if latent_only else tiles_per_batch)

    def tile(self, i):
        if self.latent_only:
            per = self.tpb - 1
            return (i // per) * self.tpb + 1 + i % per
        return i

    def mod_row(self, i):
        if self.latent_only:
            return i // (self.tpb - 1)
        return jnp.where(i % self.tpb == 0, self.batch, i // self.tpb)

    def row_spec(self, width):
        return pl.BlockSpec((ROW_TILE, width), lambda i: (self.tile(i), 0))

    def mod_spec(self, d, k):
        return pl.BlockSpec((None, 1, d), lambda i: (self.mod_row(i) * 6 + k, 0, 0))


def _const_spec(shape):
    zeros = (0,) * len(shape)
    return pl.BlockSpec(shape, lambda *_: zeros, pipeline_mode=pl.Buffered(1))


def _norm_mod_kernel(x_ref, ctx_ref, nw_ref, sh_ref, sc_ref, xo_ref, h_ref):
    def emit(src_ref):
        x = src_ref[...]
        xo_ref[...] = x
        h_ref[...] = ((_rms(x) * nw_ref[...]) * (1.0 + sc_ref[...]) + sh_ref[...]).astype(BF16)

    is_ctx = pl.program_id(1) == 0
    pl.when(is_ctx)(lambda: emit(ctx_ref))
    pl.when(jnp.logical_not(is_ctx))(lambda: emit(x_ref))


def _norm_mod(x, ctx, nw, mod_l):
    B, S, D = x.shape
    tiles = 1 + S // ROW_TILE
    mod_spec = lambda k: pl.BlockSpec((None, 1, D), lambda b, t: (jnp.where(t == 0, B, b) * 6 + k, 0, 0))
    out_spec = pl.BlockSpec((ROW_TILE, D), lambda b, t: (b * tiles + t, 0))
    return pl.pallas_call(
        _norm_mod_kernel,
        grid=(B, tiles),
        in_specs=[pl.BlockSpec((None, ROW_TILE, D), lambda b, t: (b, jnp.maximum(t - 1, 0), 0)),
                  pl.BlockSpec((None, ROW_TILE, D), lambda b, t: (b, 0, 0)),
                  _const_spec((1, D)), mod_spec(0), mod_spec(1)],
        out_specs=[out_spec, out_spec],
        out_shape=[jax.ShapeDtypeStruct((B * tiles * ROW_TILE, D), F32),
                   jax.ShapeDtypeStruct((B * tiles * ROW_TILE, D), BF16)],
        compiler_params=_params("parallel", "parallel"),
        name="norm_mod",
    )(x, ctx, nw, mod_l, mod_l)


ROTARY_KINDS = ("q_norm", "k_norm", "q_rope", "k_rope")


def _pair_halves(a):
    q = HEAD_DIM // 4
    return jnp.concatenate([a[..., 0:q], a[..., 2 * q:3 * q], a[..., q:2 * q], a[..., 3 * q:]], axis=-1)


def _rope(y, cos, sin):
    return y * cos + pltpu.roll(y, HEAD_DIM // 2, 1) * sin


def _proj_kernel(h_ref, w_ref, cos_ref, sin_ref, gain_ref, *refs, kinds):
    *out_refs, wb_ref = refs

    @pl.when(pl.program_id(1) == 0)
    def _():
        for c, (kind, _, _) in enumerate(kinds):
            w = w_ref[:, c * HEAD_DIM:(c + 1) * HEAD_DIM]
            wb_ref[:, c * HEAD_DIM:(c + 1) * HEAD_DIM] = (_pair_halves(w) if kind in ROTARY_KINDS else w).astype(BF16)

    acc = jnp.dot(h_ref[...], wb_ref[...], preferred_element_type=F32)
    for c, (kind, out_idx, out_chunk) in enumerate(kinds):
        y = acc[:, c * HEAD_DIM:(c + 1) * HEAD_DIM]
        if kind == "q_norm":
            y = _rms(y) * gain_ref[0:1, :]
        if kind == "k_norm":
            y = _rms(y) * gain_ref[1:2, :]
        if kind in ROTARY_KINDS:
            y = _rope(y, cos_ref[...], sin_ref[...])
        if kind in ("q_norm", "q_rope"):
            y = y * (ATTN_SCALE * LOG2E)
        out_refs[out_idx][:, out_chunk * HEAD_DIM:(out_chunk + 1) * HEAD_DIM] = y.astype(BF16)


def _proj(h, w_in, layer, cos, sin, gain, col0, kinds, out_widths, tiles_per_batch, name):
    M, D = h.shape
    n_col_tiles = out_widths[0][1]
    cols = len(kinds) * HEAD_DIM
    cb0 = col0 // cols
    proj_tiles_per_batch = tiles_per_batch * ROW_TILE // PROJ_ROWS
    return pl.pallas_call(
        functools.partial(_proj_kernel, kinds=kinds),
        grid=(n_col_tiles, M // PROJ_ROWS),
        in_specs=[
            pl.BlockSpec((PROJ_ROWS, D), lambda j, i: (i, 0)),
            pl.BlockSpec((None, D, cols), lambda j, i: (layer, 0, cb0 + j),
                         pipeline_mode=pl.Buffered(1 if n_col_tiles == 1 else 2)),
            pl.BlockSpec((PROJ_ROWS, HEAD_DIM), lambda j, i: (i % proj_tiles_per_batch, 0)),
            pl.BlockSpec((PROJ_ROWS, HEAD_DIM), lambda j, i: (i % proj_tiles_per_batch, 0)),
            pl.BlockSpec((2, HEAD_DIM), lambda j, i: (0, 0)),
        ],
        out_specs=[pl.BlockSpec((PROJ_ROWS, w), lambda j, i: (i, j)) for w, _ in out_widths],
        out_shape=[jax.ShapeDtypeStruct((M, w * n), BF16) for w, n in out_widths],
        scratch_shapes=[pltpu.VMEM((D, cols), BF16)],
        compiler_params=_params("parallel", "arbitrary"),
        name=name,
    )(h, w_in, cos, sin, gain)


def _stack_heads(q_ref, g0, n):
    return jnp.concatenate([q_ref[:, g * HEAD_DIM:(g + 1) * HEAD_DIM] for g in range(g0, g0 + n)], axis=0)


def _write_heads(o_ref, g0, n, o):
    rows = o_ref.shape[0]
    for j in range(n):
        o_ref[:, (g0 + j) * HEAD_DIM:(g0 + j + 1) * HEAD_DIM] = o[j * rows:(j + 1) * rows].astype(o_ref.dtype)


def _scores(q, k):
    return lax.dot_general(q, k, (((1,), (1,)), ((), ())), preferred_element_type=F32)


def _online_step(q, k, v, m, l, acc):
    s = _scores(q, k)
    m_new = jnp.maximum(m, jnp.max(s, axis=-1, keepdims=True))
    alpha = jnp.exp2(m - m_new)
    p = jnp.exp2(s - m_new)
    l = alpha * l + jnp.sum(p, axis=-1, keepdims=True)
    acc = alpha * acc + jnp.dot(p.astype(BF16), v, preferred_element_type=F32)
    return m_new, l, acc


def _dense_attn_kernel(q_ref, k_ref, v_ref, o_ref, *, ctx_len, total_len, key_chunk, ctx_tile):
    def run(n_keys, chunk):
        for g0 in range(0, GROUP, DENSE_CHAIN_HEADS):
            q = _stack_heads(q_ref, g0, DENSE_CHAIN_HEADS)
            r = q.shape[0]
            m = jnp.full((r, 1), -jnp.inf, F32)
            l = jnp.zeros((r, 1), F32)
            acc = jnp.zeros((r, HEAD_DIM), F32)
            for c in range(n_keys // chunk):
                m, l, acc = _online_step(q, k_ref[c * chunk:(c + 1) * chunk, :],
                                         v_ref[c * chunk:(c + 1) * chunk, :], m, l, acc)
            _write_heads(o_ref, g0, DENSE_CHAIN_HEADS, acc * (1.0 / l))

    if ctx_tile:
        is_ctx = pl.program_id(2) == 0
        pl.when(is_ctx)(lambda: run(ctx_len, ctx_len))
        pl.when(jnp.logical_not(is_ctx))(lambda: run(total_len, key_chunk))
    else:
        run(total_len, key_chunk)


def _window_attn_kernel(sink_ref, q_ref, k_ref, v_ref, o_ref, *, ctx_len, total_len, ctx_tile, q_off):
    kvh = pl.program_id(1)
    tq = q_ref.shape[0]
    span = tq + 2 * WINDOW
    n = WINDOW_CHAIN_HEADS

    def attend(parts):
        for g0 in range(0, GROUP, n):
            q = _stack_heads(q_ref, g0, n)
            sink = jnp.concatenate(
                [jnp.full((tq, 1), sink_ref[kvh * GROUP + g] * LOG2E, F32) for g in range(g0, g0 + n)], axis=0)
            scores = []
            m = sink
            for k, _, bias in parts:
                s = _scores(q, k)
                if bias is not None:
                    s = s + bias
                scores.append(s)
                m = jnp.maximum(m, jnp.max(s, axis=-1, keepdims=True))
            l = jnp.exp2(sink - m)
            o = jnp.zeros((n * tq, HEAD_DIM), F32)
            for s, (_, v, _) in zip(scores, parts):
                p = jnp.exp2(s - m)
                l = l + jnp.sum(p, axis=-1, keepdims=True)
                o = o + jnp.dot(p.astype(BF16), v, preferred_element_type=F32)
            _write_heads(o_ref, g0, n, o * (1.0 / l))

    def ctx_part():
        return k_ref[0:ctx_len, :], v_ref[0:ctx_len, :], None

    def run_latent():
        q0 = (pl.program_id(2) + q_off) * tq - ctx_len
        n_latent = total_len - ctx_len
        start = pl.multiple_of(jnp.clip(q0 - WINDOW, 0, n_latent - span), WINDOW)
        row = lax.broadcasted_iota(jnp.int32, (n * tq, span), 0) & (tq - 1)
        col = lax.broadcasted_iota(jnp.int32, (n * tq, span), 1)
        bias = jnp.where(jnp.abs(col - row + (start - q0)) <= WINDOW, 0.0, NEG_INF).astype(F32)
        attend([ctx_part(), (k_ref[pl.ds(ctx_len + start, span), :], v_ref[pl.ds(ctx_len + start, span), :], bias)])

    if ctx_tile:
        is_ctx = pl.program_id(2) == 0
        pl.when(is_ctx)(lambda: attend([ctx_part()]))
        pl.when(jnp.logical_not(is_ctx))(run_latent)
    else:
        run_latent()


def _attention(q, k, v, sink, batch, total_len, ctx_len, latent_only, window):
    M = q.shape[0]
    q3, k3, v3 = (a.reshape(batch, total_len, a.shape[1]) for a in (q, k, v))
    tiles = total_len // ROW_TILE
    q_off = 1 if latent_only else 0
    gw = GROUP * HEAD_DIM
    q_spec = pl.BlockSpec((None, ROW_TILE, gw), lambda b, h, i, *_: (b, i + q_off, h))
    kv_spec = pl.BlockSpec((None, total_len, HEAD_DIM), lambda b, h, i, *_: (b, 0, h))
    grid = (batch, KV_HEADS, tiles - q_off)
    out_shape = jax.ShapeDtypeStruct(q3.shape, BF16)
    if window:
        kern = functools.partial(_window_attn_kernel, ctx_len=ctx_len, total_len=total_len,
                                 ctx_tile=not latent_only, q_off=q_off)
        out = pl.pallas_call(
            kern,
            grid_spec=pltpu.PrefetchScalarGridSpec(
                num_scalar_prefetch=1, grid=grid,
                in_specs=[q_spec, kv_spec, kv_spec], out_specs=q_spec),
            out_shape=out_shape,
            compiler_params=_params("parallel", "parallel", "parallel"),
            name="window_attention",
        )(sink, q3, k3, v3)
    else:
        kern = functools.partial(_dense_attn_kernel, ctx_len=ctx_len, total_len=total_len,
                                 key_chunk=total_len // 3, ctx_tile=not latent_only)
        out = pl.pallas_call(
            kern, grid=grid, in_specs=[q_spec, kv_spec, kv_spec], out_specs=q_spec,
            out_shape=out_shape,
            compiler_params=_params("parallel", "parallel", "parallel"),
            name="dense_attention",
        )(q3, k3, v3)
    return out.reshape(M, q.shape[1])


def _merge_kernel(ya_ref, yb_ref, ga_ref, gb_ref, wa_ref, wb_ref, wo_ref, x_ref, g1_ref,
                  nw_ref, sh_ref, sc_ref, wr_ref, br_ref, xo_ref, h_ref, lg_ref):
    a = jnp.dot(ya_ref[...], wa_ref[...], preferred_element_type=F32)
    b = jnp.dot(yb_ref[...], wb_ref[...], preferred_element_type=F32)
    m = jax.nn.sigmoid(ga_ref[...].astype(F32)) * a + jax.nn.sigmoid(gb_ref[...].astype(F32)) * b
    o = jnp.dot(m.astype(BF16), wo_ref[...], preferred_element_type=F32)
    xn = x_ref[...] + g1_ref[...] * o
    xo_ref[...] = xn
    h = (_rms(xn) * nw_ref[...]) * (1.0 + sc_ref[...]) + sh_ref[...]
    h_hi = h.astype(BF16)
    bits = pltpu.bitcast(h_hi.astype(F32), jnp.uint32)
    half = h.shape[1] // 2
    words = (bits[:, :half] >> 16) | (bits[:, half:] & jnp.uint32(0xFFFF0000))
    chunks = half // LANES
    for c in range(chunks):
        h_ref[pl.ds(c, h.shape[0], stride=chunks), :] = words[:, c * LANES:(c + 1) * LANES]
    h_lo = (h - h_hi.astype(F32)).astype(BF16)
    lg = jnp.dot(h_hi, wr_ref[...], preferred_element_type=F32)
    lg_lo = jnp.dot(h_lo, wr_ref[:, :ROUTER_LANES], preferred_element_type=F32)
    lg_ref[...] = lg[:, :ROUTER_LANES] + (lg[:, ROUTER_LANES:] + lg_lo) + br_ref[...]


def _merge(ya, yb, gates, wa, wb, wo, x_all, mod_l, nw, wr, br, rows):
    M, D = x_all.shape
    W = ya.shape[1]
    chunks = D // 2 // LANES
    return pl.pallas_call(
        _merge_kernel,
        grid=(rows.n,),
        in_specs=[
            rows.row_spec(W), rows.row_spec(W),
            pl.BlockSpec((ROW_TILE, D), lambda i: (rows.tile(i), 0)),
            pl.BlockSpec((ROW_TILE, D), lambda i: (rows.tile(i), 1)),
            _const_spec((W, D)), _const_spec((W, D)), _const_spec((D, D)),
            rows.row_spec(D), rows.mod_spec(D, 2),
            _const_spec((1, D)), rows.mod_spec(D, 3), rows.mod_spec(D, 4),
            _const_spec((D, 2 * ROUTER_LANES)), _const_spec((1, ROUTER_LANES)),
        ],
        out_specs=[rows.row_spec(D),
                   pl.BlockSpec((ROW_TILE * chunks, LANES), lambda i: (rows.tile(i), 0)),
                   rows.row_spec(ROUTER_LANES)],
        out_shape=[jax.ShapeDtypeStruct((M, D), F32), jax.ShapeDtypeStruct((M * chunks, LANES), jnp.uint32),
                   jax.ShapeDtypeStruct((M, ROUTER_LANES), F32)],
        compiler_params=_params("parallel"),
        name="merge",
    )(ya, yb, gates, gates, wa, wb, wo, x_all, mod_l, nw, mod_l, mod_l, wr, br)


def _route(logits):
    n = logits.shape[0]
    gl = logits[:, :N_GROUPS]
    el = logits[:, N_GROUPS:N_GROUPS + N_EXPERTS].reshape(n, N_GROUPS, EXPERTS_PER_GROUP)
    g = jnp.argmax(gl, axis=-1).astype(jnp.int32)
    pg = 1.0 / jnp.sum(jnp.exp(gl - jnp.max(gl, axis=-1, keepdims=True)), axis=-1, keepdims=True)
    in_group = jnp.arange(N_GROUPS, dtype=jnp.int32)[None, :, None] == g[:, None, None]
    el_g = jnp.sum(jnp.where(in_group, el, 0.0), axis=1)
    tv, ti = lax.top_k(el_g, TOP_K)
    wts = jax.nn.softmax(tv, axis=-1) * pg
    eid = g[:, None] * EXPERTS_PER_GROUP + ti.astype(jnp.int32)
    return eid, wts


def _dispatch(eid, tok_rows, row_scale):
    n = eid.shape[0]
    a = n * TOP_K
    e_flat = eid.reshape(a)
    onehot = (e_flat[:, None] == jnp.arange(N_EXPERTS, dtype=jnp.int32)[None, :]).astype(jnp.int32)
    csum = jnp.cumsum(onehot, axis=0)
    counts = csum[-1]
    padded = (counts + MOE_ROWS - 1) // MOE_ROWS * MOE_ROWS
    pad_end = jnp.cumsum(padded)
    pad_start = pad_end - padded
    dest = jnp.sum(onehot * (csum - 1 + pad_start[None, :]), axis=1)
    n_grid = a // MOE_ROWS + N_EXPERTS - 1 + GATHER_AHEAD
    slot_row = jnp.zeros((n_grid * MOE_ROWS,), jnp.int32).at[dest].set(jnp.repeat(tok_rows * row_scale, TOP_K))
    blk = jnp.arange(n_grid, dtype=jnp.int32)
    n_valid = (pad_end[-1] // MOE_ROWS).astype(jnp.int32)
    blk_c = jnp.minimum(blk, n_valid - 1)
    blk_e = jnp.clip(jnp.searchsorted(pad_end, blk_c * MOE_ROWS, side="right"), 0, N_EXPERTS - 1).astype(jnp.int32)
    experts = jnp.arange(N_EXPERTS, dtype=jnp.int32)
    used = counts > 0
    first_used_from = lax.cummin(jnp.where(used, experts, N_EXPERTS), reverse=True)
    next_used = jnp.concatenate([first_used_from[1:], jnp.full((1,), N_EXPERTS, jnp.int32)])
    next_used = jnp.where(next_used == N_EXPERTS, -1, next_used)
    parity = (jnp.cumsum(used.astype(jnp.int32)) - 1) % 2
    return (dest, blk_e, next_used[blk_e].astype(jnp.int32), parity[blk_e].astype(jnp.int32), slot_row,
            n_valid.reshape(1))


def _experts_kernel(blk_e, blk_next, blk_par, slot_row, n_valid,
                    h_hbm, w1_hbm, w3_hbm, w2_hbm, o_ref,
                    xg, xsem, ws1, ws3, ws2, wsem, w1b, w3b, w2b, *, layer):
    i = pl.program_id(0)
    nv = n_valid[0]
    chunks = xg.shape[1] // MOE_ROWS
    e = blk_e[i]
    par = blk_par[i]
    fresh = jnp.logical_or(i == 0, e != blk_e[jnp.maximum(i - 1, 0)])

    def weight_copies(expert, p):
        return (pltpu.make_async_copy(w1_hbm.at[layer, expert], ws1.at[p], wsem.at[p, 0]),
                pltpu.make_async_copy(w3_hbm.at[layer, expert], ws3.at[p], wsem.at[p, 1]),
                pltpu.make_async_copy(w2_hbm.at[layer, expert], ws2.at[p], wsem.at[p, 2]))

    def for_rows(blk, fn):
        buf = lax.rem(blk, GATHER_AHEAD + 1)
        base = blk * MOE_ROWS
        for j in range(MOE_ROWS):
            r = pl.multiple_of(slot_row[base + j], chunks)
            fn(pltpu.make_async_copy(h_hbm.at[pl.ds(r, chunks), :], xg.at[buf, pl.ds(j * chunks, chunks), :],
                                     xsem.at[buf]))

    @pl.when(i == 0)
    def _():
        for cp in weight_copies(e, par):
            cp.start(priority=WEIGHT_DMA_PRIORITY)
        for blk in range(GATHER_AHEAD):
            for_rows(blk, lambda cp: cp.start())

    @pl.when(jnp.logical_and(i >= nv, i < nv + GATHER_AHEAD))
    def _():
        for_rows(i, lambda cp: cp.wait())

    @pl.when(fresh)
    def _():
        for cp in weight_copies(e, par):
            cp.wait()
        nxt = blk_next[i]

        @pl.when(nxt >= 0)
        def _():
            for cp in weight_copies(nxt, 1 - par):
                cp.start(priority=WEIGHT_DMA_PRIORITY)

        w1b[...] = ws1[par].astype(BF16)
        w3b[...] = ws3[par].astype(BF16)
        w2b[...] = ws2[par].astype(BF16)

    @pl.when(i < nv)
    def _():
        for_rows(i, lambda cp: cp.wait())
        words = jnp.concatenate(
            [xg[lax.rem(i, GATHER_AHEAD + 1), pl.ds(c, MOE_ROWS, stride=chunks), :] for c in range(chunks)],
            axis=1)
        x = jnp.concatenate([pltpu.bitcast(words << 16, F32).astype(BF16),
                             pltpu.bitcast(words & jnp.uint32(0xFFFF0000), F32).astype(BF16)], axis=1)
        for_rows(i + GATHER_AHEAD, lambda cp: cp.start())
        a = jnp.dot(x, w1b[...], preferred_element_type=F32)
        b = jnp.dot(x, w3b[...], preferred_element_type=F32)
        hid = (a * jax.nn.sigmoid(a)) * b
        o_ref[...] = jnp.dot(hid.astype(BF16), w2b[...], preferred_element_type=F32).astype(o_ref.dtype)


def _experts(h2, w1, w3, w2, layer, blk_e, blk_next, blk_par, slot_row, n_valid):
    D, de = w1.shape[-2:]
    chunks = D // 2 // LANES
    n_blk = blk_e.shape[0]
    any_spec = pl.BlockSpec(memory_space=pl.ANY)
    return pl.pallas_call(
        functools.partial(_experts_kernel, layer=layer),
        grid_spec=pltpu.PrefetchScalarGridSpec(
            num_scalar_prefetch=5, grid=(n_blk,),
            in_specs=[any_spec, any_spec, any_spec, any_spec],
            out_specs=pl.BlockSpec((MOE_ROWS, D), lambda i, *s: (jnp.minimum(i, s[4][0] - 1), 0)),
            scratch_shapes=[
                pltpu.VMEM((GATHER_AHEAD + 1, MOE_ROWS * chunks, LANES), jnp.uint32),
                pltpu.SemaphoreType.DMA((GATHER_AHEAD + 1,)),
                pltpu.VMEM((2, D, de), F32), pltpu.VMEM((2, D, de), F32), pltpu.VMEM((2, de, D), F32),
                pltpu.SemaphoreType.DMA((2, 3)),
                pltpu.VMEM((D, de), BF16), pltpu.VMEM((D, de), BF16), pltpu.VMEM((de, D), BF16),
            ],
        ),
        out_shape=jax.ShapeDtypeStruct((n_blk * MOE_ROWS, D), BF16),
        compiler_params=_params("arbitrary"),
        name="experts",
    )(blk_e, blk_next, blk_par, slot_row, n_valid, h2, w1, w3, w2)


def _moe_mix(x_ref, y0_ref, y1_ref, w_ref, g2_ref):
    w = w_ref[...]
    y = w[:, 0:1] * y0_ref[...].astype(F32) + w[:, 1:2] * y1_ref[...].astype(F32)
    return x_ref[...] + g2_ref[...] * y


def _combine_next_kernel(x_ref, y0_ref, y1_ref, w_ref, g2_ref, nw_ref, sh_ref, sc_ref, xo_ref, h_ref):
    xn = _moe_mix(x_ref, y0_ref, y1_ref, w_ref, g2_ref)
    xo_ref[...] = xn
    h_ref[...] = ((_rms(xn) * nw_ref[...]) * (1.0 + sc_ref[...]) + sh_ref[...]).astype(BF16)


def _combine_final_kernel(x_ref, y0_ref, y1_ref, w_ref, g2_ref, nf_ref, o_ref):
    o_ref[...] = _rms(_moe_mix(x_ref, y0_ref, y1_ref, w_ref, g2_ref)) * nf_ref[...]


def _token_specs(n_tokens, d):
    tiles = n_tokens // ROW_TILE
    return [pl.BlockSpec((ROW_TILE, d), lambda i: (i, 0)), pl.BlockSpec((ROW_TILE, d), lambda i: (i + tiles, 0)),
            pl.BlockSpec((ROW_TILE, TOP_K), lambda i: (i, 0))]


def _combine_next(x_new, ysel, wts, mod_l, nw_next, mod_next, rows):
    M, D = x_new.shape
    return pl.pallas_call(
        _combine_next_kernel,
        grid=(rows.n,),
        in_specs=[rows.row_spec(D), *_token_specs(M, D), rows.mod_spec(D, 5),
                  _const_spec((1, D)), rows.mod_spec(D, 0), rows.mod_spec(D, 1)],
        out_specs=[rows.row_spec(D), rows.row_spec(D)],
        out_shape=[jax.ShapeDtypeStruct((M, D), F32), jax.ShapeDtypeStruct((M, D), BF16)],
        compiler_params=_params("parallel"),
        name="combine_next",
    )(x_new, ysel, ysel, wts, mod_l, nw_next, mod_next, mod_next)


def _combine_final(x_new, ysel, wts, mod_l, nf, rows):
    M, D = x_new.shape
    n = wts.shape[0]
    return pl.pallas_call(
        _combine_final_kernel,
        grid=(rows.n,),
        in_specs=[rows.row_spec(D), *_token_specs(n, D), rows.mod_spec(D, 5), _const_spec((1, D))],
        out_specs=pl.BlockSpec((ROW_TILE, D), lambda i: (i, 0)),
        out_shape=jax.ShapeDtypeStruct((n, D), F32),
        compiler_params=_params("parallel"),
        name="combine_final",
    )(x_new, ysel, ysel, wts, mod_l, nf)


def _rope_tables(ctx_len, seq):
    quarter = HEAD_DIM // 4
    freqs = ROPE_BASE ** (-jnp.arange(quarter, dtype=F32) / quarter)
    pos = jnp.arange(seq, dtype=jnp.int32)
    row = (pos // GRID_W).astype(F32)[:, None] * freqs[None, :]
    col = (pos % GRID_W).astype(F32)[:, None] * freqs[None, :]
    cos = jnp.concatenate([jnp.cos(row), jnp.cos(col), jnp.cos(row), jnp.cos(col)], axis=-1)
    sin = jnp.concatenate([-jnp.sin(row), -jnp.sin(col), jnp.sin(row), jnp.sin(col)], axis=-1)
    cos = jnp.concatenate([jnp.ones((ctx_len, HEAD_DIM), F32), cos], axis=0)
    sin = jnp.concatenate([jnp.zeros((ctx_len, HEAD_DIM), F32), sin], axis=0)
    return cos, sin


def kernel(x, c, ctx, c_ctx, w_mod, b_mod, norm_mix, norm_ffn, w_in, qn_a, kn_a, sink_b, w_br_a, w_br_b, w_out, w_rg, b_rg, w_re, b_re, w1, w3, w2, norm_final):
    B, S, D = x.shape
    C = ctx.shape[1]
    L = w_mod.shape[0]
    T = C + S
    M = B * T
    assert C == ROW_TILE and S % ROW_TILE == 0 and T % PROJ_ROWS == 0 and B < MOD_ROWS
    tiles_per_batch = T // ROW_TILE

    cc = jnp.concatenate([c, c_ctx[None, :], jnp.zeros((MOD_ROWS - B - 1, D), F32)], axis=0)
    mod = _mod_vectors(cc, w_mod, b_mod).reshape(L, MOD_ROWS * 6, 1, D)
    cos, sin = _rope_tables(C, S)
    latent_rows = (jnp.arange(B, dtype=jnp.int32)[:, None] * T + C
                   + jnp.arange(S, dtype=jnp.int32)[None, :]).reshape(B * S)
    all_rows = jnp.arange(M, dtype=jnp.int32)

    mixer_kinds = lambda q_kind, k_kind: (tuple((q_kind, 0, k) for k in range(HEADS))
                                          + tuple((k_kind, 1, k) for k in range(KV_HEADS))
                                          + tuple(("v", 2, k) for k in range(KV_HEADS)))
    mixer_widths = [(WIDTH, 1), (KVW, 1), (KVW, 1)]
    gate_kinds = tuple(("gate", 0, k) for k in range(GATE_COLS // HEAD_DIM))
    col = {"a": 0, "b": WIDTH + 2 * KVW, "gates": 2 * WIDTH + 4 * KVW}

    every = _Rows(B, tiles_per_batch, latent_only=False)
    latent = _Rows(B, tiles_per_batch, latent_only=True)
    x_all, h = _norm_mod(x, ctx, norm_mix[0][None, :], mod[0])
    out = None
    for l in range(L):
        last = l == L - 1
        rows = latent if last else every
        proj = functools.partial(_proj, h, w_in, l, cos, sin, tiles_per_batch=tiles_per_batch)
        gains = jnp.stack([_pair_halves(qn_a[l]), _pair_halves(kn_a[l])])
        qa, ka, va = proj(gains, col["a"], mixer_kinds("q_norm", "k_norm"), mixer_widths, name="proj_a")
        qb, kb, vb = proj(gains, col["b"], mixer_kinds("q_rope", "k_rope"), mixer_widths, name="proj_b")
        (gates,) = proj(gains, col["gates"], gate_kinds, [(GATE_COLS, 2 * D // GATE_COLS)], name="proj_gates")

        ya = _attention(qa, ka, va, None, B, T, C, last, window=False)
        yb = _attention(qb, kb, vb, sink_b[l], B, T, C, last, window=True)

        w_router = jnp.concatenate(
            [w_rg[l], w_re[l], jnp.zeros((D, ROUTER_LANES - N_GROUPS - N_EXPERTS), F32)], axis=1)
        w_router_hi = w_router.astype(BF16)
        w_router = jnp.concatenate([w_router_hi, (w_router - w_router_hi.astype(F32)).astype(BF16)], axis=1)
        b_router = jnp.concatenate(
            [b_rg[l], b_re[l], jnp.zeros((ROUTER_LANES - N_GROUPS - N_EXPERTS,), F32)])[None, :]
        x_new, h2, logits = _merge(ya, yb, gates, w_br_a[l].astype(BF16), w_br_b[l].astype(BF16),
                                   w_out[l].astype(BF16), x_all, mod[l], norm_ffn[l][None, :],
                                   w_router, b_router, rows)

        tok_rows = latent_rows if last else all_rows
        eid, wts = _route(logits[tok_rows] if last else logits)
        dest, *plan = _dispatch(eid, tok_rows, D // 2 // LANES)
        ybuf = _experts(h2, w1, w3, w2, l, *plan)
        ysel = ybuf[dest.reshape(-1, TOP_K).T.reshape(-1)]
        if last:
            out = _combine_final(x_new, ysel, wts, mod[l], norm_final[None, :], rows).reshape(B, S, D)
        else:
            x_all, h = _combine_next(x_new, ysel, wts, mod[l], norm_mix[l + 1][None, :], mod[l + 1], rows)
    return out
```

```python
import functools

import jax
import jax.numpy as jnp
from jax import lax
from jax.experimental import pallas as pl
from jax.experimental.pallas import tpu as pltpu

F32 = jnp.float32
BF16 = jnp.bfloat16

GRID_W = 64
HEAD_DIM = 128
LANES = 128
HEADS = 8
KV_HEADS = 2
GROUP = HEADS // KV_HEADS
WIDTH = HEADS * HEAD_DIM
KVW = KV_HEADS * HEAD_DIM
WINDOW = 128
ROPE_BASE = 10000.0
ATTN_SCALE = HEAD_DIM ** -0.5
LOG2E = 1.4426950408889634
N_GROUPS = 8
EXPERTS_PER_GROUP = 8
N_EXPERTS = N_GROUPS * EXPERTS_PER_GROUP
TOP_K = 2
EPS = 1e-6
NEG_INF = -1e30

ROW_TILE = 256
PROJ_ROWS = 768
GATE_ROWS = 1152
GATE_COLS = 1024
MOE_ROWS = 128
GATHER_AHEAD = 2
WEIGHT_DMA_PRIORITY = 1
ROUTER_LANES = 128
DENSE_CHAIN_HEADS = 4
WINDOW_CHAIN_HEADS = 1
MOD_ROWS = 8
VMEM_LIMIT = 56 * 1024 * 1024


def _params(*sem):
    return pltpu.CompilerParams(dimension_semantics=sem, vmem_limit_bytes=VMEM_LIMIT)


def _rms(y):
    return y * lax.rsqrt(jnp.mean(y * y, axis=-1, keepdims=True) + EPS)


def _mod_kernel(c_ref, w_ref, b_ref, o_ref):
    c = c_ref[...]
    a = (c * jax.nn.sigmoid(c)).astype(BF16)
    o_ref[...] = jnp.dot(a, w_ref[...].astype(BF16), preferred_element_type=F32) + b_ref[...]


def _mod_vectors(cc, w_mod, b_mod):
    L, D, N = w_mod.shape
    tn = 1024
    return pl.pallas_call(
        _mod_kernel,
        grid=(L, N // tn),
        in_specs=[
            pl.BlockSpec((MOD_ROWS, D), lambda l, j: (0, 0)),
            pl.BlockSpec((None, D, tn), lambda l, j: (l, 0, j)),
            pl.BlockSpec((None, 1, tn), lambda l, j: (l, 0, j)),
        ],
        out_specs=pl.BlockSpec((None, MOD_ROWS, tn), lambda l, j: (l, 0, j)),
        out_shape=jax.ShapeDtypeStruct((L, MOD_ROWS, N), F32),
        compiler_params=_params("parallel", "parallel"),
        name="mod_vectors",
    )(cc, w_mod, b_mod.reshape(L, 1, N))


class _Rows:
    def __init__(self, batch, tiles_per_batch, latent_only):
        self.batch = batch
        self.tpb = tiles_per_batch
        self.latent_only = latent_only
        self.n = batch * (tiles_per_batch - 1 if latent_only else tiles_per_batch)

    def tile(self, i):
        if self.latent_only:
            per = self.tpb - 1
            return (i // per) * self.tpb + 1 + i % per
        return i

    def mod_row(self, i):
        if self.latent_only:
            return i // (self.tpb - 1)
        return jnp.where(i % self.tpb == 0, self.batch, i // self.tpb)

    def row_spec(self, width):
        return pl.BlockSpec((ROW_TILE, width), lambda i: (self.tile(i), 0))

    def mod_spec(self, d, k):
        return pl.BlockSpec((None, 1, d), lambda i: (self.mod_row(i) * 6 + k, 0, 0))


def _const_spec(shape):
    zeros = (0,) * len(shape)
    return pl.BlockSpec(shape, lambda *_: zeros, pipeline_mode=pl.Buffered(1))


def _norm_mod_kernel(x_ref, ctx_ref, nw_ref, sh_ref, sc_ref, xo_ref, h_ref):
    def emit(src_ref):
        x = src_ref[...]
        xo_ref[...] = x
        h_ref[...] = ((_rms(x) * nw_ref[...]) * (1.0 + sc_ref[...]) + sh_ref[...]).astype(BF16)

    is_ctx = pl.program_id(1) == 0
    pl.when(is_ctx)(lambda: emit(ctx_ref))
    pl.when(jnp.logical_not(is_ctx))(lambda: emit(x_ref))


def _norm_mod(x, ctx, nw, mod_l):
    B, S, D = x.shape
    tiles = 1 + S // ROW_TILE
    mod_spec = lambda k: pl.BlockSpec((None, 1, D), lambda b, t: (jnp.where(t == 0, B, b) * 6 + k, 0, 0))
    out_spec = pl.BlockSpec((ROW_TILE, D), lambda b, t: (b * tiles + t, 0))
    return pl.pallas_call(
        _norm_mod_kernel,
        grid=(B, tiles),
        in_specs=[pl.BlockSpec((None, ROW_TILE, D), lambda b, t: (b, jnp.maximum(t - 1, 0), 0)),
                  pl.BlockSpec((None, ROW_TILE, D), lambda b, t: (b, 0, 0)),
                  _const_spec((1, D)), mod_spec(0), mod_spec(1)],
        out_specs=[out_spec, out_spec],
        out_shape=[jax.ShapeDtypeStruct((B * tiles * ROW_TILE, D), F32),
                   jax.ShapeDtypeStruct((B * tiles * ROW_TILE, D), BF16)],
        compiler_params=_params("parallel", "parallel"),
        name="norm_mod",
    )(x, ctx, nw, mod_l, mod_l)


ROTARY_KINDS = ("q_norm", "k_norm", "q_rope", "k_rope")


def _pair_halves(a):
    q = HEAD_DIM // 4
    return jnp.concatenate([a[..., 0:q], a[..., 2 * q:3 * q], a[..., q:2 * q], a[..., 3 * q:]], axis=-1)


def _rope(y, cos, sin):
    return y * cos + pltpu.roll(y, HEAD_DIM // 2, 1) * sin


def _proj_kernel(h_ref, w_ref, cos_ref, sin_ref, gain_ref, *refs, kinds):
    *out_refs, wb_ref = refs

    @pl.when(pl.program_id(1) == 0)
    def _():
        for c, (kind, _, _) in enumerate(kinds):
            w = w_ref[:, c * HEAD_DIM:(c + 1) * HEAD_DIM]
            wb_ref[:, c * HEAD_DIM:(c + 1) * HEAD_DIM] = (_pair_halves(w) if kind in ROTARY_KINDS else w).astype(BF16)

    acc = jnp.dot(h_ref[...], wb_ref[...], preferred_element_type=F32)
    for c, (kind, out_idx, out_chunk) in enumerate(kinds):
        y = acc[:, c * HEAD_DIM:(c + 1) * HEAD_DIM]
        if kind == "q_norm":
            y = _rms(y) * gain_ref[0:1, :]
        if kind == "k_norm":
            y = _rms(y) * gain_ref[1:2, :]
        if kind in ROTARY_KINDS:
            y = _rope(y, cos_ref[...], sin_ref[...])
        if kind in ("q_norm", "q_rope"):
            y = y * (ATTN_SCALE * LOG2E)
        out_refs[out_idx][:, out_chunk * HEAD_DIM:(out_chunk + 1) * HEAD_DIM] = y.astype(BF16)


def _proj(h, w_in, layer, cos, sin, gain, col0, kinds, out_widths, tiles_per_batch, name, rows=PROJ_ROWS):
    M, D = h.shape
    n_col_tiles = out_widths[0][1]
    cols = len(kinds) * HEAD_DIM
    cb0 = col0 // cols
    proj_tiles_per_batch = tiles_per_batch * ROW_TILE // rows
    return pl.pallas_call(
        functools.partial(_proj_kernel, kinds=kinds),
        grid=(n_col_tiles, M // rows),
        in_specs=[
            pl.BlockSpec((rows, D), lambda j, i: (i, 0)),
            pl.BlockSpec((None, D, cols), lambda j, i: (layer, 0, cb0 + j),
                         pipeline_mode=pl.Buffered(1 if n_col_tiles == 1 else 2)),
            pl.BlockSpec((rows, HEAD_DIM), lambda j, i: (i % proj_tiles_per_batch, 0)),
            pl.BlockSpec((rows, HEAD_DIM), lambda j, i: (i % proj_tiles_per_batch, 0)),
            pl.BlockSpec((2, HEAD_DIM), lambda j, i: (0, 0)),
        ],
        out_specs=[pl.BlockSpec((rows, w), lambda j, i: (i, j)) for w, _ in out_widths],
        out_shape=[jax.ShapeDtypeStruct((M, w * n), BF16) for w, n in out_widths],
        scratch_shapes=[pltpu.VMEM((D, cols), BF16)],
        compiler_params=_params("parallel", "arbitrary"),
        name=name,
    )(h, w_in, cos, sin, gain)


def _stack_heads(q_ref, g0, n):
    return jnp.concatenate([q_ref[:, g * HEAD_DIM:(g + 1) * HEAD_DIM] for g in range(g0, g0 + n)], axis=0)


def _write_heads(o_ref, g0, n, o):
    rows = o_ref.shape[0]
    for j in range(n):
        o_ref[:, (g0 + j) * HEAD_DIM:(g0 + j + 1) * HEAD_DIM] = o[j * rows:(j + 1) * rows].astype(o_ref.dtype)


def _scores(q, k):
    return lax.dot_general(q, k, (((1,), (1,)), ((), ())), preferred_element_type=F32)


def _online_step(q, k, v, m, l, acc):
    s = _scores(q, k)
    m_new = jnp.maximum(m, jnp.max(s, axis=-1, keepdims=True))
    alpha = jnp.exp2(m - m_new)
    p = jnp.exp2(s - m_new)
    l = alpha * l + jnp.sum(p, axis=-1, keepdims=True)
    acc = alpha * acc + jnp.dot(p.astype(BF16), v, preferred_element_type=F32)
    return m_new, l, acc


def _dense_attn_kernel(q_ref, k_ref, v_ref, o_ref, *, ctx_len, total_len, key_chunk, ctx_tile):
    def run(n_keys, chunk):
        for g0 in range(0, GROUP, DENSE_CHAIN_HEADS):
            q = _stack_heads(q_ref, g0, DENSE_CHAIN_HEADS)
            r = q.shape[0]
            m = jnp.full((r, 1), -jnp.inf, F32)
            l = jnp.zeros((r, 1), F32)
            acc = jnp.zeros((r, HEAD_DIM), F32)
            for c in range(n_keys // chunk):
                m, l, acc = _online_step(q, k_ref[c * chunk:(c + 1) * chunk, :],
                                         v_ref[c * chunk:(c + 1) * chunk, :], m, l, acc)
            _write_heads(o_ref, g0, DENSE_CHAIN_HEADS, acc * (1.0 / l))

    if ctx_tile:
        is_ctx = pl.program_id(2) == 0
        pl.when(is_ctx)(lambda: run(ctx_len, ctx_len))
        pl.when(jnp.logical_not(is_ctx))(lambda: run(total_len, key_chunk))
    else:
        run(total_len, key_chunk)


def _window_attn_kernel(sink_ref, q_ref, k_ref, v_ref, o_ref, *, ctx_len, total_len, ctx_tile, q_off):
    kvh = pl.program_id(1)
    tq = q_ref.shape[0]
    span = tq + 2 * WINDOW
    n = WINDOW_CHAIN_HEADS

    def attend(parts):
        for g0 in range(0, GROUP, n):
            q = _stack_heads(q_ref, g0, n)
            sink = jnp.concatenate(
                [jnp.full((tq, 1), sink_ref[kvh * GROUP + g] * LOG2E, F32) for g in range(g0, g0 + n)], axis=0)
            scores = []
            m = sink
            for k, _, bias in parts:
                s = _scores(q, k)
                if bias is not None:
                    s = s + bias
                scores.append(s)
                m = jnp.maximum(m, jnp.max(s, axis=-1, keepdims=True))
            l = jnp.exp2(sink - m)
            o = jnp.zeros((n * tq, HEAD_DIM), F32)
            for s, (_, v, _) in zip(scores, parts):
                p = jnp.exp2(s - m)
                l = l + jnp.sum(p, axis=-1, keepdims=True)
                o = o + jnp.dot(p.astype(BF16), v, preferred_element_type=F32)
            _write_heads(o_ref, g0, n, o * (1.0 / l))

    def ctx_part():
        return k_ref[0:ctx_len, :], v_ref[0:ctx_len, :], None

    def run_latent():
        q0 = (pl.program_id(2) + q_off) * tq - ctx_len
        n_latent = total_len - ctx_len
        start = pl.multiple_of(jnp.clip(q0 - WINDOW, 0, n_latent - span), WINDOW)
        row = lax.broadcasted_iota(jnp.int32, (n * tq, span), 0) & (tq - 1)
        col = lax.broadcasted_iota(jnp.int32, (n * tq, span), 1)
        bias = jnp.where(jnp.abs(col - row + (start - q0)) <= WINDOW, 0.0, NEG_INF).astype(F32)
        attend([ctx_part(), (k_ref[pl.ds(ctx_len + start, span), :], v_ref[pl.ds(ctx_len + start, span), :], bias)])

    if ctx_tile:
        is_ctx = pl.program_id(2) == 0
        pl.when(is_ctx)(lambda: attend([ctx_part()]))
        pl.when(jnp.logical_not(is_ctx))(run_latent)
    else:
        run_latent()


def _attention(q, k, v, sink, batch, total_len, ctx_len, latent_only, window):
    M = q.shape[0]
    q3, k3, v3 = (a.reshape(batch, total_len, a.shape[1]) for a in (q, k, v))
    tiles = total_len // ROW_TILE
    q_off = 1 if latent_only else 0
    gw = GROUP * HEAD_DIM
    q_spec = pl.BlockSpec((None, ROW_TILE, gw), lambda b, h, i, *_: (b, i + q_off, h))
    kv_spec = pl.BlockSpec((None, total_len, HEAD_DIM), lambda b, h, i, *_: (b, 0, h))
    grid = (batch, KV_HEADS, tiles - q_off)
    out_shape = jax.ShapeDtypeStruct(q3.shape, BF16)
    if window:
        kern = functools.partial(_window_attn_kernel, ctx_len=ctx_len, total_len=total_len,
                                 ctx_tile=not latent_only, q_off=q_off)
        out = pl.pallas_call(
            kern,
            grid_spec=pltpu.PrefetchScalarGridSpec(
                num_scalar_prefetch=1, grid=grid,
                in_specs=[q_spec, kv_spec, kv_spec], out_specs=q_spec),
            out_shape=out_shape,
            compiler_params=_params("parallel", "parallel", "parallel"),
            name="window_attention",
        )(sink, q3, k3, v3)
    else:
        kern = functools.partial(_dense_attn_kernel, ctx_len=ctx_len, total_len=total_len,
                                 key_chunk=total_len // 3, ctx_tile=not latent_only)
        out = pl.pallas_call(
            kern, grid=grid, in_specs=[q_spec, kv_spec, kv_spec], out_specs=q_spec,
            out_shape=out_shape,
            compiler_params=_params("parallel", "parallel", "parallel"),
            name="dense_attention",
        )(q3, k3, v3)
    return out.reshape(M, q.shape[1])


def _merge_kernel(ya_ref, yb_ref, ga_ref, gb_ref, wa_ref, wb_ref, wo_ref, x_ref, g1_ref,
                  nw_ref, sh_ref, sc_ref, wr_ref, br_ref, xo_ref, h_ref, lg_ref):
    a = jnp.dot(ya_ref[...], wa_ref[...], preferred_element_type=F32)
    b = jnp.dot(yb_ref[...], wb_ref[...], preferred_element_type=F32)
    m = jax.nn.sigmoid(ga_ref[...].astype(F32)) * a + jax.nn.sigmoid(gb_ref[...].astype(F32)) * b
    o = jnp.dot(m.astype(BF16), wo_ref[...], preferred_element_type=F32)
    xn = x_ref[...] + g1_ref[...] * o
    xo_ref[...] = xn
    h = (_rms(xn) * nw_ref[...]) * (1.0 + sc_ref[...]) + sh_ref[...]
    h_hi = h.astype(BF16)
    bits = pltpu.bitcast(h_hi.astype(F32), jnp.uint32)
    half = h.shape[1] // 2
    words = (bits[:, :half] >> 16) | (bits[:, half:] & jnp.uint32(0xFFFF0000))
    chunks = half // LANES
    for c in range(chunks):
        h_ref[pl.ds(c, h.shape[0], stride=chunks), :] = words[:, c * LANES:(c + 1) * LANES]
    h_lo = (h - h_hi.astype(F32)).astype(BF16)
    lg = jnp.dot(h_hi, wr_ref[...], preferred_element_type=F32)
    lg_lo = jnp.dot(h_lo, wr_ref[:, :ROUTER_LANES], preferred_element_type=F32)
    lg_ref[...] = lg[:, :ROUTER_LANES] + (lg[:, ROUTER_LANES:] + lg_lo) + br_ref[...]


def _merge(ya, yb, gates, wa, wb, wo, x_all, mod_l, nw, wr, br, rows):
    M, D = x_all.shape
    W = ya.shape[1]
    chunks = D // 2 // LANES
    return pl.pallas_call(
        _merge_kernel,
        grid=(rows.n,),
        in_specs=[
            rows.row_spec(W), rows.row_spec(W),
            pl.BlockSpec((ROW_TILE, D), lambda i: (rows.tile(i), 0)),
            pl.BlockSpec((ROW_TILE, D), lambda i: (rows.tile(i), 1)),
            _const_spec((W, D)), _const_spec((W, D)), _const_spec((D, D)),
            rows.row_spec(D), rows.mod_spec(D, 2),
            _const_spec((1, D)), rows.mod_spec(D, 3), rows.mod_spec(D, 4),
            _const_spec((D, 2 * ROUTER_LANES)), _const_spec((1, ROUTER_LANES)),
        ],
        out_specs=[rows.row_spec(D),
                   pl.BlockSpec((ROW_TILE * chunks, LANES), lambda i: (rows.tile(i), 0)),
                   rows.row_spec(ROUTER_LANES)],
        out_shape=[jax.ShapeDtypeStruct((M, D), F32), jax.ShapeDtypeStruct((M * chunks, LANES), jnp.uint32),
                   jax.ShapeDtypeStruct((M, ROUTER_LANES), F32)],
        compiler_params=_params("parallel"),
        name="merge",
    )(ya, yb, gates, gates, wa, wb, wo, x_all, mod_l, nw, mod_l, mod_l, wr, br)


def _route(logits):
    n = logits.shape[0]
    gl = logits[:, :N_GROUPS]
    el = logits[:, N_GROUPS:N_GROUPS + N_EXPERTS].reshape(n, N_GROUPS, EXPERTS_PER_GROUP)
    g = jnp.argmax(gl, axis=-1).astype(jnp.int32)
    pg = 1.0 / jnp.sum(jnp.exp(gl - jnp.max(gl, axis=-1, keepdims=True)), axis=-1, keepdims=True)
    in_group = jnp.arange(N_GROUPS, dtype=jnp.int32)[None, :, None] == g[:, None, None]
    el_g = jnp.sum(jnp.where(in_group, el, 0.0), axis=1)
    tv, ti = lax.top_k(el_g, TOP_K)
    wts = jax.nn.softmax(tv, axis=-1) * pg
    eid = g[:, None] * EXPERTS_PER_GROUP + ti.astype(jnp.int32)
    return eid, wts


def _dispatch(eid, tok_rows, row_scale):
    n = eid.shape[0]
    a = n * TOP_K
    e_flat = eid.reshape(a)
    onehot = (e_flat[:, None] == jnp.arange(N_EXPERTS, dtype=jnp.int32)[None, :]).astype(jnp.int32)
    csum = jnp.cumsum(onehot, axis=0)
    counts = csum[-1]
    padded = (counts + MOE_ROWS - 1) // MOE_ROWS * MOE_ROWS
    pad_end = jnp.cumsum(padded)
    pad_start = pad_end - padded
    dest = jnp.sum(onehot * (csum - 1 + pad_start[None, :]), axis=1)
    n_grid = a // MOE_ROWS + N_EXPERTS - 1 + GATHER_AHEAD
    _, sorted_row = lax.sort((e_flat, jnp.repeat(tok_rows * row_scale, TOP_K)), num_keys=1, is_stable=True)
    sorted_row = jnp.concatenate([sorted_row, jnp.zeros((MOE_ROWS,), jnp.int32)])
    blk = jnp.arange(n_grid, dtype=jnp.int32)
    n_valid = (pad_end[-1] // MOE_ROWS).astype(jnp.int32)
    blk_c = jnp.minimum(blk, n_valid - 1)
    blk_e = jnp.sum((pad_end[None, :] <= (blk_c * MOE_ROWS)[:, None]).astype(jnp.int32), axis=1)
    blk_e = jnp.minimum(blk_e, N_EXPERTS - 1)
    start = jnp.cumsum(counts) - counts
    blk_first = jnp.where(blk < n_valid, start[blk_e] + blk * MOE_ROWS - pad_start[blk_e], 0).astype(jnp.int32)
    experts = jnp.arange(N_EXPERTS, dtype=jnp.int32)
    used = counts > 0
    first_used_from = lax.cummin(jnp.where(used, experts, N_EXPERTS), reverse=True)
    next_used = jnp.concatenate([first_used_from[1:], jnp.full((1,), N_EXPERTS, jnp.int32)])
    next_used = jnp.where(next_used == N_EXPERTS, -1, next_used)
    parity = (jnp.cumsum(used.astype(jnp.int32)) - 1) % 2
    return (dest, blk_e, next_used[blk_e].astype(jnp.int32), parity[blk_e].astype(jnp.int32), blk_first,
            sorted_row, n_valid.reshape(1))


def _experts_kernel(blk_e, blk_next, blk_par, blk_first, sorted_row, n_valid,
                    h_hbm, w1_hbm, w3_hbm, w2_hbm, o_ref,
                    xg, xsem, ws1, ws3, ws2, wsem, w1b, w3b, w2b, *, layer):
    i = pl.program_id(0)
    nv = n_valid[0]
    chunks = xg.shape[1] // MOE_ROWS
    e = blk_e[i]
    par = blk_par[i]
    fresh = jnp.logical_or(i == 0, e != blk_e[jnp.maximum(i - 1, 0)])

    def weight_copies(expert, p):
        return (pltpu.make_async_copy(w1_hbm.at[layer, expert], ws1.at[p], wsem.at[p, 0]),
                pltpu.make_async_copy(w3_hbm.at[layer, expert], ws3.at[p], wsem.at[p, 1]),
                pltpu.make_async_copy(w2_hbm.at[layer, expert], ws2.at[p], wsem.at[p, 2]))

    def for_rows(blk, fn):
        buf = lax.rem(blk, GATHER_AHEAD + 1)
        base = blk_first[blk]
        for j in range(MOE_ROWS):
            r = pl.multiple_of(sorted_row[base + j], chunks)
            fn(pltpu.make_async_copy(h_hbm.at[pl.ds(r, chunks), :], xg.at[buf, pl.ds(j * chunks, chunks), :],
                                     xsem.at[buf]))

    @pl.when(i == 0)
    def _():
        for cp in weight_copies(e, par):
            cp.start(priority=WEIGHT_DMA_PRIORITY)
        for blk in range(GATHER_AHEAD):
            for_rows(blk, lambda cp: cp.start())

    @pl.when(jnp.logical_and(i >= nv, i < nv + GATHER_AHEAD))
    def _():
        for_rows(i, lambda cp: cp.wait())

    @pl.when(fresh)
    def _():
        for cp in weight_copies(e, par):
            cp.wait()
        nxt = blk_next[i]

        @pl.when(nxt >= 0)
        def _():
            for cp in weight_copies(nxt, 1 - par):
                cp.start(priority=WEIGHT_DMA_PRIORITY)

        w1b[...] = ws1[par].astype(BF16)
        w3b[...] = ws3[par].astype(BF16)
        w2b[...] = ws2[par].astype(BF16)

    @pl.when(i < nv)
    def _():
        for_rows(i, lambda cp: cp.wait())
        words = jnp.concatenate(
            [xg[lax.rem(i, GATHER_AHEAD + 1), pl.ds(c, MOE_ROWS, stride=chunks), :] for c in range(chunks)],
            axis=1)
        x = jnp.concatenate([pltpu.bitcast(words << 16, F32).astype(BF16),
                             pltpu.bitcast(words & jnp.uint32(0xFFFF0000), F32).astype(BF16)], axis=1)
        for_rows(i + GATHER_AHEAD, lambda cp: cp.start())
        a = jnp.dot(x, w1b[...], preferred_element_type=F32)
        b = jnp.dot(x, w3b[...], preferred_element_type=F32)
        hid = (a * jax.nn.sigmoid(a)) * b
        o_ref[...] = jnp.dot(hid.astype(BF16), w2b[...], preferred_element_type=F32).astype(o_ref.dtype)


def _experts(h2, w1, w3, w2, layer, blk_e, blk_next, blk_par, blk_first, sorted_row, n_valid):
    D, de = w1.shape[-2:]
    chunks = D // 2 // LANES
    n_blk = blk_e.shape[0]
    any_spec = pl.BlockSpec(memory_space=pl.ANY)
    return pl.pallas_call(
        functools.partial(_experts_kernel, layer=layer),
        grid_spec=pltpu.PrefetchScalarGridSpec(
            num_scalar_prefetch=6, grid=(n_blk,),
            in_specs=[any_spec, any_spec, any_spec, any_spec],
            out_specs=pl.BlockSpec((MOE_ROWS, D), lambda i, *s: (jnp.minimum(i, s[5][0] - 1), 0)),
            scratch_shapes=[
                pltpu.VMEM((GATHER_AHEAD + 1, MOE_ROWS * chunks, LANES), jnp.uint32),
                pltpu.SemaphoreType.DMA((GATHER_AHEAD + 1,)),
                pltpu.VMEM((2, D, de), F32), pltpu.VMEM((2, D, de), F32), pltpu.VMEM((2, de, D), F32),
                pltpu.SemaphoreType.DMA((2, 3)),
                pltpu.VMEM((D, de), BF16), pltpu.VMEM((D, de), BF16), pltpu.VMEM((de, D), BF16),
            ],
        ),
        out_shape=jax.ShapeDtypeStruct((n_blk * MOE_ROWS, D), BF16),
        compiler_params=_params("arbitrary"),
        name="experts",
    )(blk_e, blk_next, blk_par, blk_first, sorted_row, n_valid, h2, w1, w3, w2)


def _moe_mix(x_ref, y0_ref, y1_ref, w_ref, g2_ref):
    w = w_ref[...]
    y = w[:, 0:1] * y0_ref[...].astype(F32) + w[:, 1:2] * y1_ref[...].astype(F32)
    return x_ref[...] + g2_ref[...] * y


def _combine_next_kernel(x_ref, y0_ref, y1_ref, w_ref, g2_ref, nw_ref, sh_ref, sc_ref, xo_ref, h_ref):
    xn = _moe_mix(x_ref, y0_ref, y1_ref, w_ref, g2_ref)
    xo_ref[...] = xn
    h_ref[...] = ((_rms(xn) * nw_ref[...]) * (1.0 + sc_ref[...]) + sh_ref[...]).astype(BF16)


def _combine_final_kernel(x_ref, y0_ref, y1_ref, w_ref, g2_ref, nf_ref, o_ref):
    o_ref[...] = _rms(_moe_mix(x_ref, y0_ref, y1_ref, w_ref, g2_ref)) * nf_ref[...]


def _token_specs(n_tokens, d):
    tiles = n_tokens // ROW_TILE
    return [pl.BlockSpec((ROW_TILE, d), lambda i: (i, 0)), pl.BlockSpec((ROW_TILE, d), lambda i: (i + tiles, 0)),
            pl.BlockSpec((ROW_TILE, TOP_K), lambda i: (i, 0))]


def _combine_next(x_new, ysel, wts, mod_l, nw_next, mod_next, rows):
    M, D = x_new.shape
    return pl.pallas_call(
        _combine_next_kernel,
        grid=(rows.n,),
        in_specs=[rows.row_spec(D), *_token_specs(M, D), rows.mod_spec(D, 5),
                  _const_spec((1, D)), rows.mod_spec(D, 0), rows.mod_spec(D, 1)],
        out_specs=[rows.row_spec(D), rows.row_spec(D)],
        out_shape=[jax.ShapeDtypeStruct((M, D), F32), jax.ShapeDtypeStruct((M, D), BF16)],
        compiler_params=_params("parallel"),
        name="combine_next",
    )(x_new, ysel, ysel, wts, mod_l, nw_next, mod_next, mod_next)


def _combine_final(x_new, ysel, wts, mod_l, nf, rows):
    M, D = x_new.shape
    n = wts.shape[0]
    return pl.pallas_call(
        _combine_final_kernel,
        grid=(rows.n,),
        in_specs=[rows.row_spec(D), *_token_specs(n, D), rows.mod_spec(D, 5), _const_spec((1, D))],
        out_specs=pl.BlockSpec((ROW_TILE, D), lambda i: (i, 0)),
        out_shape=jax.ShapeDtypeStruct((n, D), F32),
        compiler_params=_params("parallel"),
        name="combine_final",
    )(x_new, ysel, ysel, wts, mod_l, nf)


def _rope_tables(ctx_len, seq):
    quarter = HEAD_DIM // 4
    freqs = ROPE_BASE ** (-jnp.arange(quarter, dtype=F32) / quarter)
    pos = jnp.arange(seq, dtype=jnp.int32)
    row = (pos // GRID_W).astype(F32)[:, None] * freqs[None, :]
    col = (pos % GRID_W).astype(F32)[:, None] * freqs[None, :]
    cos = jnp.concatenate([jnp.cos(row), jnp.cos(col), jnp.cos(row), jnp.cos(col)], axis=-1)
    sin = jnp.concatenate([-jnp.sin(row), -jnp.sin(col), jnp.sin(row), jnp.sin(col)], axis=-1)
    cos = jnp.concatenate([jnp.ones((ctx_len, HEAD_DIM), F32), cos], axis=0)
    sin = jnp.concatenate([jnp.zeros((ctx_len, HEAD_DIM), F32), sin], axis=0)
    return cos, sin


def kernel(x, c, ctx, c_ctx, w_mod, b_mod, norm_mix, norm_ffn, w_in, qn_a, kn_a, sink_b, w_br_a, w_br_b, w_out, w_rg, b_rg, w_re, b_re, w1, w3, w2, norm_final):
    B, S, D = x.shape
    C = ctx.shape[1]
    L = w_mod.shape[0]
    T = C + S
    M = B * T
    assert C == ROW_TILE and S % ROW_TILE == 0 and T % PROJ_ROWS == 0 and B < MOD_ROWS
    tiles_per_batch = T // ROW_TILE

    cc = jnp.concatenate([c, c_ctx[None, :], jnp.zeros((MOD_ROWS - B - 1, D), F32)], axis=0)
    mod = _mod_vectors(cc, w_mod, b_mod).reshape(L, MOD_ROWS * 6, 1, D)
    cos, sin = _rope_tables(C, S)
    latent_rows = (jnp.arange(B, dtype=jnp.int32)[:, None] * T + C
                   + jnp.arange(S, dtype=jnp.int32)[None, :]).reshape(B * S)
    all_rows = jnp.arange(M, dtype=jnp.int32)

    mixer_kinds = lambda q_kind, k_kind: (tuple((q_kind, 0, k) for k in range(HEADS))
                                          + tuple((k_kind, 1, k) for k in range(KV_HEADS))
                                          + tuple(("v", 2, k) for k in range(KV_HEADS)))
    mixer_widths = [(WIDTH, 1), (KVW, 1), (KVW, 1)]
    gate_kinds = tuple(("gate", 0, k) for k in range(GATE_COLS // HEAD_DIM))
    col = {"a": 0, "b": WIDTH + 2 * KVW, "gates": 2 * WIDTH + 4 * KVW}

    every = _Rows(B, tiles_per_batch, latent_only=False)
    latent = _Rows(B, tiles_per_batch, latent_only=True)
    x_all, h = _norm_mod(x, ctx, norm_mix[0][None, :], mod[0])
    out = None
    for l in range(L):
        last = l == L - 1
        rows = latent if last else every
        proj = functools.partial(_proj, h, w_in, l, cos, sin, tiles_per_batch=tiles_per_batch)
        gains = jnp.stack([_pair_halves(qn_a[l]), _pair_halves(kn_a[l])])
        qa, ka, va = proj(gains, col["a"], mixer_kinds("q_norm", "k_norm"), mixer_widths, name="proj_a")
        qb, kb, vb = proj(gains, col["b"], mixer_kinds("q_rope", "k_rope"), mixer_widths, name="proj_b")
        (gates,) = proj(gains, col["gates"], gate_kinds, [(GATE_COLS, 2 * D // GATE_COLS)], name="proj_gates",
                        rows=GATE_ROWS)

        ya = _attention(qa, ka, va, None, B, T, C, last, window=False)
        yb = _attention(qb, kb, vb, sink_b[l], B, T, C, last, window=True)

        w_router = jnp.concatenate(
            [w_rg[l], w_re[l], jnp.zeros((D, ROUTER_LANES - N_GROUPS - N_EXPERTS), F32)], axis=1)
        w_router_hi = w_router.astype(BF16)
        w_router = jnp.concatenate([w_router_hi, (w_router - w_router_hi.astype(F32)).astype(BF16)], axis=1)
        b_router = jnp.concatenate(
            [b_rg[l], b_re[l], jnp.zeros((ROUTER_LANES - N_GROUPS - N_EXPERTS,), F32)])[None, :]
        x_new, h2, logits = _merge(ya, yb, gates, w_br_a[l].astype(BF16), w_br_b[l].astype(BF16),
                                   w_out[l].astype(BF16), x_all, mod[l], norm_ffn[l][None, :],
                                   w_router, b_router, rows)

        tok_rows = latent_rows if last else all_rows
        eid, wts = _route(logits[tok_rows] if last else logits)
        dest, *plan = _dispatch(eid, tok_rows, D // 2 // LANES)
        ybuf = _experts(h2, w1, w3, w2, l, *plan)
        ysel = ybuf[dest.reshape(-1, TOP_K).T.reshape(-1)]
        if last:
            out = _combine_final(x_new, ysel, wts, mod[l], norm_final[None, :], rows).reshape(B, S, D)
        else:
            x_all, h = _combine_next(x_new, ysel, wts, mod[l], norm_mix[l + 1][None, :], mod[l + 1], rows)
    return out
```

```python
import functools

import jax
import jax.numpy as jnp
from jax import lax
from jax.experimental import pallas as pl
from jax.experimental.pallas import tpu as pltpu

F32 = jnp.float32
BF16 = jnp.bfloat16

GRID_W = 64
HEAD_DIM = 128
LANES = 128
HEADS = 8
KV_HEADS = 2
GROUP = HEADS // KV_HEADS
WIDTH = HEADS * HEAD_DIM
KVW = KV_HEADS * HEAD_DIM
WINDOW = 128
ROPE_BASE = 10000.0
ATTN_SCALE = HEAD_DIM ** -0.5
LOG2E = 1.4426950408889634
N_GROUPS = 8
EXPERTS_PER_GROUP = 8
N_EXPERTS = N_GROUPS * EXPERTS_PER_GROUP
TOP_K = 2
EPS = 1e-6
NEG_INF = -1e30

ROW_TILE = 256
PROJ_ROWS = 768
GATE_ROWS = 1152
GATE_COLS = 1024
MOE_ROWS = 128
GATHER_AHEAD = 2
WEIGHT_DMA_PRIORITIES = (1, 1, 0)
ROUTER_LANES = 128
DENSE_CHAIN_HEADS = 4
WINDOW_CHAIN_HEADS = 1
MOD_ROWS = 8
VMEM_LIMIT = 56 * 1024 * 1024


def _params(*sem):
    return pltpu.CompilerParams(dimension_semantics=sem, vmem_limit_bytes=VMEM_LIMIT)


def _rms(y):
    return y * lax.rsqrt(jnp.mean(y * y, axis=-1, keepdims=True) + EPS)


def _mod_kernel(c_ref, w_ref, b_ref, o_ref):
    c = c_ref[...]
    a = (c * jax.nn.sigmoid(c)).astype(BF16)
    o_ref[...] = jnp.dot(a, w_ref[...].astype(BF16), preferred_element_type=F32) + b_ref[...]


def _mod_vectors(cc, w_mod, b_mod):
    L, D, N = w_mod.shape
    tn = 1024
    return pl.pallas_call(
        _mod_kernel,
        grid=(L, N // tn),
        in_specs=[
            pl.BlockSpec((MOD_ROWS, D), lambda l, j: (0, 0)),
            pl.BlockSpec((None, D, tn), lambda l, j: (l, 0, j)),
            pl.BlockSpec((None, 1, tn), lambda l, j: (l, 0, j)),
        ],
        out_specs=pl.BlockSpec((None, MOD_ROWS, tn), lambda l, j: (l, 0, j)),
        out_shape=jax.ShapeDtypeStruct((L, MOD_ROWS, N), F32),
        compiler_params=_params("parallel", "parallel"),
        name="mod_vectors",
    )(cc, w_mod, b_mod.reshape(L, 1, N))


class _Rows:
    def __init__(self, batch, tiles_per_batch, latent_only):
        self.batch = batch
        self.tpb = tiles_per_batch
        self.latent_only = latent_only
        self.n = batch * (tiles_per_batch - 1 if latent_only else tiles_per_batch)

    def tile(self, i):
        if self.latent_only:
            per = self.tpb - 1
            return (i // per) * self.tpb + 1 + i % per
        return i

    def mod_row(self, i):
        if self.latent_only:
            return i // (self.tpb - 1)
        return jnp.where(i % self.tpb == 0, self.batch, i // self.tpb)

    def row_spec(self, width):
        return pl.BlockSpec((ROW_TILE, width), lambda i: (self.tile(i), 0))

    def mod_spec(self, d, k):
        return pl.BlockSpec((None, 1, d), lambda i: (self.mod_row(i) * 6 + k, 0, 0))


def _const_spec(shape):
    zeros = (0,) * len(shape)
    return pl.BlockSpec(shape, lambda *_: zeros, pipeline_mode=pl.Buffered(1))


def _norm_mod_kernel(x_ref, ctx_ref, nw_ref, sh_ref, sc_ref, xo_ref, h_ref):
    def emit(src_ref):
        x = src_ref[...]
        xo_ref[...] = x
        h_ref[...] = ((_rms(x) * nw_ref[...]) * (1.0 + sc_ref[...]) + sh_ref[...]).astype(BF16)

    is_ctx = pl.program_id(1) == 0
    pl.when(is_ctx)(lambda: emit(ctx_ref))
    pl.when(jnp.logical_not(is_ctx))(lambda: emit(x_ref))


def _norm_mod(x, ctx, nw, mod_l):
    B, S, D = x.shape
    tiles = 1 + S // ROW_TILE
    mod_spec = lambda k: pl.BlockSpec((None, 1, D), lambda b, t: (jnp.where(t == 0, B, b) * 6 + k, 0, 0))
    out_spec = pl.BlockSpec((ROW_TILE, D), lambda b, t: (b * tiles + t, 0))
    return pl.pallas_call(
        _norm_mod_kernel,
        grid=(B, tiles),
        in_specs=[pl.BlockSpec((None, ROW_TILE, D), lambda b, t: (b, jnp.maximum(t - 1, 0), 0)),
                  pl.BlockSpec((None, ROW_TILE, D), lambda b, t: (b, 0, 0)),
                  _const_spec((1, D)), mod_spec(0), mod_spec(1)],
        out_specs=[out_spec, out_spec],
        out_shape=[jax.ShapeDtypeStruct((B * tiles * ROW_TILE, D), F32),
                   jax.ShapeDtypeStruct((B * tiles * ROW_TILE, D), BF16)],
        compiler_params=_params("parallel", "parallel"),
        name="norm_mod",
    )(x, ctx, nw, mod_l, mod_l)


ROTARY_KINDS = ("q_norm", "k_norm", "q_rope", "k_rope")


def _pair_halves(a):
    q = HEAD_DIM // 4
    return jnp.concatenate([a[..., 0:q], a[..., 2 * q:3 * q], a[..., q:2 * q], a[..., 3 * q:]], axis=-1)


def _rope(y, cos, sin):
    return y * cos + pltpu.roll(y, HEAD_DIM // 2, 1) * sin


def _proj_kernel(h_ref, w_ref, cos_ref, sin_ref, gain_ref, *refs, kinds):
    *out_refs, wb_ref = refs

    @pl.when(pl.program_id(1) == 0)
    def _():
        for c, (kind, _, _) in enumerate(kinds):
            w = w_ref[:, c * HEAD_DIM:(c + 1) * HEAD_DIM]
            wb_ref[:, c * HEAD_DIM:(c + 1) * HEAD_DIM] = (_pair_halves(w) if kind in ROTARY_KINDS else w).astype(BF16)

    acc = jnp.dot(h_ref[...], wb_ref[...], preferred_element_type=F32)
    for c, (kind, out_idx, out_chunk) in enumerate(kinds):
        y = acc[:, c * HEAD_DIM:(c + 1) * HEAD_DIM]
        if kind == "q_norm":
            y = _rms(y) * gain_ref[0:1, :]
        if kind == "k_norm":
            y = _rms(y) * gain_ref[1:2, :]
        if kind in ROTARY_KINDS:
            y = _rope(y, cos_ref[...], sin_ref[...])
        if kind in ("q_norm", "q_rope"):
            y = y * (ATTN_SCALE * LOG2E)
        out_refs[out_idx][:, out_chunk * HEAD_DIM:(out_chunk + 1) * HEAD_DIM] = y.astype(BF16)


def _proj(h, w_in, layer, cos, sin, gain, col0, kinds, out_widths, tiles_per_batch, name, rows=PROJ_ROWS):
    M, D = h.shape
    n_col_tiles = out_widths[0][1]
    cols = len(kinds) * HEAD_DIM
    cb0 = col0 // cols
    proj_tiles_per_batch = tiles_per_batch * ROW_TILE // rows
    return pl.pallas_call(
        functools.partial(_proj_kernel, kinds=kinds),
        grid=(n_col_tiles, M // rows),
        in_specs=[
            pl.BlockSpec((rows, D), lambda j, i: (i, 0)),
            pl.BlockSpec((None, D, cols), lambda j, i: (layer, 0, cb0 + j),
                         pipeline_mode=pl.Buffered(1 if n_col_tiles == 1 else 2)),
            pl.BlockSpec((rows, HEAD_DIM), lambda j, i: (i % proj_tiles_per_batch, 0)),
            pl.BlockSpec((rows, HEAD_DIM), lambda j, i: (i % proj_tiles_per_batch, 0)),
            pl.BlockSpec((2, HEAD_DIM), lambda j, i: (0, 0)),
        ],
        out_specs=[pl.BlockSpec((rows, w), lambda j, i: (i, j)) for w, _ in out_widths],
        out_shape=[jax.ShapeDtypeStruct((M, w * n), BF16) for w, n in out_widths],
        scratch_shapes=[pltpu.VMEM((D, cols), BF16)],
        compiler_params=_params("parallel", "arbitrary"),
        name=name,
    )(h, w_in, cos, sin, gain)


def _stack_heads(q_ref, g0, n):
    return jnp.concatenate([q_ref[:, g * HEAD_DIM:(g + 1) * HEAD_DIM] for g in range(g0, g0 + n)], axis=0)


def _write_heads(o_ref, g0, n, o):
    rows = o_ref.shape[0]
    for j in range(n):
        o_ref[:, (g0 + j) * HEAD_DIM:(g0 + j + 1) * HEAD_DIM] = o[j * rows:(j + 1) * rows].astype(o_ref.dtype)


def _scores(q, k):
    return lax.dot_general(q, k, (((1,), (1,)), ((), ())), preferred_element_type=F32)


def _online_step(q, k, v, m, l, acc):
    s = _scores(q, k)
    m_new = jnp.maximum(m, jnp.max(s, axis=-1, keepdims=True))
    alpha = jnp.exp2(m - m_new)
    p = jnp.exp2(s - m_new)
    l = alpha * l + jnp.sum(p, axis=-1, keepdims=True)
    acc = alpha * acc + jnp.dot(p.astype(BF16), v, preferred_element_type=F32)
    return m_new, l, acc


def _dense_attn_kernel(q_ref, k_ref, v_ref, o_ref, *, ctx_len, total_len, key_chunk, ctx_tile):
    def run(n_keys, chunk):
        for g0 in range(0, GROUP, DENSE_CHAIN_HEADS):
            q = _stack_heads(q_ref, g0, DENSE_CHAIN_HEADS)
            r = q.shape[0]
            m = jnp.full((r, 1), -jnp.inf, F32)
            l = jnp.zeros((r, 1), F32)
            acc = jnp.zeros((r, HEAD_DIM), F32)
            for c in range(n_keys // chunk):
                m, l, acc = _online_step(q, k_ref[c * chunk:(c + 1) * chunk, :],
                                         v_ref[c * chunk:(c + 1) * chunk, :], m, l, acc)
            _write_heads(o_ref, g0, DENSE_CHAIN_HEADS, acc * (1.0 / l))

    if ctx_tile:
        is_ctx = pl.program_id(2) == 0
        pl.when(is_ctx)(lambda: run(ctx_len, ctx_len))
        pl.when(jnp.logical_not(is_ctx))(lambda: run(total_len, key_chunk))
    else:
        run(total_len, key_chunk)


def _window_attn_kernel(sink_ref, q_ref, k_ref, v_ref, o_ref, *, ctx_len, total_len, ctx_tile, q_off):
    kvh = pl.program_id(1)
    tq = q_ref.shape[0]
    span = tq + 2 * WINDOW
    n = WINDOW_CHAIN_HEADS

    def attend(parts):
        for g0 in range(0, GROUP, n):
            q = _stack_heads(q_ref, g0, n)
            sink = jnp.concatenate(
                [jnp.full((tq, 1), sink_ref[kvh * GROUP + g] * LOG2E, F32) for g in range(g0, g0 + n)], axis=0)
            scores = []
            m = sink
            for k, _, bias in parts:
                s = _scores(q, k)
                if bias is not None:
                    s = s + bias
                scores.append(s)
                m = jnp.maximum(m, jnp.max(s, axis=-1, keepdims=True))
            l = jnp.exp2(sink - m)
            o = jnp.zeros((n * tq, HEAD_DIM), F32)
            for s, (_, v, _) in zip(scores, parts):
                p = jnp.exp2(s - m)
                l = l + jnp.sum(p, axis=-1, keepdims=True)
                o = o + jnp.dot(p.astype(BF16), v, preferred_element_type=F32)
            _write_heads(o_ref, g0, n, o * (1.0 / l))

    def ctx_part():
        return k_ref[0:ctx_len, :], v_ref[0:ctx_len, :], None

    def run_latent():
        q0 = (pl.program_id(2) + q_off) * tq - ctx_len
        n_latent = total_len - ctx_len
        start = pl.multiple_of(jnp.clip(q0 - WINDOW, 0, n_latent - span), WINDOW)
        row = lax.broadcasted_iota(jnp.int32, (n * tq, span), 0) & (tq - 1)
        col = lax.broadcasted_iota(jnp.int32, (n * tq, span), 1)
        bias = jnp.where(jnp.abs(col - row + (start - q0)) <= WINDOW, 0.0, NEG_INF).astype(F32)
        attend([ctx_part(), (k_ref[pl.ds(ctx_len + start, span), :], v_ref[pl.ds(ctx_len + start, span), :], bias)])

    if ctx_tile:
        is_ctx = pl.program_id(2) == 0
        pl.when(is_ctx)(lambda: attend([ctx_part()]))
        pl.when(jnp.logical_not(is_ctx))(run_latent)
    else:
        run_latent()


def _attention(q, k, v, sink, batch, total_len, ctx_len, latent_only, window):
    M = q.shape[0]
    q3, k3, v3 = (a.reshape(batch, total_len, a.shape[1]) for a in (q, k, v))
    tiles = total_len // ROW_TILE
    q_off = 1 if latent_only else 0
    gw = GROUP * HEAD_DIM
    q_spec = pl.BlockSpec((None, ROW_TILE, gw), lambda b, h, i, *_: (b, i + q_off, h))
    kv_spec = pl.BlockSpec((None, total_len, HEAD_DIM), lambda b, h, i, *_: (b, 0, h))
    grid = (batch, KV_HEADS, tiles - q_off)
    out_shape = jax.ShapeDtypeStruct(q3.shape, BF16)
    if window:
        kern = functools.partial(_window_attn_kernel, ctx_len=ctx_len, total_len=total_len,
                                 ctx_tile=not latent_only, q_off=q_off)
        out = pl.pallas_call(
            kern,
            grid_spec=pltpu.PrefetchScalarGridSpec(
                num_scalar_prefetch=1, grid=grid,
                in_specs=[q_spec, kv_spec, kv_spec], out_specs=q_spec),
            out_shape=out_shape,
            compiler_params=_params("parallel", "parallel", "parallel"),
            name="window_attention",
        )(sink, q3, k3, v3)
    else:
        kern = functools.partial(_dense_attn_kernel, ctx_len=ctx_len, total_len=total_len,
                                 key_chunk=total_len // 3, ctx_tile=not latent_only)
        out = pl.pallas_call(
            kern, grid=grid, in_specs=[q_spec, kv_spec, kv_spec], out_specs=q_spec,
            out_shape=out_shape,
            compiler_params=_params("parallel", "parallel", "parallel"),
            name="dense_attention",
        )(q3, k3, v3)
    return out.reshape(M, q.shape[1])


def _merge_kernel(ya_ref, yb_ref, ga_ref, gb_ref, wa_ref, wb_ref, wo_ref, x_ref, g1_ref,
                  nw_ref, sh_ref, sc_ref, wr_ref, br_ref, xo_ref, h_ref, lg_ref):
    a = jnp.dot(ya_ref[...], wa_ref[...], preferred_element_type=F32)
    b = jnp.dot(yb_ref[...], wb_ref[...], preferred_element_type=F32)
    m = jax.nn.sigmoid(ga_ref[...].astype(F32)) * a + jax.nn.sigmoid(gb_ref[...].astype(F32)) * b
    o = jnp.dot(m.astype(BF16), wo_ref[...], preferred_element_type=F32)
    xn = x_ref[...] + g1_ref[...] * o
    xo_ref[...] = xn
    h = (_rms(xn) * nw_ref[...]) * (1.0 + sc_ref[...]) + sh_ref[...]
    h_hi = h.astype(BF16)
    bits = pltpu.bitcast(h_hi.astype(F32), jnp.uint32)
    half = h.shape[1] // 2
    words = (bits[:, :half] >> 16) | (bits[:, half:] & jnp.uint32(0xFFFF0000))
    chunks = half // LANES
    for c in range(chunks):
        h_ref[pl.ds(c, h.shape[0], stride=chunks), :] = words[:, c * LANES:(c + 1) * LANES]
    h_lo = (h - h_hi.astype(F32)).astype(BF16)
    lg = jnp.dot(h_hi, wr_ref[...], preferred_element_type=F32)
    lg_lo = jnp.dot(h_lo, wr_ref[:, :ROUTER_LANES], preferred_element_type=F32)
    lg_ref[...] = lg[:, :ROUTER_LANES] + (lg[:, ROUTER_LANES:] + lg_lo) + br_ref[...]


def _merge(ya, yb, gates, wa, wb, wo, x_all, mod_l, nw, wr, br, rows):
    M, D = x_all.shape
    W = ya.shape[1]
    chunks = D // 2 // LANES
    return pl.pallas_call(
        _merge_kernel,
        grid=(rows.n,),
        in_specs=[
            rows.row_spec(W), rows.row_spec(W),
            pl.BlockSpec((ROW_TILE, D), lambda i: (rows.tile(i), 0)),
            pl.BlockSpec((ROW_TILE, D), lambda i: (rows.tile(i), 1)),
            _const_spec((W, D)), _const_spec((W, D)), _const_spec((D, D)),
            rows.row_spec(D), rows.mod_spec(D, 2),
            _const_spec((1, D)), rows.mod_spec(D, 3), rows.mod_spec(D, 4),
            _const_spec((D, 2 * ROUTER_LANES)), _const_spec((1, ROUTER_LANES)),
        ],
        out_specs=[rows.row_spec(D),
                   pl.BlockSpec((ROW_TILE * chunks, LANES), lambda i: (rows.tile(i), 0)),
                   rows.row_spec(ROUTER_LANES)],
        out_shape=[jax.ShapeDtypeStruct((M, D), F32), jax.ShapeDtypeStruct((M * chunks, LANES), jnp.uint32),
                   jax.ShapeDtypeStruct((M, ROUTER_LANES), F32)],
        compiler_params=_params("parallel"),
        name="merge",
    )(ya, yb, gates, gates, wa, wb, wo, x_all, mod_l, nw, mod_l, mod_l, wr, br)


def _route(logits):
    n = logits.shape[0]
    gl = logits[:, :N_GROUPS]
    el = logits[:, N_GROUPS:N_GROUPS + N_EXPERTS].reshape(n, N_GROUPS, EXPERTS_PER_GROUP)
    g = jnp.argmax(gl, axis=-1).astype(jnp.int32)
    pg = 1.0 / jnp.sum(jnp.exp(gl - jnp.max(gl, axis=-1, keepdims=True)), axis=-1, keepdims=True)
    in_group = jnp.arange(N_GROUPS, dtype=jnp.int32)[None, :, None] == g[:, None, None]
    el_g = jnp.sum(jnp.where(in_group, el, 0.0), axis=1)
    tv, ti = lax.top_k(el_g, TOP_K)
    wts = jax.nn.softmax(tv, axis=-1) * pg
    eid = g[:, None] * EXPERTS_PER_GROUP + ti.astype(jnp.int32)
    return eid, wts


def _dispatch(eid, tok_rows, row_scale):
    n = eid.shape[0]
    a = n * TOP_K
    e_flat = eid.reshape(a)
    onehot = (e_flat[:, None] == jnp.arange(N_EXPERTS, dtype=jnp.int32)[None, :]).astype(jnp.int32)
    csum = jnp.cumsum(onehot, axis=0)
    counts = csum[-1]
    padded = (counts + MOE_ROWS - 1) // MOE_ROWS * MOE_ROWS
    pad_end = jnp.cumsum(padded)
    pad_start = pad_end - padded
    dest = jnp.sum(onehot * (csum - 1 + pad_start[None, :]), axis=1)
    n_grid = a // MOE_ROWS + N_EXPERTS - 1 + GATHER_AHEAD
    _, sorted_row = lax.sort((e_flat, jnp.repeat(tok_rows * row_scale, TOP_K)), num_keys=1, is_stable=True)
    sorted_row = jnp.concatenate([sorted_row, jnp.zeros((MOE_ROWS,), jnp.int32)])
    blk = jnp.arange(n_grid, dtype=jnp.int32)
    n_valid = (pad_end[-1] // MOE_ROWS).astype(jnp.int32)
    blk_c = jnp.minimum(blk, n_valid - 1)
    blk_e = jnp.sum((pad_end[None, :] <= (blk_c * MOE_ROWS)[:, None]).astype(jnp.int32), axis=1)
    blk_e = jnp.minimum(blk_e, N_EXPERTS - 1)
    start = jnp.cumsum(counts) - counts
    blk_first = jnp.where(blk < n_valid, start[blk_e] + blk * MOE_ROWS - pad_start[blk_e], 0).astype(jnp.int32)
    experts = jnp.arange(N_EXPERTS, dtype=jnp.int32)
    used = counts > 0
    first_used_from = lax.cummin(jnp.where(used, experts, N_EXPERTS), reverse=True)
    next_used = jnp.concatenate([first_used_from[1:], jnp.full((1,), N_EXPERTS, jnp.int32)])
    next_used = jnp.where(next_used == N_EXPERTS, -1, next_used)
    parity = (jnp.cumsum(used.astype(jnp.int32)) - 1) % 2
    return (dest, blk_e, next_used[blk_e].astype(jnp.int32), parity[blk_e].astype(jnp.int32), blk_first,
            sorted_row, n_valid.reshape(1))


def _experts_kernel(blk_e, blk_next, blk_par, blk_first, sorted_row, n_valid,
                    h_hbm, w1_hbm, w3_hbm, w2_hbm, o_ref,
                    xg, xsem, ws1, ws3, ws2, wsem, w1b, w3b, w2b, *, layer):
    i = pl.program_id(0)
    nv = n_valid[0]
    chunks = xg.shape[1] // MOE_ROWS
    e = blk_e[i]
    par = blk_par[i]
    fresh = jnp.logical_or(i == 0, e != blk_e[jnp.maximum(i - 1, 0)])

    def weight_copies(expert, p):
        return (pltpu.make_async_copy(w1_hbm.at[layer, expert], ws1.at[p], wsem.at[p, 0]),
                pltpu.make_async_copy(w3_hbm.at[layer, expert], ws3.at[p], wsem.at[p, 1]),
                pltpu.make_async_copy(w2_hbm.at[layer, expert], ws2.at[p], wsem.at[p, 2]))

    def for_rows(blk, fn):
        buf = lax.rem(blk, GATHER_AHEAD + 1)
        base = blk_first[blk]
        for j in range(MOE_ROWS):
            r = pl.multiple_of(sorted_row[base + j], chunks)
            fn(pltpu.make_async_copy(h_hbm.at[pl.ds(r, chunks), :], xg.at[buf, pl.ds(j * chunks, chunks), :],
                                     xsem.at[buf]))

    @pl.when(i == 0)
    def _():
        for cp, priority in zip(weight_copies(e, par), WEIGHT_DMA_PRIORITIES):
            cp.start(priority=priority)
        for blk in range(GATHER_AHEAD):
            for_rows(blk, lambda cp: cp.start())

    @pl.when(jnp.logical_and(i >= nv, i < nv + GATHER_AHEAD))
    def _():
        for_rows(i, lambda cp: cp.wait())

    @pl.when(fresh)
    def _():
        for cp in weight_copies(e, par):
            cp.wait()
        nxt = blk_next[i]

        @pl.when(nxt >= 0)
        def _():
            for cp, priority in zip(weight_copies(nxt, 1 - par), WEIGHT_DMA_PRIORITIES):
                cp.start(priority=priority)

        w1b[...] = ws1[par].astype(BF16)
        w3b[...] = ws3[par].astype(BF16)
        w2b[...] = ws2[par].astype(BF16)

    @pl.when(i < nv)
    def _():
        for_rows(i, lambda cp: cp.wait())
        words = jnp.concatenate(
            [xg[lax.rem(i, GATHER_AHEAD + 1), pl.ds(c, MOE_ROWS, stride=chunks), :] for c in range(chunks)],
            axis=1)
        x = jnp.concatenate([pltpu.bitcast(words << 16, F32).astype(BF16),
                             pltpu.bitcast(words & jnp.uint32(0xFFFF0000), F32).astype(BF16)], axis=1)
        for_rows(i + GATHER_AHEAD, lambda cp: cp.start())
        a = jnp.dot(x, w1b[...], preferred_element_type=F32)
        b = jnp.dot(x, w3b[...], preferred_element_type=F32)
        hid = (a * jax.nn.sigmoid(a)) * b
        o_ref[...] = jnp.dot(hid.astype(BF16), w2b[...], preferred_element_type=F32).astype(o_ref.dtype)


def _experts(h2, w1, w3, w2, layer, blk_e, blk_next, blk_par, blk_first, sorted_row, n_valid):
    D, de = w1.shape[-2:]
    chunks = D // 2 // LANES
    n_blk = blk_e.shape[0]
    any_spec = pl.BlockSpec(memory_space=pl.ANY)
    return pl.pallas_call(
        functools.partial(_experts_kernel, layer=layer),
        grid_spec=pltpu.PrefetchScalarGridSpec(
            num_scalar_prefetch=6, grid=(n_blk,),
            in_specs=[any_spec, any_spec, any_spec, any_spec],
            out_specs=pl.BlockSpec((MOE_ROWS, D), lambda i, *s: (jnp.minimum(i, s[5][0] - 1), 0)),
            scratch_shapes=[
                pltpu.VMEM((GATHER_AHEAD + 1, MOE_ROWS * chunks, LANES), jnp.uint32),
                pltpu.SemaphoreType.DMA((GATHER_AHEAD + 1,)),
                pltpu.VMEM((2, D, de), F32), pltpu.VMEM((2, D, de), F32), pltpu.VMEM((2, de, D), F32),
                pltpu.SemaphoreType.DMA((2, 3)),
                pltpu.VMEM((D, de), BF16), pltpu.VMEM((D, de), BF16), pltpu.VMEM((de, D), BF16),
            ],
        ),
        out_shape=jax.ShapeDtypeStruct((n_blk * MOE_ROWS, D), BF16),
        compiler_params=_params("arbitrary"),
        name="experts",
    )(blk_e, blk_next, blk_par, blk_first, sorted_row, n_valid, h2, w1, w3, w2)


def _moe_mix(x_ref, y0_ref, y1_ref, w_ref, g2_ref):
    w = w_ref[...]
    y = w[:, 0:1] * y0_ref[...].astype(F32) + w[:, 1:2] * y1_ref[...].astype(F32)
    return x_ref[...] + g2_ref[...] * y


def _combine_next_kernel(x_ref, y0_ref, y1_ref, w_ref, g2_ref, nw_ref, sh_ref, sc_ref, xo_ref, h_ref):
    xn = _moe_mix(x_ref, y0_ref, y1_ref, w_ref, g2_ref)
    xo_ref[...] = xn
    h_ref[...] = ((_rms(xn) * nw_ref[...]) * (1.0 + sc_ref[...]) + sh_ref[...]).astype(BF16)


def _combine_final_kernel(x_ref, y0_ref, y1_ref, w_ref, g2_ref, nf_ref, o_ref):
    o_ref[...] = _rms(_moe_mix(x_ref, y0_ref, y1_ref, w_ref, g2_ref)) * nf_ref[...]


def _token_specs(n_tokens, d):
    tiles = n_tokens // ROW_TILE
    return [pl.BlockSpec((ROW_TILE, d), lambda i: (i, 0)), pl.BlockSpec((ROW_TILE, d), lambda i: (i + tiles, 0)),
            pl.BlockSpec((ROW_TILE, TOP_K), lambda i: (i, 0))]


def _combine_next(x_new, ysel, wts, mod_l, nw_next, mod_next, rows):
    M, D = x_new.shape
    return pl.pallas_call(
        _combine_next_kernel,
        grid=(rows.n,),
        in_specs=[rows.row_spec(D), *_token_specs(M, D), rows.mod_spec(D, 5),
                  _const_spec((1, D)), rows.mod_spec(D, 0), rows.mod_spec(D, 1)],
        out_specs=[rows.row_spec(D), rows.row_spec(D)],
        out_shape=[jax.ShapeDtypeStruct((M, D), F32), jax.ShapeDtypeStruct((M, D), BF16)],
        compiler_params=_params("parallel"),
        name="combine_next",
    )(x_new, ysel, ysel, wts, mod_l, nw_next, mod_next, mod_next)


def _combine_final(x_new, ysel, wts, mod_l, nf, rows):
    M, D = x_new.shape
    n = wts.shape[0]
    return pl.pallas_call(
        _combine_final_kernel,
        grid=(rows.n,),
        in_specs=[rows.row_spec(D), *_token_specs(n, D), rows.mod_spec(D, 5), _const_spec((1, D))],
        out_specs=pl.BlockSpec((ROW_TILE, D), lambda i: (i, 0)),
        out_shape=jax.ShapeDtypeStruct((n, D), F32),
        compiler_params=_params("parallel"),
        name="combine_final",
    )(x_new, ysel, ysel, wts, mod_l, nf)


def _rope_tables(ctx_len, seq):
    quarter = HEAD_DIM // 4
    freqs = ROPE_BASE ** (-jnp.arange(quarter, dtype=F32) / quarter)
    pos = jnp.arange(seq, dtype=jnp.int32)
    row = (pos // GRID_W).astype(F32)[:, None] * freqs[None, :]
    col = (pos % GRID_W).astype(F32)[:, None] * freqs[None, :]
    cos = jnp.concatenate([jnp.cos(row), jnp.cos(col), jnp.cos(row), jnp.cos(col)], axis=-1)
    sin = jnp.concatenate([-jnp.sin(row), -jnp.sin(col), jnp.sin(row), jnp.sin(col)], axis=-1)
    cos = jnp.concatenate([jnp.ones((ctx_len, HEAD_DIM), F32), cos], axis=0)
    sin = jnp.concatenate([jnp.zeros((ctx_len, HEAD_DIM), F32), sin], axis=0)
    return cos, sin


def kernel(x, c, ctx, c_ctx, w_mod, b_mod, norm_mix, norm_ffn, w_in, qn_a, kn_a, sink_b, w_br_a, w_br_b, w_out, w_rg, b_rg, w_re, b_re, w1, w3, w2, norm_final):
    B, S, D = x.shape
    C = ctx.shape[1]
    L = w_mod.shape[0]
    T = C + S
    M = B * T
    assert C == ROW_TILE and S % ROW_TILE == 0 and T % PROJ_ROWS == 0 and B < MOD_ROWS
    tiles_per_batch = T // ROW_TILE

    cc = jnp.concatenate([c, c_ctx[None, :], jnp.zeros((MOD_ROWS - B - 1, D), F32)], axis=0)
    mod = _mod_vectors(cc, w_mod, b_mod).reshape(L, MOD_ROWS * 6, 1, D)
    cos, sin = _rope_tables(C, S)
    latent_rows = (jnp.arange(B, dtype=jnp.int32)[:, None] * T + C
                   + jnp.arange(S, dtype=jnp.int32)[None, :]).reshape(B * S)
    all_rows = jnp.arange(M, dtype=jnp.int32)

    mixer_kinds = lambda q_kind, k_kind: (tuple((q_kind, 0, k) for k in range(HEADS))
                                          + tuple((k_kind, 1, k) for k in range(KV_HEADS))
                                          + tuple(("v", 2, k) for k in range(KV_HEADS)))
    mixer_widths = [(WIDTH, 1), (KVW, 1), (KVW, 1)]
    gate_kinds = tuple(("gate", 0, k) for k in range(GATE_COLS // HEAD_DIM))
    col = {"a": 0, "b": WIDTH + 2 * KVW, "gates": 2 * WIDTH + 4 * KVW}

    every = _Rows(B, tiles_per_batch, latent_only=False)
    latent = _Rows(B, tiles_per_batch, latent_only=True)
    x_all, h = _norm_mod(x, ctx, norm_mix[0][None, :], mod[0])
    out = None
    for l in range(L):
        last = l == L - 1
        rows = latent if last else every
        proj = functools.partial(_proj, h, w_in, l, cos, sin, tiles_per_batch=tiles_per_batch)
        gains = jnp.stack([_pair_halves(qn_a[l]), _pair_halves(kn_a[l])])
        qa, ka, va = proj(gains, col["a"], mixer_kinds("q_norm", "k_norm"), mixer_widths, name="proj_a")
        qb, kb, vb = proj(gains, col["b"], mixer_kinds("q_rope", "k_rope"), mixer_widths, name="proj_b")
        (gates,) = proj(gains, col["gates"], gate_kinds, [(GATE_COLS, 2 * D // GATE_COLS)], name="proj_gates",
                        rows=GATE_ROWS)

        ya = _attention(qa, ka, va, None, B, T, C, last, window=False)
        yb = _attention(qb, kb, vb, sink_b[l], B, T, C, last, window=True)

        w_router = jnp.concatenate(
            [w_rg[l], w_re[l], jnp.zeros((D, ROUTER_LANES - N_GROUPS - N_EXPERTS), F32)], axis=1)
        w_router_hi = w_router.astype(BF16)
        w_router = jnp.concatenate([w_router_hi, (w_router - w_router_hi.astype(F32)).astype(BF16)], axis=1)
        b_router = jnp.concatenate(
            [b_rg[l], b_re[l], jnp.zeros((ROUTER_LANES - N_GROUPS - N_EXPERTS,), F32)])[None, :]
        x_new, h2, logits = _merge(ya, yb, gates, w_br_a[l].astype(BF16), w_br_b[l].astype(BF16),
                                   w_out[l].astype(BF16), x_all, mod[l], norm_ffn[l][None, :],
                                   w_router, b_router, rows)

        tok_rows = latent_rows if last else all_rows
        eid, wts = _route(logits[tok_rows] if last else logits)
        dest, *plan = _dispatch(eid, tok_rows, D // 2 // LANES)
        ybuf = _experts(h2, w1, w3, w2, l, *plan)
        ysel = ybuf[dest.reshape(-1, TOP_K).T.reshape(-1)]
        if last:
            out = _combine_final(x_new, ysel, wts, mod[l], norm_final[None, :], rows).reshape(B, S, D)
        else:
            x_all, h = _combine_next(x_new, ysel, wts, mod[l], norm_mix[l + 1][None, :], mod[l + 1], rows)
    return out
```

```python
import functools

import jax
import jax.numpy as jnp
from jax import lax
from jax.experimental import pallas as pl
from jax.experimental.pallas import tpu as pltpu

F32 = jnp.float32
BF16 = jnp.bfloat16

GRID_W = 64
HEAD_DIM = 128
LANES = 128
HEADS = 8
KV_HEADS = 2
GROUP = HEADS // KV_HEADS
WIDTH = HEADS * HEAD_DIM
KVW = KV_HEADS * HEAD_DIM
WINDOW = 128
ROPE_BASE = 10000.0
ATTN_SCALE = HEAD_DIM ** -0.5
LOG2E = 1.4426950408889634
N_GROUPS = 8
EXPERTS_PER_GROUP = 8
N_EXPERTS = N_GROUPS * EXPERTS_PER_GROUP
TOP_K = 2
EPS = 1e-6
NEG_INF = -1e30

ROW_TILE = 256
PROJ_ROWS = 768
GATE_ROWS = 1152
GATE_COLS = 1024
MOE_ROWS = 128
GATHER_AHEAD = 2
WEIGHT_DMA_PRIORITIES = (1, 1, 0)
ROUTER_LANES = 128
DENSE_CHAIN_HEADS = 4
WINDOW_CHAIN_HEADS = 1
MOD_ROWS = 8
VMEM_LIMIT = 56 * 1024 * 1024


def _params(*sem):
    return pltpu.CompilerParams(dimension_semantics=sem, vmem_limit_bytes=VMEM_LIMIT)


def _rms(y):
    return y * lax.rsqrt(jnp.mean(y * y, axis=-1, keepdims=True) + EPS)


def _mod_kernel(c_ref, w_ref, b_ref, o_ref):
    c = c_ref[...]
    a = (c * jax.nn.sigmoid(c)).astype(BF16)
    o_ref[...] = jnp.dot(a, w_ref[...].astype(BF16), preferred_element_type=F32) + b_ref[...]


def _mod_vectors(cc, w_mod, b_mod):
    L, D, N = w_mod.shape
    tn = 1024
    return pl.pallas_call(
        _mod_kernel,
        grid=(L, N // tn),
        in_specs=[
            pl.BlockSpec((MOD_ROWS, D), lambda l, j: (0, 0)),
            pl.BlockSpec((None, D, tn), lambda l, j: (l, 0, j)),
            pl.BlockSpec((None, 1, tn), lambda l, j: (l, 0, j)),
        ],
        out_specs=pl.BlockSpec((None, MOD_ROWS, tn), lambda l, j: (l, 0, j)),
        out_shape=jax.ShapeDtypeStruct((L, MOD_ROWS, N), F32),
        compiler_params=_params("parallel", "parallel"),
        name="mod_vectors",
    )(cc, w_mod, b_mod.reshape(L, 1, N))


class _Rows:
    def __init__(self, batch, tiles_per_batch, latent_only):
        self.batch = batch
        self.tpb = tiles_per_batch
        self.latent_only = latent_only
        self.n = batch * (tiles_per_batch - 1 if latent_only else tiles_per_batch)

    def tile(self, i):
        if self.latent_only:
            per = self.tpb - 1
            return (i // per) * self.tpb + 1 + i % per
        return i

    def mod_row(self, i):
        if self.latent_only:
            return i // (self.tpb - 1)
        return jnp.where(i % self.tpb == 0, self.batch, i // self.tpb)

    def row_spec(self, width):
        return pl.BlockSpec((ROW_TILE, width), lambda i: (self.tile(i), 0))

    def mod_spec(self, d, k):
        return pl.BlockSpec((None, 1, d), lambda i: (self.mod_row(i) * 6 + k, 0, 0))


def _const_spec(shape):
    zeros = (0,) * len(shape)
    return pl.BlockSpec(shape, lambda *_: zeros, pipeline_mode=pl.Buffered(1))


def _norm_mod_kernel(x_ref, ctx_ref, nw_ref, sh_ref, sc_ref, xo_ref, h_ref):
    def emit(src_ref):
        x = src_ref[...]
        xo_ref[...] = x
        h_ref[...] = ((_rms(x) * nw_ref[...]) * (1.0 + sc_ref[...]) + sh_ref[...]).astype(BF16)

    is_ctx = pl.program_id(1) == 0
    pl.when(is_ctx)(lambda: emit(ctx_ref))
    pl.when(jnp.logical_not(is_ctx))(lambda: emit(x_ref))


def _norm_mod(x, ctx, nw, mod_l):
    B, S, D = x.shape
    tiles = 1 + S // ROW_TILE
    mod_spec = lambda k: pl.BlockSpec((None, 1, D), lambda b, t: (jnp.where(t == 0, B, b) * 6 + k, 0, 0))
    out_spec = pl.BlockSpec((ROW_TILE, D), lambda b, t: (b * tiles + t, 0))
    return pl.pallas_call(
        _norm_mod_kernel,
        grid=(B, tiles),
        in_specs=[pl.BlockSpec((None, ROW_TILE, D), lambda b, t: (b, jnp.maximum(t - 1, 0), 0)),
                  pl.BlockSpec((None, ROW_TILE, D), lambda b, t: (b, 0, 0)),
                  _const_spec((1, D)), mod_spec(0), mod_spec(1)],
        out_specs=[out_spec, out_spec],
        out_shape=[jax.ShapeDtypeStruct((B * tiles * ROW_TILE, D), F32),
                   jax.ShapeDtypeStruct((B * tiles * ROW_TILE, D), BF16)],
        compiler_params=_params("parallel", "parallel"),
        name="norm_mod",
    )(x, ctx, nw, mod_l, mod_l)


ROTARY_KINDS = ("q_norm", "k_norm", "q_rope", "k_rope")


def _pair_halves(a):
    q = HEAD_DIM // 4
    return jnp.concatenate([a[..., 0:q], a[..., 2 * q:3 * q], a[..., q:2 * q], a[..., 3 * q:]], axis=-1)


def _rope(y, cos, sin):
    return y * cos + pltpu.roll(y, HEAD_DIM // 2, 1) * sin


def _proj_kernel(h_ref, w_ref, cos_ref, sin_ref, gain_ref, *refs, kinds):
    *out_refs, wb_ref = refs

    @pl.when(pl.program_id(1) == 0)
    def _():
        for c, (kind, _, _) in enumerate(kinds):
            w = w_ref[:, c * HEAD_DIM:(c + 1) * HEAD_DIM]
            wb_ref[:, c * HEAD_DIM:(c + 1) * HEAD_DIM] = (_pair_halves(w) if kind in ROTARY_KINDS else w).astype(BF16)

    acc = jnp.dot(h_ref[...], wb_ref[...], preferred_element_type=F32)
    for c, (kind, out_idx, out_chunk) in enumerate(kinds):
        y = acc[:, c * HEAD_DIM:(c + 1) * HEAD_DIM]
        if kind == "q_norm":
            y = _rms(y) * gain_ref[0:1, :]
        if kind == "k_norm":
            y = _rms(y) * gain_ref[1:2, :]
        if kind in ROTARY_KINDS:
            y = _rope(y, cos_ref[...], sin_ref[...])
        if kind in ("q_norm", "q_rope"):
            y = y * (ATTN_SCALE * LOG2E)
        out_refs[out_idx][:, out_chunk * HEAD_DIM:(out_chunk + 1) * HEAD_DIM] = y.astype(BF16)


def _proj(h, w_in, layer, cos, sin, gain, col0, kinds, out_widths, tiles_per_batch, name, rows=PROJ_ROWS):
    M, D = h.shape
    n_col_tiles = out_widths[0][1]
    cols = len(kinds) * HEAD_DIM
    cb0 = col0 // cols
    proj_tiles_per_batch = tiles_per_batch * ROW_TILE // rows
    return pl.pallas_call(
        functools.partial(_proj_kernel, kinds=kinds),
        grid=(n_col_tiles, M // rows),
        in_specs=[
            pl.BlockSpec((rows, D), lambda j, i: (i, 0)),
            pl.BlockSpec((None, D, cols), lambda j, i: (layer, 0, cb0 + j),
                         pipeline_mode=pl.Buffered(1 if n_col_tiles == 1 else 2)),
            pl.BlockSpec((rows, HEAD_DIM), lambda j, i: (i % proj_tiles_per_batch, 0)),
            pl.BlockSpec((rows, HEAD_DIM), lambda j, i: (i % proj_tiles_per_batch, 0)),
            pl.BlockSpec((2, HEAD_DIM), lambda j, i: (0, 0)),
        ],
        out_specs=[pl.BlockSpec((rows, w), lambda j, i: (i, j)) for w, _ in out_widths],
        out_shape=[jax.ShapeDtypeStruct((M, w * n), BF16) for w, n in out_widths],
        scratch_shapes=[pltpu.VMEM((D, cols), BF16)],
        compiler_params=_params("parallel", "arbitrary"),
        name=name,
    )(h, w_in, cos, sin, gain)


def _stack_heads(q_ref, g0, n):
    return jnp.concatenate([q_ref[:, g * HEAD_DIM:(g + 1) * HEAD_DIM] for g in range(g0, g0 + n)], axis=0)


def _write_heads(o_ref, g0, n, o):
    rows = o_ref.shape[0]
    for j in range(n):
        o_ref[:, (g0 + j) * HEAD_DIM:(g0 + j + 1) * HEAD_DIM] = o[j * rows:(j + 1) * rows].astype(o_ref.dtype)


def _scores(q, k):
    return lax.dot_general(q, k, (((1,), (1,)), ((), ())), preferred_element_type=F32)


def _online_step(q, k, v, m, l, acc):
    s = _scores(q, k)
    m_new = jnp.maximum(m, jnp.max(s, axis=-1, keepdims=True))
    alpha = jnp.exp2(m - m_new)
    p = jnp.exp2(s - m_new)
    l = alpha * l + jnp.sum(p, axis=-1, keepdims=True)
    acc = alpha * acc + jnp.dot(p.astype(BF16), v, preferred_element_type=F32)
    return m_new, l, acc


def _dense_attn_kernel(q_ref, k_ref, v_ref, o_ref, *, ctx_len, total_len, key_chunk, ctx_tile):
    def run(n_keys, chunk):
        for g0 in range(0, GROUP, DENSE_CHAIN_HEADS):
            q = _stack_heads(q_ref, g0, DENSE_CHAIN_HEADS)
            r = q.shape[0]
            m = jnp.full((r, 1), -jnp.inf, F32)
            l = jnp.zeros((r, 1), F32)
            acc = jnp.zeros((r, HEAD_DIM), F32)
            for c in range(n_keys // chunk):
                m, l, acc = _online_step(q, k_ref[c * chunk:(c + 1) * chunk, :],
                                         v_ref[c * chunk:(c + 1) * chunk, :], m, l, acc)
            _write_heads(o_ref, g0, DENSE_CHAIN_HEADS, acc * (1.0 / l))

    if ctx_tile:
        is_ctx = pl.program_id(2) == 0
        pl.when(is_ctx)(lambda: run(ctx_len, ctx_len))
        pl.when(jnp.logical_not(is_ctx))(lambda: run(total_len, key_chunk))
    else:
        run(total_len, key_chunk)


def _window_attn_kernel(sink_ref, q_ref, k_ref, v_ref, o_ref, *, ctx_len, total_len, ctx_tile, q_off):
    kvh = pl.program_id(1)
    tq = q_ref.shape[0]
    span = tq + 2 * WINDOW
    n = WINDOW_CHAIN_HEADS

    def attend(parts):
        for g0 in range(0, GROUP, n):
            q = _stack_heads(q_ref, g0, n)
            sink = jnp.concatenate(
                [jnp.full((tq, 1), sink_ref[kvh * GROUP + g] * LOG2E, F32) for g in range(g0, g0 + n)], axis=0)
            scores = []
            m = sink
            for k, _, bias in parts:
                s = _scores(q, k)
                if bias is not None:
                    s = s + bias
                scores.append(s)
                m = jnp.maximum(m, jnp.max(s, axis=-1, keepdims=True))
            l = jnp.exp2(sink - m)
            o = jnp.zeros((n * tq, HEAD_DIM), F32)
            for s, (_, v, _) in zip(scores, parts):
                p = jnp.exp2(s - m)
                l = l + jnp.sum(p, axis=-1, keepdims=True)
                o = o + jnp.dot(p.astype(BF16), v, preferred_element_type=F32)
            _write_heads(o_ref, g0, n, o * (1.0 / l))

    def ctx_part():
        return k_ref[0:ctx_len, :], v_ref[0:ctx_len, :], None

    def run_latent():
        q0 = (pl.program_id(2) + q_off) * tq - ctx_len
        n_latent = total_len - ctx_len
        start = pl.multiple_of(jnp.clip(q0 - WINDOW, 0, n_latent - span), WINDOW)
        row = lax.broadcasted_iota(jnp.int32, (n * tq, span), 0) & (tq - 1)
        col = lax.broadcasted_iota(jnp.int32, (n * tq, span), 1)
        bias = jnp.where(jnp.abs(col - row + (start - q0)) <= WINDOW, 0.0, NEG_INF).astype(F32)
        attend([ctx_part(), (k_ref[pl.ds(ctx_len + start, span), :], v_ref[pl.ds(ctx_len + start, span), :], bias)])

    if ctx_tile:
        is_ctx = pl.program_id(2) == 0
        pl.when(is_ctx)(lambda: attend([ctx_part()]))
        pl.when(jnp.logical_not(is_ctx))(run_latent)
    else:
        run_latent()


def _attention(q, k, v, sink, batch, total_len, ctx_len, latent_only, window):
    M = q.shape[0]
    q3, k3, v3 = (a.reshape(batch, total_len, a.shape[1]) for a in (q, k, v))
    tiles = total_len // ROW_TILE
    q_off = 1 if latent_only else 0
    gw = GROUP * HEAD_DIM
    q_spec = pl.BlockSpec((None, ROW_TILE, gw), lambda b, h, i, *_: (b, i + q_off, h))
    kv_spec = pl.BlockSpec((None, total_len, HEAD_DIM), lambda b, h, i, *_: (b, 0, h))
    grid = (batch, KV_HEADS, tiles - q_off)
    out_shape = jax.ShapeDtypeStruct(q3.shape, BF16)
    if window:
        kern = functools.partial(_window_attn_kernel, ctx_len=ctx_len, total_len=total_len,
                                 ctx_tile=not latent_only, q_off=q_off)
        out = pl.pallas_call(
            kern,
            grid_spec=pltpu.PrefetchScalarGridSpec(
                num_scalar_prefetch=1, grid=grid,
                in_specs=[q_spec, kv_spec, kv_spec], out_specs=q_spec),
            out_shape=out_shape,
            compiler_params=_params("parallel", "parallel", "parallel"),
            name="window_attention",
        )(sink, q3, k3, v3)
    else:
        kern = functools.partial(_dense_attn_kernel, ctx_len=ctx_len, total_len=total_len,
                                 key_chunk=total_len // 3, ctx_tile=not latent_only)
        out = pl.pallas_call(
            kern, grid=grid, in_specs=[q_spec, kv_spec, kv_spec], out_specs=q_spec,
            out_shape=out_shape,
            compiler_params=_params("parallel", "parallel", "parallel"),
            name="dense_attention",
        )(q3, k3, v3)
    return out.reshape(M, q.shape[1])


def _merge_kernel(ya_ref, yb_ref, ga_ref, gb_ref, wa_ref, wb_ref, wo_ref, x_ref, g1_ref,
                  nw_ref, sh_ref, sc_ref, wr_ref, br_ref, xo_ref, h_ref, lg_ref):
    a = jnp.dot(ya_ref[...], wa_ref[...], preferred_element_type=F32)
    b = jnp.dot(yb_ref[...], wb_ref[...], preferred_element_type=F32)
    m = jax.nn.sigmoid(ga_ref[...].astype(F32)) * a + jax.nn.sigmoid(gb_ref[...].astype(F32)) * b
    o = jnp.dot(m.astype(BF16), wo_ref[...], preferred_element_type=F32)
    xn = x_ref[...] + g1_ref[...] * o
    xo_ref[...] = xn
    h = (_rms(xn) * nw_ref[...]) * (1.0 + sc_ref[...]) + sh_ref[...]
    h_hi = h.astype(BF16)
    bits = pltpu.bitcast(h_hi.astype(F32), jnp.uint32)
    half = h.shape[1] // 2
    words = (bits[:, :half] >> 16) | (bits[:, half:] & jnp.uint32(0xFFFF0000))
    chunks = half // LANES
    for c in range(chunks):
        h_ref[pl.ds(c, h.shape[0], stride=chunks), :] = words[:, c * LANES:(c + 1) * LANES]
    h_lo = (h - h_hi.astype(F32)).astype(BF16)
    lg = jnp.dot(h_hi, wr_ref[...], preferred_element_type=F32)
    lg_lo = jnp.dot(h_lo, wr_ref[:, :ROUTER_LANES], preferred_element_type=F32)
    lg_ref[...] = lg[:, :ROUTER_LANES] + (lg[:, ROUTER_LANES:] + lg_lo) + br_ref[...]


def _merge(ya, yb, gates, wa, wb, wo, x_all, mod_l, nw, wr, br, rows):
    M, D = x_all.shape
    W = ya.shape[1]
    chunks = D // 2 // LANES
    return pl.pallas_call(
        _merge_kernel,
        grid=(rows.n,),
        in_specs=[
            rows.row_spec(W), rows.row_spec(W),
            pl.BlockSpec((ROW_TILE, D), lambda i: (rows.tile(i), 0)),
            pl.BlockSpec((ROW_TILE, D), lambda i: (rows.tile(i), 1)),
            _const_spec((W, D)), _const_spec((W, D)), _const_spec((D, D)),
            rows.row_spec(D), rows.mod_spec(D, 2),
            _const_spec((1, D)), rows.mod_spec(D, 3), rows.mod_spec(D, 4),
            _const_spec((D, 2 * ROUTER_LANES)), _const_spec((1, ROUTER_LANES)),
        ],
        out_specs=[rows.row_spec(D),
                   pl.BlockSpec((ROW_TILE * chunks, LANES), lambda i: (rows.tile(i), 0)),
                   rows.row_spec(ROUTER_LANES)],
        out_shape=[jax.ShapeDtypeStruct((M, D), F32), jax.ShapeDtypeStruct((M * chunks, LANES), jnp.uint32),
                   jax.ShapeDtypeStruct((M, ROUTER_LANES), F32)],
        compiler_params=_params("parallel"),
        name="merge",
    )(ya, yb, gates, gates, wa, wb, wo, x_all, mod_l, nw, mod_l, mod_l, wr, br)


def _route(logits):
    n = logits.shape[0]
    gl = logits[:, :N_GROUPS]
    el = logits[:, N_GROUPS:N_GROUPS + N_EXPERTS].reshape(n, N_GROUPS, EXPERTS_PER_GROUP)
    g = jnp.argmax(gl, axis=-1).astype(jnp.int32)
    pg = 1.0 / jnp.sum(jnp.exp(gl - jnp.max(gl, axis=-1, keepdims=True)), axis=-1, keepdims=True)
    in_group = jnp.arange(N_GROUPS, dtype=jnp.int32)[None, :, None] == g[:, None, None]
    el_g = jnp.sum(jnp.where(in_group, el, 0.0), axis=1)
    tv, ti = lax.top_k(el_g, TOP_K)
    wts = jax.nn.softmax(tv, axis=-1) * pg
    eid = g[:, None] * EXPERTS_PER_GROUP + ti.astype(jnp.int32)
    return eid, wts


def _dispatch(eid, tok_rows, row_scale):
    n = eid.shape[0]
    a = n * TOP_K
    e_flat = eid.reshape(a)
    onehot = (e_flat[:, None] == jnp.arange(N_EXPERTS, dtype=jnp.int32)[None, :]).astype(jnp.int32)
    csum = jnp.cumsum(onehot, axis=0)
    counts = csum[-1]
    padded = (counts + MOE_ROWS - 1) // MOE_ROWS * MOE_ROWS
    pad_end = jnp.cumsum(padded)
    pad_start = pad_end - padded
    dest = jnp.sum(onehot * (csum - 1 + pad_start[None, :]), axis=1)
    n_grid = a // MOE_ROWS + N_EXPERTS - 1 + GATHER_AHEAD
    _, sorted_row = lax.sort((e_flat, jnp.repeat(tok_rows * row_scale, TOP_K)), num_keys=1, is_stable=True)
    sorted_row = jnp.concatenate([sorted_row, jnp.zeros((MOE_ROWS,), jnp.int32)])
    blk = jnp.arange(n_grid, dtype=jnp.int32)
    n_valid = (pad_end[-1] // MOE_ROWS).astype(jnp.int32)
    blk_c = jnp.minimum(blk, n_valid - 1)
    blk_e = jnp.sum((pad_end[None, :] <= (blk_c * MOE_ROWS)[:, None]).astype(jnp.int32), axis=1)
    blk_e = jnp.minimum(blk_e, N_EXPERTS - 1)
    start = jnp.cumsum(counts) - counts
    blk_first = jnp.where(blk < n_valid, start[blk_e] + blk * MOE_ROWS - pad_start[blk_e], 0).astype(jnp.int32)
    experts = jnp.arange(N_EXPERTS, dtype=jnp.int32)
    used = counts > 0
    first_used_from = lax.cummin(jnp.where(used, experts, N_EXPERTS), reverse=True)
    next_used = jnp.concatenate([first_used_from[1:], jnp.full((1,), N_EXPERTS, jnp.int32)])
    next_used = jnp.where(next_used == N_EXPERTS, -1, next_used)
    parity = (jnp.cumsum(used.astype(jnp.int32)) - 1) % 2
    after_next = jnp.where(next_used >= 0, next_used[jnp.maximum(next_used, 0)], -1)
    return (dest, blk_e, after_next[blk_e].astype(jnp.int32), parity[blk_e].astype(jnp.int32), blk_first,
            sorted_row, next_used[blk_e[:1]].astype(jnp.int32), n_valid.reshape(1))


def _experts_kernel(blk_e, blk_after_next, blk_par, blk_first, sorted_row, second_expert, n_valid,
                    h_hbm, w1_hbm, w3_hbm, w2_hbm, o_ref,
                    xg, xsem, ws1, ws3, ws2, wsem, w1b, w3b, w2b, *, layer):
    i = pl.program_id(0)
    nv = n_valid[0]
    chunks = xg.shape[1] // MOE_ROWS
    e = blk_e[i]
    par = blk_par[i]
    fresh = jnp.logical_or(i == 0, e != blk_e[jnp.maximum(i - 1, 0)])

    def weight_copies(expert, p):
        return (pltpu.make_async_copy(w1_hbm.at[layer, expert], ws1.at[p], wsem.at[p, 0]),
                pltpu.make_async_copy(w3_hbm.at[layer, expert], ws3.at[p], wsem.at[p, 1]),
                pltpu.make_async_copy(w2_hbm.at[layer, expert], ws2.at[p], wsem.at[p, 2]))

    def for_rows(blk, fn):
        buf = lax.rem(blk, GATHER_AHEAD + 1)
        base = blk_first[blk]
        for j in range(MOE_ROWS):
            r = pl.multiple_of(sorted_row[base + j], chunks)
            fn(pltpu.make_async_copy(h_hbm.at[pl.ds(r, chunks), :], xg.at[buf, pl.ds(j * chunks, chunks), :],
                                     xsem.at[buf]))

    def start_weights(expert, p):
        for cp, priority in zip(weight_copies(expert, p), WEIGHT_DMA_PRIORITIES):
            cp.start(priority=priority)

    @pl.when(i == 0)
    def _():
        start_weights(e, par)
        pl.when(second_expert[0] >= 0)(lambda: start_weights(second_expert[0], 1 - par))
        for blk in range(GATHER_AHEAD):
            for_rows(blk, lambda cp: cp.start())

    @pl.when(jnp.logical_and(i >= nv, i < nv + GATHER_AHEAD))
    def _():
        for_rows(i, lambda cp: cp.wait())

    @pl.when(fresh)
    def _():
        for cp in weight_copies(e, par):
            cp.wait()
        w1b[...] = ws1[par].astype(BF16)
        w3b[...] = ws3[par].astype(BF16)
        w2b[...] = ws2[par].astype(BF16)
        pl.when(blk_after_next[i] >= 0)(lambda: start_weights(blk_after_next[i], par))

    @pl.when(i < nv)
    def _():
        for_rows(i, lambda cp: cp.wait())
        words = jnp.concatenate(
            [xg[lax.rem(i, GATHER_AHEAD + 1), pl.ds(c, MOE_ROWS, stride=chunks), :] for c in range(chunks)],
            axis=1)
        x = jnp.concatenate([pltpu.bitcast(words << 16, F32).astype(BF16),
                             pltpu.bitcast(words & jnp.uint32(0xFFFF0000), F32).astype(BF16)], axis=1)
        for_rows(i + GATHER_AHEAD, lambda cp: cp.start())
        a = jnp.dot(x, w1b[...], preferred_element_type=F32)
        b = jnp.dot(x, w3b[...], preferred_element_type=F32)
        hid = (a * jax.nn.sigmoid(a)) * b
        o_ref[...] = jnp.dot(hid.astype(BF16), w2b[...], preferred_element_type=F32).astype(o_ref.dtype)


def _experts(h2, w1, w3, w2, layer, blk_e, blk_after_next, blk_par, blk_first, sorted_row, second_expert, n_valid):
    D, de = w1.shape[-2:]
    chunks = D // 2 // LANES
    n_blk = blk_e.shape[0]
    any_spec = pl.BlockSpec(memory_space=pl.ANY)
    return pl.pallas_call(
        functools.partial(_experts_kernel, layer=layer),
        grid_spec=pltpu.PrefetchScalarGridSpec(
            num_scalar_prefetch=7, grid=(n_blk,),
            in_specs=[any_spec, any_spec, any_spec, any_spec],
            out_specs=pl.BlockSpec((MOE_ROWS, D), lambda i, *s: (jnp.minimum(i, s[6][0] - 1), 0)),
            scratch_shapes=[
                pltpu.VMEM((GATHER_AHEAD + 1, MOE_ROWS * chunks, LANES), jnp.uint32),
                pltpu.SemaphoreType.DMA((GATHER_AHEAD + 1,)),
                pltpu.VMEM((2, D, de), F32), pltpu.VMEM((2, D, de), F32), pltpu.VMEM((2, de, D), F32),
                pltpu.SemaphoreType.DMA((2, 3)),
                pltpu.VMEM((D, de), BF16), pltpu.VMEM((D, de), BF16), pltpu.VMEM((de, D), BF16),
            ],
        ),
        out_shape=jax.ShapeDtypeStruct((n_blk * MOE_ROWS, D), BF16),
        compiler_params=_params("arbitrary"),
        name="experts",
    )(blk_e, blk_after_next, blk_par, blk_first, sorted_row, second_expert, n_valid, h2, w1, w3, w2)


def _moe_mix(x_ref, y0_ref, y1_ref, w_ref, g2_ref):
    w = w_ref[...]
    y = w[:, 0:1] * y0_ref[...].astype(F32) + w[:, 1:2] * y1_ref[...].astype(F32)
    return x_ref[...] + g2_ref[...] * y


def _combine_next_kernel(x_ref, y0_ref, y1_ref, w_ref, g2_ref, nw_ref, sh_ref, sc_ref, xo_ref, h_ref):
    xn = _moe_mix(x_ref, y0_ref, y1_ref, w_ref, g2_ref)
    xo_ref[...] = xn
    h_ref[...] = ((_rms(xn) * nw_ref[...]) * (1.0 + sc_ref[...]) + sh_ref[...]).astype(BF16)


def _combine_final_kernel(x_ref, y0_ref, y1_ref, w_ref, g2_ref, nf_ref, o_ref):
    o_ref[...] = _rms(_moe_mix(x_ref, y0_ref, y1_ref, w_ref, g2_ref)) * nf_ref[...]


def _token_specs(n_tokens, d):
    tiles = n_tokens // ROW_TILE
    return [pl.BlockSpec((ROW_TILE, d), lambda i: (i, 0)), pl.BlockSpec((ROW_TILE, d), lambda i: (i + tiles, 0)),
            pl.BlockSpec((ROW_TILE, TOP_K), lambda i: (i, 0))]


def _combine_next(x_new, ysel, wts, mod_l, nw_next, mod_next, rows):
    M, D = x_new.shape
    return pl.pallas_call(
        _combine_next_kernel,
        grid=(rows.n,),
        in_specs=[rows.row_spec(D), *_token_specs(M, D), rows.mod_spec(D, 5),
                  _const_spec((1, D)), rows.mod_spec(D, 0), rows.mod_spec(D, 1)],
        out_specs=[rows.row_spec(D), rows.row_spec(D)],
        out_shape=[jax.ShapeDtypeStruct((M, D), F32), jax.ShapeDtypeStruct((M, D), BF16)],
        compiler_params=_params("parallel"),
        name="combine_next",
    )(x_new, ysel, ysel, wts, mod_l, nw_next, mod_next, mod_next)


def _combine_final(x_new, ysel, wts, mod_l, nf, rows):
    M, D = x_new.shape
    n = wts.shape[0]
    return pl.pallas_call(
        _combine_final_kernel,
        grid=(rows.n,),
        in_specs=[rows.row_spec(D), *_token_specs(n, D), rows.mod_spec(D, 5), _const_spec((1, D))],
        out_specs=pl.BlockSpec((ROW_TILE, D), lambda i: (i, 0)),
        out_shape=jax.ShapeDtypeStruct((n, D), F32),
        compiler_params=_params("parallel"),
        name="combine_final",
    )(x_new, ysel, ysel, wts, mod_l, nf)


def _rope_tables(ctx_len, seq):
    quarter = HEAD_DIM // 4
    freqs = ROPE_BASE ** (-jnp.arange(quarter, dtype=F32) / quarter)
    pos = jnp.arange(seq, dtype=jnp.int32)
    row = (pos // GRID_W).astype(F32)[:, None] * freqs[None, :]
    col = (pos % GRID_W).astype(F32)[:, None] * freqs[None, :]
    cos = jnp.concatenate([jnp.cos(row), jnp.cos(col), jnp.cos(row), jnp.cos(col)], axis=-1)
    sin = jnp.concatenate([-jnp.sin(row), -jnp.sin(col), jnp.sin(row), jnp.sin(col)], axis=-1)
    cos = jnp.concatenate([jnp.ones((ctx_len, HEAD_DIM), F32), cos], axis=0)
    sin = jnp.concatenate([jnp.zeros((ctx_len, HEAD_DIM), F32), sin], axis=0)
    return cos, sin


def kernel(x, c, ctx, c_ctx, w_mod, b_mod, norm_mix, norm_ffn, w_in, qn_a, kn_a, sink_b, w_br_a, w_br_b, w_out, w_rg, b_rg, w_re, b_re, w1, w3, w2, norm_final):
    B, S, D = x.shape
    C = ctx.shape[1]
    L = w_mod.shape[0]
    T = C + S
    M = B * T
    assert C == ROW_TILE and S % ROW_TILE == 0 and T % PROJ_ROWS == 0 and B < MOD_ROWS
    tiles_per_batch = T // ROW_TILE

    cc = jnp.concatenate([c, c_ctx[None, :], jnp.zeros((MOD_ROWS - B - 1, D), F32)], axis=0)
    mod = _mod_vectors(cc, w_mod, b_mod).reshape(L, MOD_ROWS * 6, 1, D)
    cos, sin = _rope_tables(C, S)
    latent_rows = (jnp.arange(B, dtype=jnp.int32)[:, None] * T + C
                   + jnp.arange(S, dtype=jnp.int32)[None, :]).reshape(B * S)
    all_rows = jnp.arange(M, dtype=jnp.int32)

    mixer_kinds = lambda q_kind, k_kind: (tuple((q_kind, 0, k) for k in range(HEADS))
                                          + tuple((k_kind, 1, k) for k in range(KV_HEADS))
                                          + tuple(("v", 2, k) for k in range(KV_HEADS)))
    mixer_widths = [(WIDTH, 1), (KVW, 1), (KVW, 1)]
    gate_kinds = tuple(("gate", 0, k) for k in range(GATE_COLS // HEAD_DIM))
    col = {"a": 0, "b": WIDTH + 2 * KVW, "gates": 2 * WIDTH + 4 * KVW}

    every = _Rows(B, tiles_per_batch, latent_only=False)
    latent = _Rows(B, tiles_per_batch, latent_only=True)
    x_all, h = _norm_mod(x, ctx, norm_mix[0][None, :], mod[0])
    out = None
    for l in range(L):
        last = l == L - 1
        rows = latent if last else every
        proj = functools.partial(_proj, h, w_in, l, cos, sin, tiles_per_batch=tiles_per_batch)
        gains = jnp.stack([_pair_halves(qn_a[l]), _pair_halves(kn_a[l])])
        qa, ka, va = proj(gains, col["a"], mixer_kinds("q_norm", "k_norm"), mixer_widths, name="proj_a")
        qb, kb, vb = proj(gains, col["b"], mixer_kinds("q_rope", "k_rope"), mixer_widths, name="proj_b")
        (gates,) = proj(gains, col["gates"], gate_kinds, [(GATE_COLS, 2 * D // GATE_COLS)], name="proj_gates",
                        rows=GATE_ROWS)

        ya = _attention(qa, ka, va, None, B, T, C, last, window=False)
        yb = _attention(qb, kb, vb, sink_b[l], B, T, C, last, window=True)

        w_router = jnp.concatenate(
            [w_rg[l], w_re[l], jnp.zeros((D, ROUTER_LANES - N_GROUPS - N_EXPERTS), F32)], axis=1)
        w_router_hi = w_router.astype(BF16)
        w_router = jnp.concatenate([w_router_hi, (w_router - w_router_hi.astype(F32)).astype(BF16)], axis=1)
        b_router = jnp.concatenate(
            [b_rg[l], b_re[l], jnp.zeros((ROUTER_LANES - N_GROUPS - N_EXPERTS,), F32)])[None, :]
        x_new, h2, logits = _merge(ya, yb, gates, w_br_a[l].astype(BF16), w_br_b[l].astype(BF16),
                                   w_out[l].astype(BF16), x_all, mod[l], norm_ffn[l][None, :],
                                   w_router, b_router, rows)

        tok_rows = latent_rows if last else all_rows
        eid, wts = _route(logits[tok_rows] if last else logits)
        dest, *plan = _dispatch(eid, tok_rows, D // 2 // LANES)
        ybuf = _experts(h2, w1, w3, w2, l, *plan)
        ysel = ybuf[dest.reshape(-1, TOP_K).T.reshape(-1)]
        if last:
            out = _combine_final(x_new, ysel, wts, mod[l], norm_final[None, :], rows).reshape(B, S, D)
        else:
            x_all, h = _combine_next(x_new, ysel, wts, mod[l], norm_mix[l + 1][None, :], mod[l + 1], rows)
    return out
```

```python
import functools

import jax
import jax.numpy as jnp
from jax import lax
from jax.experimental import pallas as pl
from jax.experimental.pallas import tpu as pltpu

F32 = jnp.float32
BF16 = jnp.bfloat16

GRID_W = 64
HEAD_DIM = 128
LANES = 128
HEADS = 8
KV_HEADS = 2
GROUP = HEADS // KV_HEADS
WIDTH = HEADS * HEAD_DIM
KVW = KV_HEADS * HEAD_DIM
WINDOW = 128
ROPE_BASE = 10000.0
ATTN_SCALE = HEAD_DIM ** -0.5
LOG2E = 1.4426950408889634
N_GROUPS = 8
EXPERTS_PER_GROUP = 8
N_EXPERTS = N_GROUPS * EXPERTS_PER_GROUP
TOP_K = 2
EPS = 1e-6
NEG_INF = -1e30

ROW_TILE = 256
PROJ_ROWS = 768
GATE_ROWS = 1152
GATE_COLS = 1024
MOE_ROWS = 128
GATHER_AHEAD = 2
WEIGHT_DMA_PRIORITIES = (1, 1, 0)
ROUTER_LANES = 128
DENSE_CHAIN_HEADS = 4
WINDOW_CHAIN_HEADS = 1
MOD_ROWS = 8
VMEM_LIMIT = 56 * 1024 * 1024


def _params(*sem):
    return pltpu.CompilerParams(dimension_semantics=sem, vmem_limit_bytes=VMEM_LIMIT)


def _rms(y):
    return y * lax.rsqrt(jnp.mean(y * y, axis=-1, keepdims=True) + EPS)


def _mod_kernel(c_ref, w_ref, b_ref, o_ref):
    c = c_ref[...]
    a = (c * jax.nn.sigmoid(c)).astype(BF16)
    o_ref[...] = jnp.dot(a, w_ref[...].astype(BF16), preferred_element_type=F32) + b_ref[...]


def _mod_vectors(cc, w_mod, b_mod):
    L, D, N = w_mod.shape
    tn = 1024
    return pl.pallas_call(
        _mod_kernel,
        grid=(L, N // tn),
        in_specs=[
            pl.BlockSpec((MOD_ROWS, D), lambda l, j: (0, 0)),
            pl.BlockSpec((None, D, tn), lambda l, j: (l, 0, j)),
            pl.BlockSpec((None, 1, tn), lambda l, j: (l, 0, j)),
        ],
        out_specs=pl.BlockSpec((None, MOD_ROWS, tn), lambda l, j: (l, 0, j)),
        out_shape=jax.ShapeDtypeStruct((L, MOD_ROWS, N), F32),
        compiler_params=_params("parallel", "parallel"),
        name="mod_vectors",
    )(cc, w_mod, b_mod.reshape(L, 1, N))


class _Rows:
    def __init__(self, batch, tiles_per_batch, latent_only):
        self.batch = batch
        self.tpb = tiles_per_batch
        self.latent_only = latent_only
        self.n = batch * (tiles_per_batch - 1 if latent_only else tiles_per_batch)

    def tile(self, i):
        if self.latent_only:
            per = self.tpb - 1
            return (i // per) * self.tpb + 1 + i % per
        return i

    def mod_row(self, i):
        if self.latent_only:
            return i // (self.tpb - 1)
        return jnp.where(i % self.tpb == 0, self.batch, i // self.tpb)

    def row_spec(self, width):
        return pl.BlockSpec((ROW_TILE, width), lambda i: (self.tile(i), 0))

    def mod_spec(self, d, k):
        return pl.BlockSpec((None, 1, d), lambda i: (self.mod_row(i) * 6 + k, 0, 0))


def _const_spec(shape):
    zeros = (0,) * len(shape)
    return pl.BlockSpec(shape, lambda *_: zeros, pipeline_mode=pl.Buffered(1))


def _norm_mod_kernel(x_ref, ctx_ref, nw_ref, sh_ref, sc_ref, xo_ref, h_ref):
    def emit(src_ref):
        x = src_ref[...]
        xo_ref[...] = x
        h_ref[...] = ((_rms(x) * nw_ref[...]) * (1.0 + sc_ref[...]) + sh_ref[...]).astype(BF16)

    is_ctx = pl.program_id(1) == 0
    pl.when(is_ctx)(lambda: emit(ctx_ref))
    pl.when(jnp.logical_not(is_ctx))(lambda: emit(x_ref))


def _norm_mod(x, ctx, nw, mod_l):
    B, S, D = x.shape
    tiles = 1 + S // ROW_TILE
    mod_spec = lambda k: pl.BlockSpec((None, 1, D), lambda b, t: (jnp.where(t == 0, B, b) * 6 + k, 0, 0))
    out_spec = pl.BlockSpec((ROW_TILE, D), lambda b, t: (b * tiles + t, 0))
    return pl.pallas_call(
        _norm_mod_kernel,
        grid=(B, tiles),
        in_specs=[pl.BlockSpec((None, ROW_TILE, D), lambda b, t: (b, jnp.maximum(t - 1, 0), 0)),
                  pl.BlockSpec((None, ROW_TILE, D), lambda b, t: (b, 0, 0)),
                  _const_spec((1, D)), mod_spec(0), mod_spec(1)],
        out_specs=[out_spec, out_spec],
        out_shape=[jax.ShapeDtypeStruct((B * tiles * ROW_TILE, D), F32),
                   jax.ShapeDtypeStruct((B * tiles * ROW_TILE, D), BF16)],
        compiler_params=_params("parallel", "parallel"),
        name="norm_mod",
    )(x, ctx, nw, mod_l, mod_l)


ROTARY_KINDS = ("q_norm", "k_norm", "q_rope", "k_rope")


def _pair_halves(a):
    q = HEAD_DIM // 4
    return jnp.concatenate([a[..., 0:q], a[..., 2 * q:3 * q], a[..., q:2 * q], a[..., 3 * q:]], axis=-1)


def _rope(y, cos, sin):
    return y * cos + pltpu.roll(y, HEAD_DIM // 2, 1) * sin


def _proj_kernel(h_ref, w_ref, cos_ref, sin_ref, gain_ref, *refs, kinds):
    *out_refs, wb_ref = refs

    @pl.when(pl.program_id(1) == 0)
    def _():
        for c, (kind, _, _) in enumerate(kinds):
            w = w_ref[:, c * HEAD_DIM:(c + 1) * HEAD_DIM]
            wb_ref[:, c * HEAD_DIM:(c + 1) * HEAD_DIM] = (_pair_halves(w) if kind in ROTARY_KINDS else w).astype(BF16)

    acc = jnp.dot(h_ref[...], wb_ref[...], preferred_element_type=F32)
    for c, (kind, out_idx, out_chunk) in enumerate(kinds):
        y = acc[:, c * HEAD_DIM:(c + 1) * HEAD_DIM]
        if kind == "q_norm":
            y = _rms(y) * gain_ref[0:1, :]
        if kind == "k_norm":
            y = _rms(y) * gain_ref[1:2, :]
        if kind in ROTARY_KINDS:
            y = _rope(y, cos_ref[...], sin_ref[...])
        if kind in ("q_norm", "q_rope"):
            y = y * (ATTN_SCALE * LOG2E)
        out_refs[out_idx][:, out_chunk * HEAD_DIM:(out_chunk + 1) * HEAD_DIM] = y.astype(BF16)


def _proj(h, w_in, layer, cos, sin, gain, col0, kinds, out_widths, tiles_per_batch, name, rows=PROJ_ROWS):
    M, D = h.shape
    n_col_tiles = out_widths[0][1]
    cols = len(kinds) * HEAD_DIM
    cb0 = col0 // cols
    proj_tiles_per_batch = tiles_per_batch * ROW_TILE // rows
    return pl.pallas_call(
        functools.partial(_proj_kernel, kinds=kinds),
        grid=(n_col_tiles, M // rows),
        in_specs=[
            pl.BlockSpec((rows, D), lambda j, i: (i, 0)),
            pl.BlockSpec((None, D, cols), lambda j, i: (layer, 0, cb0 + j),
                         pipeline_mode=pl.Buffered(1 if n_col_tiles == 1 else 2)),
            pl.BlockSpec((rows, HEAD_DIM), lambda j, i: (i % proj_tiles_per_batch, 0)),
            pl.BlockSpec((rows, HEAD_DIM), lambda j, i: (i % proj_tiles_per_batch, 0)),
            pl.BlockSpec((2, HEAD_DIM), lambda j, i: (0, 0)),
        ],
        out_specs=[pl.BlockSpec((rows, w), lambda j, i: (i, j)) for w, _ in out_widths],
        out_shape=[jax.ShapeDtypeStruct((M, w * n), BF16) for w, n in out_widths],
        scratch_shapes=[pltpu.VMEM((D, cols), BF16)],
        compiler_params=_params("parallel", "arbitrary"),
        name=name,
    )(h, w_in, cos, sin, gain)


def _stack_heads(q_ref, g0, n):
    return jnp.concatenate([q_ref[:, g * HEAD_DIM:(g + 1) * HEAD_DIM] for g in range(g0, g0 + n)], axis=0)


def _write_heads(o_ref, g0, n, o):
    rows = o_ref.shape[0]
    for j in range(n):
        o_ref[:, (g0 + j) * HEAD_DIM:(g0 + j + 1) * HEAD_DIM] = o[j * rows:(j + 1) * rows].astype(o_ref.dtype)


def _scores(q, k):
    return lax.dot_general(q, k, (((1,), (1,)), ((), ())), preferred_element_type=F32)


def _online_step(q, k, v, m, l, acc):
    s = _scores(q, k)
    m_new = jnp.maximum(m, jnp.max(s, axis=-1, keepdims=True))
    alpha = jnp.exp2(m - m_new)
    p = jnp.exp2(s - m_new)
    l = alpha * l + jnp.sum(p, axis=-1, keepdims=True)
    acc = alpha * acc + jnp.dot(p.astype(BF16), v, preferred_element_type=F32)
    return m_new, l, acc


def _dense_attn_kernel(q_ref, k_ref, v_ref, o_ref, *, ctx_len, total_len, key_chunk, ctx_tile):
    def run(n_keys, chunk):
        for g0 in range(0, GROUP, DENSE_CHAIN_HEADS):
            q = _stack_heads(q_ref, g0, DENSE_CHAIN_HEADS)
            r = q.shape[0]
            m = jnp.full((r, 1), -jnp.inf, F32)
            l = jnp.zeros((r, 1), F32)
            acc = jnp.zeros((r, HEAD_DIM), F32)
            for c in range(n_keys // chunk):
                m, l, acc = _online_step(q, k_ref[c * chunk:(c + 1) * chunk, :],
                                         v_ref[c * chunk:(c + 1) * chunk, :], m, l, acc)
            _write_heads(o_ref, g0, DENSE_CHAIN_HEADS, acc * (1.0 / l))

    if ctx_tile:
        is_ctx = pl.program_id(2) == 0
        pl.when(is_ctx)(lambda: run(ctx_len, ctx_len))
        pl.when(jnp.logical_not(is_ctx))(lambda: run(total_len, key_chunk))
    else:
        run(total_len, key_chunk)


def _window_attn_kernel(sink_ref, q_ref, k_ref, v_ref, o_ref, *, ctx_len, total_len, ctx_tile, q_off):
    kvh = pl.program_id(1)
    tq = q_ref.shape[0]
    span = tq + 2 * WINDOW
    n = WINDOW_CHAIN_HEADS

    def attend(parts):
        for g0 in range(0, GROUP, n):
            q = _stack_heads(q_ref, g0, n)
            sink = jnp.concatenate(
                [jnp.full((tq, 1), sink_ref[kvh * GROUP + g] * LOG2E, F32) for g in range(g0, g0 + n)], axis=0)
            scores = []
            m = sink
            for k, _, bias in parts:
                s = _scores(q, k)
                if bias is not None:
                    s = s + bias
                scores.append(s)
                m = jnp.maximum(m, jnp.max(s, axis=-1, keepdims=True))
            l = jnp.exp2(sink - m)
            o = jnp.zeros((n * tq, HEAD_DIM), F32)
            for s, (_, v, _) in zip(scores, parts):
                p = jnp.exp2(s - m)
                l = l + jnp.sum(p, axis=-1, keepdims=True)
                o = o + jnp.dot(p.astype(BF16), v, preferred_element_type=F32)
            _write_heads(o_ref, g0, n, o * (1.0 / l))

    def ctx_part():
        return k_ref[0:ctx_len, :], v_ref[0:ctx_len, :], None

    def run_latent():
        q0 = (pl.program_id(2) + q_off) * tq - ctx_len
        n_latent = total_len - ctx_len
        start = pl.multiple_of(jnp.clip(q0 - WINDOW, 0, n_latent - span), WINDOW)
        row = lax.broadcasted_iota(jnp.int32, (n * tq, span), 0) & (tq - 1)
        col = lax.broadcasted_iota(jnp.int32, (n * tq, span), 1)
        bias = jnp.where(jnp.abs(col - row + (start - q0)) <= WINDOW, 0.0, NEG_INF).astype(F32)
        attend([ctx_part(), (k_ref[pl.ds(ctx_len + start, span), :], v_ref[pl.ds(ctx_len + start, span), :], bias)])

    if ctx_tile:
        is_ctx = pl.program_id(2) == 0
        pl.when(is_ctx)(lambda: attend([ctx_part()]))
        pl.when(jnp.logical_not(is_ctx))(run_latent)
    else:
        run_latent()


def _attention(q, k, v, sink, batch, total_len, ctx_len, latent_only, window):
    M = q.shape[0]
    q3, k3, v3 = (a.reshape(batch, total_len, a.shape[1]) for a in (q, k, v))
    tiles = total_len // ROW_TILE
    q_off = 1 if latent_only else 0
    gw = GROUP * HEAD_DIM
    q_spec = pl.BlockSpec((None, ROW_TILE, gw), lambda b, h, i, *_: (b, i + q_off, h))
    kv_spec = pl.BlockSpec((None, total_len, HEAD_DIM), lambda b, h, i, *_: (b, 0, h))
    grid = (batch, KV_HEADS, tiles - q_off)
    out_shape = jax.ShapeDtypeStruct(q3.shape, BF16)
    if window:
        kern = functools.partial(_window_attn_kernel, ctx_len=ctx_len, total_len=total_len,
                                 ctx_tile=not latent_only, q_off=q_off)
        out = pl.pallas_call(
            kern,
            grid_spec=pltpu.PrefetchScalarGridSpec(
                num_scalar_prefetch=1, grid=grid,
                in_specs=[q_spec, kv_spec, kv_spec], out_specs=q_spec),
            out_shape=out_shape,
            compiler_params=_params("parallel", "parallel", "parallel"),
            name="window_attention",
        )(sink, q3, k3, v3)
    else:
        kern = functools.partial(_dense_attn_kernel, ctx_len=ctx_len, total_len=total_len,
                                 key_chunk=total_len // 3, ctx_tile=not latent_only)
        out = pl.pallas_call(
            kern, grid=grid, in_specs=[q_spec, kv_spec, kv_spec], out_specs=q_spec,
            out_shape=out_shape,
            compiler_params=_params("parallel", "parallel", "parallel"),
            name="dense_attention",
        )(q3, k3, v3)
    return out.reshape(M, q.shape[1])


def _merge_kernel(ya_ref, yb_ref, ga_ref, gb_ref, wa_ref, wb_ref, wo_ref, x_ref, g1_ref,
                  nw_ref, sh_ref, sc_ref, wr_ref, br_ref, xo_ref, h_ref, lg_ref):
    a = jnp.dot(ya_ref[...], wa_ref[...], preferred_element_type=F32)
    b = jnp.dot(yb_ref[...], wb_ref[...], preferred_element_type=F32)
    m = jax.nn.sigmoid(ga_ref[...].astype(F32)) * a + jax.nn.sigmoid(gb_ref[...].astype(F32)) * b
    o = jnp.dot(m.astype(BF16), wo_ref[...], preferred_element_type=F32)
    xn = x_ref[...] + g1_ref[...] * o
    xo_ref[...] = xn
    h = (_rms(xn) * nw_ref[...]) * (1.0 + sc_ref[...]) + sh_ref[...]
    h_hi = h.astype(BF16)
    bits = pltpu.bitcast(h_hi.astype(F32), jnp.uint32)
    half = h.shape[1] // 2
    words = (bits[:, :half] >> 16) | (bits[:, half:] & jnp.uint32(0xFFFF0000))
    chunks = half // LANES
    for c in range(chunks):
        h_ref[pl.ds(c, h.shape[0], stride=chunks), :] = words[:, c * LANES:(c + 1) * LANES]
    h_lo = (h - h_hi.astype(F32)).astype(BF16)
    lg = jnp.dot(h_hi, wr_ref[...], preferred_element_type=F32)
    lg_lo = jnp.dot(h_lo, wr_ref[:, :ROUTER_LANES], preferred_element_type=F32)
    lg_ref[...] = _route(lg[:, :ROUTER_LANES] + (lg[:, ROUTER_LANES:] + lg_lo) + br_ref[...])


def _route(logits):
    lane = lax.broadcasted_iota(jnp.int32, logits.shape, 1)
    first = lambda hit: jnp.min(jnp.where(hit, lane, ROUTER_LANES), axis=-1, keepdims=True)
    is_group = lane < N_GROUPS
    gl = jnp.where(is_group, logits, -jnp.inf)
    g_max = jnp.max(gl, axis=-1, keepdims=True)
    g = first(gl == g_max)
    p_group = 1.0 / jnp.sum(jnp.where(is_group, jnp.exp(logits - g_max), 0.0), axis=-1, keepdims=True)
    lo = N_GROUPS + g * EXPERTS_PER_GROUP
    el = jnp.where(jnp.logical_and(lane >= lo, lane < lo + EXPERTS_PER_GROUP), logits, -jnp.inf)
    v0 = jnp.max(el, axis=-1, keepdims=True)
    i0 = first(el == v0)
    el = jnp.where(lane == i0, -jnp.inf, el)
    v1 = jnp.max(el, axis=-1, keepdims=True)
    i1 = first(el == v1)
    r = jnp.exp(v1 - v0)
    w0 = p_group / (1.0 + r)
    out = jnp.where(lane == 0, (i0 - N_GROUPS).astype(F32), 0.0)
    out = jnp.where(lane == 1, (i1 - N_GROUPS).astype(F32), out)
    out = jnp.where(lane == 2, w0, out)
    return jnp.where(lane == 3, w0 * r, out)


def _merge(ya, yb, gates, wa, wb, wo, x_all, mod_l, nw, wr, br, rows):
    M, D = x_all.shape
    W = ya.shape[1]
    chunks = D // 2 // LANES
    return pl.pallas_call(
        _merge_kernel,
        grid=(rows.n,),
        in_specs=[
            rows.row_spec(W), rows.row_spec(W),
            pl.BlockSpec((ROW_TILE, D), lambda i: (rows.tile(i), 0)),
            pl.BlockSpec((ROW_TILE, D), lambda i: (rows.tile(i), 1)),
            _const_spec((W, D)), _const_spec((W, D)), _const_spec((D, D)),
            rows.row_spec(D), rows.mod_spec(D, 2),
            _const_spec((1, D)), rows.mod_spec(D, 3), rows.mod_spec(D, 4),
            _const_spec((D, 2 * ROUTER_LANES)), _const_spec((1, ROUTER_LANES)),
        ],
        out_specs=[rows.row_spec(D),
                   pl.BlockSpec((ROW_TILE * chunks, LANES), lambda i: (rows.tile(i), 0)),
                   rows.row_spec(ROUTER_LANES)],
        out_shape=[jax.ShapeDtypeStruct((M, D), F32), jax.ShapeDtypeStruct((M * chunks, LANES), jnp.uint32),
                   jax.ShapeDtypeStruct((M, ROUTER_LANES), F32)],
        compiler_params=_params("parallel"),
        name="merge",
    )(ya, yb, gates, gates, wa, wb, wo, x_all, mod_l, nw, mod_l, mod_l, wr, br)


def _dispatch(eid, tok_rows, row_scale):
    n = eid.shape[0]
    a = n * TOP_K
    e_flat = eid.reshape(a)
    onehot = (e_flat[:, None] == jnp.arange(N_EXPERTS, dtype=jnp.int32)[None, :]).astype(jnp.int32)
    csum = jnp.cumsum(onehot, axis=0)
    counts = csum[-1]
    padded = (counts + MOE_ROWS - 1) // MOE_ROWS * MOE_ROWS
    pad_end = jnp.cumsum(padded)
    pad_start = pad_end - padded
    dest = jnp.sum(onehot * (csum - 1 + pad_start[None, :]), axis=1)
    n_grid = a // MOE_ROWS + N_EXPERTS - 1 + GATHER_AHEAD
    _, sorted_row = lax.sort((e_flat, jnp.repeat(tok_rows * row_scale, TOP_K)), num_keys=1, is_stable=True)
    sorted_row = jnp.concatenate([sorted_row, jnp.zeros((MOE_ROWS,), jnp.int32)])
    blk = jnp.arange(n_grid, dtype=jnp.int32)
    n_valid = (pad_end[-1] // MOE_ROWS).astype(jnp.int32)
    blk_c = jnp.minimum(blk, n_valid - 1)
    blk_e = jnp.sum((pad_end[None, :] <= (blk_c * MOE_ROWS)[:, None]).astype(jnp.int32), axis=1)
    blk_e = jnp.minimum(blk_e, N_EXPERTS - 1)
    start = jnp.cumsum(counts) - counts
    blk_first = jnp.where(blk < n_valid, start[blk_e] + blk * MOE_ROWS - pad_start[blk_e], 0).astype(jnp.int32)
    experts = jnp.arange(N_EXPERTS, dtype=jnp.int32)
    used = counts > 0
    first_used_from = lax.cummin(jnp.where(used, experts, N_EXPERTS), reverse=True)
    next_used = jnp.concatenate([first_used_from[1:], jnp.full((1,), N_EXPERTS, jnp.int32)])
    next_used = jnp.where(next_used == N_EXPERTS, -1, next_used)
    parity = (jnp.cumsum(used.astype(jnp.int32)) - 1) % 2
    after_next = jnp.where(next_used >= 0, next_used[jnp.maximum(next_used, 0)], -1)
    return (dest, blk_e, after_next[blk_e].astype(jnp.int32), parity[blk_e].astype(jnp.int32), blk_first,
            sorted_row, next_used[blk_e[:1]].astype(jnp.int32), n_valid.reshape(1))


def _experts_kernel(blk_e, blk_after_next, blk_par, blk_first, sorted_row, second_expert, n_valid,
                    h_hbm, w1_hbm, w3_hbm, w2_hbm, o_ref,
                    xg, xsem, ws1, ws3, ws2, wsem, w1b, w3b, w2b, *, layer):
    i = pl.program_id(0)
    nv = n_valid[0]
    chunks = xg.shape[1] // MOE_ROWS
    e = blk_e[i]
    par = blk_par[i]
    fresh = jnp.logical_or(i == 0, e != blk_e[jnp.maximum(i - 1, 0)])

    def weight_copies(expert, p):
        return (pltpu.make_async_copy(w1_hbm.at[layer, expert], ws1.at[p], wsem.at[p, 0]),
                pltpu.make_async_copy(w3_hbm.at[layer, expert], ws3.at[p], wsem.at[p, 1]),
                pltpu.make_async_copy(w2_hbm.at[layer, expert], ws2.at[p], wsem.at[p, 2]))

    def for_rows(blk, fn):
        buf = lax.rem(blk, GATHER_AHEAD + 1)
        base = blk_first[blk]
        for j in range(MOE_ROWS):
            r = pl.multiple_of(sorted_row[base + j], chunks)
            fn(pltpu.make_async_copy(h_hbm.at[pl.ds(r, chunks), :], xg.at[buf, pl.ds(j * chunks, chunks), :],
                                     xsem.at[buf]))

    def start_weights(expert, p):
        for cp, priority in zip(weight_copies(expert, p), WEIGHT_DMA_PRIORITIES):
            cp.start(priority=priority)

    @pl.when(i == 0)
    def _():
        start_weights(e, par)
        pl.when(second_expert[0] >= 0)(lambda: start_weights(second_expert[0], 1 - par))
        for blk in range(GATHER_AHEAD):
            for_rows(blk, lambda cp: cp.start())

    @pl.when(jnp.logical_and(i >= nv, i < nv + GATHER_AHEAD))
    def _():
        for_rows(i, lambda cp: cp.wait())

    @pl.when(fresh)
    def _():
        for cp in weight_copies(e, par):
            cp.wait()
        w1b[...] = ws1[par].astype(BF16)
        w3b[...] = ws3[par].astype(BF16)
        w2b[...] = ws2[par].astype(BF16)
        pl.when(blk_after_next[i] >= 0)(lambda: start_weights(blk_after_next[i], par))

    @pl.when(i < nv)
    def _():
        for_rows(i, lambda cp: cp.wait())
        words = jnp.concatenate(
            [xg[lax.rem(i, GATHER_AHEAD + 1), pl.ds(c, MOE_ROWS, stride=chunks), :] for c in range(chunks)],
            axis=1)
        x = jnp.concatenate([pltpu.bitcast(words << 16, F32).astype(BF16),
                             pltpu.bitcast(words & jnp.uint32(0xFFFF0000), F32).astype(BF16)], axis=1)
        for_rows(i + GATHER_AHEAD, lambda cp: cp.start())
        a = jnp.dot(x, w1b[...], preferred_element_type=F32)
        b = jnp.dot(x, w3b[...], preferred_element_type=F32)
        hid = (a * jax.nn.sigmoid(a)) * b
        o_ref[...] = jnp.dot(hid.astype(BF16), w2b[...], preferred_element_type=F32).astype(o_ref.dtype)


def _experts(h2, w1, w3, w2, layer, blk_e, blk_after_next, blk_par, blk_first, sorted_row, second_expert, n_valid):
    D, de = w1.shape[-2:]
    chunks = D // 2 // LANES
    n_blk = blk_e.shape[0]
    any_spec = pl.BlockSpec(memory_space=pl.ANY)
    return pl.pallas_call(
        functools.partial(_experts_kernel, layer=layer),
        grid_spec=pltpu.PrefetchScalarGridSpec(
            num_scalar_prefetch=7, grid=(n_blk,),
            in_specs=[any_spec, any_spec, any_spec, any_spec],
            out_specs=pl.BlockSpec((MOE_ROWS, D), lambda i, *s: (jnp.minimum(i, s[6][0] - 1), 0)),
            scratch_shapes=[
                pltpu.VMEM((GATHER_AHEAD + 1, MOE_ROWS * chunks, LANES), jnp.uint32),
                pltpu.SemaphoreType.DMA((GATHER_AHEAD + 1,)),
                pltpu.VMEM((2, D, de), F32), pltpu.VMEM((2, D, de), F32), pltpu.VMEM((2, de, D), F32),
                pltpu.SemaphoreType.DMA((2, 3)),
                pltpu.VMEM((D, de), BF16), pltpu.VMEM((D, de), BF16), pltpu.VMEM((de, D), BF16),
            ],
        ),
        out_shape=jax.ShapeDtypeStruct((n_blk * MOE_ROWS, D), BF16),
        compiler_params=_params("arbitrary"),
        name="experts",
    )(blk_e, blk_after_next, blk_par, blk_first, sorted_row, second_expert, n_valid, h2, w1, w3, w2)


def _moe_mix(x_ref, y0_ref, y1_ref, w_ref, g2_ref):
    w = w_ref[...]
    y = w[:, 0:1] * y0_ref[...].astype(F32) + w[:, 1:2] * y1_ref[...].astype(F32)
    return x_ref[...] + g2_ref[...] * y


def _combine_next_kernel(x_ref, y0_ref, y1_ref, w_ref, g2_ref, nw_ref, sh_ref, sc_ref, xo_ref, h_ref):
    xn = _moe_mix(x_ref, y0_ref, y1_ref, w_ref, g2_ref)
    xo_ref[...] = xn
    h_ref[...] = ((_rms(xn) * nw_ref[...]) * (1.0 + sc_ref[...]) + sh_ref[...]).astype(BF16)


def _combine_final_kernel(x_ref, y0_ref, y1_ref, w_ref, g2_ref, nf_ref, o_ref):
    o_ref[...] = _rms(_moe_mix(x_ref, y0_ref, y1_ref, w_ref, g2_ref)) * nf_ref[...]


def _token_specs(n_tokens, d):
    tiles = n_tokens // ROW_TILE
    return [pl.BlockSpec((ROW_TILE, d), lambda i: (i, 0)), pl.BlockSpec((ROW_TILE, d), lambda i: (i + tiles, 0)),
            pl.BlockSpec((ROW_TILE, TOP_K), lambda i: (i, 0))]


def _combine_next(x_new, ysel, wts, mod_l, nw_next, mod_next, rows):
    M, D = x_new.shape
    return pl.pallas_call(
        _combine_next_kernel,
        grid=(rows.n,),
        in_specs=[rows.row_spec(D), *_token_specs(M, D), rows.mod_spec(D, 5),
                  _const_spec((1, D)), rows.mod_spec(D, 0), rows.mod_spec(D, 1)],
        out_specs=[rows.row_spec(D), rows.row_spec(D)],
        out_shape=[jax.ShapeDtypeStruct((M, D), F32), jax.ShapeDtypeStruct((M, D), BF16)],
        compiler_params=_params("parallel"),
        name="combine_next",
    )(x_new, ysel, ysel, wts, mod_l, nw_next, mod_next, mod_next)


def _combine_final(x_new, ysel, wts, mod_l, nf, rows):
    M, D = x_new.shape
    n = wts.shape[0]
    return pl.pallas_call(
        _combine_final_kernel,
        grid=(rows.n,),
        in_specs=[rows.row_spec(D), *_token_specs(n, D), rows.mod_spec(D, 5), _const_spec((1, D))],
        out_specs=pl.BlockSpec((ROW_TILE, D), lambda i: (i, 0)),
        out_shape=jax.ShapeDtypeStruct((n, D), F32),
        compiler_params=_params("parallel"),
        name="combine_final",
    )(x_new, ysel, ysel, wts, mod_l, nf)


def _rope_tables(ctx_len, seq):
    quarter = HEAD_DIM // 4
    freqs = ROPE_BASE ** (-jnp.arange(quarter, dtype=F32) / quarter)
    pos = jnp.arange(seq, dtype=jnp.int32)
    row = (pos // GRID_W).astype(F32)[:, None] * freqs[None, :]
    col = (pos % GRID_W).astype(F32)[:, None] * freqs[None, :]
    cos = jnp.concatenate([jnp.cos(row), jnp.cos(col), jnp.cos(row), jnp.cos(col)], axis=-1)
    sin = jnp.concatenate([-jnp.sin(row), -jnp.sin(col), jnp.sin(row), jnp.sin(col)], axis=-1)
    cos = jnp.concatenate([jnp.ones((ctx_len, HEAD_DIM), F32), cos], axis=0)
    sin = jnp.concatenate([jnp.zeros((ctx_len, HEAD_DIM), F32), sin], axis=0)
    return cos, sin


def kernel(x, c, ctx, c_ctx, w_mod, b_mod, norm_mix, norm_ffn, w_in, qn_a, kn_a, sink_b, w_br_a, w_br_b, w_out, w_rg, b_rg, w_re, b_re, w1, w3, w2, norm_final):
    B, S, D = x.shape
    C = ctx.shape[1]
    L = w_mod.shape[0]
    T = C + S
    M = B * T
    assert C == ROW_TILE and S % ROW_TILE == 0 and T % PROJ_ROWS == 0 and B < MOD_ROWS
    tiles_per_batch = T // ROW_TILE

    cc = jnp.concatenate([c, c_ctx[None, :], jnp.zeros((MOD_ROWS - B - 1, D), F32)], axis=0)
    mod = _mod_vectors(cc, w_mod, b_mod).reshape(L, MOD_ROWS * 6, 1, D)
    cos, sin = _rope_tables(C, S)
    latent_rows = (jnp.arange(B, dtype=jnp.int32)[:, None] * T + C
                   + jnp.arange(S, dtype=jnp.int32)[None, :]).reshape(B * S)
    all_rows = jnp.arange(M, dtype=jnp.int32)

    mixer_kinds = lambda q_kind, k_kind: (tuple((q_kind, 0, k) for k in range(HEADS))
                                          + tuple((k_kind, 1, k) for k in range(KV_HEADS))
                                          + tuple(("v", 2, k) for k in range(KV_HEADS)))
    mixer_widths = [(WIDTH, 1), (KVW, 1), (KVW, 1)]
    gate_kinds = tuple(("gate", 0, k) for k in range(GATE_COLS // HEAD_DIM))
    col = {"a": 0, "b": WIDTH + 2 * KVW, "gates": 2 * WIDTH + 4 * KVW}

    every = _Rows(B, tiles_per_batch, latent_only=False)
    latent = _Rows(B, tiles_per_batch, latent_only=True)
    x_all, h = _norm_mod(x, ctx, norm_mix[0][None, :], mod[0])
    out = None
    for l in range(L):
        last = l == L - 1
        rows = latent if last else every
        proj = functools.partial(_proj, h, w_in, l, cos, sin, tiles_per_batch=tiles_per_batch)
        gains = jnp.stack([_pair_halves(qn_a[l]), _pair_halves(kn_a[l])])
        qa, ka, va = proj(gains, col["a"], mixer_kinds("q_norm", "k_norm"), mixer_widths, name="proj_a")
        qb, kb, vb = proj(gains, col["b"], mixer_kinds("q_rope", "k_rope"), mixer_widths, name="proj_b")
        (gates,) = proj(gains, col["gates"], gate_kinds, [(GATE_COLS, 2 * D // GATE_COLS)], name="proj_gates",
                        rows=GATE_ROWS)

        ya = _attention(qa, ka, va, None, B, T, C, last, window=False)
        yb = _attention(qb, kb, vb, sink_b[l], B, T, C, last, window=True)

        w_router = jnp.concatenate(
            [w_rg[l], w_re[l], jnp.zeros((D, ROUTER_LANES - N_GROUPS - N_EXPERTS), F32)], axis=1)
        w_router_hi = w_router.astype(BF16)
        w_router = jnp.concatenate([w_router_hi, (w_router - w_router_hi.astype(F32)).astype(BF16)], axis=1)
        b_router = jnp.concatenate(
            [b_rg[l], b_re[l], jnp.zeros((ROUTER_LANES - N_GROUPS - N_EXPERTS,), F32)])[None, :]
        x_new, h2, route = _merge(ya, yb, gates, w_br_a[l].astype(BF16), w_br_b[l].astype(BF16),
                                   w_out[l].astype(BF16), x_all, mod[l], norm_ffn[l][None, :],
                                   w_router, b_router, rows)

        tok_rows = latent_rows if last else all_rows
        route = route[tok_rows] if last else route
        eid, wts = route[:, :TOP_K].astype(jnp.int32), route[:, TOP_K:2 * TOP_K]
        dest, *plan = _dispatch(eid, tok_rows, D // 2 // LANES)
        ybuf = _experts(h2, w1, w3, w2, l, *plan)
        ysel = ybuf[dest.reshape(-1, TOP_K).T.reshape(-1)]
        if last:
            out = _combine_final(x_new, ysel, wts, mod[l], norm_final[None, :], rows).reshape(B, S, D)
        else:
            x_all, h = _combine_next(x_new, ysel, wts, mod[l], norm_mix[l + 1][None, :], mod[l + 1], rows)
    return out
```

```python
import functools

import jax
import jax.numpy as jnp
from jax import lax
from jax.experimental import pallas as pl
from jax.experimental.pallas import tpu as pltpu

F32 = jnp.float32
BF16 = jnp.bfloat16

GRID_W = 64
HEAD_DIM = 128
LANES = 128
HEADS = 8
KV_HEADS = 2
GROUP = HEADS // KV_HEADS
WIDTH = HEADS * HEAD_DIM
KVW = KV_HEADS * HEAD_DIM
WINDOW = 128
ROPE_BASE = 10000.0
ATTN_SCALE = HEAD_DIM ** -0.5
LOG2E = 1.4426950408889634
N_GROUPS = 8
EXPERTS_PER_GROUP = 8
N_EXPERTS = N_GROUPS * EXPERTS_PER_GROUP
TOP_K = 2
EPS = 1e-6
NEG_INF = -1e30

ROW_TILE = 256
PROJ_ROWS = 768
GATE_ROWS = 1152
GATE_COLS = 1024
MOE_ROWS = 128
GATHER_AHEAD = 2
WEIGHT_DMA_PRIORITIES = (1, 1, 0)
ROUTER_LANES = 128
DENSE_CHAIN_HEADS = 4
WINDOW_CHAIN_HEADS = 1
MOD_ROWS = 8
VMEM_LIMIT = 56 * 1024 * 1024


def _params(*sem):
    return pltpu.CompilerParams(dimension_semantics=sem, vmem_limit_bytes=VMEM_LIMIT)


def _rms(y):
    return y * lax.rsqrt(jnp.mean(y * y, axis=-1, keepdims=True) + EPS)


def _mod_kernel(c_ref, w_ref, b_ref, o_ref):
    c = c_ref[...]
    a = (c * jax.nn.sigmoid(c)).astype(BF16)
    o_ref[...] = jnp.dot(a, w_ref[...].astype(BF16), preferred_element_type=F32) + b_ref[...]


def _mod_vectors(cc, w_mod, b_mod):
    L, D, N = w_mod.shape
    tn = 1024
    return pl.pallas_call(
        _mod_kernel,
        grid=(L, N // tn),
        in_specs=[
            pl.BlockSpec((MOD_ROWS, D), lambda l, j: (0, 0)),
            pl.BlockSpec((None, D, tn), lambda l, j: (l, 0, j)),
            pl.BlockSpec((None, 1, tn), lambda l, j: (l, 0, j)),
        ],
        out_specs=pl.BlockSpec((None, MOD_ROWS, tn), lambda l, j: (l, 0, j)),
        out_shape=jax.ShapeDtypeStruct((L, MOD_ROWS, N), F32),
        compiler_params=_params("parallel", "parallel"),
        name="mod_vectors",
    )(cc, w_mod, b_mod.reshape(L, 1, N))


class _Rows:
    def __init__(self, batch, tiles_per_batch, latent_only):
        self.batch = batch
        self.tpb = tiles_per_batch
        self.latent_only = latent_only
        self.n = batch * (tiles_per_batch - 1 if latent_only else tiles_per_batch)

    def tile(self, i):
        if self.latent_only:
            per = self.tpb - 1
            return (i // per) * self.tpb + 1 + i % per
        return i

    def mod_row(self, i):
        if self.latent_only:
            return i // (self.tpb - 1)
        return jnp.where(i % self.tpb == 0, self.batch, i // self.tpb)

    def row_spec(self, width):
        return pl.BlockSpec((ROW_TILE, width), lambda i: (self.tile(i), 0))

    def mod_spec(self, d, k):
        return pl.BlockSpec((None, 1, d), lambda i: (self.mod_row(i) * 6 + k, 0, 0))


def _const_spec(shape):
    zeros = (0,) * len(shape)
    return pl.BlockSpec(shape, lambda *_: zeros, pipeline_mode=pl.Buffered(1))


def _norm_mod_kernel(x_ref, ctx_ref, nw_ref, sh_ref, sc_ref, xo_ref, h_ref):
    def emit(src_ref):
        x = src_ref[...]
        xo_ref[...] = x
        h_ref[...] = ((_rms(x) * nw_ref[...]) * (1.0 + sc_ref[...]) + sh_ref[...]).astype(BF16)

    is_ctx = pl.program_id(1) == 0
    pl.when(is_ctx)(lambda: emit(ctx_ref))
    pl.when(jnp.logical_not(is_ctx))(lambda: emit(x_ref))


def _norm_mod(x, ctx, nw, mod_l):
    B, S, D = x.shape
    tiles = 1 + S // ROW_TILE
    mod_spec = lambda k: pl.BlockSpec((None, 1, D), lambda b, t: (jnp.where(t == 0, B, b) * 6 + k, 0, 0))
    out_spec = pl.BlockSpec((ROW_TILE, D), lambda b, t: (b * tiles + t, 0))
    return pl.pallas_call(
        _norm_mod_kernel,
        grid=(B, tiles),
        in_specs=[pl.BlockSpec((None, ROW_TILE, D), lambda b, t: (b, jnp.maximum(t - 1, 0), 0)),
                  pl.BlockSpec((None, ROW_TILE, D), lambda b, t: (b, 0, 0)),
                  _const_spec((1, D)), mod_spec(0), mod_spec(1)],
        out_specs=[out_spec, out_spec],
        out_shape=[jax.ShapeDtypeStruct((B * tiles * ROW_TILE, D), F32),
                   jax.ShapeDtypeStruct((B * tiles * ROW_TILE, D), BF16)],
        compiler_params=_params("parallel", "parallel"),
        name="norm_mod",
    )(x, ctx, nw, mod_l, mod_l)


ROTARY_KINDS = ("q_norm", "k_norm", "q_rope", "k_rope")


def _pair_halves(a):
    q = HEAD_DIM // 4
    return jnp.concatenate([a[..., 0:q], a[..., 2 * q:3 * q], a[..., q:2 * q], a[..., 3 * q:]], axis=-1)


def _rope(y, cos, sin):
    return y * cos + pltpu.roll(y, HEAD_DIM // 2, 1) * sin


def _proj_kernel(h_ref, w_ref, cos_ref, sin_ref, gain_ref, *refs, kinds):
    *out_refs, wb_ref = refs

    @pl.when(pl.program_id(1) == 0)
    def _():
        for c, (kind, _, _) in enumerate(kinds):
            w = w_ref[:, c * HEAD_DIM:(c + 1) * HEAD_DIM]
            wb_ref[:, c * HEAD_DIM:(c + 1) * HEAD_DIM] = (_pair_halves(w) if kind in ROTARY_KINDS else w).astype(BF16)

    acc = jnp.dot(h_ref[...], wb_ref[...], preferred_element_type=F32)
    for c, (kind, out_idx, out_chunk) in enumerate(kinds):
        y = acc[:, c * HEAD_DIM:(c + 1) * HEAD_DIM]
        if kind == "q_norm":
            y = _rms(y) * gain_ref[0:1, :]
        if kind == "k_norm":
            y = _rms(y) * gain_ref[1:2, :]
        if kind in ROTARY_KINDS:
            y = _rope(y, cos_ref[...], sin_ref[...])
        if kind in ("q_norm", "q_rope"):
            y = y * (ATTN_SCALE * LOG2E)
        out_refs[out_idx][:, out_chunk * HEAD_DIM:(out_chunk + 1) * HEAD_DIM] = y.astype(BF16)


def _proj(h, w_in, layer, cos, sin, gain, col0, kinds, out_widths, tiles_per_batch, name, rows=PROJ_ROWS):
    M, D = h.shape
    n_col_tiles = out_widths[0][1]
    cols = len(kinds) * HEAD_DIM
    cb0 = col0 // cols
    proj_tiles_per_batch = tiles_per_batch * ROW_TILE // rows
    return pl.pallas_call(
        functools.partial(_proj_kernel, kinds=kinds),
        grid=(n_col_tiles, M // rows),
        in_specs=[
            pl.BlockSpec((rows, D), lambda j, i: (i, 0)),
            pl.BlockSpec((None, D, cols), lambda j, i: (layer, 0, cb0 + j),
                         pipeline_mode=pl.Buffered(1 if n_col_tiles == 1 else 2)),
            pl.BlockSpec((rows, HEAD_DIM), lambda j, i: (i % proj_tiles_per_batch, 0)),
            pl.BlockSpec((rows, HEAD_DIM), lambda j, i: (i % proj_tiles_per_batch, 0)),
            pl.BlockSpec((2, HEAD_DIM), lambda j, i: (0, 0)),
        ],
        out_specs=[pl.BlockSpec((rows, w), lambda j, i: (i, j)) for w, _ in out_widths],
        out_shape=[jax.ShapeDtypeStruct((M, w * n), BF16) for w, n in out_widths],
        scratch_shapes=[pltpu.VMEM((D, cols), BF16)],
        compiler_params=_params("parallel", "arbitrary"),
        name=name,
    )(h, w_in, cos, sin, gain)


def _stack_heads(q_ref, g0, n):
    return jnp.concatenate([q_ref[:, g * HEAD_DIM:(g + 1) * HEAD_DIM] for g in range(g0, g0 + n)], axis=0)


def _write_heads(o_ref, g0, n, o):
    rows = o_ref.shape[0]
    for j in range(n):
        o_ref[:, (g0 + j) * HEAD_DIM:(g0 + j + 1) * HEAD_DIM] = o[j * rows:(j + 1) * rows].astype(o_ref.dtype)


def _scores(q, k):
    return lax.dot_general(q, k, (((1,), (1,)), ((), ())), preferred_element_type=F32)


def _online_step(q, k, v, m, l, acc):
    s = _scores(q, k)
    m_new = jnp.maximum(m, jnp.max(s, axis=-1, keepdims=True))
    alpha = jnp.exp2(m - m_new)
    p = jnp.exp2(s - m_new)
    l = alpha * l + jnp.sum(p, axis=-1, keepdims=True)
    acc = alpha * acc + jnp.dot(p.astype(BF16), v, preferred_element_type=F32)
    return m_new, l, acc


def _dense_attn_kernel(q_ref, k_ref, v_ref, o_ref, *, ctx_len, total_len, key_chunk, ctx_tile):
    def run(n_keys, chunk):
        for g0 in range(0, GROUP, DENSE_CHAIN_HEADS):
            q = _stack_heads(q_ref, g0, DENSE_CHAIN_HEADS)
            r = q.shape[0]
            m = jnp.full((r, 1), -jnp.inf, F32)
            l = jnp.zeros((r, 1), F32)
            acc = jnp.zeros((r, HEAD_DIM), F32)
            for c in range(n_keys // chunk):
                m, l, acc = _online_step(q, k_ref[c * chunk:(c + 1) * chunk, :],
                                         v_ref[c * chunk:(c + 1) * chunk, :], m, l, acc)
            _write_heads(o_ref, g0, DENSE_CHAIN_HEADS, acc * (1.0 / l))

    if ctx_tile:
        is_ctx = pl.program_id(2) == 0
        pl.when(is_ctx)(lambda: run(ctx_len, ctx_len))
        pl.when(jnp.logical_not(is_ctx))(lambda: run(total_len, key_chunk))
    else:
        run(total_len, key_chunk)


def _window_attn_kernel(sink_ref, q_ref, k_ref, v_ref, o_ref, *, ctx_len, total_len, ctx_tile, q_off):
    kvh = pl.program_id(1)
    tq = q_ref.shape[0]
    span = tq + 2 * WINDOW
    n = WINDOW_CHAIN_HEADS

    def attend(parts):
        for g0 in range(0, GROUP, n):
            q = _stack_heads(q_ref, g0, n)
            sink = jnp.concatenate(
                [jnp.full((tq, 1), sink_ref[kvh * GROUP + g] * LOG2E, F32) for g in range(g0, g0 + n)], axis=0)
            scores = []
            m = sink
            for k, _, bias in parts:
                s = _scores(q, k)
                if bias is not None:
                    s = s + bias
                scores.append(s)
                m = jnp.maximum(m, jnp.max(s, axis=-1, keepdims=True))
            l = jnp.exp2(sink - m)
            o = jnp.zeros((n * tq, HEAD_DIM), F32)
            for s, (_, v, _) in zip(scores, parts):
                p = jnp.exp2(s - m)
                l = l + jnp.sum(p, axis=-1, keepdims=True)
                o = o + jnp.dot(p.astype(BF16), v, preferred_element_type=F32)
            _write_heads(o_ref, g0, n, o * (1.0 / l))

    def ctx_part():
        return k_ref[0:ctx_len, :], v_ref[0:ctx_len, :], None

    def run_latent():
        q0 = (pl.program_id(2) + q_off) * tq - ctx_len
        n_latent = total_len - ctx_len
        start = pl.multiple_of(jnp.clip(q0 - WINDOW, 0, n_latent - span), WINDOW)
        row = lax.broadcasted_iota(jnp.int32, (n * tq, span), 0) & (tq - 1)
        col = lax.broadcasted_iota(jnp.int32, (n * tq, span), 1)
        bias = jnp.where(jnp.abs(col - row + (start - q0)) <= WINDOW, 0.0, NEG_INF).astype(F32)
        attend([ctx_part(), (k_ref[pl.ds(ctx_len + start, span), :], v_ref[pl.ds(ctx_len + start, span), :], bias)])

    if ctx_tile:
        is_ctx = pl.program_id(2) == 0
        pl.when(is_ctx)(lambda: attend([ctx_part()]))
        pl.when(jnp.logical_not(is_ctx))(run_latent)
    else:
        run_latent()


def _attention(q, k, v, sink, batch, total_len, ctx_len, latent_only, window):
    M = q.shape[0]
    q3, k3, v3 = (a.reshape(batch, total_len, a.shape[1]) for a in (q, k, v))
    tiles = total_len // ROW_TILE
    q_off = 1 if latent_only else 0
    gw = GROUP * HEAD_DIM
    q_spec = pl.BlockSpec((None, ROW_TILE, gw), lambda b, h, i, *_: (b, i + q_off, h))
    kv_spec = pl.BlockSpec((None, total_len, HEAD_DIM), lambda b, h, i, *_: (b, 0, h))
    grid = (batch, KV_HEADS, tiles - q_off)
    out_shape = jax.ShapeDtypeStruct(q3.shape, BF16)
    if window:
        kern = functools.partial(_window_attn_kernel, ctx_len=ctx_len, total_len=total_len,
                                 ctx_tile=not latent_only, q_off=q_off)
        out = pl.pallas_call(
            kern,
            grid_spec=pltpu.PrefetchScalarGridSpec(
                num_scalar_prefetch=1, grid=grid,
                in_specs=[q_spec, kv_spec, kv_spec], out_specs=q_spec),
            out_shape=out_shape,
            compiler_params=_params("parallel", "parallel", "parallel"),
            name="window_attention",
        )(sink, q3, k3, v3)
    else:
        kern = functools.partial(_dense_attn_kernel, ctx_len=ctx_len, total_len=total_len,
                                 key_chunk=total_len // 3, ctx_tile=not latent_only)
        out = pl.pallas_call(
            kern, grid=grid, in_specs=[q_spec, kv_spec, kv_spec], out_specs=q_spec,
            out_shape=out_shape,
            compiler_params=_params("parallel", "parallel", "parallel"),
            name="dense_attention",
        )(q3, k3, v3)
    return out.reshape(M, q.shape[1])


def _merge_kernel(ya_ref, yb_ref, ga_ref, gb_ref, wa_ref, wb_ref, wo_ref, x_ref, g1_ref,
                  nw_ref, sh_ref, sc_ref, wr_ref, br_ref, xo_ref, h_ref, lg_ref):
    a = jnp.dot(ya_ref[...], wa_ref[...], preferred_element_type=F32)
    b = jnp.dot(yb_ref[...], wb_ref[...], preferred_element_type=F32)
    m = jax.nn.sigmoid(ga_ref[...].astype(F32)) * a + jax.nn.sigmoid(gb_ref[...].astype(F32)) * b
    o = jnp.dot(m.astype(BF16), wo_ref[...], preferred_element_type=F32)
    xn = x_ref[...] + g1_ref[...] * o
    xo_ref[...] = xn
    h = (_rms(xn) * nw_ref[...]) * (1.0 + sc_ref[...]) + sh_ref[...]
    h_hi = h.astype(BF16)
    bits = pltpu.bitcast(h_hi.astype(F32), jnp.uint32)
    half = h.shape[1] // 2
    words = (bits[:, :half] >> 16) | (bits[:, half:] & jnp.uint32(0xFFFF0000))
    chunks = half // LANES
    for c in range(chunks):
        h_ref[pl.ds(c, h.shape[0], stride=chunks), :] = words[:, c * LANES:(c + 1) * LANES]
    h_lo = (h - h_hi.astype(F32)).astype(BF16)
    lg = jnp.dot(h_hi, wr_ref[...], preferred_element_type=F32)
    lg_lo = jnp.dot(h_lo, wr_ref[:, :ROUTER_LANES], preferred_element_type=F32)
    lg_ref[...] = _route(lg[:, :ROUTER_LANES] + (lg[:, ROUTER_LANES:] + lg_lo) + br_ref[...])


def _route(logits):
    lane = lax.broadcasted_iota(jnp.int32, logits.shape, 1)
    first_max = lambda a: jnp.argmax(a, axis=-1, keepdims=True).astype(jnp.int32)
    is_group = lane < N_GROUPS
    gl = jnp.where(is_group, logits, -jnp.inf)
    g_max = jnp.max(gl, axis=-1, keepdims=True)
    g = first_max(gl)
    p_group = 1.0 / jnp.sum(jnp.where(is_group, jnp.exp(logits - g_max), 0.0), axis=-1, keepdims=True)
    lo = N_GROUPS + g * EXPERTS_PER_GROUP
    el = jnp.where(jnp.logical_and(lane >= lo, lane < lo + EXPERTS_PER_GROUP), logits, -jnp.inf)
    v0 = jnp.max(el, axis=-1, keepdims=True)
    i0 = first_max(el)
    el = jnp.where(lane == i0, -jnp.inf, el)
    v1 = jnp.max(el, axis=-1, keepdims=True)
    i1 = first_max(el)
    r = jnp.exp(v1 - v0)
    w0 = p_group / (1.0 + r)
    out = jnp.where(lane == 0, (i0 - N_GROUPS).astype(F32), 0.0)
    out = jnp.where(lane == 1, (i1 - N_GROUPS).astype(F32), out)
    out = jnp.where(lane == 2, w0, out)
    return jnp.where(lane == 3, w0 * r, out)


def _merge(ya, yb, gates, wa, wb, wo, x_all, mod_l, nw, wr, br, rows):
    M, D = x_all.shape
    W = ya.shape[1]
    chunks = D // 2 // LANES
    return pl.pallas_call(
        _merge_kernel,
        grid=(rows.n,),
        in_specs=[
            rows.row_spec(W), rows.row_spec(W),
            pl.BlockSpec((ROW_TILE, D), lambda i: (rows.tile(i), 0)),
            pl.BlockSpec((ROW_TILE, D), lambda i: (rows.tile(i), 1)),
            _const_spec((W, D)), _const_spec((W, D)), _const_spec((D, D)),
            rows.row_spec(D), rows.mod_spec(D, 2),
            _const_spec((1, D)), rows.mod_spec(D, 3), rows.mod_spec(D, 4),
            _const_spec((D, 2 * ROUTER_LANES)), _const_spec((1, ROUTER_LANES)),
        ],
        out_specs=[rows.row_spec(D),
                   pl.BlockSpec((ROW_TILE * chunks, LANES), lambda i: (rows.tile(i), 0)),
                   rows.row_spec(ROUTER_LANES)],
        out_shape=[jax.ShapeDtypeStruct((M, D), F32), jax.ShapeDtypeStruct((M * chunks, LANES), jnp.uint32),
                   jax.ShapeDtypeStruct((M, ROUTER_LANES), F32)],
        compiler_params=_params("parallel"),
        name="merge",
    )(ya, yb, gates, gates, wa, wb, wo, x_all, mod_l, nw, mod_l, mod_l, wr, br)


def _dispatch(eid, tok_rows, row_scale):
    n = eid.shape[0]
    a = n * TOP_K
    e_flat = eid.reshape(a)
    onehot = (e_flat[:, None] == jnp.arange(N_EXPERTS, dtype=jnp.int32)[None, :]).astype(jnp.int32)
    csum = jnp.cumsum(onehot, axis=0)
    counts = csum[-1]
    padded = (counts + MOE_ROWS - 1) // MOE_ROWS * MOE_ROWS
    pad_end = jnp.cumsum(padded)
    pad_start = pad_end - padded
    dest = jnp.sum(onehot * (csum - 1 + pad_start[None, :]), axis=1)
    n_grid = a // MOE_ROWS + N_EXPERTS - 1 + GATHER_AHEAD
    _, sorted_row = lax.sort((e_flat, jnp.repeat(tok_rows * row_scale, TOP_K)), num_keys=1, is_stable=True)
    sorted_row = jnp.concatenate([sorted_row, jnp.zeros((MOE_ROWS,), jnp.int32)])
    blk = jnp.arange(n_grid, dtype=jnp.int32)
    n_valid = (pad_end[-1] // MOE_ROWS).astype(jnp.int32)
    blk_c = jnp.minimum(blk, n_valid - 1)
    blk_e = jnp.sum((pad_end[None, :] <= (blk_c * MOE_ROWS)[:, None]).astype(jnp.int32), axis=1)
    blk_e = jnp.minimum(blk_e, N_EXPERTS - 1)
    start = jnp.cumsum(counts) - counts
    blk_first = jnp.where(blk < n_valid, start[blk_e] + blk * MOE_ROWS - pad_start[blk_e], 0).astype(jnp.int32)
    experts = jnp.arange(N_EXPERTS, dtype=jnp.int32)
    used = counts > 0
    first_used_from = lax.cummin(jnp.where(used, experts, N_EXPERTS), reverse=True)
    next_used = jnp.concatenate([first_used_from[1:], jnp.full((1,), N_EXPERTS, jnp.int32)])
    next_used = jnp.where(next_used == N_EXPERTS, -1, next_used)
    parity = (jnp.cumsum(used.astype(jnp.int32)) - 1) % 2
    after_next = jnp.where(next_used >= 0, next_used[jnp.maximum(next_used, 0)], -1)
    return (dest, blk_e, after_next[blk_e].astype(jnp.int32), parity[blk_e].astype(jnp.int32), blk_first,
            sorted_row, next_used[blk_e[:1]].astype(jnp.int32), n_valid.reshape(1))


def _experts_kernel(blk_e, blk_after_next, blk_par, blk_first, sorted_row, second_expert, n_valid,
                    h_hbm, w1_hbm, w3_hbm, w2_hbm, o_ref,
                    xg, xsem, ws1, ws3, ws2, wsem, w1b, w3b, w2b, *, layer):
    i = pl.program_id(0)
    nv = n_valid[0]
    chunks = xg.shape[1] // MOE_ROWS
    e = blk_e[i]
    par = blk_par[i]
    fresh = jnp.logical_or(i == 0, e != blk_e[jnp.maximum(i - 1, 0)])

    def weight_copies(expert, p):
        return (pltpu.make_async_copy(w1_hbm.at[layer, expert], ws1.at[p], wsem.at[p, 0]),
                pltpu.make_async_copy(w3_hbm.at[layer, expert], ws3.at[p], wsem.at[p, 1]),
                pltpu.make_async_copy(w2_hbm.at[layer, expert], ws2.at[p], wsem.at[p, 2]))

    def for_rows(blk, fn):
        buf = lax.rem(blk, GATHER_AHEAD + 1)
        base = blk_first[blk]
        for j in range(MOE_ROWS):
            r = pl.multiple_of(sorted_row[base + j], chunks)
            fn(pltpu.make_async_copy(h_hbm.at[pl.ds(r, chunks), :], xg.at[buf, pl.ds(j * chunks, chunks), :],
                                     xsem.at[buf]))

    def start_weights(expert, p):
        for cp, priority in zip(weight_copies(expert, p), WEIGHT_DMA_PRIORITIES):
            cp.start(priority=priority)

    @pl.when(i == 0)
    def _():
        start_weights(e, par)
        pl.when(second_expert[0] >= 0)(lambda: start_weights(second_expert[0], 1 - par))
        for blk in range(GATHER_AHEAD):
            for_rows(blk, lambda cp: cp.start())

    @pl.when(jnp.logical_and(i >= nv, i < nv + GATHER_AHEAD))
    def _():
        for_rows(i, lambda cp: cp.wait())

    @pl.when(fresh)
    def _():
        for cp in weight_copies(e, par):
            cp.wait()
        w1b[...] = ws1[par].astype(BF16)
        w3b[...] = ws3[par].astype(BF16)
        w2b[...] = ws2[par].astype(BF16)
        pl.when(blk_after_next[i] >= 0)(lambda: start_weights(blk_after_next[i], par))

    @pl.when(i < nv)
    def _():
        for_rows(i, lambda cp: cp.wait())
        words = jnp.concatenate(
            [xg[lax.rem(i, GATHER_AHEAD + 1), pl.ds(c, MOE_ROWS, stride=chunks), :] for c in range(chunks)],
            axis=1)
        x = jnp.concatenate([pltpu.bitcast(words << 16, F32).astype(BF16),
                             pltpu.bitcast(words & jnp.uint32(0xFFFF0000), F32).astype(BF16)], axis=1)
        for_rows(i + GATHER_AHEAD, lambda cp: cp.start())
        a = jnp.dot(x, w1b[...], preferred_element_type=F32)
        b = jnp.dot(x, w3b[...], preferred_element_type=F32)
        hid = (a * jax.nn.sigmoid(a)) * b
        o_ref[...] = jnp.dot(hid.astype(BF16), w2b[...], preferred_element_type=F32).astype(o_ref.dtype)


def _experts(h2, w1, w3, w2, layer, blk_e, blk_after_next, blk_par, blk_first, sorted_row, second_expert, n_valid):
    D, de = w1.shape[-2:]
    chunks = D // 2 // LANES
    n_blk = blk_e.shape[0]
    any_spec = pl.BlockSpec(memory_space=pl.ANY)
    return pl.pallas_call(
        functools.partial(_experts_kernel, layer=layer),
        grid_spec=pltpu.PrefetchScalarGridSpec(
            num_scalar_prefetch=7, grid=(n_blk,),
            in_specs=[any_spec, any_spec, any_spec, any_spec],
            out_specs=pl.BlockSpec((MOE_ROWS, D), lambda i, *s: (jnp.minimum(i, s[6][0] - 1), 0)),
            scratch_shapes=[
                pltpu.VMEM((GATHER_AHEAD + 1, MOE_ROWS * chunks, LANES), jnp.uint32),
                pltpu.SemaphoreType.DMA((GATHER_AHEAD + 1,)),
                pltpu.VMEM((2, D, de), F32), pltpu.VMEM((2, D, de), F32), pltpu.VMEM((2, de, D), F32),
                pltpu.SemaphoreType.DMA((2, 3)),
                pltpu.VMEM((D, de), BF16), pltpu.VMEM((D, de), BF16), pltpu.VMEM((de, D), BF16),
            ],
        ),
        out_shape=jax.ShapeDtypeStruct((n_blk * MOE_ROWS, D), BF16),
        compiler_params=_params("arbitrary"),
        name="experts",
    )(blk_e, blk_after_next, blk_par, blk_first, sorted_row, second_expert, n_valid, h2, w1, w3, w2)


def _moe_mix(x_ref, y0_ref, y1_ref, w_ref, g2_ref):
    w = w_ref[...]
    y = w[:, 0:1] * y0_ref[...].astype(F32) + w[:, 1:2] * y1_ref[...].astype(F32)
    return x_ref[...] + g2_ref[...] * y


def _combine_next_kernel(x_ref, y0_ref, y1_ref, w_ref, g2_ref, nw_ref, sh_ref, sc_ref, xo_ref, h_ref):
    xn = _moe_mix(x_ref, y0_ref, y1_ref, w_ref, g2_ref)
    xo_ref[...] = xn
    h_ref[...] = ((_rms(xn) * nw_ref[...]) * (1.0 + sc_ref[...]) + sh_ref[...]).astype(BF16)


def _combine_final_kernel(x_ref, y0_ref, y1_ref, w_ref, g2_ref, nf_ref, o_ref):
    o_ref[...] = _rms(_moe_mix(x_ref, y0_ref, y1_ref, w_ref, g2_ref)) * nf_ref[...]


def _token_specs(n_tokens, d):
    tiles = n_tokens // ROW_TILE
    return [pl.BlockSpec((ROW_TILE, d), lambda i: (i, 0)), pl.BlockSpec((ROW_TILE, d), lambda i: (i + tiles, 0)),
            pl.BlockSpec((ROW_TILE, TOP_K), lambda i: (i, 0))]


def _combine_next(x_new, ysel, wts, mod_l, nw_next, mod_next, rows):
    M, D = x_new.shape
    return pl.pallas_call(
        _combine_next_kernel,
        grid=(rows.n,),
        in_specs=[rows.row_spec(D), *_token_specs(M, D), rows.mod_spec(D, 5),
                  _const_spec((1, D)), rows.mod_spec(D, 0), rows.mod_spec(D, 1)],
        out_specs=[rows.row_spec(D), rows.row_spec(D)],
        out_shape=[jax.ShapeDtypeStruct((M, D), F32), jax.ShapeDtypeStruct((M, D), BF16)],
        compiler_params=_params("parallel"),
        name="combine_next",
    )(x_new, ysel, ysel, wts, mod_l, nw_next, mod_next, mod_next)


def _combine_final(x_new, ysel, wts, mod_l, nf, rows):
    M, D = x_new.shape
    n = wts.shape[0]
    return pl.pallas_call(
        _combine_final_kernel,
        grid=(rows.n,),
        in_specs=[rows.row_spec(D), *_token_specs(n, D), rows.mod_spec(D, 5), _const_spec((1, D))],
        out_specs=pl.BlockSpec((ROW_TILE, D), lambda i: (i, 0)),
        out_shape=jax.ShapeDtypeStruct((n, D), F32),
        compiler_params=_params("parallel"),
        name="combine_final",
    )(x_new, ysel, ysel, wts, mod_l, nf)


def _rope_tables(ctx_len, seq):
    quarter = HEAD_DIM // 4
    freqs = ROPE_BASE ** (-jnp.arange(quarter, dtype=F32) / quarter)
    pos = jnp.arange(seq, dtype=jnp.int32)
    row = (pos // GRID_W).astype(F32)[:, None] * freqs[None, :]
    col = (pos % GRID_W).astype(F32)[:, None] * freqs[None, :]
    cos = jnp.concatenate([jnp.cos(row), jnp.cos(col), jnp.cos(row), jnp.cos(col)], axis=-1)
    sin = jnp.concatenate([-jnp.sin(row), -jnp.sin(col), jnp.sin(row), jnp.sin(col)], axis=-1)
    cos = jnp.concatenate([jnp.ones((ctx_len, HEAD_DIM), F32), cos], axis=0)
    sin = jnp.concatenate([jnp.zeros((ctx_len, HEAD_DIM), F32), sin], axis=0)
    return cos, sin


def kernel(x, c, ctx, c_ctx, w_mod, b_mod, norm_mix, norm_ffn, w_in, qn_a, kn_a, sink_b, w_br_a, w_br_b, w_out, w_rg, b_rg, w_re, b_re, w1, w3, w2, norm_final):
    B, S, D = x.shape
    C = ctx.shape[1]
    L = w_mod.shape[0]
    T = C + S
    M = B * T
    assert C == ROW_TILE and S % ROW_TILE == 0 and T % PROJ_ROWS == 0 and B < MOD_ROWS
    tiles_per_batch = T // ROW_TILE

    cc = jnp.concatenate([c, c_ctx[None, :], jnp.zeros((MOD_ROWS - B - 1, D), F32)], axis=0)
    mod = _mod_vectors(cc, w_mod, b_mod).reshape(L, MOD_ROWS * 6, 1, D)
    cos, sin = _rope_tables(C, S)
    latent_rows = (jnp.arange(B, dtype=jnp.int32)[:, None] * T + C
                   + jnp.arange(S, dtype=jnp.int32)[None, :]).reshape(B * S)
    all_rows = jnp.arange(M, dtype=jnp.int32)

    mixer_kinds = lambda q_kind, k_kind: (tuple((q_kind, 0, k) for k in range(HEADS))
                                          + tuple((k_kind, 1, k) for k in range(KV_HEADS))
                                          + tuple(("v", 2, k) for k in range(KV_HEADS)))
    mixer_widths = [(WIDTH, 1), (KVW, 1), (KVW, 1)]
    gate_kinds = tuple(("gate", 0, k) for k in range(GATE_COLS // HEAD_DIM))
    col = {"a": 0, "b": WIDTH + 2 * KVW, "gates": 2 * WIDTH + 4 * KVW}

    every = _Rows(B, tiles_per_batch, latent_only=False)
    latent = _Rows(B, tiles_per_batch, latent_only=True)
    x_all, h = _norm_mod(x, ctx, norm_mix[0][None, :], mod[0])
    out = None
    for l in range(L):
        last = l == L - 1
        rows = latent if last else every
        proj = functools.partial(_proj, h, w_in, l, cos, sin, tiles_per_batch=tiles_per_batch)
        gains = jnp.stack([_pair_halves(qn_a[l]), _pair_halves(kn_a[l])])
        qa, ka, va = proj(gains, col["a"], mixer_kinds("q_norm", "k_norm"), mixer_widths, name="proj_a")
        qb, kb, vb = proj(gains, col["b"], mixer_kinds("q_rope", "k_rope"), mixer_widths, name="proj_b")
        (gates,) = proj(gains, col["gates"], gate_kinds, [(GATE_COLS, 2 * D // GATE_COLS)], name="proj_gates",
                        rows=GATE_ROWS)

        ya = _attention(qa, ka, va, None, B, T, C, last, window=False)
        yb = _attention(qb, kb, vb, sink_b[l], B, T, C, last, window=True)

        w_router = jnp.concatenate(
            [w_rg[l], w_re[l], jnp.zeros((D, ROUTER_LANES - N_GROUPS - N_EXPERTS), F32)], axis=1)
        w_router_hi = w_router.astype(BF16)
        w_router = jnp.concatenate([w_router_hi, (w_router - w_router_hi.astype(F32)).astype(BF16)], axis=1)
        b_router = jnp.concatenate(
            [b_rg[l], b_re[l], jnp.zeros((ROUTER_LANES - N_GROUPS - N_EXPERTS,), F32)])[None, :]
        x_new, h2, route = _merge(ya, yb, gates, w_br_a[l].astype(BF16), w_br_b[l].astype(BF16),
                                   w_out[l].astype(BF16), x_all, mod[l], norm_ffn[l][None, :],
                                   w_router, b_router, rows)

        tok_rows = latent_rows if last else all_rows
        route = route[tok_rows] if last else route
        eid, wts = route[:, :TOP_K].astype(jnp.int32), route[:, TOP_K:2 * TOP_K]
        dest, *plan = _dispatch(eid, tok_rows, D // 2 // LANES)
        ybuf = _experts(h2, w1, w3, w2, l, *plan)
        ysel = ybuf[dest.reshape(-1, TOP_K).T.reshape(-1)]
        if last:
            out = _combine_final(x_new, ysel, wts, mod[l], norm_final[None, :], rows).reshape(B, S, D)
        else:
            x_all, h = _combine_next(x_new, ysel, wts, mod[l], norm_mix[l + 1][None, :], mod[l + 1], rows)
    return out
```

```python
import functools

import jax
import jax.numpy as jnp
from jax import lax
from jax.experimental import pallas as pl
from jax.experimental.pallas import tpu as pltpu

F32 = jnp.float32
BF16 = jnp.bfloat16

GRID_W = 64
HEAD_DIM = 128
LANES = 128
HEADS = 8
KV_HEADS = 2
GROUP = HEADS // KV_HEADS
WIDTH = HEADS * HEAD_DIM
KVW = KV_HEADS * HEAD_DIM
WINDOW = 128
ROPE_BASE = 10000.0
ATTN_SCALE = HEAD_DIM ** -0.5
LOG2E = 1.4426950408889634
N_GROUPS = 8
EXPERTS_PER_GROUP = 8
N_EXPERTS = N_GROUPS * EXPERTS_PER_GROUP
TOP_K = 2
EPS = 1e-6
NEG_INF = -1e30

ROW_TILE = 256
PROJ_ROWS = 768
GATE_ROWS = 1152
GATE_COLS = 1024
MOE_ROWS = 128
GATHER_AHEAD = 2
WEIGHT_DMA_PRIORITIES = (1, 1, 0)
ROUTER_LANES = 128
DENSE_KEY_CHUNKS = 6
DENSE_CHAIN_HEADS = 4
WINDOW_CHAIN_HEADS = 1
MOD_ROWS = 8
VMEM_LIMIT = 56 * 1024 * 1024


def _params(*sem):
    return pltpu.CompilerParams(dimension_semantics=sem, vmem_limit_bytes=VMEM_LIMIT)


def _rms(y):
    return y * lax.rsqrt(jnp.mean(y * y, axis=-1, keepdims=True) + EPS)


def _mod_kernel(c_ref, w_ref, b_ref, o_ref):
    c = c_ref[...]
    a = (c * jax.nn.sigmoid(c)).astype(BF16)
    o_ref[...] = jnp.dot(a, w_ref[...].astype(BF16), preferred_element_type=F32) + b_ref[...]


def _mod_vectors(cc, w_mod, b_mod):
    L, D, N = w_mod.shape
    tn = 1024
    return pl.pallas_call(
        _mod_kernel,
        grid=(L, N // tn),
        in_specs=[
            pl.BlockSpec((MOD_ROWS, D), lambda l, j: (0, 0)),
            pl.BlockSpec((None, D, tn), lambda l, j: (l, 0, j)),
            pl.BlockSpec((None, 1, tn), lambda l, j: (l, 0, j)),
        ],
        out_specs=pl.BlockSpec((None, MOD_ROWS, tn), lambda l, j: (l, 0, j)),
        out_shape=jax.ShapeDtypeStruct((L, MOD_ROWS, N), F32),
        compiler_params=_params("parallel", "parallel"),
        name="mod_vectors",
    )(cc, w_mod, b_mod.reshape(L, 1, N))


class _Rows:
    def __init__(self, batch, tiles_per_batch, latent_only):
        self.batch = batch
        self.tpb = tiles_per_batch
        self.latent_only = latent_only
        self.n = batch * (tiles_per_batch - 1 if latent_only else tiles_per_batch)

    def tile(self, i):
        if self.latent_only:
            per = self.tpb - 1
            return (i // per) * self.tpb + 1 + i % per
        return i

    def mod_row(self, i):
        if self.latent_only:
            return i // (self.tpb - 1)
        return jnp.where(i % self.tpb == 0, self.batch, i // self.tpb)

    def row_spec(self, width):
        return pl.BlockSpec((ROW_TILE, width), lambda i: (self.tile(i), 0))

    def mod_spec(self, d, k):
        return pl.BlockSpec((None, 1, d), lambda i: (self.mod_row(i) * 6 + k, 0, 0))


def _const_spec(shape):
    zeros = (0,) * len(shape)
    return pl.BlockSpec(shape, lambda *_: zeros, pipeline_mode=pl.Buffered(1))


def _norm_mod_kernel(x_ref, ctx_ref, nw_ref, sh_ref, sc_ref, xo_ref, h_ref):
    def emit(src_ref):
        x = src_ref[...]
        xo_ref[...] = x
        h_ref[...] = ((_rms(x) * nw_ref[...]) * (1.0 + sc_ref[...]) + sh_ref[...]).astype(BF16)

    is_ctx = pl.program_id(1) == 0
    pl.when(is_ctx)(lambda: emit(ctx_ref))
    pl.when(jnp.logical_not(is_ctx))(lambda: emit(x_ref))


def _norm_mod(x, ctx, nw, mod_l):
    B, S, D = x.shape
    tiles = 1 + S // ROW_TILE
    mod_spec = lambda k: pl.BlockSpec((None, 1, D), lambda b, t: (jnp.where(t == 0, B, b) * 6 + k, 0, 0))
    out_spec = pl.BlockSpec((ROW_TILE, D), lambda b, t: (b * tiles + t, 0))
    return pl.pallas_call(
        _norm_mod_kernel,
        grid=(B, tiles),
        in_specs=[pl.BlockSpec((None, ROW_TILE, D), lambda b, t: (b, jnp.maximum(t - 1, 0), 0)),
                  pl.BlockSpec((None, ROW_TILE, D), lambda b, t: (b, 0, 0)),
                  _const_spec((1, D)), mod_spec(0), mod_spec(1)],
        out_specs=[out_spec, out_spec],
        out_shape=[jax.ShapeDtypeStruct((B * tiles * ROW_TILE, D), F32),
                   jax.ShapeDtypeStruct((B * tiles * ROW_TILE, D), BF16)],
        compiler_params=_params("parallel", "parallel"),
        name="norm_mod",
    )(x, ctx, nw, mod_l, mod_l)


ROTARY_KINDS = ("q_norm", "k_norm", "q_rope", "k_rope")


def _pair_halves(a):
    q = HEAD_DIM // 4
    return jnp.concatenate([a[..., 0:q], a[..., 2 * q:3 * q], a[..., q:2 * q], a[..., 3 * q:]], axis=-1)


def _rope(y, cos, sin):
    return y * cos + pltpu.roll(y, HEAD_DIM // 2, 1) * sin


def _proj_kernel(h_ref, w_ref, cos_ref, sin_ref, gain_ref, *refs, kinds):
    *out_refs, wb_ref = refs

    @pl.when(pl.program_id(1) == 0)
    def _():
        for c, (kind, _, _) in enumerate(kinds):
            w = w_ref[:, c * HEAD_DIM:(c + 1) * HEAD_DIM]
            wb_ref[:, c * HEAD_DIM:(c + 1) * HEAD_DIM] = (_pair_halves(w) if kind in ROTARY_KINDS else w).astype(BF16)

    acc = jnp.dot(h_ref[...], wb_ref[...], preferred_element_type=F32)
    for c, (kind, out_idx, out_chunk) in enumerate(kinds):
        y = acc[:, c * HEAD_DIM:(c + 1) * HEAD_DIM]
        if kind == "q_norm":
            y = _rms(y) * gain_ref[0:1, :]
        if kind == "k_norm":
            y = _rms(y) * gain_ref[1:2, :]
        if kind in ROTARY_KINDS:
            y = _rope(y, cos_ref[...], sin_ref[...])
        if kind in ("q_norm", "q_rope"):
            y = y * (ATTN_SCALE * LOG2E)
        out_refs[out_idx][:, out_chunk * HEAD_DIM:(out_chunk + 1) * HEAD_DIM] = y.astype(BF16)


def _proj(h, w_in, layer, cos, sin, gain, col0, kinds, out_widths, tiles_per_batch, name, rows=PROJ_ROWS):
    M, D = h.shape
    n_col_tiles = out_widths[0][1]
    cols = len(kinds) * HEAD_DIM
    cb0 = col0 // cols
    proj_tiles_per_batch = tiles_per_batch * ROW_TILE // rows
    return pl.pallas_call(
        functools.partial(_proj_kernel, kinds=kinds),
        grid=(n_col_tiles, M // rows),
        in_specs=[
            pl.BlockSpec((rows, D), lambda j, i: (i, 0)),
            pl.BlockSpec((None, D, cols), lambda j, i: (layer, 0, cb0 + j),
                         pipeline_mode=pl.Buffered(1 if n_col_tiles == 1 else 2)),
            pl.BlockSpec((rows, HEAD_DIM), lambda j, i: (i % proj_tiles_per_batch, 0)),
            pl.BlockSpec((rows, HEAD_DIM), lambda j, i: (i % proj_tiles_per_batch, 0)),
            pl.BlockSpec((2, HEAD_DIM), lambda j, i: (0, 0)),
        ],
        out_specs=[pl.BlockSpec((rows, w), lambda j, i: (i, j)) for w, _ in out_widths],
        out_shape=[jax.ShapeDtypeStruct((M, w * n), BF16) for w, n in out_widths],
        scratch_shapes=[pltpu.VMEM((D, cols), BF16)],
        compiler_params=_params("parallel", "arbitrary"),
        name=name,
    )(h, w_in, cos, sin, gain)


def _stack_heads(q_ref, g0, n):
    return jnp.concatenate([q_ref[:, g * HEAD_DIM:(g + 1) * HEAD_DIM] for g in range(g0, g0 + n)], axis=0)


def _write_heads(o_ref, g0, n, o):
    rows = o_ref.shape[0]
    for j in range(n):
        o_ref[:, (g0 + j) * HEAD_DIM:(g0 + j + 1) * HEAD_DIM] = o[j * rows:(j + 1) * rows].astype(o_ref.dtype)


def _scores(q, k):
    return lax.dot_general(q, k, (((1,), (1,)), ((), ())), preferred_element_type=F32)


def _online_step(q, k, v, m, l, acc):
    s = _scores(q, k)
    m_new = jnp.maximum(m, jnp.max(s, axis=-1, keepdims=True))
    alpha = jnp.exp2(m - m_new)
    p = jnp.exp2(s - m_new)
    l = alpha * l + jnp.sum(p, axis=-1, keepdims=True)
    acc = alpha * acc + jnp.dot(p.astype(BF16), v, preferred_element_type=F32)
    return m_new, l, acc


def _dense_attn_kernel(q_ref, k_ref, v_ref, o_ref, *, ctx_len, total_len, key_chunk, ctx_tile):
    def run(n_keys, chunk):
        for g0 in range(0, GROUP, DENSE_CHAIN_HEADS):
            q = _stack_heads(q_ref, g0, DENSE_CHAIN_HEADS)
            r = q.shape[0]
            m = jnp.full((r, 1), -jnp.inf, F32)
            l = jnp.zeros((r, 1), F32)
            acc = jnp.zeros((r, HEAD_DIM), F32)
            for c in range(n_keys // chunk):
                m, l, acc = _online_step(q, k_ref[c * chunk:(c + 1) * chunk, :],
                                         v_ref[c * chunk:(c + 1) * chunk, :], m, l, acc)
            _write_heads(o_ref, g0, DENSE_CHAIN_HEADS, acc * (1.0 / l))

    if ctx_tile:
        is_ctx = pl.program_id(2) == 0
        pl.when(is_ctx)(lambda: run(ctx_len, ctx_len))
        pl.when(jnp.logical_not(is_ctx))(lambda: run(total_len, key_chunk))
    else:
        run(total_len, key_chunk)


def _window_attn_kernel(sink_ref, q_ref, k_ref, v_ref, o_ref, *, ctx_len, total_len, ctx_tile, q_off):
    kvh = pl.program_id(1)
    tq = q_ref.shape[0]
    span = tq + 2 * WINDOW
    n = WINDOW_CHAIN_HEADS

    def attend(parts):
        for g0 in range(0, GROUP, n):
            q = _stack_heads(q_ref, g0, n)
            sink = jnp.concatenate(
                [jnp.full((tq, 1), sink_ref[kvh * GROUP + g] * LOG2E, F32) for g in range(g0, g0 + n)], axis=0)
            scores = []
            m = sink
            for k, _, bias in parts:
                s = _scores(q, k)
                if bias is not None:
                    s = s + bias
                scores.append(s)
                m = jnp.maximum(m, jnp.max(s, axis=-1, keepdims=True))
            l = jnp.exp2(sink - m)
            o = jnp.zeros((n * tq, HEAD_DIM), F32)
            for s, (_, v, _) in zip(scores, parts):
                p = jnp.exp2(s - m)
                l = l + jnp.sum(p, axis=-1, keepdims=True)
                o = o + jnp.dot(p.astype(BF16), v, preferred_element_type=F32)
            _write_heads(o_ref, g0, n, o * (1.0 / l))

    def ctx_part():
        return k_ref[0:ctx_len, :], v_ref[0:ctx_len, :], None

    def run_latent():
        q0 = (pl.program_id(2) + q_off) * tq - ctx_len
        n_latent = total_len - ctx_len
        start = pl.multiple_of(jnp.clip(q0 - WINDOW, 0, n_latent - span), WINDOW)
        row = lax.broadcasted_iota(jnp.int32, (n * tq, span), 0) & (tq - 1)
        col = lax.broadcasted_iota(jnp.int32, (n * tq, span), 1)
        bias = jnp.where(jnp.abs(col - row + (start - q0)) <= WINDOW, 0.0, NEG_INF).astype(F32)
        attend([ctx_part(), (k_ref[pl.ds(ctx_len + start, span), :], v_ref[pl.ds(ctx_len + start, span), :], bias)])

    if ctx_tile:
        is_ctx = pl.program_id(2) == 0
        pl.when(is_ctx)(lambda: attend([ctx_part()]))
        pl.when(jnp.logical_not(is_ctx))(run_latent)
    else:
        run_latent()


def _attention(q, k, v, sink, batch, total_len, ctx_len, latent_only, window):
    M = q.shape[0]
    q3, k3, v3 = (a.reshape(batch, total_len, a.shape[1]) for a in (q, k, v))
    tiles = total_len // ROW_TILE
    q_off = 1 if latent_only else 0
    gw = GROUP * HEAD_DIM
    q_spec = pl.BlockSpec((None, ROW_TILE, gw), lambda b, h, i, *_: (b, i + q_off, h))
    kv_spec = pl.BlockSpec((None, total_len, HEAD_DIM), lambda b, h, i, *_: (b, 0, h))
    grid = (batch, KV_HEADS, tiles - q_off)
    out_shape = jax.ShapeDtypeStruct(q3.shape, BF16)
    if window:
        kern = functools.partial(_window_attn_kernel, ctx_len=ctx_len, total_len=total_len,
                                 ctx_tile=not latent_only, q_off=q_off)
        out = pl.pallas_call(
            kern,
            grid_spec=pltpu.PrefetchScalarGridSpec(
                num_scalar_prefetch=1, grid=grid,
                in_specs=[q_spec, kv_spec, kv_spec], out_specs=q_spec),
            out_shape=out_shape,
            compiler_params=_params("parallel", "parallel", "parallel"),
            name="window_attention",
        )(sink, q3, k3, v3)
    else:
        kern = functools.partial(_dense_attn_kernel, ctx_len=ctx_len, total_len=total_len,
                                 key_chunk=total_len // DENSE_KEY_CHUNKS, ctx_tile=not latent_only)
        out = pl.pallas_call(
            kern, grid=grid, in_specs=[q_spec, kv_spec, kv_spec], out_specs=q_spec,
            out_shape=out_shape,
            compiler_params=_params("parallel", "parallel", "parallel"),
            name="dense_attention",
        )(q3, k3, v3)
    return out.reshape(M, q.shape[1])


def _merge_kernel(ya_ref, yb_ref, ga_ref, gb_ref, wa_ref, wb_ref, wo_ref, x_ref, g1_ref,
                  nw_ref, sh_ref, sc_ref, wr_ref, br_ref, xo_ref, h_ref, lg_ref):
    a = jnp.dot(ya_ref[...], wa_ref[...], preferred_element_type=F32)
    b = jnp.dot(yb_ref[...], wb_ref[...], preferred_element_type=F32)
    m = jax.nn.sigmoid(ga_ref[...].astype(F32)) * a + jax.nn.sigmoid(gb_ref[...].astype(F32)) * b
    o = jnp.dot(m.astype(BF16), wo_ref[...], preferred_element_type=F32)
    xn = x_ref[...] + g1_ref[...] * o
    xo_ref[...] = xn
    h = (_rms(xn) * nw_ref[...]) * (1.0 + sc_ref[...]) + sh_ref[...]
    h_hi = h.astype(BF16)
    bits = pltpu.bitcast(h_hi.astype(F32), jnp.uint32)
    half = h.shape[1] // 2
    words = (bits[:, :half] >> 16) | (bits[:, half:] & jnp.uint32(0xFFFF0000))
    chunks = half // LANES
    for c in range(chunks):
        h_ref[pl.ds(c, h.shape[0], stride=chunks), :] = words[:, c * LANES:(c + 1) * LANES]
    h_lo = (h - h_hi.astype(F32)).astype(BF16)
    lg = jnp.dot(h_hi, wr_ref[...], preferred_element_type=F32)
    lg_lo = jnp.dot(h_lo, wr_ref[:, :ROUTER_LANES], preferred_element_type=F32)
    lg_ref[...] = _route(lg[:, :ROUTER_LANES] + (lg[:, ROUTER_LANES:] + lg_lo) + br_ref[...])


def _route(logits):
    lane = lax.broadcasted_iota(jnp.int32, logits.shape, 1)
    first = lambda hit: jnp.min(jnp.where(hit, lane, ROUTER_LANES), axis=-1, keepdims=True)
    is_group = lane < N_GROUPS
    gl = jnp.where(is_group, logits, -jnp.inf)
    g_max = jnp.max(gl, axis=-1, keepdims=True)
    g = first(gl == g_max)
    p_group = 1.0 / jnp.sum(jnp.where(is_group, jnp.exp(logits - g_max), 0.0), axis=-1, keepdims=True)
    lo = N_GROUPS + g * EXPERTS_PER_GROUP
    el = jnp.where(jnp.logical_and(lane >= lo, lane < lo + EXPERTS_PER_GROUP), logits, -jnp.inf)
    v0 = jnp.max(el, axis=-1, keepdims=True)
    i0 = first(el == v0)
    el = jnp.where(lane == i0, -jnp.inf, el)
    v1 = jnp.max(el, axis=-1, keepdims=True)
    i1 = first(el == v1)
    r = jnp.exp(v1 - v0)
    w0 = p_group / (1.0 + r)
    out = jnp.where(lane == 0, (i0 - N_GROUPS).astype(F32), 0.0)
    out = jnp.where(lane == 1, (i1 - N_GROUPS).astype(F32), out)
    out = jnp.where(lane == 2, w0, out)
    return jnp.where(lane == 3, w0 * r, out)


def _merge(ya, yb, gates, wa, wb, wo, x_all, mod_l, nw, wr, br, rows):
    M, D = x_all.shape
    W = ya.shape[1]
    chunks = D // 2 // LANES
    return pl.pallas_call(
        _merge_kernel,
        grid=(rows.n,),
        in_specs=[
            rows.row_spec(W), rows.row_spec(W),
            pl.BlockSpec((ROW_TILE, D), lambda i: (rows.tile(i), 0)),
            pl.BlockSpec((ROW_TILE, D), lambda i: (rows.tile(i), 1)),
            _const_spec((W, D)), _const_spec((W, D)), _const_spec((D, D)),
            rows.row_spec(D), rows.mod_spec(D, 2),
            _const_spec((1, D)), rows.mod_spec(D, 3), rows.mod_spec(D, 4),
            _const_spec((D, 2 * ROUTER_LANES)), _const_spec((1, ROUTER_LANES)),
        ],
        out_specs=[rows.row_spec(D),
                   pl.BlockSpec((ROW_TILE * chunks, LANES), lambda i: (rows.tile(i), 0)),
                   rows.row_spec(ROUTER_LANES)],
        out_shape=[jax.ShapeDtypeStruct((M, D), F32), jax.ShapeDtypeStruct((M * chunks, LANES), jnp.uint32),
                   jax.ShapeDtypeStruct((M, ROUTER_LANES), F32)],
        compiler_params=_params("parallel"),
        name="merge",
    )(ya, yb, gates, gates, wa, wb, wo, x_all, mod_l, nw, mod_l, mod_l, wr, br)


def _dispatch(eid, tok_rows, row_scale):
    n = eid.shape[0]
    a = n * TOP_K
    e_flat = eid.reshape(a)
    onehot = (e_flat[:, None] == jnp.arange(N_EXPERTS, dtype=jnp.int32)[None, :]).astype(jnp.int32)
    csum = jnp.cumsum(onehot, axis=0)
    counts = csum[-1]
    padded = (counts + MOE_ROWS - 1) // MOE_ROWS * MOE_ROWS
    pad_end = jnp.cumsum(padded)
    pad_start = pad_end - padded
    dest = jnp.sum(onehot * (csum - 1 + pad_start[None, :]), axis=1)
    n_grid = a // MOE_ROWS + N_EXPERTS - 1 + GATHER_AHEAD
    _, sorted_row = lax.sort((e_flat, jnp.repeat(tok_rows * row_scale, TOP_K)), num_keys=1, is_stable=True)
    sorted_row = jnp.concatenate([sorted_row, jnp.zeros((MOE_ROWS,), jnp.int32)])
    blk = jnp.arange(n_grid, dtype=jnp.int32)
    n_valid = (pad_end[-1] // MOE_ROWS).astype(jnp.int32)
    blk_c = jnp.minimum(blk, n_valid - 1)
    blk_e = jnp.sum((pad_end[None, :] <= (blk_c * MOE_ROWS)[:, None]).astype(jnp.int32), axis=1)
    blk_e = jnp.minimum(blk_e, N_EXPERTS - 1)
    start = jnp.cumsum(counts) - counts
    blk_first = jnp.where(blk < n_valid, start[blk_e] + blk * MOE_ROWS - pad_start[blk_e], 0).astype(jnp.int32)
    experts = jnp.arange(N_EXPERTS, dtype=jnp.int32)
    used = counts > 0
    first_used_from = lax.cummin(jnp.where(used, experts, N_EXPERTS), reverse=True)
    next_used = jnp.concatenate([first_used_from[1:], jnp.full((1,), N_EXPERTS, jnp.int32)])
    next_used = jnp.where(next_used == N_EXPERTS, -1, next_used)
    parity = (jnp.cumsum(used.astype(jnp.int32)) - 1) % 2
    after_next = jnp.where(next_used >= 0, next_used[jnp.maximum(next_used, 0)], -1)
    return (dest, blk_e, after_next[blk_e].astype(jnp.int32), parity[blk_e].astype(jnp.int32), blk_first,
            sorted_row, next_used[blk_e[:1]].astype(jnp.int32), n_valid.reshape(1))


def _experts_kernel(blk_e, blk_after_next, blk_par, blk_first, sorted_row, second_expert, n_valid,
                    h_hbm, w1_hbm, w3_hbm, w2_hbm, o_ref,
                    xg, xsem, ws1, ws3, ws2, wsem, w1b, w3b, w2b, *, layer):
    i = pl.program_id(0)
    nv = n_valid[0]
    chunks = xg.shape[1] // MOE_ROWS
    e = blk_e[i]
    par = blk_par[i]
    fresh = jnp.logical_or(i == 0, e != blk_e[jnp.maximum(i - 1, 0)])

    def weight_copies(expert, p):
        return (pltpu.make_async_copy(w1_hbm.at[layer, expert], ws1.at[p], wsem.at[p, 0]),
                pltpu.make_async_copy(w3_hbm.at[layer, expert], ws3.at[p], wsem.at[p, 1]),
                pltpu.make_async_copy(w2_hbm.at[layer, expert], ws2.at[p], wsem.at[p, 2]))

    def for_rows(blk, fn):
        buf = lax.rem(blk, GATHER_AHEAD + 1)
        base = blk_first[blk]
        for j in range(MOE_ROWS):
            r = pl.multiple_of(sorted_row[base + j], chunks)
            fn(pltpu.make_async_copy(h_hbm.at[pl.ds(r, chunks), :], xg.at[buf, pl.ds(j * chunks, chunks), :],
                                     xsem.at[buf]))

    def start_weights(expert, p):
        for cp, priority in zip(weight_copies(expert, p), WEIGHT_DMA_PRIORITIES):
            cp.start(priority=priority)

    @pl.when(i == 0)
    def _():
        start_weights(e, par)
        pl.when(second_expert[0] >= 0)(lambda: start_weights(second_expert[0], 1 - par))
        for blk in range(GATHER_AHEAD):
            for_rows(blk, lambda cp: cp.start())

    @pl.when(jnp.logical_and(i >= nv, i < nv + GATHER_AHEAD))
    def _():
        for_rows(i, lambda cp: cp.wait())

    @pl.when(fresh)
    def _():
        for cp in weight_copies(e, par):
            cp.wait()
        w1b[...] = ws1[par].astype(BF16)
        w3b[...] = ws3[par].astype(BF16)
        w2b[...] = ws2[par].astype(BF16)
        pl.when(blk_after_next[i] >= 0)(lambda: start_weights(blk_after_next[i], par))

    @pl.when(i < nv)
    def _():
        for_rows(i, lambda cp: cp.wait())
        words = jnp.concatenate(
            [xg[lax.rem(i, GATHER_AHEAD + 1), pl.ds(c, MOE_ROWS, stride=chunks), :] for c in range(chunks)],
            axis=1)
        x = jnp.concatenate([pltpu.bitcast(words << 16, F32).astype(BF16),
                             pltpu.bitcast(words & jnp.uint32(0xFFFF0000), F32).astype(BF16)], axis=1)
        for_rows(i + GATHER_AHEAD, lambda cp: cp.start())
        a = jnp.dot(x, w1b[...], preferred_element_type=F32)
        b = jnp.dot(x, w3b[...], preferred_element_type=F32)
        hid = (a * jax.nn.sigmoid(a)) * b
        o_ref[...] = jnp.dot(hid.astype(BF16), w2b[...], preferred_element_type=F32).astype(o_ref.dtype)


def _experts(h2, w1, w3, w2, layer, blk_e, blk_after_next, blk_par, blk_first, sorted_row, second_expert, n_valid):
    D, de = w1.shape[-2:]
    chunks = D // 2 // LANES
    n_blk = blk_e.shape[0]
    any_spec = pl.BlockSpec(memory_space=pl.ANY)
    return pl.pallas_call(
        functools.partial(_experts_kernel, layer=layer),
        grid_spec=pltpu.PrefetchScalarGridSpec(
            num_scalar_prefetch=7, grid=(n_blk,),
            in_specs=[any_spec, any_spec, any_spec, any_spec],
            out_specs=pl.BlockSpec((MOE_ROWS, D), lambda i, *s: (jnp.minimum(i, s[6][0] - 1), 0)),
            scratch_shapes=[
                pltpu.VMEM((GATHER_AHEAD + 1, MOE_ROWS * chunks, LANES), jnp.uint32),
                pltpu.SemaphoreType.DMA((GATHER_AHEAD + 1,)),
                pltpu.VMEM((2, D, de), F32), pltpu.VMEM((2, D, de), F32), pltpu.VMEM((2, de, D), F32),
                pltpu.SemaphoreType.DMA((2, 3)),
                pltpu.VMEM((D, de), BF16), pltpu.VMEM((D, de), BF16), pltpu.VMEM((de, D), BF16),
            ],
        ),
        out_shape=jax.ShapeDtypeStruct((n_blk * MOE_ROWS, D), BF16),
        compiler_params=_params("arbitrary"),
        name="experts",
    )(blk_e, blk_after_next, blk_par, blk_first, sorted_row, second_expert, n_valid, h2, w1, w3, w2)


def _moe_mix(x_ref, y0_ref, y1_ref, w_ref, g2_ref):
    w = w_ref[...]
    y = w[:, 0:1] * y0_ref[...].astype(F32) + w[:, 1:2] * y1_ref[...].astype(F32)
    return x_ref[...] + g2_ref[...] * y


def _combine_next_kernel(x_ref, y0_ref, y1_ref, w_ref, g2_ref, nw_ref, sh_ref, sc_ref, xo_ref, h_ref):
    xn = _moe_mix(x_ref, y0_ref, y1_ref, w_ref, g2_ref)
    xo_ref[...] = xn
    h_ref[...] = ((_rms(xn) * nw_ref[...]) * (1.0 + sc_ref[...]) + sh_ref[...]).astype(BF16)


def _combine_final_kernel(x_ref, y0_ref, y1_ref, w_ref, g2_ref, nf_ref, o_ref):
    o_ref[...] = _rms(_moe_mix(x_ref, y0_ref, y1_ref, w_ref, g2_ref)) * nf_ref[...]


def _token_specs(n_tokens, d):
    tiles = n_tokens // ROW_TILE
    return [pl.BlockSpec((ROW_TILE, d), lambda i: (i, 0)), pl.BlockSpec((ROW_TILE, d), lambda i: (i + tiles, 0)),
            pl.BlockSpec((ROW_TILE, TOP_K), lambda i: (i, 0))]


def _combine_next(x_new, ysel, wts, mod_l, nw_next, mod_next, rows):
    M, D = x_new.shape
    return pl.pallas_call(
        _combine_next_kernel,
        grid=(rows.n,),
        in_specs=[rows.row_spec(D), *_token_specs(M, D), rows.mod_spec(D, 5),
                  _const_spec((1, D)), rows.mod_spec(D, 0), rows.mod_spec(D, 1)],
        out_specs=[rows.row_spec(D), rows.row_spec(D)],
        out_shape=[jax.ShapeDtypeStruct((M, D), F32), jax.ShapeDtypeStruct((M, D), BF16)],
        compiler_params=_params("parallel"),
        name="combine_next",
    )(x_new, ysel, ysel, wts, mod_l, nw_next, mod_next, mod_next)


def _combine_final(x_new, ysel, wts, mod_l, nf, rows):
    M, D = x_new.shape
    n = wts.shape[0]
    return pl.pallas_call(
        _combine_final_kernel,
        grid=(rows.n,),
        in_specs=[rows.row_spec(D), *_token_specs(n, D), rows.mod_spec(D, 5), _const_spec((1, D))],
        out_specs=pl.BlockSpec((ROW_TILE, D), lambda i: (i, 0)),
        out_shape=jax.ShapeDtypeStruct((n, D), F32),
        compiler_params=_params("parallel"),
        name="combine_final",
    )(x_new, ysel, ysel, wts, mod_l, nf)


def _rope_tables(ctx_len, seq):
    quarter = HEAD_DIM // 4
    freqs = ROPE_BASE ** (-jnp.arange(quarter, dtype=F32) / quarter)
    pos = jnp.arange(seq, dtype=jnp.int32)
    row = (pos // GRID_W).astype(F32)[:, None] * freqs[None, :]
    col = (pos % GRID_W).astype(F32)[:, None] * freqs[None, :]
    cos = jnp.concatenate([jnp.cos(row), jnp.cos(col), jnp.cos(row), jnp.cos(col)], axis=-1)
    sin = jnp.concatenate([-jnp.sin(row), -jnp.sin(col), jnp.sin(row), jnp.sin(col)], axis=-1)
    cos = jnp.concatenate([jnp.ones((ctx_len, HEAD_DIM), F32), cos], axis=0)
    sin = jnp.concatenate([jnp.zeros((ctx_len, HEAD_DIM), F32), sin], axis=0)
    return cos, sin


def kernel(x, c, ctx, c_ctx, w_mod, b_mod, norm_mix, norm_ffn, w_in, qn_a, kn_a, sink_b, w_br_a, w_br_b, w_out, w_rg, b_rg, w_re, b_re, w1, w3, w2, norm_final):
    B, S, D = x.shape
    C = ctx.shape[1]
    L = w_mod.shape[0]
    T = C + S
    M = B * T
    assert C == ROW_TILE and S % ROW_TILE == 0 and T % PROJ_ROWS == 0 and B < MOD_ROWS
    tiles_per_batch = T // ROW_TILE

    cc = jnp.concatenate([c, c_ctx[None, :], jnp.zeros((MOD_ROWS - B - 1, D), F32)], axis=0)
    mod = _mod_vectors(cc, w_mod, b_mod).reshape(L, MOD_ROWS * 6, 1, D)
    cos, sin = _rope_tables(C, S)
    latent_rows = (jnp.arange(B, dtype=jnp.int32)[:, None] * T + C
                   + jnp.arange(S, dtype=jnp.int32)[None, :]).reshape(B * S)
    all_rows = jnp.arange(M, dtype=jnp.int32)

    mixer_kinds = lambda q_kind, k_kind: (tuple((q_kind, 0, k) for k in range(HEADS))
                                          + tuple((k_kind, 1, k) for k in range(KV_HEADS))
                                          + tuple(("v", 2, k) for k in range(KV_HEADS)))
    mixer_widths = [(WIDTH, 1), (KVW, 1), (KVW, 1)]
    gate_kinds = tuple(("gate", 0, k) for k in range(GATE_COLS // HEAD_DIM))
    col = {"a": 0, "b": WIDTH + 2 * KVW, "gates": 2 * WIDTH + 4 * KVW}

    every = _Rows(B, tiles_per_batch, latent_only=False)
    latent = _Rows(B, tiles_per_batch, latent_only=True)
    x_all, h = _norm_mod(x, ctx, norm_mix[0][None, :], mod[0])
    out = None
    for l in range(L):
        last = l == L - 1
        rows = latent if last else every
        proj = functools.partial(_proj, h, w_in, l, cos, sin, tiles_per_batch=tiles_per_batch)
        gains = jnp.stack([_pair_halves(qn_a[l]), _pair_halves(kn_a[l])])
        qa, ka, va = proj(gains, col["a"], mixer_kinds("q_norm", "k_norm"), mixer_widths, name="proj_a")
        qb, kb, vb = proj(gains, col["b"], mixer_kinds("q_rope", "k_rope"), mixer_widths, name="proj_b")
        (gates,) = proj(gains, col["gates"], gate_kinds, [(GATE_COLS, 2 * D // GATE_COLS)], name="proj_gates",
                        rows=GATE_ROWS)

        ya = _attention(qa, ka, va, None, B, T, C, last, window=False)
        yb = _attention(qb, kb, vb, sink_b[l], B, T, C, last, window=True)

        w_router = jnp.concatenate(
            [w_rg[l], w_re[l], jnp.zeros((D, ROUTER_LANES - N_GROUPS - N_EXPERTS), F32)], axis=1)
        w_router_hi = w_router.astype(BF16)
        w_router = jnp.concatenate([w_router_hi, (w_router - w_router_hi.astype(F32)).astype(BF16)], axis=1)
        b_router = jnp.concatenate(
            [b_rg[l], b_re[l], jnp.zeros((ROUTER_LANES - N_GROUPS - N_EXPERTS,), F32)])[None, :]
        x_new, h2, route = _merge(ya, yb, gates, w_br_a[l].astype(BF16), w_br_b[l].astype(BF16),
                                   w_out[l].astype(BF16), x_all, mod[l], norm_ffn[l][None, :],
                                   w_router, b_router, rows)

        tok_rows = latent_rows if last else all_rows
        route = route[tok_rows] if last else route
        eid, wts = route[:, :TOP_K].astype(jnp.int32), route[:, TOP_K:2 * TOP_K]
        dest, *plan = _dispatch(eid, tok_rows, D // 2 // LANES)
        ybuf = _experts(h2, w1, w3, w2, l, *plan)
        ysel = ybuf[dest.reshape(-1, TOP_K).T.reshape(-1)]
        if last:
            out = _combine_final(x_new, ysel, wts, mod[l], norm_final[None, :], rows).reshape(B, S, D)
        else:
            x_all, h = _combine_next(x_new, ysel, wts, mod[l], norm_mix[l + 1][None, :], mod[l + 1], rows)
    return out
```

```python
import functools

import jax
import jax.numpy as jnp
from jax import lax
from jax.experimental import pallas as pl
from jax.experimental.pallas import tpu as pltpu

F32 = jnp.float32
BF16 = jnp.bfloat16

GRID_W = 64
HEAD_DIM = 128
LANES = 128
HEADS = 8
KV_HEADS = 2
GROUP = HEADS // KV_HEADS
WIDTH = HEADS * HEAD_DIM
KVW = KV_HEADS * HEAD_DIM
WINDOW = 128
ROPE_BASE = 10000.0
ATTN_SCALE = HEAD_DIM ** -0.5
LOG2E = 1.4426950408889634
N_GROUPS = 8
EXPERTS_PER_GROUP = 8
N_EXPERTS = N_GROUPS * EXPERTS_PER_GROUP
TOP_K = 2
EPS = 1e-6
NEG_INF = -1e30

ROW_TILE = 256
PROJ_ROWS = 768
GATE_ROWS = 1152
GATE_COLS = 1024
MOE_ROWS = 128
GATHER_AHEAD = 2
WEIGHT_DMA_PRIORITIES = (1, 1, 0)
ROUTER_LANES = 128
DENSE_CHAIN_HEADS = 4
WINDOW_CHAIN_HEADS = 1
MOD_ROWS = 8
VMEM_LIMIT = 56 * 1024 * 1024


def _params(*sem):
    return pltpu.CompilerParams(dimension_semantics=sem, vmem_limit_bytes=VMEM_LIMIT)


def _rms(y):
    return y * lax.rsqrt(jnp.mean(y * y, axis=-1, keepdims=True) + EPS)


def _mod_kernel(c_ref, w_ref, b_ref, o_ref):
    c = c_ref[...]
    a = (c * jax.nn.sigmoid(c)).astype(BF16)
    o_ref[...] = jnp.dot(a, w_ref[...].astype(BF16), preferred_element_type=F32) + b_ref[...]


def _mod_vectors(cc, w_mod, b_mod):
    L, D, N = w_mod.shape
    tn = 1024
    return pl.pallas_call(
        _mod_kernel,
        grid=(L, N // tn),
        in_specs=[
            pl.BlockSpec((MOD_ROWS, D), lambda l, j: (0, 0)),
            pl.BlockSpec((None, D, tn), lambda l, j: (l, 0, j)),
            pl.BlockSpec((None, 1, tn), lambda l, j: (l, 0, j)),
        ],
        out_specs=pl.BlockSpec((None, MOD_ROWS, tn), lambda l, j: (l, 0, j)),
        out_shape=jax.ShapeDtypeStruct((L, MOD_ROWS, N), F32),
        compiler_params=_params("parallel", "parallel"),
        name="mod_vectors",
    )(cc, w_mod, b_mod.reshape(L, 1, N))


class _Rows:
    def __init__(self, batch, tiles_per_batch, latent_only):
        self.batch = batch
        self.tpb = tiles_per_batch
        self.latent_only = latent_only
        self.n = batch * (tiles_per_batch - 1 if latent_only else tiles_per_batch)

    def tile(self, i):
        if self.latent_only:
            per = self.tpb - 1
            return (i // per) * self.tpb + 1 + i % per
        return i

    def mod_row(self, i):
        if self.latent_only:
            return i // (self.tpb - 1)
        return jnp.where(i % self.tpb == 0, self.batch, i // self.tpb)

    def row_spec(self, width):
        return pl.BlockSpec((ROW_TILE, width), lambda i: (self.tile(i), 0))

    def mod_spec(self, d, k):
        return pl.BlockSpec((None, 1, d), lambda i: (self.mod_row(i) * 6 + k, 0, 0))


def _const_spec(shape):
    zeros = (0,) * len(shape)
    return pl.BlockSpec(shape, lambda *_: zeros, pipeline_mode=pl.Buffered(1))


def _norm_mod_kernel(x_ref, ctx_ref, nw_ref, sh_ref, sc_ref, xo_ref, h_ref):
    def emit(src_ref):
        x = src_ref[...]
        xo_ref[...] = x
        h_ref[...] = ((_rms(x) * nw_ref[...]) * (1.0 + sc_ref[...]) + sh_ref[...]).astype(BF16)

    is_ctx = pl.program_id(1) == 0
    pl.when(is_ctx)(lambda: emit(ctx_ref))
    pl.when(jnp.logical_not(is_ctx))(lambda: emit(x_ref))


def _norm_mod(x, ctx, nw, mod_l):
    B, S, D = x.shape
    tiles = 1 + S // ROW_TILE
    mod_spec = lambda k: pl.BlockSpec((None, 1, D), lambda b, t: (jnp.where(t == 0, B, b) * 6 + k, 0, 0))
    out_spec = pl.BlockSpec((ROW_TILE, D), lambda b, t: (b * tiles + t, 0))
    return pl.pallas_call(
        _norm_mod_kernel,
        grid=(B, tiles),
        in_specs=[pl.BlockSpec((None, ROW_TILE, D), lambda b, t: (b, jnp.maximum(t - 1, 0), 0)),
                  pl.BlockSpec((None, ROW_TILE, D), lambda b, t: (b, 0, 0)),
                  _const_spec((1, D)), mod_spec(0), mod_spec(1)],
        out_specs=[out_spec, out_spec],
        out_shape=[jax.ShapeDtypeStruct((B * tiles * ROW_TILE, D), F32),
                   jax.ShapeDtypeStruct((B * tiles * ROW_TILE, D), BF16)],
        compiler_params=_params("parallel", "parallel"),
        name="norm_mod",
    )(x, ctx, nw, mod_l, mod_l)


ROTARY_KINDS = ("q_norm", "k_norm", "q_rope", "k_rope")


def _pair_halves(a):
    q = HEAD_DIM // 4
    return jnp.concatenate([a[..., 0:q], a[..., 2 * q:3 * q], a[..., q:2 * q], a[..., 3 * q:]], axis=-1)


def _rope(y, cos, sin):
    return y * cos + pltpu.roll(y, HEAD_DIM // 2, 1) * sin


def _proj_kernel(h_ref, w_ref, cos_ref, sin_ref, gain_ref, *refs, kinds):
    *out_refs, wb_ref = refs

    @pl.when(pl.program_id(1) == 0)
    def _():
        for c, (kind, _, _) in enumerate(kinds):
            w = w_ref[:, c * HEAD_DIM:(c + 1) * HEAD_DIM]
            wb_ref[:, c * HEAD_DIM:(c + 1) * HEAD_DIM] = (_pair_halves(w) if kind in ROTARY_KINDS else w).astype(BF16)

    acc = jnp.dot(h_ref[...], wb_ref[...], preferred_element_type=F32)
    for c, (kind, out_idx, out_chunk) in enumerate(kinds):
        y = acc[:, c * HEAD_DIM:(c + 1) * HEAD_DIM]
        if kind == "q_norm":
            y = _rms(y) * gain_ref[0:1, :]
        if kind == "k_norm":
            y = _rms(y) * gain_ref[1:2, :]
        if kind in ROTARY_KINDS:
            y = _rope(y, cos_ref[...], sin_ref[...])
        if kind in ("q_norm", "q_rope"):
            y = y * (ATTN_SCALE * LOG2E)
        out_refs[out_idx][:, out_chunk * HEAD_DIM:(out_chunk + 1) * HEAD_DIM] = y.astype(BF16)


def _proj(h, w_in, layer, cos, sin, gain, col0, kinds, out_widths, tiles_per_batch, name, rows=PROJ_ROWS):
    M, D = h.shape
    n_col_tiles = out_widths[0][1]
    cols = len(kinds) * HEAD_DIM
    cb0 = col0 // cols
    proj_tiles_per_batch = tiles_per_batch * ROW_TILE // rows
    return pl.pallas_call(
        functools.partial(_proj_kernel, kinds=kinds),
        grid=(n_col_tiles, M // rows),
        in_specs=[
            pl.BlockSpec((rows, D), lambda j, i: (i, 0)),
            pl.BlockSpec((None, D, cols), lambda j, i: (layer, 0, cb0 + j),
                         pipeline_mode=pl.Buffered(1 if n_col_tiles == 1 else 2)),
            pl.BlockSpec((rows, HEAD_DIM), lambda j, i: (i % proj_tiles_per_batch, 0)),
            pl.BlockSpec((rows, HEAD_DIM), lambda j, i: (i % proj_tiles_per_batch, 0)),
            pl.BlockSpec((2, HEAD_DIM), lambda j, i: (0, 0)),
        ],
        out_specs=[pl.BlockSpec((rows, w), lambda j, i: (i, j)) for w, _ in out_widths],
        out_shape=[jax.ShapeDtypeStruct((M, w * n), BF16) for w, n in out_widths],
        scratch_shapes=[pltpu.VMEM((D, cols), BF16)],
        compiler_params=_params("parallel", "arbitrary"),
        name=name,
    )(h, w_in, cos, sin, gain)


def _stack_heads(q_ref, g0, n):
    return jnp.concatenate([q_ref[:, g * HEAD_DIM:(g + 1) * HEAD_DIM] for g in range(g0, g0 + n)], axis=0)


def _write_heads(o_ref, g0, n, o):
    rows = o_ref.shape[0]
    for j in range(n):
        o_ref[:, (g0 + j) * HEAD_DIM:(g0 + j + 1) * HEAD_DIM] = o[j * rows:(j + 1) * rows].astype(o_ref.dtype)


def _scores(q, k):
    return lax.dot_general(q, k, (((1,), (1,)), ((), ())), preferred_element_type=F32)


def _online_step(q, k, v, m, l, acc):
    s = _scores(q, k)
    m_new = jnp.maximum(m, jnp.max(s, axis=-1, keepdims=True))
    alpha = jnp.exp2(m - m_new)
    p = jnp.exp2(s - m_new)
    l = alpha * l + jnp.sum(p, axis=-1, keepdims=True)
    acc = alpha * acc + jnp.dot(p.astype(BF16), v, preferred_element_type=F32)
    return m_new, l, acc


def _dense_attn_kernel(q_ref, k_ref, v_ref, o_ref, *, ctx_len, total_len, key_chunk, ctx_tile):
    def run(n_keys, chunk):
        for g0 in range(0, GROUP, DENSE_CHAIN_HEADS):
            q = _stack_heads(q_ref, g0, DENSE_CHAIN_HEADS)
            r = q.shape[0]
            m = jnp.full((r, 1), -jnp.inf, F32)
            l = jnp.zeros((r, 1), F32)
            acc = jnp.zeros((r, HEAD_DIM), F32)
            for c in range(n_keys // chunk):
                m, l, acc = _online_step(q, k_ref[c * chunk:(c + 1) * chunk, :],
                                         v_ref[c * chunk:(c + 1) * chunk, :], m, l, acc)
            _write_heads(o_ref, g0, DENSE_CHAIN_HEADS, acc * (1.0 / l))

    if ctx_tile:
        is_ctx = pl.program_id(2) == 0
        pl.when(is_ctx)(lambda: run(ctx_len, ctx_len))
        pl.when(jnp.logical_not(is_ctx))(lambda: run(total_len, key_chunk))
    else:
        run(total_len, key_chunk)


def _window_attn_kernel(sink_ref, q_ref, k_ref, v_ref, o_ref, *, ctx_len, total_len, ctx_tile, q_off):
    kvh = pl.program_id(1)
    tq = q_ref.shape[0]
    span = tq + 2 * WINDOW
    n = WINDOW_CHAIN_HEADS

    def attend(parts):
        for g0 in range(0, GROUP, n):
            q = _stack_heads(q_ref, g0, n)
            sink = jnp.concatenate(
                [jnp.full((tq, 1), sink_ref[kvh * GROUP + g] * LOG2E, F32) for g in range(g0, g0 + n)], axis=0)
            scores = []
            m = sink
            for k, _, bias in parts:
                s = _scores(q, k)
                if bias is not None:
                    s = s + bias
                scores.append(s)
                m = jnp.maximum(m, jnp.max(s, axis=-1, keepdims=True))
            l = jnp.exp2(sink - m)
            o = jnp.zeros((n * tq, HEAD_DIM), F32)
            for s, (_, v, _) in zip(scores, parts):
                p = jnp.exp2(s - m)
                l = l + jnp.sum(p, axis=-1, keepdims=True)
                o = o + jnp.dot(p.astype(BF16), v, preferred_element_type=F32)
            _write_heads(o_ref, g0, n, o * (1.0 / l))

    def ctx_part():
        return k_ref[0:ctx_len, :], v_ref[0:ctx_len, :], None

    def run_latent():
        q0 = (pl.program_id(2) + q_off) * tq - ctx_len
        n_latent = total_len - ctx_len
        start = pl.multiple_of(jnp.clip(q0 - WINDOW, 0, n_latent - span), WINDOW)
        row = lax.broadcasted_iota(jnp.int32, (n * tq, span), 0) & (tq - 1)
        col = lax.broadcasted_iota(jnp.int32, (n * tq, span), 1)
        bias = jnp.where(jnp.abs(col - row + (start - q0)) <= WINDOW, 0.0, NEG_INF).astype(F32)
        attend([ctx_part(), (k_ref[pl.ds(ctx_len + start, span), :], v_ref[pl.ds(ctx_len + start, span), :], bias)])

    if ctx_tile:
        is_ctx = pl.program_id(2) == 0
        pl.when(is_ctx)(lambda: attend([ctx_part()]))
        pl.when(jnp.logical_not(is_ctx))(run_latent)
    else:
        run_latent()


def _attention(q, k, v, sink, batch, total_len, ctx_len, latent_only, window):
    M = q.shape[0]
    q3, k3, v3 = (a.reshape(batch, total_len, a.shape[1]) for a in (q, k, v))
    tiles = total_len // ROW_TILE
    q_off = 1 if latent_only else 0
    gw = GROUP * HEAD_DIM
    q_spec = pl.BlockSpec((None, ROW_TILE, gw), lambda b, h, i, *_: (b, i + q_off, h))
    kv_spec = pl.BlockSpec((None, total_len, HEAD_DIM), lambda b, h, i, *_: (b, 0, h))
    grid = (batch, KV_HEADS, tiles - q_off)
    out_shape = jax.ShapeDtypeStruct(q3.shape, BF16)
    if window:
        kern = functools.partial(_window_attn_kernel, ctx_len=ctx_len, total_len=total_len,
                                 ctx_tile=not latent_only, q_off=q_off)
        out = pl.pallas_call(
            kern,
            grid_spec=pltpu.PrefetchScalarGridSpec(
                num_scalar_prefetch=1, grid=grid,
                in_specs=[q_spec, kv_spec, kv_spec], out_specs=q_spec),
            out_shape=out_shape,
            compiler_params=_params("parallel", "parallel", "parallel"),
            name="window_attention",
        )(sink, q3, k3, v3)
    else:
        kern = functools.partial(_dense_attn_kernel, ctx_len=ctx_len, total_len=total_len,
                                 key_chunk=total_len // 3, ctx_tile=not latent_only)
        out = pl.pallas_call(
            kern, grid=grid, in_specs=[q_spec, kv_spec, kv_spec], out_specs=q_spec,
            out_shape=out_shape,
            compiler_params=_params("parallel", "parallel", "parallel"),
            name="dense_attention",
        )(q3, k3, v3)
    return out.reshape(M, q.shape[1])


def _merge_kernel(ya_ref, yb_ref, ga_ref, gb_ref, wa_ref, wb_ref, wo_ref, x_ref, g1_ref,
                  nw_ref, sh_ref, sc_ref, wr_ref, br_ref, xo_ref, h_ref, lg_ref):
    a = jnp.dot(ya_ref[...], wa_ref[...], preferred_element_type=F32)
    b = jnp.dot(yb_ref[...], wb_ref[...], preferred_element_type=F32)
    m = jax.nn.sigmoid(ga_ref[...].astype(F32)) * a + jax.nn.sigmoid(gb_ref[...].astype(F32)) * b
    o = jnp.dot(m.astype(BF16), wo_ref[...], preferred_element_type=F32)
    xn = x_ref[...] + g1_ref[...] * o
    xo_ref[...] = xn
    h = (_rms(xn) * nw_ref[...]) * (1.0 + sc_ref[...]) + sh_ref[...]
    h_hi = h.astype(BF16)
    bits = pltpu.bitcast(h_hi.astype(F32), jnp.uint32)
    half = h.shape[1] // 2
    words = (bits[:, :half] >> 16) | (bits[:, half:] & jnp.uint32(0xFFFF0000))
    chunks = half // LANES
    for c in range(chunks):
        h_ref[pl.ds(c, h.shape[0], stride=chunks), :] = words[:, c * LANES:(c + 1) * LANES]
    h_lo = (h - h_hi.astype(F32)).astype(BF16)
    lg = jnp.dot(h_hi, wr_ref[...], preferred_element_type=F32)
    lg_lo = jnp.dot(h_lo, wr_ref[:, :ROUTER_LANES], preferred_element_type=F32)
    lg_ref[...] = _route(lg[:, :ROUTER_LANES] + (lg[:, ROUTER_LANES:] + lg_lo) + br_ref[...])


def _route(logits):
    lane = lax.broadcasted_iota(jnp.int32, logits.shape, 1)
    first = lambda hit: jnp.min(jnp.where(hit, lane, ROUTER_LANES), axis=-1, keepdims=True)
    is_group = lane < N_GROUPS
    gl = jnp.where(is_group, logits, -jnp.inf)
    g_max = jnp.max(gl, axis=-1, keepdims=True)
    g = first(gl == g_max)
    p_group = 1.0 / jnp.sum(jnp.where(is_group, jnp.exp(logits - g_max), 0.0), axis=-1, keepdims=True)
    lo = N_GROUPS + g * EXPERTS_PER_GROUP
    el = jnp.where(jnp.logical_and(lane >= lo, lane < lo + EXPERTS_PER_GROUP), logits, -jnp.inf)
    v0 = jnp.max(el, axis=-1, keepdims=True)
    i0 = first(el == v0)
    el = jnp.where(lane == i0, -jnp.inf, el)
    v1 = jnp.max(el, axis=-1, keepdims=True)
    i1 = first(el == v1)
    r = jnp.exp(v1 - v0)
    w0 = p_group / (1.0 + r)
    out = jnp.where(lane == 0, (i0 - N_GROUPS).astype(F32), 0.0)
    out = jnp.where(lane == 1, (i1 - N_GROUPS).astype(F32), out)
    out = jnp.where(lane == 2, w0, out)
    return jnp.where(lane == 3, w0 * r, out)


def _merge(ya, yb, gates, wa, wb, wo, x_all, mod_l, nw, wr, br, rows):
    M, D = x_all.shape
    W = ya.shape[1]
    chunks = D // 2 // LANES
    return pl.pallas_call(
        _merge_kernel,
        grid=(rows.n,),
        in_specs=[
            rows.row_spec(W), rows.row_spec(W),
            pl.BlockSpec((ROW_TILE, D), lambda i: (rows.tile(i), 0)),
            pl.BlockSpec((ROW_TILE, D), lambda i: (rows.tile(i), 1)),
            _const_spec((W, D)), _const_spec((W, D)), _const_spec((D, D)),
            rows.row_spec(D), rows.mod_spec(D, 2),
            _const_spec((1, D)), rows.mod_spec(D, 3), rows.mod_spec(D, 4),
            _const_spec((D, 2 * ROUTER_LANES)), _const_spec((1, ROUTER_LANES)),
        ],
        out_specs=[rows.row_spec(D),
                   pl.BlockSpec((ROW_TILE * chunks, LANES), lambda i: (rows.tile(i), 0)),
                   rows.row_spec(ROUTER_LANES)],
        out_shape=[jax.ShapeDtypeStruct((M, D), F32), jax.ShapeDtypeStruct((M * chunks, LANES), jnp.uint32),
                   jax.ShapeDtypeStruct((M, ROUTER_LANES), F32)],
        compiler_params=_params("parallel"),
        name="merge",
    )(ya, yb, gates, gates, wa, wb, wo, x_all, mod_l, nw, mod_l, mod_l, wr, br)


def _dispatch(eid, tok_rows, row_scale):
    n = eid.shape[0]
    a = n * TOP_K
    e_flat = eid.reshape(a)
    onehot = (e_flat[:, None] == jnp.arange(N_EXPERTS, dtype=jnp.int32)[None, :]).astype(jnp.int32)
    csum = jnp.cumsum(onehot, axis=0)
    counts = csum[-1]
    padded = (counts + MOE_ROWS - 1) // MOE_ROWS * MOE_ROWS
    pad_end = jnp.cumsum(padded)
    pad_start = pad_end - padded
    dest = jnp.sum(onehot * (csum - 1 + pad_start[None, :]), axis=1)
    n_grid = a // MOE_ROWS + N_EXPERTS - 1 + GATHER_AHEAD
    _, sorted_row = lax.sort((e_flat, jnp.repeat(tok_rows * row_scale, TOP_K)), num_keys=1, is_stable=True)
    sorted_row = jnp.concatenate([sorted_row, jnp.zeros((MOE_ROWS,), jnp.int32)])
    blk = jnp.arange(n_grid, dtype=jnp.int32)
    n_valid = (pad_end[-1] // MOE_ROWS).astype(jnp.int32)
    blk_c = jnp.minimum(blk, n_valid - 1)
    blk_e = jnp.sum((pad_end[None, :] <= (blk_c * MOE_ROWS)[:, None]).astype(jnp.int32), axis=1)
    blk_e = jnp.minimum(blk_e, N_EXPERTS - 1)
    start = jnp.cumsum(counts) - counts
    blk_first = jnp.where(blk < n_valid, start[blk_e] + blk * MOE_ROWS - pad_start[blk_e], 0).astype(jnp.int32)
    experts = jnp.arange(N_EXPERTS, dtype=jnp.int32)
    used = counts > 0
    first_used_from = lax.cummin(jnp.where(used, experts, N_EXPERTS), reverse=True)
    next_used = jnp.concatenate([first_used_from[1:], jnp.full((1,), N_EXPERTS, jnp.int32)])
    next_used = jnp.where(next_used == N_EXPERTS, -1, next_used)
    parity = (jnp.cumsum(used.astype(jnp.int32)) - 1) % 2
    after_next = jnp.where(next_used >= 0, next_used[jnp.maximum(next_used, 0)], -1)
    return (dest, blk_e, after_next[blk_e].astype(jnp.int32), parity[blk_e].astype(jnp.int32), blk_first,
            sorted_row, next_used[blk_e[:1]].astype(jnp.int32), n_valid.reshape(1))


def _experts_kernel(blk_e, blk_after_next, blk_par, blk_first, sorted_row, second_expert, n_valid,
                    h_hbm, w1_hbm, w3_hbm, w2_hbm, o_ref,
                    xg, xsem, ws1, ws3, ws2, wsem, w1b, w3b, w2b, *, layer):
    i = pl.program_id(0)
    nv = n_valid[0]
    chunks = xg.shape[1] // MOE_ROWS
    e = blk_e[i]
    par = blk_par[i]
    fresh = jnp.logical_or(i == 0, e != blk_e[jnp.maximum(i - 1, 0)])

    def weight_copies(expert, p):
        return (pltpu.make_async_copy(w1_hbm.at[layer, expert], ws1.at[p], wsem.at[p, 0]),
                pltpu.make_async_copy(w3_hbm.at[layer, expert], ws3.at[p], wsem.at[p, 1]),
                pltpu.make_async_copy(w2_hbm.at[layer, expert], ws2.at[p], wsem.at[p, 2]))

    def for_rows(blk, fn):
        buf = lax.rem(blk, GATHER_AHEAD + 1)
        base = blk_first[blk]
        for j in range(MOE_ROWS):
            r = pl.multiple_of(sorted_row[base + j], chunks)
            fn(pltpu.make_async_copy(h_hbm.at[pl.ds(r, chunks), :], xg.at[buf, pl.ds(j * chunks, chunks), :],
                                     xsem.at[buf]))

    def start_weights(expert, p):
        for cp, priority in zip(weight_copies(expert, p), WEIGHT_DMA_PRIORITIES):
            cp.start(priority=priority)

    @pl.when(i == 0)
    def _():
        start_weights(e, par)
        pl.when(second_expert[0] >= 0)(lambda: start_weights(second_expert[0], 1 - par))
        for blk in range(GATHER_AHEAD):
            for_rows(blk, lambda cp: cp.start())

    @pl.when(jnp.logical_and(i >= nv, i < nv + GATHER_AHEAD))
    def _():
        for_rows(i, lambda cp: cp.wait())

    def compute(first_block_of_expert):
        for_rows(i, lambda cp: cp.wait())
        words = jnp.concatenate(
            [xg[lax.rem(i, GATHER_AHEAD + 1), pl.ds(c, MOE_ROWS, stride=chunks), :] for c in range(chunks)],
            axis=1)
        x = jnp.concatenate([pltpu.bitcast(words << 16, F32).astype(BF16),
                             pltpu.bitcast(words & jnp.uint32(0xFFFF0000), F32).astype(BF16)], axis=1)
        for_rows(i + GATHER_AHEAD, lambda cp: cp.start())
        if first_block_of_expert:
            for cp in weight_copies(e, par):
                cp.wait()
            w1b[...] = ws1[par].astype(BF16)
        a = jnp.dot(x, w1b[...], preferred_element_type=F32)
        if first_block_of_expert:
            w3b[...] = ws3[par].astype(BF16)
        b = jnp.dot(x, w3b[...], preferred_element_type=F32)
        if first_block_of_expert:
            w2b[...] = ws2[par].astype(BF16)
            pl.when(blk_after_next[i] >= 0)(lambda: start_weights(blk_after_next[i], par))
        hid = (a * jax.nn.sigmoid(a)) * b
        o_ref[...] = jnp.dot(hid.astype(BF16), w2b[...], preferred_element_type=F32).astype(o_ref.dtype)

    pl.when(fresh)(lambda: compute(True))
    pl.when(jnp.logical_and(jnp.logical_not(fresh), i < nv))(lambda: compute(False))


def _experts(h2, w1, w3, w2, layer, blk_e, blk_after_next, blk_par, blk_first, sorted_row, second_expert, n_valid):
    D, de = w1.shape[-2:]
    chunks = D // 2 // LANES
    n_blk = blk_e.shape[0]
    any_spec = pl.BlockSpec(memory_space=pl.ANY)
    return pl.pallas_call(
        functools.partial(_experts_kernel, layer=layer),
        grid_spec=pltpu.PrefetchScalarGridSpec(
            num_scalar_prefetch=7, grid=(n_blk,),
            in_specs=[any_spec, any_spec, any_spec, any_spec],
            out_specs=pl.BlockSpec((MOE_ROWS, D), lambda i, *s: (jnp.minimum(i, s[6][0] - 1), 0)),
            scratch_shapes=[
                pltpu.VMEM((GATHER_AHEAD + 1, MOE_ROWS * chunks, LANES), jnp.uint32),
                pltpu.SemaphoreType.DMA((GATHER_AHEAD + 1,)),
                pltpu.VMEM((2, D, de), F32), pltpu.VMEM((2, D, de), F32), pltpu.VMEM((2, de, D), F32),
                pltpu.SemaphoreType.DMA((2, 3)),
                pltpu.VMEM((D, de), BF16), pltpu.VMEM((D, de), BF16), pltpu.VMEM((de, D), BF16),
            ],
        ),
        out_shape=jax.ShapeDtypeStruct((n_blk * MOE_ROWS, D), BF16),
        compiler_params=_params("arbitrary"),
        name="experts",
    )(blk_e, blk_after_next, blk_par, blk_first, sorted_row, second_expert, n_valid, h2, w1, w3, w2)


def _moe_mix(x_ref, y0_ref, y1_ref, w_ref, g2_ref):
    w = w_ref[...]
    y = w[:, 0:1] * y0_ref[...].astype(F32) + w[:, 1:2] * y1_ref[...].astype(F32)
    return x_ref[...] + g2_ref[...] * y


def _combine_next_kernel(x_ref, y0_ref, y1_ref, w_ref, g2_ref, nw_ref, sh_ref, sc_ref, xo_ref, h_ref):
    xn = _moe_mix(x_ref, y0_ref, y1_ref, w_ref, g2_ref)
    xo_ref[...] = xn
    h_ref[...] = ((_rms(xn) * nw_ref[...]) * (1.0 + sc_ref[...]) + sh_ref[...]).astype(BF16)


def _combine_final_kernel(x_ref, y0_ref, y1_ref, w_ref, g2_ref, nf_ref, o_ref):
    o_ref[...] = _rms(_moe_mix(x_ref, y0_ref, y1_ref, w_ref, g2_ref)) * nf_ref[...]


def _token_specs(n_tokens, d):
    tiles = n_tokens // ROW_TILE
    return [pl.BlockSpec((ROW_TILE, d), lambda i: (i, 0)), pl.BlockSpec((ROW_TILE, d), lambda i: (i + tiles, 0)),
            pl.BlockSpec((ROW_TILE, TOP_K), lambda i: (i, 0))]


def _combine_next(x_new, ysel, wts, mod_l, nw_next, mod_next, rows):
    M, D = x_new.shape
    return pl.pallas_call(
        _combine_next_kernel,
        grid=(rows.n,),
        in_specs=[rows.row_spec(D), *_token_specs(M, D), rows.mod_spec(D, 5),
                  _const_spec((1, D)), rows.mod_spec(D, 0), rows.mod_spec(D, 1)],
        out_specs=[rows.row_spec(D), rows.row_spec(D)],
        out_shape=[jax.ShapeDtypeStruct((M, D), F32), jax.ShapeDtypeStruct((M, D), BF16)],
        compiler_params=_params("parallel"),
        name="combine_next",
    )(x_new, ysel, ysel, wts, mod_l, nw_next, mod_next, mod_next)


def _combine_final(x_new, ysel, wts, mod_l, nf, rows):
    M, D = x_new.shape
    n = wts.shape[0]
    return pl.pallas_call(
        _combine_final_kernel,
        grid=(rows.n,),
        in_specs=[rows.row_spec(D), *_token_specs(n, D), rows.mod_spec(D, 5), _const_spec((1, D))],
        out_specs=pl.BlockSpec((ROW_TILE, D), lambda i: (i, 0)),
        out_shape=jax.ShapeDtypeStruct((n, D), F32),
        compiler_params=_params("parallel"),
        name="combine_final",
    )(x_new, ysel, ysel, wts, mod_l, nf)


def _rope_tables(ctx_len, seq):
    quarter = HEAD_DIM // 4
    freqs = ROPE_BASE ** (-jnp.arange(quarter, dtype=F32) / quarter)
    pos = jnp.arange(seq, dtype=jnp.int32)
    row = (pos // GRID_W).astype(F32)[:, None] * freqs[None, :]
    col = (pos % GRID_W).astype(F32)[:, None] * freqs[None, :]
    cos = jnp.concatenate([jnp.cos(row), jnp.cos(col), jnp.cos(row), jnp.cos(col)], axis=-1)
    sin = jnp.concatenate([-jnp.sin(row), -jnp.sin(col), jnp.sin(row), jnp.sin(col)], axis=-1)
    cos = jnp.concatenate([jnp.ones((ctx_len, HEAD_DIM), F32), cos], axis=0)
    sin = jnp.concatenate([jnp.zeros((ctx_len, HEAD_DIM), F32), sin], axis=0)
    return cos, sin


def kernel(x, c, ctx, c_ctx, w_mod, b_mod, norm_mix, norm_ffn, w_in, qn_a, kn_a, sink_b, w_br_a, w_br_b, w_out, w_rg, b_rg, w_re, b_re, w1, w3, w2, norm_final):
    B, S, D = x.shape
    C = ctx.shape[1]
    L = w_mod.shape[0]
    T = C + S
    M = B * T
    assert C == ROW_TILE and S % ROW_TILE == 0 and T % PROJ_ROWS == 0 and B < MOD_ROWS
    tiles_per_batch = T // ROW_TILE

    cc = jnp.concatenate([c, c_ctx[None, :], jnp.zeros((MOD_ROWS - B - 1, D), F32)], axis=0)
    mod = _mod_vectors(cc, w_mod, b_mod).reshape(L, MOD_ROWS * 6, 1, D)
    cos, sin = _rope_tables(C, S)
    latent_rows = (jnp.arange(B, dtype=jnp.int32)[:, None] * T + C
                   + jnp.arange(S, dtype=jnp.int32)[None, :]).reshape(B * S)
    all_rows = jnp.arange(M, dtype=jnp.int32)

    mixer_kinds = lambda q_kind, k_kind: (tuple((q_kind, 0, k) for k in range(HEADS))
                                          + tuple((k_kind, 1, k) for k in range(KV_HEADS))
                                          + tuple(("v", 2, k) for k in range(KV_HEADS)))
    mixer_widths = [(WIDTH, 1), (KVW, 1), (KVW, 1)]
    gate_kinds = tuple(("gate", 0, k) for k in range(GATE_COLS // HEAD_DIM))
    col = {"a": 0, "b": WIDTH + 2 * KVW, "gates": 2 * WIDTH + 4 * KVW}

    every = _Rows(B, tiles_per_batch, latent_only=False)
    latent = _Rows(B, tiles_per_batch, latent_only=True)
    x_all, h = _norm_mod(x, ctx, norm_mix[0][None, :], mod[0])
    out = None
    for l in range(L):
        last = l == L - 1
        rows = latent if last else every
        proj = functools.partial(_proj, h, w_in, l, cos, sin, tiles_per_batch=tiles_per_batch)
        gains = jnp.stack([_pair_halves(qn_a[l]), _pair_halves(kn_a[l])])
        qa, ka, va = proj(gains, col["a"], mixer_kinds("q_norm", "k_norm"), mixer_widths, name="proj_a")
        qb, kb, vb = proj(gains, col["b"], mixer_kinds("q_rope", "k_rope"), mixer_widths, name="proj_b")
        (gates,) = proj(gains, col["gates"], gate_kinds, [(GATE_COLS, 2 * D // GATE_COLS)], name="proj_gates",
                        rows=GATE_ROWS)

        ya = _attention(qa, ka, va, None, B, T, C, last, window=False)
        yb = _attention(qb, kb, vb, sink_b[l], B, T, C, last, window=True)

        w_router = jnp.concatenate(
            [w_rg[l], w_re[l], jnp.zeros((D, ROUTER_LANES - N_GROUPS - N_EXPERTS), F32)], axis=1)
        w_router_hi = w_router.astype(BF16)
        w_router = jnp.concatenate([w_router_hi, (w_router - w_router_hi.astype(F32)).astype(BF16)], axis=1)
        b_router = jnp.concatenate(
            [b_rg[l], b_re[l], jnp.zeros((ROUTER_LANES - N_GROUPS - N_EXPERTS,), F32)])[None, :]
        x_new, h2, route = _merge(ya, yb, gates, w_br_a[l].astype(BF16), w_br_b[l].astype(BF16),
                                   w_out[l].astype(BF16), x_all, mod[l], norm_ffn[l][None, :],
                                   w_router, b_router, rows)

        tok_rows = latent_rows if last else all_rows
        route = route[tok_rows] if last else route
        eid, wts = route[:, :TOP_K].astype(jnp.int32), route[:, TOP_K:2 * TOP_K]
        dest, *plan = _dispatch(eid, tok_rows, D // 2 // LANES)
        ybuf = _experts(h2, w1, w3, w2, l, *plan)
        ysel = ybuf[dest.reshape(-1, TOP_K).T.reshape(-1)]
        if last:
            out = _combine_final(x_new, ysel, wts, mod[l], norm_final[None, :], rows).reshape(B, S, D)
        else:
            x_all, h = _combine_next(x_new, ysel, wts, mod[l], norm_mix[l + 1][None, :], mod[l + 1], rows)
    return out
```

```python
import functools

import jax
import jax.numpy as jnp
from jax import lax
from jax.experimental import pallas as pl
from jax.experimental.pallas import tpu as pltpu

F32 = jnp.float32
BF16 = jnp.bfloat16

GRID_W = 64
HEAD_DIM = 128
LANES = 128
HEADS = 8
KV_HEADS = 2
GROUP = HEADS // KV_HEADS
WIDTH = HEADS * HEAD_DIM
KVW = KV_HEADS * HEAD_DIM
WINDOW = 128
ROPE_BASE = 10000.0
ATTN_SCALE = HEAD_DIM ** -0.5
LOG2E = 1.4426950408889634
N_GROUPS = 8
EXPERTS_PER_GROUP = 8
N_EXPERTS = N_GROUPS * EXPERTS_PER_GROUP
TOP_K = 2
EPS = 1e-6
NEG_INF = -1e30

ROW_TILE = 256
PROJ_ROWS = 768
GATE_ROWS = 1152
GATE_COLS = 1024
MOE_ROWS = 128
GATHER_AHEAD = 2
WEIGHT_DMA_PRIORITIES = (1, 1, 0)
ROUTER_LANES = 128
DENSE_CHAIN_HEADS = 4
WINDOW_CHAIN_HEADS = 1
MOD_ROWS = 8
VMEM_LIMIT = 56 * 1024 * 1024


def _params(*sem):
    return pltpu.CompilerParams(dimension_semantics=sem, vmem_limit_bytes=VMEM_LIMIT)


def _rms(y):
    return y * lax.rsqrt(jnp.mean(y * y, axis=-1, keepdims=True) + EPS)


def _mod_kernel(c_ref, w_ref, b_ref, o_ref):
    c = c_ref[...]
    a = (c * jax.nn.sigmoid(c)).astype(BF16)
    o_ref[...] = jnp.dot(a, w_ref[...].astype(BF16), preferred_element_type=F32) + b_ref[...]


def _mod_vectors(cc, w_mod, b_mod):
    L, D, N = w_mod.shape
    tn = 1024
    return pl.pallas_call(
        _mod_kernel,
        grid=(L, N // tn),
        in_specs=[
            pl.BlockSpec((MOD_ROWS, D), lambda l, j: (0, 0)),
            pl.BlockSpec((None, D, tn), lambda l, j: (l, 0, j)),
            pl.BlockSpec((None, 1, tn), lambda l, j: (l, 0, j)),
        ],
        out_specs=pl.BlockSpec((None, MOD_ROWS, tn), lambda l, j: (l, 0, j)),
        out_shape=jax.ShapeDtypeStruct((L, MOD_ROWS, N), F32),
        compiler_params=_params("parallel", "parallel"),
        name="mod_vectors",
    )(cc, w_mod, b_mod.reshape(L, 1, N))


class _Rows:
    def __init__(self, batch, tiles_per_batch, latent_only):
        self.batch = batch
        self.tpb = tiles_per_batch
        self.latent_only = latent_only
        self.n = batch * (tiles_per_batch - 1 if latent_only else tiles_per_batch)

    def tile(self, i):
        if self.latent_only:
            per = self.tpb - 1
            return (i // per) * self.tpb + 1 + i % per
        return i

    def mod_row(self, i):
        if self.latent_only:
            return i // (self.tpb - 1)
        return jnp.where(i % self.tpb == 0, self.batch, i // self.tpb)

    def row_spec(self, width):
        return pl.BlockSpec((ROW_TILE, width), lambda i: (self.tile(i), 0))

    def mod_spec(self, d, k):
        return pl.BlockSpec((None, 1, d), lambda i: (self.mod_row(i) * 6 + k, 0, 0))


def _const_spec(shape):
    zeros = (0,) * len(shape)
    return pl.BlockSpec(shape, lambda *_: zeros, pipeline_mode=pl.Buffered(1))


def _norm_mod_kernel(x_ref, ctx_ref, nw_ref, sh_ref, sc_ref, xo_ref, h_ref):
    def emit(src_ref):
        x = src_ref[...]
        xo_ref[...] = x
        h_ref[...] = ((_rms(x) * nw_ref[...]) * (1.0 + sc_ref[...]) + sh_ref[...]).astype(BF16)

    is_ctx = pl.program_id(1) == 0
    pl.when(is_ctx)(lambda: emit(ctx_ref))
    pl.when(jnp.logical_not(is_ctx))(lambda: emit(x_ref))


def _norm_mod(x, ctx, nw, mod_l):
    B, S, D = x.shape
    tiles = 1 + S // ROW_TILE
    mod_spec = lambda k: pl.BlockSpec((None, 1, D), lambda b, t: (jnp.where(t == 0, B, b) * 6 + k, 0, 0))
    out_spec = pl.BlockSpec((ROW_TILE, D), lambda b, t: (b * tiles + t, 0))
    return pl.pallas_call(
        _norm_mod_kernel,
        grid=(B, tiles),
        in_specs=[pl.BlockSpec((None, ROW_TILE, D), lambda b, t: (b, jnp.maximum(t - 1, 0), 0)),
                  pl.BlockSpec((None, ROW_TILE, D), lambda b, t: (b, 0, 0)),
                  _const_spec((1, D)), mod_spec(0), mod_spec(1)],
        out_specs=[out_spec, out_spec],
        out_shape=[jax.ShapeDtypeStruct((B * tiles * ROW_TILE, D), F32),
                   jax.ShapeDtypeStruct((B * tiles * ROW_TILE, D), BF16)],
        compiler_params=_params("parallel", "parallel"),
        name="norm_mod",
    )(x, ctx, nw, mod_l, mod_l)


ROTARY_KINDS = ("q_norm", "k_norm", "q_rope", "k_rope")


def _pair_halves(a):
    q = HEAD_DIM // 4
    return jnp.concatenate([a[..., 0:q], a[..., 2 * q:3 * q], a[..., q:2 * q], a[..., 3 * q:]], axis=-1)


def _head_rms(y):
    mean_sq = jnp.dot((y * y).astype(BF16), jnp.full((HEAD_DIM, HEAD_DIM), 1.0 / HEAD_DIM, BF16),
                      preferred_element_type=F32)
    return y * lax.rsqrt(mean_sq + EPS)


def _rope(y, cos, sin):
    return y * cos + pltpu.roll(y, HEAD_DIM // 2, 1) * sin


def _proj_kernel(h_ref, w_ref, cos_ref, sin_ref, gain_ref, *refs, kinds):
    *out_refs, wb_ref = refs

    @pl.when(pl.program_id(1) == 0)
    def _():
        for c, (kind, _, _) in enumerate(kinds):
            w = w_ref[:, c * HEAD_DIM:(c + 1) * HEAD_DIM]
            wb_ref[:, c * HEAD_DIM:(c + 1) * HEAD_DIM] = (_pair_halves(w) if kind in ROTARY_KINDS else w).astype(BF16)

    acc = jnp.dot(h_ref[...], wb_ref[...], preferred_element_type=F32)
    for c, (kind, out_idx, out_chunk) in enumerate(kinds):
        y = acc[:, c * HEAD_DIM:(c + 1) * HEAD_DIM]
        if kind == "q_norm":
            y = _head_rms(y) * gain_ref[0:1, :]
        if kind == "k_norm":
            y = _head_rms(y) * gain_ref[1:2, :]
        if kind in ROTARY_KINDS:
            y = _rope(y, cos_ref[...], sin_ref[...])
        if kind in ("q_norm", "q_rope"):
            y = y * (ATTN_SCALE * LOG2E)
        out_refs[out_idx][:, out_chunk * HEAD_DIM:(out_chunk + 1) * HEAD_DIM] = y.astype(BF16)


def _proj(h, w_in, layer, cos, sin, gain, col0, kinds, out_widths, tiles_per_batch, name, rows=PROJ_ROWS):
    M, D = h.shape
    n_col_tiles = out_widths[0][1]
    cols = len(kinds) * HEAD_DIM
    cb0 = col0 // cols
    proj_tiles_per_batch = tiles_per_batch * ROW_TILE // rows
    return pl.pallas_call(
        functools.partial(_proj_kernel, kinds=kinds),
        grid=(n_col_tiles, M // rows),
        in_specs=[
            pl.BlockSpec((rows, D), lambda j, i: (i, 0)),
            pl.BlockSpec((None, D, cols), lambda j, i: (layer, 0, cb0 + j),
                         pipeline_mode=pl.Buffered(1 if n_col_tiles == 1 else 2)),
            pl.BlockSpec((rows, HEAD_DIM), lambda j, i: (i % proj_tiles_per_batch, 0)),
            pl.BlockSpec((rows, HEAD_DIM), lambda j, i: (i % proj_tiles_per_batch, 0)),
            pl.BlockSpec((2, HEAD_DIM), lambda j, i: (0, 0)),
        ],
        out_specs=[pl.BlockSpec((rows, w), lambda j, i: (i, j)) for w, _ in out_widths],
        out_shape=[jax.ShapeDtypeStruct((M, w * n), BF16) for w, n in out_widths],
        scratch_shapes=[pltpu.VMEM((D, cols), BF16)],
        compiler_params=_params("parallel", "arbitrary"),
        name=name,
    )(h, w_in, cos, sin, gain)


def _stack_heads(q_ref, g0, n):
    return jnp.concatenate([q_ref[:, g * HEAD_DIM:(g + 1) * HEAD_DIM] for g in range(g0, g0 + n)], axis=0)


def _write_heads(o_ref, g0, n, o):
    rows = o_ref.shape[0]
    for j in range(n):
        o_ref[:, (g0 + j) * HEAD_DIM:(g0 + j + 1) * HEAD_DIM] = o[j * rows:(j + 1) * rows].astype(o_ref.dtype)


def _scores(q, k):
    return lax.dot_general(q, k, (((1,), (1,)), ((), ())), preferred_element_type=F32)


def _online_step(q, k, v, m, l, acc):
    s = _scores(q, k)
    m_new = jnp.maximum(m, jnp.max(s, axis=-1, keepdims=True))
    alpha = jnp.exp2(m - m_new)
    p = jnp.exp2(s - m_new)
    l = alpha * l + jnp.sum(p, axis=-1, keepdims=True)
    acc = alpha * acc + jnp.dot(p.astype(BF16), v, preferred_element_type=F32)
    return m_new, l, acc


def _dense_attn_kernel(q_ref, k_ref, v_ref, o_ref, *, ctx_len, total_len, key_chunk, ctx_tile):
    def run(n_keys, chunk):
        for g0 in range(0, GROUP, DENSE_CHAIN_HEADS):
            q = _stack_heads(q_ref, g0, DENSE_CHAIN_HEADS)
            r = q.shape[0]
            m = jnp.full((r, 1), -jnp.inf, F32)
            l = jnp.zeros((r, 1), F32)
            acc = jnp.zeros((r, HEAD_DIM), F32)
            for c in range(n_keys // chunk):
                m, l, acc = _online_step(q, k_ref[c * chunk:(c + 1) * chunk, :],
                                         v_ref[c * chunk:(c + 1) * chunk, :], m, l, acc)
            _write_heads(o_ref, g0, DENSE_CHAIN_HEADS, acc * (1.0 / l))

    if ctx_tile:
        is_ctx = pl.program_id(2) == 0
        pl.when(is_ctx)(lambda: run(ctx_len, ctx_len))
        pl.when(jnp.logical_not(is_ctx))(lambda: run(total_len, key_chunk))
    else:
        run(total_len, key_chunk)


def _window_attn_kernel(sink_ref, q_ref, k_ref, v_ref, o_ref, *, ctx_len, total_len, ctx_tile, q_off):
    kvh = pl.program_id(1)
    tq = q_ref.shape[0]
    span = tq + 2 * WINDOW
    n = WINDOW_CHAIN_HEADS

    def attend(parts):
        for g0 in range(0, GROUP, n):
            q = _stack_heads(q_ref, g0, n)
            sink = jnp.concatenate(
                [jnp.full((tq, 1), sink_ref[kvh * GROUP + g] * LOG2E, F32) for g in range(g0, g0 + n)], axis=0)
            scores = []
            m = sink
            for k, _, bias in parts:
                s = _scores(q, k)
                if bias is not None:
                    s = s + bias
                scores.append(s)
                m = jnp.maximum(m, jnp.max(s, axis=-1, keepdims=True))
            l = jnp.exp2(sink - m)
            o = jnp.zeros((n * tq, HEAD_DIM), F32)
            for s, (_, v, _) in zip(scores, parts):
                p = jnp.exp2(s - m)
                l = l + jnp.sum(p, axis=-1, keepdims=True)
                o = o + jnp.dot(p.astype(BF16), v, preferred_element_type=F32)
            _write_heads(o_ref, g0, n, o * (1.0 / l))

    def ctx_part():
        return k_ref[0:ctx_len, :], v_ref[0:ctx_len, :], None

    def run_latent():
        q0 = (pl.program_id(2) + q_off) * tq - ctx_len
        n_latent = total_len - ctx_len
        start = pl.multiple_of(jnp.clip(q0 - WINDOW, 0, n_latent - span), WINDOW)
        row = lax.broadcasted_iota(jnp.int32, (n * tq, span), 0) & (tq - 1)
        col = lax.broadcasted_iota(jnp.int32, (n * tq, span), 1)
        bias = jnp.where(jnp.abs(col - row + (start - q0)) <= WINDOW, 0.0, NEG_INF).astype(F32)
        attend([ctx_part(), (k_ref[pl.ds(ctx_len + start, span), :], v_ref[pl.ds(ctx_len + start, span), :], bias)])

    if ctx_tile:
        is_ctx = pl.program_id(2) == 0
        pl.when(is_ctx)(lambda: attend([ctx_part()]))
        pl.when(jnp.logical_not(is_ctx))(run_latent)
    else:
        run_latent()


def _attention(q, k, v, sink, batch, total_len, ctx_len, latent_only, window):
    M = q.shape[0]
    q3, k3, v3 = (a.reshape(batch, total_len, a.shape[1]) for a in (q, k, v))
    tiles = total_len // ROW_TILE
    q_off = 1 if latent_only else 0
    gw = GROUP * HEAD_DIM
    q_spec = pl.BlockSpec((None, ROW_TILE, gw), lambda b, h, i, *_: (b, i + q_off, h))
    kv_spec = pl.BlockSpec((None, total_len, HEAD_DIM), lambda b, h, i, *_: (b, 0, h))
    grid = (batch, KV_HEADS, tiles - q_off)
    out_shape = jax.ShapeDtypeStruct(q3.shape, BF16)
    if window:
        kern = functools.partial(_window_attn_kernel, ctx_len=ctx_len, total_len=total_len,
                                 ctx_tile=not latent_only, q_off=q_off)
        out = pl.pallas_call(
            kern,
            grid_spec=pltpu.PrefetchScalarGridSpec(
                num_scalar_prefetch=1, grid=grid,
                in_specs=[q_spec, kv_spec, kv_spec], out_specs=q_spec),
            out_shape=out_shape,
            compiler_params=_params("parallel", "parallel", "parallel"),
            name="window_attention",
        )(sink, q3, k3, v3)
    else:
        kern = functools.partial(_dense_attn_kernel, ctx_len=ctx_len, total_len=total_len,
                                 key_chunk=total_len // 3, ctx_tile=not latent_only)
        out = pl.pallas_call(
            kern, grid=grid, in_specs=[q_spec, kv_spec, kv_spec], out_specs=q_spec,
            out_shape=out_shape,
            compiler_params=_params("parallel", "parallel", "parallel"),
            name="dense_attention",
        )(q3, k3, v3)
    return out.reshape(M, q.shape[1])


def _merge_kernel(ya_ref, yb_ref, ga_ref, gb_ref, wa_ref, wb_ref, wo_ref, x_ref, g1_ref,
                  nw_ref, sh_ref, sc_ref, wr_ref, br_ref, xo_ref, h_ref, lg_ref):
    a = jnp.dot(ya_ref[...], wa_ref[...], preferred_element_type=F32)
    b = jnp.dot(yb_ref[...], wb_ref[...], preferred_element_type=F32)
    m = jax.nn.sigmoid(ga_ref[...].astype(F32)) * a + jax.nn.sigmoid(gb_ref[...].astype(F32)) * b
    o = jnp.dot(m.astype(BF16), wo_ref[...], preferred_element_type=F32)
    xn = x_ref[...] + g1_ref[...] * o
    xo_ref[...] = xn
    h = (_rms(xn) * nw_ref[...]) * (1.0 + sc_ref[...]) + sh_ref[...]
    h_hi = h.astype(BF16)
    bits = pltpu.bitcast(h_hi.astype(F32), jnp.uint32)
    half = h.shape[1] // 2
    words = (bits[:, :half] >> 16) | (bits[:, half:] & jnp.uint32(0xFFFF0000))
    chunks = half // LANES
    for c in range(chunks):
        h_ref[pl.ds(c, h.shape[0], stride=chunks), :] = words[:, c * LANES:(c + 1) * LANES]
    h_lo = (h - h_hi.astype(F32)).astype(BF16)
    lg = jnp.dot(h_hi, wr_ref[...], preferred_element_type=F32)
    lg_lo = jnp.dot(h_lo, wr_ref[:, :ROUTER_LANES], preferred_element_type=F32)
    lg_ref[...] = _route(lg[:, :ROUTER_LANES] + (lg[:, ROUTER_LANES:] + lg_lo) + br_ref[...])


def _route(logits):
    lane = lax.broadcasted_iota(jnp.int32, logits.shape, 1)
    first = lambda hit: jnp.min(jnp.where(hit, lane, ROUTER_LANES), axis=-1, keepdims=True)
    is_group = lane < N_GROUPS
    gl = jnp.where(is_group, logits, -jnp.inf)
    g_max = jnp.max(gl, axis=-1, keepdims=True)
    g = first(gl == g_max)
    p_group = 1.0 / jnp.sum(jnp.where(is_group, jnp.exp(logits - g_max), 0.0), axis=-1, keepdims=True)
    lo = N_GROUPS + g * EXPERTS_PER_GROUP
    el = jnp.where(jnp.logical_and(lane >= lo, lane < lo + EXPERTS_PER_GROUP), logits, -jnp.inf)
    v0 = jnp.max(el, axis=-1, keepdims=True)
    i0 = first(el == v0)
    el = jnp.where(lane == i0, -jnp.inf, el)
    v1 = jnp.max(el, axis=-1, keepdims=True)
    i1 = first(el == v1)
    r = jnp.exp(v1 - v0)
    w0 = p_group / (1.0 + r)
    out = jnp.where(lane == 0, (i0 - N_GROUPS).astype(F32), 0.0)
    out = jnp.where(lane == 1, (i1 - N_GROUPS).astype(F32), out)
    out = jnp.where(lane == 2, w0, out)
    return jnp.where(lane == 3, w0 * r, out)


def _merge(ya, yb, gates, wa, wb, wo, x_all, mod_l, nw, wr, br, rows):
    M, D = x_all.shape
    W = ya.shape[1]
    chunks = D // 2 // LANES
    return pl.pallas_call(
        _merge_kernel,
        grid=(rows.n,),
        in_specs=[
            rows.row_spec(W), rows.row_spec(W),
            pl.BlockSpec((ROW_TILE, D), lambda i: (rows.tile(i), 0)),
            pl.BlockSpec((ROW_TILE, D), lambda i: (rows.tile(i), 1)),
            _const_spec((W, D)), _const_spec((W, D)), _const_spec((D, D)),
            rows.row_spec(D), rows.mod_spec(D, 2),
            _const_spec((1, D)), rows.mod_spec(D, 3), rows.mod_spec(D, 4),
            _const_spec((D, 2 * ROUTER_LANES)), _const_spec((1, ROUTER_LANES)),
        ],
        out_specs=[rows.row_spec(D),
                   pl.BlockSpec((ROW_TILE * chunks, LANES), lambda i: (rows.tile(i), 0)),
                   rows.row_spec(ROUTER_LANES)],
        out_shape=[jax.ShapeDtypeStruct((M, D), F32), jax.ShapeDtypeStruct((M * chunks, LANES), jnp.uint32),
                   jax.ShapeDtypeStruct((M, ROUTER_LANES), F32)],
        compiler_params=_params("parallel"),
        name="merge",
    )(ya, yb, gates, gates, wa, wb, wo, x_all, mod_l, nw, mod_l, mod_l, wr, br)


def _dispatch(eid, tok_rows, row_scale):
    n = eid.shape[0]
    a = n * TOP_K
    e_flat = eid.reshape(a)
    onehot = (e_flat[:, None] == jnp.arange(N_EXPERTS, dtype=jnp.int32)[None, :]).astype(jnp.int32)
    csum = jnp.cumsum(onehot, axis=0)
    counts = csum[-1]
    padded = (counts + MOE_ROWS - 1) // MOE_ROWS * MOE_ROWS
    pad_end = jnp.cumsum(padded)
    pad_start = pad_end - padded
    dest = jnp.sum(onehot * (csum - 1 + pad_start[None, :]), axis=1)
    n_grid = a // MOE_ROWS + N_EXPERTS - 1 + GATHER_AHEAD
    _, sorted_row = lax.sort((e_flat, jnp.repeat(tok_rows * row_scale, TOP_K)), num_keys=1, is_stable=True)
    sorted_row = jnp.concatenate([sorted_row, jnp.zeros((MOE_ROWS,), jnp.int32)])
    blk = jnp.arange(n_grid, dtype=jnp.int32)
    n_valid = (pad_end[-1] // MOE_ROWS).astype(jnp.int32)
    blk_c = jnp.minimum(blk, n_valid - 1)
    blk_e = jnp.sum((pad_end[None, :] <= (blk_c * MOE_ROWS)[:, None]).astype(jnp.int32), axis=1)
    blk_e = jnp.minimum(blk_e, N_EXPERTS - 1)
    start = jnp.cumsum(counts) - counts
    blk_first = jnp.where(blk < n_valid, start[blk_e] + blk * MOE_ROWS - pad_start[blk_e], 0).astype(jnp.int32)
    experts = jnp.arange(N_EXPERTS, dtype=jnp.int32)
    used = counts > 0
    first_used_from = lax.cummin(jnp.where(used, experts, N_EXPERTS), reverse=True)
    next_used = jnp.concatenate([first_used_from[1:], jnp.full((1,), N_EXPERTS, jnp.int32)])
    next_used = jnp.where(next_used == N_EXPERTS, -1, next_used)
    parity = (jnp.cumsum(used.astype(jnp.int32)) - 1) % 2
    after_next = jnp.where(next_used >= 0, next_used[jnp.maximum(next_used, 0)], -1)
    return (dest, blk_e, after_next[blk_e].astype(jnp.int32), parity[blk_e].astype(jnp.int32), blk_first,
            sorted_row, next_used[blk_e[:1]].astype(jnp.int32), n_valid.reshape(1))


def _experts_kernel(blk_e, blk_after_next, blk_par, blk_first, sorted_row, second_expert, n_valid,
                    h_hbm, w1_hbm, w3_hbm, w2_hbm, o_ref,
                    xg, xsem, ws1, ws3, ws2, wsem, w1b, w3b, w2b, *, layer):
    i = pl.program_id(0)
    nv = n_valid[0]
    chunks = xg.shape[1] // MOE_ROWS
    e = blk_e[i]
    par = blk_par[i]
    fresh = jnp.logical_or(i == 0, e != blk_e[jnp.maximum(i - 1, 0)])

    def weight_copies(expert, p):
        return (pltpu.make_async_copy(w1_hbm.at[layer, expert], ws1.at[p], wsem.at[p, 0]),
                pltpu.make_async_copy(w3_hbm.at[layer, expert], ws3.at[p], wsem.at[p, 1]),
                pltpu.make_async_copy(w2_hbm.at[layer, expert], ws2.at[p], wsem.at[p, 2]))

    def for_rows(blk, fn):
        buf = lax.rem(blk, GATHER_AHEAD + 1)
        base = blk_first[blk]
        for j in range(MOE_ROWS):
            r = pl.multiple_of(sorted_row[base + j], chunks)
            fn(pltpu.make_async_copy(h_hbm.at[pl.ds(r, chunks), :], xg.at[buf, pl.ds(j * chunks, chunks), :],
                                     xsem.at[buf]))

    def start_weights(expert, p):
        for cp, priority in zip(weight_copies(expert, p), WEIGHT_DMA_PRIORITIES):
            cp.start(priority=priority)

    @pl.when(i == 0)
    def _():
        start_weights(e, par)
        pl.when(second_expert[0] >= 0)(lambda: start_weights(second_expert[0], 1 - par))
        for blk in range(GATHER_AHEAD):
            for_rows(blk, lambda cp: cp.start())

    @pl.when(jnp.logical_and(i >= nv, i < nv + GATHER_AHEAD))
    def _():
        for_rows(i, lambda cp: cp.wait())

    def compute(first_block_of_expert):
        for_rows(i, lambda cp: cp.wait())
        words = jnp.concatenate(
            [xg[lax.rem(i, GATHER_AHEAD + 1), pl.ds(c, MOE_ROWS, stride=chunks), :] for c in range(chunks)],
            axis=1)
        x = jnp.concatenate([pltpu.bitcast(words << 16, F32).astype(BF16),
                             pltpu.bitcast(words & jnp.uint32(0xFFFF0000), F32).astype(BF16)], axis=1)
        for_rows(i + GATHER_AHEAD, lambda cp: cp.start())
        if first_block_of_expert:
            for cp in weight_copies(e, par):
                cp.wait()
            w1b[...] = ws1[par].astype(BF16)
        a = jnp.dot(x, w1b[...], preferred_element_type=F32)
        if first_block_of_expert:
            w3b[...] = ws3[par].astype(BF16)
        b = jnp.dot(x, w3b[...], preferred_element_type=F32)
        if first_block_of_expert:
            w2b[...] = ws2[par].astype(BF16)
            pl.when(blk_after_next[i] >= 0)(lambda: start_weights(blk_after_next[i], par))
        hid = (a * jax.nn.sigmoid(a)) * b
        o_ref[...] = jnp.dot(hid.astype(BF16), w2b[...], preferred_element_type=F32).astype(o_ref.dtype)

    pl.when(fresh)(lambda: compute(True))
    pl.when(jnp.logical_and(jnp.logical_not(fresh), i < nv))(lambda: compute(False))


def _experts(h2, w1, w3, w2, layer, blk_e, blk_after_next, blk_par, blk_first, sorted_row, second_expert, n_valid):
    D, de = w1.shape[-2:]
    chunks = D // 2 // LANES
    n_blk = blk_e.shape[0]
    any_spec = pl.BlockSpec(memory_space=pl.ANY)
    return pl.pallas_call(
        functools.partial(_experts_kernel, layer=layer),
        grid_spec=pltpu.PrefetchScalarGridSpec(
            num_scalar_prefetch=7, grid=(n_blk,),
            in_specs=[any_spec, any_spec, any_spec, any_spec],
            out_specs=pl.BlockSpec((MOE_ROWS, D), lambda i, *s: (jnp.minimum(i, s[6][0] - 1), 0)),
            scratch_shapes=[
                pltpu.VMEM((GATHER_AHEAD + 1, MOE_ROWS * chunks, LANES), jnp.uint32),
                pltpu.SemaphoreType.DMA((GATHER_AHEAD + 1,)),
                pltpu.VMEM((2, D, de), F32), pltpu.VMEM((2, D, de), F32), pltpu.VMEM((2, de, D), F32),
                pltpu.SemaphoreType.DMA((2, 3)),
                pltpu.VMEM((D, de), BF16), pltpu.VMEM((D, de), BF16), pltpu.VMEM((de, D), BF16),
            ],
        ),
        out_shape=jax.ShapeDtypeStruct((n_blk * MOE_ROWS, D), BF16),
        compiler_params=_params("arbitrary"),
        name="experts",
    )(blk_e, blk_after_next, blk_par, blk_first, sorted_row, second_expert, n_valid, h2, w1, w3, w2)


def _moe_mix(x_ref, y0_ref, y1_ref, w_ref, g2_ref):
    w = w_ref[...]
    y = w[:, 0:1] * y0_ref[...].astype(F32) + w[:, 1:2] * y1_ref[...].astype(F32)
    return x_ref[...] + g2_ref[...] * y


def _combine_next_kernel(x_ref, y0_ref, y1_ref, w_ref, g2_ref, nw_ref, sh_ref, sc_ref, xo_ref, h_ref):
    xn = _moe_mix(x_ref, y0_ref, y1_ref, w_ref, g2_ref)
    xo_ref[...] = xn
    h_ref[...] = ((_rms(xn) * nw_ref[...]) * (1.0 + sc_ref[...]) + sh_ref[...]).astype(BF16)


def _combine_final_kernel(x_ref, y0_ref, y1_ref, w_ref, g2_ref, nf_ref, o_ref):
    o_ref[...] = _rms(_moe_mix(x_ref, y0_ref, y1_ref, w_ref, g2_ref)) * nf_ref[...]


def _token_specs(n_tokens, d):
    tiles = n_tokens // ROW_TILE
    return [pl.BlockSpec((ROW_TILE, d), lambda i: (i, 0)), pl.BlockSpec((ROW_TILE, d), lambda i: (i + tiles, 0)),
            pl.BlockSpec((ROW_TILE, TOP_K), lambda i: (i, 0))]


def _combine_next(x_new, ysel, wts, mod_l, nw_next, mod_next, rows):
    M, D = x_new.shape
    return pl.pallas_call(
        _combine_next_kernel,
        grid=(rows.n,),
        in_specs=[rows.row_spec(D), *_token_specs(M, D), rows.mod_spec(D, 5),
                  _const_spec((1, D)), rows.mod_spec(D, 0), rows.mod_spec(D, 1)],
        out_specs=[rows.row_spec(D), rows.row_spec(D)],
        out_shape=[jax.ShapeDtypeStruct((M, D), F32), jax.ShapeDtypeStruct((M, D), BF16)],
        compiler_params=_params("parallel"),
        name="combine_next",
    )(x_new, ysel, ysel, wts, mod_l, nw_next, mod_next, mod_next)


def _combine_final(x_new, ysel, wts, mod_l, nf, rows):
    M, D = x_new.shape
    n = wts.shape[0]
    return pl.pallas_call(
        _combine_final_kernel,
        grid=(rows.n,),
        in_specs=[rows.row_spec(D), *_token_specs(n, D), rows.mod_spec(D, 5), _const_spec((1, D))],
        out_specs=pl.BlockSpec((ROW_TILE, D), lambda i: (i, 0)),
        out_shape=jax.ShapeDtypeStruct((n, D), F32),
        compiler_params=_params("parallel"),
        name="combine_final",
    )(x_new, ysel, ysel, wts, mod_l, nf)


def _rope_tables(ctx_len, seq):
    quarter = HEAD_DIM // 4
    freqs = ROPE_BASE ** (-jnp.arange(quarter, dtype=F32) / quarter)
    pos = jnp.arange(seq, dtype=jnp.int32)
    row = (pos // GRID_W).astype(F32)[:, None] * freqs[None, :]
    col = (pos % GRID_W).astype(F32)[:, None] * freqs[None, :]
    cos = jnp.concatenate([jnp.cos(row), jnp.cos(col), jnp.cos(row), jnp.cos(col)], axis=-1)
    sin = jnp.concatenate([-jnp.sin(row), -jnp.sin(col), jnp.sin(row), jnp.sin(col)], axis=-1)
    cos = jnp.concatenate([jnp.ones((ctx_len, HEAD_DIM), F32), cos], axis=0)
    sin = jnp.concatenate([jnp.zeros((ctx_len, HEAD_DIM), F32), sin], axis=0)
    return cos, sin


def kernel(x, c, ctx, c_ctx, w_mod, b_mod, norm_mix, norm_ffn, w_in, qn_a, kn_a, sink_b, w_br_a, w_br_b, w_out, w_rg, b_rg, w_re, b_re, w1, w3, w2, norm_final):
    B, S, D = x.shape
    C = ctx.shape[1]
    L = w_mod.shape[0]
    T = C + S
    M = B * T
    assert C == ROW_TILE and S % ROW_TILE == 0 and T % PROJ_ROWS == 0 and B < MOD_ROWS
    tiles_per_batch = T // ROW_TILE

    cc = jnp.concatenate([c, c_ctx[None, :], jnp.zeros((MOD_ROWS - B - 1, D), F32)], axis=0)
    mod = _mod_vectors(cc, w_mod, b_mod).reshape(L, MOD_ROWS * 6, 1, D)
    cos, sin = _rope_tables(C, S)
    latent_rows = (jnp.arange(B, dtype=jnp.int32)[:, None] * T + C
                   + jnp.arange(S, dtype=jnp.int32)[None, :]).reshape(B * S)
    all_rows = jnp.arange(M, dtype=jnp.int32)

    mixer_kinds = lambda q_kind, k_kind: (tuple((q_kind, 0, k) for k in range(HEADS))
                                          + tuple((k_kind, 1, k) for k in range(KV_HEADS))
                                          + tuple(("v", 2, k) for k in range(KV_HEADS)))
    mixer_widths = [(WIDTH, 1), (KVW, 1), (KVW, 1)]
    gate_kinds = tuple(("gate", 0, k) for k in range(GATE_COLS // HEAD_DIM))
    col = {"a": 0, "b": WIDTH + 2 * KVW, "gates": 2 * WIDTH + 4 * KVW}

    every = _Rows(B, tiles_per_batch, latent_only=False)
    latent = _Rows(B, tiles_per_batch, latent_only=True)
    x_all, h = _norm_mod(x, ctx, norm_mix[0][None, :], mod[0])
    out = None
    for l in range(L):
        last = l == L - 1
        rows = latent if last else every
        proj = functools.partial(_proj, h, w_in, l, cos, sin, tiles_per_batch=tiles_per_batch)
        gains = jnp.stack([_pair_halves(qn_a[l]), _pair_halves(kn_a[l])])
        qa, ka, va = proj(gains, col["a"], mixer_kinds("q_norm", "k_norm"), mixer_widths, name="proj_a")
        qb, kb, vb = proj(gains, col["b"], mixer_kinds("q_rope", "k_rope"), mixer_widths, name="proj_b")
        (gates,) = proj(gains, col["gates"], gate_kinds, [(GATE_COLS, 2 * D // GATE_COLS)], name="proj_gates",
                        rows=GATE_ROWS)

        ya = _attention(qa, ka, va, None, B, T, C, last, window=False)
        yb = _attention(qb, kb, vb, sink_b[l], B, T, C, last, window=True)

        w_router = jnp.concatenate(
            [w_rg[l], w_re[l], jnp.zeros((D, ROUTER_LANES - N_GROUPS - N_EXPERTS), F32)], axis=1)
        w_router_hi = w_router.astype(BF16)
        w_router = jnp.concatenate([w_router_hi, (w_router - w_router_hi.astype(F32)).astype(BF16)], axis=1)
        b_router = jnp.concatenate(
            [b_rg[l], b_re[l], jnp.zeros((ROUTER_LANES - N_GROUPS - N_EXPERTS,), F32)])[None, :]
        x_new, h2, route = _merge(ya, yb, gates, w_br_a[l].astype(BF16), w_br_b[l].astype(BF16),
                                   w_out[l].astype(BF16), x_all, mod[l], norm_ffn[l][None, :],
                                   w_router, b_router, rows)

        tok_rows = latent_rows if last else all_rows
        route = route[tok_rows] if last else route
        eid, wts = route[:, :TOP_K].astype(jnp.int32), route[:, TOP_K:2 * TOP_K]
        dest, *plan = _dispatch(eid, tok_rows, D // 2 // LANES)
        ybuf = _experts(h2, w1, w3, w2, l, *plan)
        ysel = ybuf[dest.reshape(-1, TOP_K).T.reshape(-1)]
        if last:
            out = _combine_final(x_new, ysel, wts, mod[l], norm_final[None, :], rows).reshape(B, S, D)
        else:
            x_all, h = _combine_next(x_new, ysel, wts, mod[l], norm_mix[l + 1][None, :], mod[l + 1], rows)
    return out
```
